```python
import math
import jax, jax.numpy as jnp
from jax import lax
import numpy as np

D_MODEL = 2048
BATCH = 8
SEQ = 8192
DEPTH = 1

CHUNK = 64
Q_BLOCK = 128
EPS = 1e-6
SB_HEADS = 8
SB_HEAD_DIM = 128
SB_WIDTH = SB_HEADS * SB_HEAD_DIM
MLA_HEADS = 8
MLA_NOPE_DIM = 128
MLA_ROPE_DIM = 64
MLA_QK_DIM = MLA_NOPE_DIM + MLA_ROPE_DIM
MLA_V_DIM = 128
MLA_WIDTH = MLA_HEADS * MLA_V_DIM
Q_LORA_RANK = 512
KV_LORA_RANK = 256
ROPE_THETA = 10000.0
D_MIX = SB_WIDTH + MLA_WIDTH
IN_SPLITS = (SB_WIDTH, SB_WIDTH, SB_WIDTH, SB_WIDTH,
             Q_LORA_RANK, KV_LORA_RANK, MLA_ROPE_DIM, MLA_WIDTH)
D_IN = sum(IN_SPLITS)

kernel_name = 'hybrid_stickbreak_mla_block'


def rms_norm(x, w):
    xf = x.astype(jnp.float32)
    y = xf * lax.rsqrt(jnp.mean(xf * xf, axis=-1, keepdims=True) + EPS)
    return (y * w.astype(jnp.float32)).astype(x.dtype)


def rope_tables(positions):
    inv_freq = ROPE_THETA ** (-jnp.arange(0, MLA_ROPE_DIM, 2, dtype=jnp.float32) / MLA_ROPE_DIM)
    ang = positions.astype(jnp.float32)[..., None] * inv_freq
    return jnp.cos(ang), jnp.sin(ang)


def apply_rope(x, cos, sin):
    x1, x2 = jnp.split(x.astype(jnp.float32), 2, axis=-1)
    out = jnp.concatenate([x1 * cos - x2 * sin, x2 * cos + x1 * sin], axis=-1)
    return out.astype(x.dtype)


def to_heads(t, n_heads):
    b, s, _ = t.shape
    return t.reshape(b, s, n_heads, -1).transpose(0, 2, 1, 3)


def from_heads(t):
    b, h, s, d = t.shape
    return t.transpose(0, 2, 1, 3).reshape(b, s, h * d)


def stick_breaking_attention(q, k, v):
    seq, d = q.shape[2], q.shape[3]
    scale = 1.0 / math.sqrt(d)
    outs = []
    for b0 in range(0, seq, Q_BLOCK):
        kl = b0 + Q_BLOCK
        z = jnp.einsum('bhqd,bhkd->bhqk', q[:, :, b0:kl], k[:, :, :kl]).astype(jnp.float32) * scale
        t_idx = b0 + jnp.arange(Q_BLOCK)[:, None]
        s_idx = jnp.arange(kl)[None, :]
        before = s_idx < t_idx
        log_keep = jnp.where(before, jax.nn.log_sigmoid(-z), 0.0)
        later = lax.cumsum(log_keep, axis=3, reverse=True) - log_keep
        a = jnp.where(before, jnp.exp(jax.nn.log_sigmoid(z) + later), 0.0)
        outs.append(jnp.einsum('bhqk,bhkd->bhqd', a.astype(v.dtype), v[:, :, :kl]))
    return jnp.concatenate(outs, axis=2)


def chunk_causal_softmax_attention(q, k, v):
    seq, d = q.shape[2], q.shape[3]
    scale = 1.0 / math.sqrt(d)
    outs = []
    for b0 in range(0, seq, Q_BLOCK):
        kl = b0 + Q_BLOCK
        s_ = jnp.einsum('bhqd,bhkd->bhqk', q[:, :, b0:kl], k[:, :, :kl]).astype(jnp.float32) * scale
        t_chunk = (b0 + jnp.arange(Q_BLOCK))[:, None] // CHUNK
        s_chunk = jnp.arange(kl)[None, :] // CHUNK
        s_ = jnp.where(s_chunk <= t_chunk, s_, -jnp.inf)
        p = jax.nn.softmax(s_, axis=-1)
        outs.append(jnp.einsum('bhqk,bhkd->bhqd', p.astype(v.dtype), v[:, :, :kl]))
    return jnp.concatenate(outs, axis=2)


def hybrid_layer(x, cos, sin, pre_norm_w, w_in, q_norm_w, w_q_up, kv_norm_w, w_kv_up, w_out, post_norm_w):
    b, s, _ = x.shape
    h = rms_norm(x, pre_norm_w)
    proj = h @ w_in
    split_pts = tuple(int(i) for i in np.cumsum(IN_SPLITS)[:-1])
    sb_q, sb_k, sb_v, sb_gate, c_q, c_kv, k_rope, mla_gate = jnp.split(proj, split_pts, axis=-1)

    o_a = from_heads(stick_breaking_attention(to_heads(sb_q, SB_HEADS),
                                              to_heads(sb_k, SB_HEADS),
                                              to_heads(sb_v, SB_HEADS)))

    q_full = (rms_norm(c_q, q_norm_w) @ w_q_up).reshape(b, s, MLA_HEADS, MLA_QK_DIM)
    q_nope, q_rot = jnp.split(q_full, [MLA_NOPE_DIM], axis=-1)
    q_rot = apply_rope(q_rot, cos[:, :, None, :], sin[:, :, None, :])
    kv = (rms_norm(c_kv, kv_norm_w) @ w_kv_up).reshape(b, s, MLA_HEADS, MLA_NOPE_DIM + MLA_V_DIM)
    k_nope, v_mla = jnp.split(kv, [MLA_NOPE_DIM], axis=-1)
    k_rot = apply_rope(k_rope, cos, sin)
    k_rot = jnp.broadcast_to(k_rot[:, :, None, :], (b, s, MLA_HEADS, MLA_ROPE_DIM))
    q_mla = jnp.concatenate([q_nope, q_rot], axis=-1).transpose(0, 2, 1, 3)
    k_mla = jnp.concatenate([k_nope, k_rot], axis=-1).transpose(0, 2, 1, 3)
    o_b = from_heads(chunk_causal_softmax_attention(q_mla, k_mla, v_mla.transpose(0, 2, 1, 3)))

    mixed = jnp.concatenate([o_a * jax.nn.silu(sb_gate), o_b * jax.nn.silu(mla_gate)], axis=-1)
    y = mixed @ w_out
    return x + rms_norm(y, post_norm_w)


def _fwd_setup_inputs(seed: int = 0) -> dict:
    key = jax.random.key(seed)
    ks = jax.random.split(key, 12)
    f32 = jnp.float32
    x = jax.random.normal(ks[0], (BATCH, SEQ, D_MODEL), f32)
    positions = jnp.broadcast_to(jnp.arange(SEQ, dtype=jnp.int32)[None, :], (BATCH, SEQ))
    pre_norm_w = 1.0 + 0.05 * jax.random.normal(ks[1], (DEPTH, D_MODEL), f32)
    w_in = jax.random.normal(ks[2], (DEPTH, D_MODEL, D_IN), f32) * D_MODEL ** -0.5
    q_norm_w = 1.0 + 0.05 * jax.random.normal(ks[3], (DEPTH, Q_LORA_RANK), f32)
    w_q_up = jax.random.normal(ks[4], (DEPTH, Q_LORA_RANK, MLA_HEADS * MLA_QK_DIM), f32) * Q_LORA_RANK ** -0.5
    kv_norm_w = 1.0 + 0.05 * jax.random.normal(ks[5], (DEPTH, KV_LORA_RANK), f32)
    w_kv_up = jax.random.normal(ks[6], (DEPTH, KV_LORA_RANK, MLA_HEADS * (MLA_NOPE_DIM + MLA_V_DIM)), f32) * KV_LORA_RANK ** -0.5
    w_out = jax.random.normal(ks[7], (DEPTH, D_MIX, D_MODEL), f32) * D_MIX ** -0.5
    post_norm_w = 1.0 + 0.05 * jax.random.normal(ks[8], (DEPTH, D_MODEL), f32)
    return {'x': x, 'positions': positions, 'pre_norm_w': pre_norm_w, 'w_in': w_in,
            'q_norm_w': q_norm_w, 'w_q_up': w_q_up, 'kv_norm_w': kv_norm_w, 'w_kv_up': w_kv_up,
            'w_out': w_out, 'post_norm_w': post_norm_w}


def _fwd_reference(x, positions, pre_norm_w, w_in, q_norm_w, w_q_up, kv_norm_w, w_kv_up, w_out, post_norm_w):
    cos, sin = rope_tables(positions)
    for i in range(DEPTH):
        x = hybrid_layer(x, cos, sin, pre_norm_w[i], w_in[i], q_norm_w[i], w_q_up[i],
                         kv_norm_w[i], w_kv_up[i], w_out[i], post_norm_w[i])
    return x


import jax as _jax
import jax.numpy as _jnp

TWIN_FORMAT = 'train_step'
FWD_PARAMS = ['x', 'positions', 'pre_norm_w', 'w_in', 'q_norm_w', 'w_q_up', 'kv_norm_w', 'w_kv_up', 'w_out', 'post_norm_w']
TWIN_WEIGHTS = ['pre_norm_w', 'w_in', 'q_norm_w', 'w_q_up', 'kv_norm_w', 'w_kv_up', 'w_out', 'post_norm_w']
TWIN_DIFF_INPUT = 'x'
TWIN_INPUTS = ['x', 'positions', 'pre_norm_w', 'w_in', 'q_norm_w', 'w_q_up', 'kv_norm_w', 'w_kv_up', 'w_out', 'post_norm_w', 'loss_target', 'm_pre_norm_w', 'm_w_in', 'm_q_norm_w', 'm_w_q_up', 'm_kv_norm_w', 'm_w_kv_up', 'm_w_out', 'm_post_norm_w', 'v_pre_norm_w', 'v_w_in', 'v_q_norm_w', 'v_w_q_up', 'v_kv_norm_w', 'v_w_kv_up', 'v_w_out', 'v_post_norm_w']
TWIN_OUTPUTS = ['loss', 'grad_x', 'grad_pre_norm_w', 'grad_w_in', 'grad_q_norm_w', 'grad_w_q_up', 'grad_kv_norm_w', 'grad_w_kv_up', 'grad_w_out', 'grad_post_norm_w', 'delta_pre_norm_w', 'delta_w_in', 'delta_q_norm_w', 'delta_w_q_up', 'delta_kv_norm_w', 'delta_w_kv_up', 'delta_w_out', 'delta_post_norm_w', 'new_m_pre_norm_w', 'new_m_w_in', 'new_m_q_norm_w', 'new_m_w_q_up', 'new_m_kv_norm_w', 'new_m_w_kv_up', 'new_m_w_out', 'new_m_post_norm_w', 'new_v_pre_norm_w', 'new_v_w_in', 'new_v_q_norm_w', 'new_v_w_q_up', 'new_v_kv_norm_w', 'new_v_w_kv_up', 'new_v_w_out', 'new_v_post_norm_w']
TWIN_LEAF_KINDS = {'loss': 'loss', 'grad_x': 'grad_x', 'grad_pre_norm_w': 'grad_w', 'grad_w_in': 'grad_w', 'grad_q_norm_w': 'grad_w', 'grad_w_q_up': 'grad_w', 'grad_kv_norm_w': 'grad_w', 'grad_w_kv_up': 'grad_w', 'grad_w_out': 'grad_w', 'grad_post_norm_w': 'grad_w', 'delta_pre_norm_w': 'delta_w', 'delta_w_in': 'delta_w', 'delta_q_norm_w': 'delta_w', 'delta_w_q_up': 'delta_w', 'delta_kv_norm_w': 'delta_w', 'delta_w_kv_up': 'delta_w', 'delta_w_out': 'delta_w', 'delta_post_norm_w': 'delta_w', 'new_m_pre_norm_w': 'new_m', 'new_m_w_in': 'new_m', 'new_m_q_norm_w': 'new_m', 'new_m_w_q_up': 'new_m', 'new_m_kv_norm_w': 'new_m', 'new_m_w_kv_up': 'new_m', 'new_m_w_out': 'new_m', 'new_m_post_norm_w': 'new_m', 'new_v_pre_norm_w': 'new_v', 'new_v_w_in': 'new_v', 'new_v_q_norm_w': 'new_v', 'new_v_w_q_up': 'new_v', 'new_v_kv_norm_w': 'new_v', 'new_v_w_kv_up': 'new_v', 'new_v_w_out': 'new_v', 'new_v_post_norm_w': 'new_v'}


def _forward(args):
    return _fwd_reference(*[args[k] for k in FWD_PARAMS])


def _output_shape():
    def fwd():
        inp = _fwd_setup_inputs(0)
        return _fwd_reference(*[inp[k] for k in FWD_PARAMS])
    out = _jax.eval_shape(fwd)
    return out.shape, out.dtype

N_MICROBATCH = 1
ADAM_LR = 0.001
ADAM_B1 = 0.9
ADAM_B2 = 0.999
ADAM_EPS = 1e-08
ADAM_WD = 0.01
ADAM_STEP = 10
PER_EXAMPLE_BATCH_AXIS = {'x': 0, 'positions': 0, 'loss_target': 0}
SHARED_INPUTS = []
_WEIGHT_DTYPES = {'pre_norm_w': _jnp.float32, 'w_in': _jnp.float32, 'q_norm_w': _jnp.float32, 'w_q_up': _jnp.float32, 'kv_norm_w': _jnp.float32, 'w_kv_up': _jnp.float32, 'w_out': _jnp.float32, 'post_norm_w': _jnp.float32}
MOMENT_SCALE = {'pre_norm_w': 2.837511e-01, 'w_in': 1.676395e-01, 'q_norm_w': 8.169072e-02, 'w_q_up': 4.825122e-02, 'kv_norm_w': 1.773491e-01, 'w_kv_up': 5.570308e-02, 'w_out': 1.847774e-01, 'post_norm_w': 3.210973e+01}


def _to_microbatches(a, axis):
    t = _jnp.moveaxis(a, axis, 0)
    t = t.reshape((N_MICROBATCH, t.shape[0] // N_MICROBATCH) + t.shape[1:])
    return _jnp.moveaxis(t, 1, axis + 1)


def setup_inputs(seed: int = 0) -> dict:
    inp = _fwd_setup_inputs(seed)
    key = _jax.random.fold_in(_jax.random.key(seed), 7919)
    shape, _ = _output_shape()
    out = dict(inp)
    out["loss_target"] = _jax.random.normal(_jax.random.fold_in(key, 0), shape, _jnp.float32)
    for i, name in enumerate(TWIN_WEIGHTS):
        w = inp[name].astype(_jnp.float32)
        if MOMENT_SCALE is None:
            s = _jnp.sqrt(_jnp.mean(_jnp.square(w)) + 1e-30)
        else:
            s = MOMENT_SCALE[name]
        km, kv = _jax.random.split(_jax.random.fold_in(key, i + 1))
        out[name] = w
        out["m_" + name] = s * _jax.random.normal(km, w.shape, _jnp.float32)
        out["v_" + name] = (s * s) * _jax.random.uniform(kv, w.shape, _jnp.float32, 0.5, 1.5)
    if N_MICROBATCH > 1:
        for name, axis in PER_EXAMPLE_BATCH_AXIS.items():
            out[name] = _to_microbatches(out[name], axis)
    return {'x': out['x'], 'positions': out['positions'], 'pre_norm_w': out['pre_norm_w'], 'w_in': out['w_in'], 'q_norm_w': out['q_norm_w'], 'w_q_up': out['w_q_up'], 'kv_norm_w': out['kv_norm_w'], 'w_kv_up': out['w_kv_up'], 'w_out': out['w_out'], 'post_norm_w': out['post_norm_w'], 'loss_target': out['loss_target'], 'm_pre_norm_w': out['m_pre_norm_w'], 'm_w_in': out['m_w_in'], 'm_q_norm_w': out['m_q_norm_w'], 'm_w_q_up': out['m_w_q_up'], 'm_kv_norm_w': out['m_kv_norm_w'], 'm_w_kv_up': out['m_w_kv_up'], 'm_w_out': out['m_w_out'], 'm_post_norm_w': out['m_post_norm_w'], 'v_pre_norm_w': out['v_pre_norm_w'], 'v_w_in': out['v_w_in'], 'v_q_norm_w': out['v_q_norm_w'], 'v_w_q_up': out['v_w_q_up'], 'v_kv_norm_w': out['v_kv_norm_w'], 'v_w_kv_up': out['v_w_kv_up'], 'v_w_out': out['v_w_out'], 'v_post_norm_w': out['v_post_norm_w']}


def _loss(weights, diff, rest, loss_target):
    with _jax.named_scope("forward"):
        args = {**rest, TWIN_DIFF_INPUT: diff, **{k: w.astype(_WEIGHT_DTYPES[k]) for k, w in weights.items()}}
        y = _forward(args)
    with _jax.named_scope("loss_head"):
        err = _jnp.square(y.astype(_jnp.float32) - loss_target)
        return 0.5 * _jnp.sum(_jnp.mean(err, axis=-1)) if err.ndim else 0.5 * err


def _adamw(w, g, m, v):
    m = ADAM_B1 * m + (1.0 - ADAM_B1) * g
    v = ADAM_B2 * v + (1.0 - ADAM_B2) * _jnp.square(g)
    m_hat = m / (1.0 - ADAM_B1 ** ADAM_STEP)
    v_hat = v / (1.0 - ADAM_B2 ** ADAM_STEP)
    delta = -ADAM_LR * (m_hat / (_jnp.sqrt(v_hat) + ADAM_EPS) + ADAM_WD * w)
    return delta, m, v


def reference(x, positions, pre_norm_w, w_in, q_norm_w, w_q_up, kv_norm_w, w_kv_up, w_out, post_norm_w, loss_target, m_pre_norm_w, m_w_in, m_q_norm_w, m_w_q_up, m_kv_norm_w, m_w_kv_up, m_w_out, m_post_norm_w, v_pre_norm_w, v_w_in, v_q_norm_w, v_w_q_up, v_kv_norm_w, v_w_kv_up, v_w_out, v_post_norm_w):
    given = dict(x=x, positions=positions, pre_norm_w=pre_norm_w, w_in=w_in, q_norm_w=q_norm_w, w_q_up=w_q_up, kv_norm_w=kv_norm_w, w_kv_up=w_kv_up, w_out=w_out, post_norm_w=post_norm_w, loss_target=loss_target, m_pre_norm_w=m_pre_norm_w, m_w_in=m_w_in, m_q_norm_w=m_q_norm_w, m_w_q_up=m_w_q_up, m_kv_norm_w=m_kv_norm_w, m_w_kv_up=m_w_kv_up, m_w_out=m_w_out, m_post_norm_w=m_post_norm_w, v_pre_norm_w=v_pre_norm_w, v_w_in=v_w_in, v_q_norm_w=v_q_norm_w, v_w_q_up=v_w_q_up, v_kv_norm_w=v_kv_norm_w, v_w_kv_up=v_w_kv_up, v_w_out=v_w_out, v_post_norm_w=v_post_norm_w)
    weights = {n: given[n] for n in TWIN_WEIGHTS}
    shared = {n: given[n] for n in SHARED_INPUTS}
    per_example = {n: given[n] for n in ['x', 'positions']}
    grad_fn = _jax.value_and_grad(_loss, argnums=(0, 1))

    def one_microbatch(ex, loss_target):
        ex = dict(ex)
        diff = ex.pop(TWIN_DIFF_INPUT)
        return grad_fn(weights, diff, {**shared, **ex}, loss_target)

    if N_MICROBATCH == 1:
        loss, (grad_w, grad_x) = one_microbatch(per_example, given["loss_target"])
    else:
        def body(carry, xs):
            loss_sum, grad_sum = carry
            l_k, (gw_k, gx_k) = one_microbatch(xs[0], xs[1])
            with _jax.named_scope("update"):
                return (loss_sum + l_k, _jax.tree.map(_jnp.add, grad_sum, gw_k)), gx_k

        init = (_jnp.zeros((), _jnp.float32), _jax.tree.map(_jnp.zeros_like, weights))
        (loss, grad_w), grad_x = _jax.lax.scan(body, init, (per_example, given["loss_target"]))
    with _jax.named_scope("update"):
        delta_w, new_m, new_v = {}, {}, {}
        for n in TWIN_WEIGHTS:
            delta_w[n], new_m[n], new_v[n] = _adamw(weights[n], grad_w[n], given["m_" + n], given["v_" + n])
    return (loss, grad_x, *[grad_w[n] for n in TWIN_WEIGHTS], *[delta_w[n] for n in TWIN_WEIGHTS],
            *[new_m[n] for n in TWIN_WEIGHTS], *[new_v[n] for n in TWIN_WEIGHTS])
```

```python
import functools
import math

import numpy as np
import jax
import jax.numpy as jnp
from jax import lax
from jax.experimental import pallas as pl
from jax.experimental.pallas import tpu as pltpu

F32 = jnp.float32
BF16 = jnp.bfloat16
MESH = pl.DeviceIdType.MESH

D_MODEL = 2048
HEADS = 8
HEAD_DIM = 128
SB_WIDTH = HEADS * HEAD_DIM
MLA_ROPE = 64
MLA_QK_PAD = 256
Q_RANK = 512
KV_RANK = 256
CHUNK = 64
EPS = 1e-6
ROPE_THETA = 10000.0
D_IN = 5952
D_INP = 6144
PB_W = 3072
PF_W = D_INP - PB_W
SB_SCALE = 1.0 / math.sqrt(HEAD_DIM)
MLA_SCALE = 1.0 / math.sqrt(HEAD_DIM + MLA_ROPE)
NEG = -1e30

ADAM_LR, ADAM_B1, ADAM_B2, ADAM_EPS, ADAM_WD, ADAM_STEP = 0.001, 0.9, 0.999, 1e-08, 0.01, 10
ADAM_C1 = 1.0 - ADAM_B1 ** ADAM_STEP
ADAM_C2 = 1.0 - ADAM_B2 ** ADAM_STEP

VMEM_LIMIT = 56 * 1024 * 1024
TQ = 256


def _cp(sem=None, **kw):
    return pltpu.CompilerParams(dimension_semantics=sem, vmem_limit_bytes=VMEM_LIMIT, **kw)


def _dot(a, b, dims):
    return lax.dot_general(a, b, (dims, ((), ())), preferred_element_type=F32)


def _nn(a, b):
    return _dot(a, b, ((1,), (0,)))


def _nt(a, b):
    return _dot(a, b, ((1,), (1,)))


def _tn(a, b):
    return _dot(a, b, ((0,), (0,)))


def _split_dot(x, tri):
    hi = x.astype(BF16)
    lo = (x - hi.astype(F32)).astype(BF16)
    return _nn(hi, tri) + _nn(lo, tri)


def _rope(x, c, s1, s2):
    return x * c + pltpu.roll(x, 32, 1) * s1 + pltpu.roll(x, 96, 1) * s2


def _rope_t(x, c, s1, s2):
    return x * c - pltpu.roll(x, 32, 1) * s1 - pltpu.roll(x, 96, 1) * s2


def _silu_parts(g):
    sg = jax.nn.sigmoid(g)
    return g * sg, sg * (1.0 + g * (1.0 - sg))


def _matmul(a, b, *, mode, out_dtype, tm, tn, tk, name, n=None, b_off=0):
    if mode == "tn":
        kk, m = a.shape
        n = b.shape[1] if n is None else n
    else:
        m, kk = a.shape
        n = (b.shape[1] if mode == "nn" else b.shape[0]) if n is None else n
    nk = kk // tk
    a_spec = {"nn": pl.BlockSpec((tm, tk), lambda j, i, k: (i, k)),
              "nt": pl.BlockSpec((tm, tk), lambda j, i, k: (i, k)),
              "tn": pl.BlockSpec((tk, tm), lambda j, i, k: (k, i))}[mode]
    b_spec = {"nn": pl.BlockSpec((tk, tn), lambda j, i, k: (k, j + b_off)),
              "nt": pl.BlockSpec((tn, tk), lambda j, i, k: (j, k)),
              "tn": pl.BlockSpec((tk, tn), lambda j, i, k: (k, j))}[mode]
    dims = {"nn": ((1,), (0,)), "nt": ((1,), (1,)), "tn": ((0,), (0,))}[mode]

    def body(a_ref, b_ref, o_ref, acc_ref):
        k = pl.program_id(2)
        part = _dot(a_ref[...], b_ref[...], dims)
        if nk == 1:
            o_ref[...] = part.astype(out_dtype)
        else:
            @pl.when(k == 0)
            def _():
                acc_ref[...] = part

            @pl.when(k > 0)
            def _():
                acc_ref[...] += part

            @pl.when(k == nk - 1)
            def _():
                o_ref[...] = acc_ref[...].astype(out_dtype)

    return pl.pallas_call(
        body, name=name, grid=(n // tn, m // tm, nk),
        in_specs=[a_spec, b_spec], out_specs=pl.BlockSpec((tm, tn), lambda j, i, k: (i, j)),
        out_shape=jax.ShapeDtypeStruct((m, n), out_dtype),
        scratch_shapes=[pltpu.VMEM((tm, tn) if nk > 1 else (8, 128), F32)],
        compiler_params=_cp(("parallel", "parallel", "arbitrary")),
    )(a, b)


def _prenorm(x, w, tm=512):
    s, d = x.shape

    def body(x_ref, w_ref, h_ref):
        xv = x_ref[...]
        r = lax.rsqrt(jnp.mean(xv * xv, axis=1, keepdims=True) + EPS)
        h_ref[...] = ((xv * r) * w_ref[...]).astype(BF16)

    return pl.pallas_call(
        body, name="prenorm", grid=(s // tm,),
        in_specs=[pl.BlockSpec((tm, d), lambda i: (i, 0)), pl.BlockSpec((1, d), lambda i: (0, 0))],
        out_specs=pl.BlockSpec((tm, d), lambda i: (i, 0)),
        out_shape=jax.ShapeDtypeStruct((s, d), BF16),
        compiler_params=_cp(("parallel",)),
    )(x, w)


def _tri(shape, strict_lower):
    row = lax.broadcasted_iota(jnp.int32, shape, 0)
    col = lax.broadcasted_iota(jnp.int32, shape, 1)
    return row, col, ((row > col) if strict_lower else (row < col)).astype(BF16)


def _sb_tile(q, k, row, col, masked):
    z = _nt(q, k) * SB_SCALE
    sp = jnp.maximum(z, 0.0) + jnp.log1p(jnp.exp(-jnp.abs(z)))
    lk = -sp
    if masked:
        lk = jnp.where(col < row, lk, 0.0)
    return z - sp, lk


def _sb_fwd(projb, tq=TQ):
    s = projb.shape[0]

    def body(q_ref, k_ref, v_ref, o_ref, acc_ref, car_ref):
        i = pl.program_id(1)
        q = q_ref[...]
        row, col, tri = _tri((tq, tq), True)
        acc_ref[...] = jnp.zeros_like(acc_ref)
        car_ref[...] = jnp.zeros_like(car_ref)

        def tile(j, masked):
            rows = pl.ds(pl.multiple_of(j * tq, tq), tq)
            ls, lk = _sb_tile(q, k_ref[rows, :], row, col, masked)
            a = jnp.exp(ls + _split_dot(lk, tri) + car_ref[...])
            if masked:
                a = jnp.where(col < row, a, 0.0)
            acc_ref[...] += _nn(a.astype(BF16), v_ref[rows, :])
            car_ref[...] += jnp.sum(lk, axis=1, keepdims=True)

        tile(i, True)

        def step(jj, c):
            tile(i - 1 - jj, False)
            return c

        lax.fori_loop(0, i, step, 0)
        o_ref[...] = acc_ref[...]

    return pl.pallas_call(
        body, name="sb_fwd", grid=(HEADS, s // tq),
        in_specs=[pl.BlockSpec((tq, HEAD_DIM), lambda h, i: (i, h)),
                  pl.BlockSpec((s, HEAD_DIM), lambda h, i: (0, HEADS + h)),
                  pl.BlockSpec((s, HEAD_DIM), lambda h, i: (0, 2 * HEADS + h))],
        out_specs=pl.BlockSpec((tq, HEAD_DIM), lambda h, i: (i, h)),
        out_shape=jax.ShapeDtypeStruct((s, SB_WIDTH), F32),
        scratch_shapes=[pltpu.VMEM((tq, HEAD_DIM), F32), pltpu.VMEM((tq, 1), F32)],
        compiler_params=_cp(("parallel", "arbitrary")),
    )(projb, projb, projb)


def _mla_prep(projf, qw, kvw, wq, wk, wv, rc, rs1, rs2, tm=512):
    s = projf.shape[0]

    def body(cq_ref, ckv_ref, kr_ref, qw_ref, kvw_ref, wq_ref, wk_ref, wv_ref, c_ref, s1_ref, s2_ref,
             nq_ref, nkv_ref, q_ref, k_ref, v_ref):
        c, s1, s2 = c_ref[...], s1_ref[...], s2_ref[...]
        cq = cq_ref[...]
        nq = ((cq * lax.rsqrt(jnp.mean(cq * cq, axis=1, keepdims=True) + EPS)) * qw_ref[...]).astype(BF16)
        nq_ref[...] = nq
        qf = _nn(nq, wq_ref[...])
        ckv = ckv_ref[...]
        nkv = ((ckv * lax.rsqrt(jnp.mean(ckv * ckv, axis=1, keepdims=True) + EPS)) * kvw_ref[...]).astype(BF16)
        nkv_ref[...] = nkv
        kn = _nn(nkv, wk_ref[...])
        v_ref[...] = _nn(nkv, wv_ref[...]).astype(BF16)
        krot = _rope(kr_ref[...], c, s1, s2).astype(BF16)
        for h in range(HEADS):
            lo = h * MLA_QK_PAD
            q_ref[:, lo:lo + 128] = qf[:, lo:lo + 128].astype(BF16)
            q_ref[:, lo + 128:lo + 256] = _rope(qf[:, lo + 128:lo + 256], c, s1, s2).astype(BF16)
            k_ref[:, lo:lo + 128] = kn[:, h * 128:(h + 1) * 128].astype(BF16)
            k_ref[:, lo + 128:lo + 256] = krot

    row = lambda w, b: pl.BlockSpec((tm, w), lambda i: (i, b))
    full = lambda a: pl.BlockSpec(a.shape, lambda i: (0, 0))
    return pl.pallas_call(
        body, name="mla_prep", grid=(s // tm,),
        in_specs=[row(Q_RANK, 2048 // Q_RANK), row(KV_RANK, 2560 // KV_RANK), row(128, 2816 // 128),
                  full(qw), full(kvw), full(wq), full(wk), full(wv), row(128, 0), row(128, 0), row(128, 0)],
        out_specs=[row(Q_RANK, 0), row(KV_RANK, 0), row(HEADS * MLA_QK_PAD, 0), row(HEADS * MLA_QK_PAD, 0),
                   row(SB_WIDTH, 0)],
        out_shape=[jax.ShapeDtypeStruct((s, Q_RANK), BF16), jax.ShapeDtypeStruct((s, KV_RANK), BF16),
                   jax.ShapeDtypeStruct((s, HEADS * MLA_QK_PAD), BF16),
                   jax.ShapeDtypeStruct((s, HEADS * MLA_QK_PAD), BF16),
                   jax.ShapeDtypeStruct((s, SB_WIDTH), BF16)],
        compiler_params=_cp(("parallel",)),
    )(projf, projf, projf, qw, kvw, wq, wk, wv, rc, rs1, rs2)


def _mla_mask(row, col):
    return (col // CHUNK) <= (row // CHUNK)


def _mla_fwd(qm, km, vm, tq=TQ):
    s = qm.shape[0]

    def body(q_ref, k_ref, v_ref, o_ref, lse_ref, acc_ref, m_ref, l_ref):
        i = pl.program_id(1)
        q = q_ref[...]
        row = lax.broadcasted_iota(jnp.int32, (tq, tq), 0)
        col = lax.broadcasted_iota(jnp.int32, (tq, tq), 1)
        acc_ref[...] = jnp.zeros_like(acc_ref)
        m_ref[...] = jnp.full_like(m_ref, NEG)
        l_ref[...] = jnp.zeros_like(l_ref)

        def tile(j, masked):
            rows = pl.ds(pl.multiple_of(j * tq, tq), tq)
            sc = _nt(q, k_ref[rows, :]) * MLA_SCALE
            if masked:
                sc = jnp.where(_mla_mask(row, col), sc, NEG)
            m_old = m_ref[...]
            m_new = jnp.maximum(m_old, jnp.max(sc, axis=1, keepdims=True))
            p = jnp.exp(sc - m_new)
            alpha = jnp.exp(m_old - m_new)
            l_ref[...] = alpha * l_ref[...] + jnp.sum(p, axis=1, keepdims=True)
            acc_ref[...] = alpha * acc_ref[...] + _nn(p.astype(BF16), v_ref[rows, :])
            m_ref[...] = m_new

        tile(i, True)

        def step(j, c):
            tile(j, False)
            return c

        lax.fori_loop(0, i, step, 0)
        o_ref[...] = acc_ref[...] / l_ref[...]
        lse_ref[...] = jnp.broadcast_to(m_ref[...] + jnp.log(l_ref[...]), (tq, HEAD_DIM))

    return pl.pallas_call(
        body, name="mla_fwd", grid=(HEADS, s // tq),
        in_specs=[pl.BlockSpec((tq, MLA_QK_PAD), lambda h, i: (i, h)),
                  pl.BlockSpec((s, MLA_QK_PAD), lambda h, i: (0, h)),
                  pl.BlockSpec((s, HEAD_DIM), lambda h, i: (0, h))],
        out_specs=[pl.BlockSpec((tq, HEAD_DIM), lambda h, i: (i, h)),
                   pl.BlockSpec((tq, HEAD_DIM), lambda h, i: (i, h))],
        out_shape=[jax.ShapeDtypeStruct((s, SB_WIDTH), F32), jax.ShapeDtypeStruct((s, SB_WIDTH), F32)],
        scratch_shapes=[pltpu.VMEM((tq, HEAD_DIM), F32), pltpu.VMEM((tq, 1), F32), pltpu.VMEM((tq, 1), F32)],
        compiler_params=_cp(("parallel", "arbitrary")),
    )(qm, km, vm)


def _out_post(oa, ob, projf, wout, x, target, pw, tm=256):
    s, d = x.shape

    def body(oa_ref, ob_ref, ga_ref, gb_ref, w_ref, x_ref, t_ref, pw_ref,
             mix_ref, dy_ref, dout_ref, loss_ref, dpw_ref):
        i = pl.program_id(0)
        sa, _ = _silu_parts(ga_ref[...])
        sb, _ = _silu_parts(gb_ref[...])
        mix_ref[:, :SB_WIDTH] = (oa_ref[...] * sa).astype(BF16)
        mix_ref[:, SB_WIDTH:] = (ob_ref[...] * sb).astype(BF16)
        y = _nn(mix_ref[...], w_ref[...])
        r = lax.rsqrt(jnp.mean(y * y, axis=1, keepdims=True) + EPS)
        yhat = y * r
        pwv = pw_ref[...]
        err = (x_ref[...] + yhat * pwv) - t_ref[...]
        dout = err * (1.0 / d)
        dout_ref[...] = dout
        g = dout * pwv
        dy_ref[...] = (r * (g - yhat * jnp.mean(g * yhat, axis=1, keepdims=True))).astype(BF16)

        @pl.when(i == 0)
        def _():
            loss_ref[...] = jnp.zeros_like(loss_ref)
            dpw_ref[...] = jnp.zeros_like(dpw_ref)

        loss_ref[...] += jnp.sum(err * err, axis=0, keepdims=True)
        dpw_ref[...] += jnp.sum(dout * yhat, axis=0, keepdims=True)

    row = lambda w, b: pl.BlockSpec((tm, w), lambda i: (i, b))
    vec = pl.BlockSpec((1, d), lambda i: (0, 0))
    return pl.pallas_call(
        body, name="out_post", grid=(s // tm,),
        in_specs=[row(SB_WIDTH, 0), row(SB_WIDTH, 0), row(SB_WIDTH, 0), row(SB_WIDTH, 1),
                  pl.BlockSpec(wout.shape, lambda i: (0, 0)), row(d, 0), row(d, 0), vec],
        out_specs=[row(d, 0), row(d, 0), row(d, 0), vec, vec],
        out_shape=[jax.ShapeDtypeStruct((s, d), BF16), jax.ShapeDtypeStruct((s, d), BF16),
                   jax.ShapeDtypeStruct((s, d), F32), jax.ShapeDtypeStruct((1, d), F32),
                   jax.ShapeDtypeStruct((1, d), F32)],
        compiler_params=_cp(("arbitrary",)),
    )(oa, ob, projf, projf, wout, x, target, pw)


def _sb_bwd(projb, projf, dmixed, oa, tq=TQ):
    s = projb.shape[0]
    nq = s // tq

    def body(q_ref, k_ref, v_ref, dm_ref, g_ref, o_ref, dq_ref, dk_ref, dv_ref, dg_ref,
             dka_ref, dva_ref, dqa_ref, bs_ref, cg_ref):
        i = pl.program_id(1)

        @pl.when(i == 0)
        def _():
            dka_ref[...] = jnp.zeros_like(dka_ref)
            dva_ref[...] = jnp.zeros_like(dva_ref)

        q = q_ref[...]
        silu, dsilu = _silu_parts(g_ref[...])
        dm = dm_ref[...]
        dg_ref[...] = (dm * o_ref[...] * dsilu).astype(BF16)
        do = (dm * silu).astype(BF16)
        row, col, tri = _tri((tq, tq), True)
        _, _, tri2 = _tri((tq, tq), False)

        def sums(j, masked):
            rows = pl.ds(pl.multiple_of(j * tq, tq), tq)
            _, lk = _sb_tile(q, k_ref[rows, :], row, col, masked)
            bs_ref[j] = jnp.sum(lk, axis=1, keepdims=True)

        def p1(j, c):
            sums(j, False)
            return c

        lax.fori_loop(0, i, p1, 0)
        sums(i, True)

        def suffix(jj, c):
            j = i - jj
            b = bs_ref[j]
            bs_ref[j] = c
            return c + b

        lax.fori_loop(0, i + 1, suffix, jnp.zeros((tq, 1), F32))

        dqa_ref[...] = jnp.zeros_like(dqa_ref)
        cg_ref[...] = jnp.zeros_like(cg_ref)

        def tile(j, masked):
            rows = pl.ds(pl.multiple_of(j * tq, tq), tq)
            k = k_ref[rows, :]
            ls, lk = _sb_tile(q, k, row, col, masked)
            a = jnp.exp(ls + _split_dot(lk, tri) + bs_ref[j])
            if masked:
                a = jnp.where(col < row, a, 0.0)
            g = a * _nt(do, v_ref[rows, :])
            dva_ref[rows, :] += _tn(a.astype(BF16), do)
            cum = _split_dot(g, tri2) + cg_ref[...]
            beta = jnp.exp(ls)
            dz = (g * (1.0 - beta) - beta * cum) * SB_SCALE
            if masked:
                dz = jnp.where(col < row, dz, 0.0)
            dz = dz.astype(BF16)
            dqa_ref[...] += _nn(dz, k)
            dka_ref[rows, :] += _tn(dz, q)
            cg_ref[...] += jnp.sum(g, axis=1, keepdims=True)

        def p2(j, c):
            tile(j, False)
            return c

        lax.fori_loop(0, i, p2, 0)
        tile(i, True)
        dq_ref[...] = dqa_ref[...].astype(BF16)

        @pl.when(i == nq - 1)
        def _():
            dk_ref[...] = dka_ref[...].astype(BF16)
            dv_ref[...] = dva_ref[...].astype(BF16)

    blk = lambda off: pl.BlockSpec((tq, HEAD_DIM), lambda h, i: (i, off + h))
    whole = lambda off: pl.BlockSpec((s, HEAD_DIM), lambda h, i: (0, off + h))
    o_sd = jax.ShapeDtypeStruct((s, SB_WIDTH), BF16)
    return pl.pallas_call(
        body, name="sb_bwd", grid=(HEADS, nq),
        in_specs=[blk(0), whole(HEADS), whole(2 * HEADS), blk(0), blk(0), blk(0)],
        out_specs=[blk(0), whole(0), whole(0), blk(0)],
        out_shape=[o_sd, o_sd, o_sd, o_sd],
        scratch_shapes=[pltpu.VMEM((s, HEAD_DIM), F32), pltpu.VMEM((s, HEAD_DIM), F32),
                        pltpu.VMEM((tq, HEAD_DIM), F32), pltpu.VMEM((nq, tq, 1), F32),
                        pltpu.VMEM((tq, 1), F32)],
        compiler_params=_cp(("parallel", "arbitrary")),
    )(projb, projb, projb, dmixed, projf, oa)


def _mla_bwd(qm, km, vm, projf, dmixed, ob, lse, tq=TQ):
    s = qm.shape[0]
    nq = s // tq

    def body(q_ref, k_ref, v_ref, dm_ref, g_ref, o_ref, lse_ref, dq_ref, dk_ref, dv_ref, dg_ref,
             dva_ref, dqa_ref):
        i = pl.program_id(1)

        @pl.when(i == 0)
        def _():
            dk_ref[...] = jnp.zeros_like(dk_ref)
            dva_ref[...] = jnp.zeros_like(dva_ref)

        q = q_ref[...]
        silu, dsilu = _silu_parts(g_ref[...])
        dm = dm_ref[...]
        o = o_ref[...]
        dg_ref[...] = (dm * o * dsilu).astype(BF16)
        dof = dm * silu
        delta = jnp.sum(dof * o, axis=1, keepdims=True)
        do = dof.astype(BF16)
        lse_col = lse_ref[:, 0:1]
        row = lax.broadcasted_iota(jnp.int32, (tq, tq), 0)
        col = lax.broadcasted_iota(jnp.int32, (tq, tq), 1)
        dqa_ref[...] = jnp.zeros_like(dqa_ref)

        def tile(j, masked):
            rows = pl.ds(pl.multiple_of(j * tq, tq), tq)
            k = k_ref[rows, :]
            p = jnp.exp(_nt(q, k) * MLA_SCALE - lse_col)
            if masked:
                p = jnp.where(_mla_mask(row, col), p, 0.0)
            ds = (p * (_nt(do, v_ref[rows, :]) - delta) * MLA_SCALE).astype(BF16)
            dva_ref[rows, :] += _tn(p.astype(BF16), do)
            dk_ref[rows, :] += _tn(ds, q)
            dqa_ref[...] += _nn(ds, k)

        def step(j, c):
            tile(j, False)
            return c

        lax.fori_loop(0, i, step, 0)
        tile(i, True)
        dq_ref[...] = dqa_ref[...]

        @pl.when(i == nq - 1)
        def _():
            dv_ref[...] = dva_ref[...].astype(BF16)

    blk = lambda w, off: pl.BlockSpec((tq, w), lambda h, i: (i, off + h))
    whole = lambda w: pl.BlockSpec((s, w), lambda h, i: (0, h))
    return pl.pallas_call(
        body, name="mla_bwd", grid=(HEADS, nq),
        in_specs=[blk(MLA_QK_PAD, 0), whole(MLA_QK_PAD), whole(HEAD_DIM), blk(HEAD_DIM, HEADS),
                  blk(HEAD_DIM, HEADS), blk(HEAD_DIM, 0), blk(HEAD_DIM, 0)],
        out_specs=[blk(MLA_QK_PAD, 0), whole(MLA_QK_PAD), whole(HEAD_DIM), blk(HEAD_DIM, 0)],
        out_shape=[jax.ShapeDtypeStruct((s, HEADS * MLA_QK_PAD), F32),
                   jax.ShapeDtypeStruct((s, HEADS * MLA_QK_PAD), F32),
                   jax.ShapeDtypeStruct((s, SB_WIDTH), BF16), jax.ShapeDtypeStruct((s, SB_WIDTH), BF16)],
        scratch_shapes=[pltpu.VMEM((s, HEAD_DIM), F32), pltpu.VMEM((tq, MLA_QK_PAD), F32)],
        compiler_params=_cp(("parallel", "arbitrary")),
    )(qm, km, vm, dmixed, projf, ob, lse)


def _norm_bwd(x, w, dn):
    r = lax.rsqrt(jnp.mean(x * x, axis=1, keepdims=True) + EPS)
    xhat = x * r
    g = dn * w
    return r * (g - xhat * jnp.mean(g * xhat, axis=1, keepdims=True)), dn * xhat


def _mla_bwd_post(dqm, dkm, dvm, projf, qw, kvw, wq, wk, wv, rc, rs1, rs2, tm=256):
    s = dqm.shape[0]

    def body(dq_ref, dk_ref, dv_ref, cq_ref, ckv_ref, qw_ref, kvw_ref, wq_ref, wk_ref, wv_ref,
             c_ref, s1_ref, s2_ref, dqp_ref, dkn_ref, dcq_ref, dckv_ref, dkr_ref, dqw_ref, dkvw_ref):
        i = pl.program_id(0)
        c, s1, s2 = c_ref[...], s1_ref[...], s2_ref[...]
        drot = jnp.zeros((tm, 128), F32)
        for h in range(HEADS):
            lo = h * MLA_QK_PAD
            dqp_ref[:, lo:lo + 128] = dq_ref[:, lo:lo + 128].astype(BF16)
            dqp_ref[:, lo + 128:lo + 256] = _rope_t(dq_ref[:, lo + 128:lo + 256], c, s1, s2).astype(BF16)
            dkn_ref[:, h * 128:(h + 1) * 128] = dk_ref[:, lo:lo + 128].astype(BF16)
            drot = drot + dk_ref[:, lo + 128:lo + 256]
        dkr_ref[...] = _rope_t(drot, c, s1, s2).astype(BF16)
        dcq, dqw = _norm_bwd(cq_ref[...], qw_ref[...], _nt(dqp_ref[...], wq_ref[...]))
        dcq_ref[...] = dcq.astype(BF16)
        dnkv = _nt(dkn_ref[...], wk_ref[...]) + _nt(dv_ref[...], wv_ref[...])
        dckv, dkvw = _norm_bwd(ckv_ref[...], kvw_ref[...], dnkv)
        dckv_ref[...] = dckv.astype(BF16)

        @pl.when(i == 0)
        def _():
            dqw_ref[...] = jnp.zeros_like(dqw_ref)
            dkvw_ref[...] = jnp.zeros_like(dkvw_ref)

        dqw_ref[...] += jnp.sum(dqw, axis=0, keepdims=True)
        dkvw_ref[...] += jnp.sum(dkvw, axis=0, keepdims=True)

    row = lambda w, b: pl.BlockSpec((tm, w), lambda i: (i, b))
    full = lambda a: pl.BlockSpec(a.shape, lambda i: (0, 0))
    sd = jax.ShapeDtypeStruct
    return pl.pallas_call(
        body, name="mla_bwd_post", grid=(s // tm,),
        in_specs=[row(HEADS * MLA_QK_PAD, 0), row(HEADS * MLA_QK_PAD, 0), row(SB_WIDTH, 0),
                  row(Q_RANK, 2048 // Q_RANK), row(KV_RANK, 2560 // KV_RANK),
                  full(qw), full(kvw), full(wq), full(wk), full(wv), row(128, 0), row(128, 0), row(128, 0)],
        out_specs=[row(HEADS * MLA_QK_PAD, 0), row(SB_WIDTH, 0), row(Q_RANK, 0), row(KV_RANK, 0), row(128, 0),
                   pl.BlockSpec((1, Q_RANK), lambda i: (0, 0)), pl.BlockSpec((1, KV_RANK), lambda i: (0, 0))],
        out_shape=[sd((s, HEADS * MLA_QK_PAD), BF16), sd((s, SB_WIDTH), BF16), sd((s, Q_RANK), BF16),
                   sd((s, KV_RANK), BF16), sd((s, 128), BF16), sd((1, Q_RANK), F32), sd((1, KV_RANK), F32)],
        compiler_params=_cp(("arbitrary",)),
    )(dqm, dkm, dvm, projf, projf, qw, kvw, wq, wk, wv, rc, rs1, rs2)


def _pre_bwd(dproj, win, x, pw, dout, tm=512, tk=1024):
    s, d = x.shape
    nk = dproj.shape[1] // tk

    def body(dp_ref, w_ref, x_ref, pw_ref, do_ref, dx_ref, dpw_ref, acc_ref):
        i, k = pl.program_id(0), pl.program_id(1)
        part = _nt(dp_ref[...], w_ref[...])

        @pl.when(k == 0)
        def _():
            acc_ref[...] = part

        @pl.when(k > 0)
        def _():
            acc_ref[...] += part

        @pl.when(k == nk - 1)
        def _():
            dx, dw = _norm_bwd(x_ref[...], pw_ref[...], acc_ref[...])
            dx_ref[...] = do_ref[...] + dx

            @pl.when(i == 0)
            def _():
                dpw_ref[...] = jnp.zeros_like(dpw_ref)

            dpw_ref[...] += jnp.sum(dw, axis=0, keepdims=True)

    rowd = pl.BlockSpec((tm, d), lambda i, k: (i, 0))
    vec = pl.BlockSpec((1, d), lambda i, k: (0, 0))
    return pl.pallas_call(
        body, name="pre_bwd", grid=(s // tm, nk),
        in_specs=[pl.BlockSpec((tm, tk), lambda i, k: (i, k)), pl.BlockSpec((d, tk), lambda i, k: (0, k)),
                  rowd, vec, rowd],
        out_specs=[rowd, vec],
        out_shape=[jax.ShapeDtypeStruct((s, d), F32), jax.ShapeDtypeStruct((1, d), F32)],
        scratch_shapes=[pltpu.VMEM((tm, d), F32)],
        compiler_params=_cp(("arbitrary", "arbitrary")),
    )(dproj, win, x, pw, dout)


def _adamw(w, g, m, v):
    m = ADAM_B1 * m + (1.0 - ADAM_B1) * g
    v = ADAM_B2 * v + (1.0 - ADAM_B2) * (g * g)
    delta = -ADAM_LR * ((m / ADAM_C1) / (jnp.sqrt(v / ADAM_C2) + ADAM_EPS) + ADAM_WD * w)
    return delta, m, v


def _sum_adamw(parts, w, m, v, tr=1728):
    n, r, lanes = parts.shape

    def body(p_ref, w_ref, m_ref, v_ref, g_ref, d_ref, nm_ref, nv_ref):
        g = p_ref[0]
        for c in range(1, n):
            g = g + p_ref[c]
        g_ref[...] = g
        d_ref[...], nm_ref[...], nv_ref[...] = _adamw(w_ref[...], g, m_ref[...], v_ref[...])

    blk = pl.BlockSpec((tr, lanes), lambda i: (i, 0))
    sd = jax.ShapeDtypeStruct((r, lanes), F32)
    return pl.pallas_call(
        body, name="sum_adamw", grid=(r // tr,),
        in_specs=[pl.BlockSpec((n, tr, lanes), lambda i: (0, i, 0)), blk, blk, blk],
        out_specs=[blk, blk, blk, blk], out_shape=[sd, sd, sd, sd],
        compiler_params=_cp(("parallel",)),
    )(parts, w, m, v)


def _pair_sum(mine, got, tr=1728):
    n, r, lanes = mine.shape

    def body(a_ref, b_ref, o_ref):
        o_ref[...] = a_ref[...] + b_ref[...]

    blk = pl.BlockSpec((n, tr, lanes), lambda i: (0, i, 0))
    return pl.pallas_call(
        body, name="pair_sum", grid=(r // tr,), in_specs=[blk, blk], out_specs=blk,
        out_shape=jax.ShapeDtypeStruct(mine.shape, F32), compiler_params=_cp(("parallel",)),
    )(mine, got)


def _place():
    return lax.axis_index("x"), lax.axis_index("y"), lax.axis_index("c")


ANY = pl.BlockSpec(memory_space=pl.ANY)


def _gather_weights(packed):
    r, lanes = packed.shape
    half = r // 2

    def body(w_ref, out_ref, send_sems, recv_sems, local_sem):
        x, y, c = _place()
        chips = [(1 - x, y), (x, 1 - y), (1 - x, 1 - y)]

        def rows(chip, core):
            return out_ref.at[2 * chip[0] + chip[1], pl.ds(core * half, half), :]

        def copy(k, chip, core, to, src=None):
            return pltpu.make_async_remote_copy(
                src_ref=rows(chip, core) if src is None else src, dst_ref=rows(chip, core),
                send_sem=send_sems.at[k], recv_sem=recv_sems.at[k], device_id=to, device_id_type=MESH)

        mine = pltpu.make_async_copy(w_ref, out_ref.at[2 * x + y], local_sem)
        mine.start()
        first = [copy(j, (x, y), c, (*chip, c), src=w_ref.at[pl.ds(c * half, half), :])
                 for j, chip in enumerate(chips)]
        for cp in first:
            cp.start()
        passed = [copy(3 + j, chip, c, (x, y, 1 - c)) for j, chip in enumerate(chips)]
        for j, chip in enumerate(chips):
            copy(j, chip, c, (x, y, c)).wait_recv()
            passed[j].start()
        for j, chip in enumerate(chips):
            copy(3 + j, chip, 1 - c, (x, y, c)).wait_recv()
        for cp in first + passed:
            cp.wait_send()
        mine.wait()

    return pl.pallas_call(
        body, name="gather_weights", in_specs=[ANY], out_specs=ANY,
        out_shape=jax.ShapeDtypeStruct((4, r, lanes), packed.dtype),
        scratch_shapes=[pltpu.SemaphoreType.DMA((6,)), pltpu.SemaphoreType.DMA((6,)), pltpu.SemaphoreType.DMA],
        compiler_params=pltpu.CompilerParams(has_side_effects=True),
    )(packed)


def _swap_halves(grads):
    n, _, r, lanes = grads.shape

    def body(g_ref, got_ref, send_sems, recv_sems):
        x, y, c = _place()
        copies = [pltpu.make_async_remote_copy(
            src_ref=g_ref.at[k, 1 - c], dst_ref=got_ref.at[k], send_sem=send_sems.at[k], recv_sem=recv_sems.at[k],
            device_id=(x, y, 1 - c), device_id_type=MESH) for k in range(n)]
        for cp in copies:
            cp.start()
        for cp in copies:
            cp.wait()

    return pl.pallas_call(
        body, name="swap_halves", in_specs=[ANY], out_specs=ANY,
        out_shape=jax.ShapeDtypeStruct((n, r, lanes), grads.dtype),
        scratch_shapes=[pltpu.SemaphoreType.DMA((n,)), pltpu.SemaphoreType.DMA((n,))],
        compiler_params=pltpu.CompilerParams(has_side_effects=True),
    )(grads)


def _exchange_chips(psum):
    n, r, lanes = psum.shape

    def body(p_ref, got_ref, send_sems, recv_sems, local_sem):
        x, y, c = _place()
        me = 2 * x + y
        chips = [(1 - x, y), (x, 1 - y), (1 - x, 1 - y)]
        mine = pltpu.make_async_copy(p_ref.at[me], got_ref.at[me], local_sem)
        mine.start()
        copies = [pltpu.make_async_remote_copy(
            src_ref=p_ref.at[2 * chip[0] + chip[1]], dst_ref=got_ref.at[me],
            send_sem=send_sems.at[j], recv_sem=recv_sems.at[j], device_id=(*chip, c), device_id_type=MESH)
            for j, chip in enumerate(chips)]
        for cp in copies:
            cp.start()
        for j, chip in enumerate(chips):
            pltpu.make_async_remote_copy(
                src_ref=p_ref.at[me], dst_ref=got_ref.at[2 * chip[0] + chip[1]], send_sem=send_sems.at[j],
                recv_sem=recv_sems.at[j], device_id=(*chip, c), device_id_type=MESH).wait_recv()
        for cp in copies:
            cp.wait_send()
        mine.wait()

    return pl.pallas_call(
        body, name="exchange_chips", in_specs=[ANY], out_specs=ANY,
        out_shape=jax.ShapeDtypeStruct((n, r, lanes), psum.dtype),
        scratch_shapes=[pltpu.SemaphoreType.DMA((3,)), pltpu.SemaphoreType.DMA((3,)), pltpu.SemaphoreType.DMA],
        compiler_params=pltpu.CompilerParams(has_side_effects=True),
    )(psum)


def _share_with_sibling(arrs):
    na = len(arrs)
    r, lanes = arrs[0].shape

    def body(*refs):
        ins, outs = refs[:na], refs[na:2 * na]
        send_sems, recv_sems, local_sems = refs[2 * na:]
        x, y, c = _place()
        local = [pltpu.make_async_copy(ins[k], outs[k].at[c], local_sems.at[k]) for k in range(na)]
        remote = [pltpu.make_async_remote_copy(
            src_ref=ins[k], dst_ref=outs[k].at[c], send_sem=send_sems.at[k], recv_sem=recv_sems.at[k],
            device_id=(x, y, 1 - c), device_id_type=MESH) for k in range(na)]
        for cp in local + remote:
            cp.start()
        for cp in remote + local:
            cp.wait()

    return pl.pallas_call(
        body, name="share_with_sibling", in_specs=[ANY] * na, out_specs=[ANY] * na,
        out_shape=[jax.ShapeDtypeStruct((2, r, lanes), F32)] * na,
        scratch_shapes=[pltpu.SemaphoreType.DMA((na,)), pltpu.SemaphoreType.DMA((na,)),
                        pltpu.SemaphoreType.DMA((na,))],
        compiler_params=pltpu.CompilerParams(has_side_effects=True),
    )(*arrs)


def _norm_allreduce_adamw(part, w, m, v):
    r, lanes = part.shape

    def body(p_ref, w_ref, m_ref, v_ref, g_ref, d_ref, nm_ref, nv_ref, all_ref, send_sems, recv_sems):
        x, y, c = _place()
        me = 4 * x + 2 * y + c
        all_ref[me] = p_ref[...]
        copies = []
        for k in range(1, 8):
            peer = (x ^ (k >> 2), y ^ ((k >> 1) & 1), c ^ (k & 1))
            copies.append(pltpu.make_async_remote_copy(
                src_ref=p_ref, dst_ref=all_ref.at[me], send_sem=send_sems.at[k - 1], recv_sem=recv_sems.at[k - 1],
                device_id=peer, device_id_type=MESH))
        for cp in copies:
            cp.start()
        for cp in copies:
            cp.wait()
        g = all_ref[0]
        for k in range(1, 8):
            g = g + all_ref[k]
        g_ref[...] = g
        d_ref[...], nm_ref[...], nv_ref[...] = _adamw(w_ref[...], g, m_ref[...], v_ref[...])

    vm = pl.BlockSpec(memory_space=pltpu.VMEM)
    sd = jax.ShapeDtypeStruct((r, lanes), F32)
    return pl.pallas_call(
        body, name="norm_allreduce_adamw", in_specs=[vm] * 4, out_specs=[vm] * 4, out_shape=[sd] * 4,
        scratch_shapes=[pltpu.VMEM((8, r, lanes), F32), pltpu.SemaphoreType.DMA((7,)),
                        pltpu.SemaphoreType.DMA((7,))],
        compiler_params=pltpu.CompilerParams(has_side_effects=True),
    )(part, w, m, v)


W_SHAPES = {"w_in": (D_MODEL, D_IN // 4), "w_q_up": (Q_RANK, 1536 // 4), "w_kv_up": (KV_RANK, 2048 // 4),
            "w_out": (D_MODEL // 4, D_MODEL)}
W_NAMES = ("w_in", "w_q_up", "w_kv_up", "w_out")
LANES = 128


def _pack_shard(ws):
    return jnp.concatenate([w.reshape(-1, LANES) for w in ws], axis=0)


def _unpack_shard(packed):
    out, at = [], 0
    for name in W_NAMES:
        rows = W_SHAPES[name][0] * W_SHAPES[name][1] // LANES
        out.append(packed[..., at:at + rows, :].reshape(packed.shape[:-2] + W_SHAPES[name]))
        at += rows
    return out


def _pack_halves(ws, c):
    parts = []
    for w in ws:
        hr = w.shape[0] // 2
        parts.append(lax.dynamic_slice_in_dim(w, c * hr, hr, axis=0).reshape(-1, LANES))
    return jnp.concatenate(parts, axis=0)


def _unpack_halves(both):
    out, at = [], 0
    for name in W_NAMES:
        rws, cols = W_SHAPES[name]
        n = rws * cols // 2 // LANES
        out.append(both[:, at:at + n, :].reshape(rws, cols))
        at += n
    return out


def _perm_in(w):
    return jnp.concatenate([w[:, :4096], w[:, 4928:5952], w[:, 4096:4928],
                            jnp.zeros((w.shape[0], D_INP - D_IN), w.dtype)], axis=1)


def _unperm_in(w):
    return jnp.concatenate([w[:, :4096], w[:, 5120:5952], w[:, 4096:5120]], axis=1)


def _grad_chunks(dwin, dwq, dwkv, dwout):
    def per(g, axis):
        sh = jnp.stack(jnp.split(g, 4, axis=axis))
        return sh.reshape(4, 2, -1, LANES)
    return jnp.concatenate([per(dwin, 1), per(dwq, 1), per(dwkv, 1), per(dwout, 0)], axis=2)


NORM_NAMES = ("pre_norm_w", "q_norm_w", "kv_norm_w", "post_norm_w")
NORM_SIZES = (D_MODEL, Q_RANK, KV_RANK, D_MODEL)
NORM_ROWS = 40


def _pack_norm(vs):
    flat = jnp.concatenate([v.reshape(-1) for v in vs])
    return jnp.pad(flat, (0, NORM_ROWS * LANES - flat.shape[0])).reshape(NORM_ROWS, LANES)


def _unpack_norm(p):
    flat, out, at = p.reshape(-1), [], 0
    for n in NORM_SIZES:
        out.append(flat[at:at + n].reshape(1, n))
        at += n
    return out


def _rope_tables(positions):
    inv_freq = ROPE_THETA ** (-jnp.arange(0, MLA_ROPE, 2, dtype=F32) / MLA_ROPE)
    ang = positions.astype(F32)[:, None] * inv_freq
    cos, sin, z = jnp.cos(ang), jnp.sin(ang), jnp.zeros_like(ang)
    return (jnp.concatenate([cos, cos, z, z], axis=1), jnp.concatenate([z, sin, z, z], axis=1),
            jnp.concatenate([-sin, z, z, z], axis=1))


def _local_step(x, positions, pre_w, win, q_w, wq, kv_w, wk, wv, wout, post_w, target):
    s = x.shape[0]
    rc, rs1, rs2 = _rope_tables(positions)
    h = _prenorm(x, pre_w)
    projb = _matmul(h, win, mode="nn", out_dtype=BF16, tm=512, tn=1024, tk=D_MODEL, name="proj_b", n=PB_W)
    projf = _matmul(h, win, mode="nn", out_dtype=F32, tm=512, tn=1024, tk=D_MODEL, name="proj_f", n=PF_W,
                    b_off=PB_W // 1024)
    oa = _sb_fwd(projb)
    nq, nkv, qm, km, vm = _mla_prep(projf, q_w, kv_w, wq, wk, wv, rc, rs1, rs2)
    ob, lse = _mla_fwd(qm, km, vm)
    mixed, dy, dout, err2, dpost = _out_post(oa, ob, projf, wout, x, target, post_w)

    dwout = _matmul(mixed, dy, mode="tn", out_dtype=F32, tm=1024, tn=1024, tk=min(1024, s), name="dw_out")
    dmixed = _matmul(dy, wout, mode="nt", out_dtype=F32, tm=512, tn=1024, tk=D_MODEL, name="d_mixed")
    dqa, dka, dva, dga = _sb_bwd(projb, projf, dmixed, oa)
    dqm, dkm, dvm, dgb = _mla_bwd(qm, km, vm, projf, dmixed, ob, lse)
    dqp, dkn, dcq, dckv, dkr, dqw, dkvw = _mla_bwd_post(dqm, dkm, dvm, projf, q_w, kv_w, wq, wk, wv, rc, rs1, rs2)
    tks = min(1024, s)
    dwq = _matmul(nq, dqp, mode="tn", out_dtype=F32, tm=Q_RANK, tn=1024, tk=tks, name="dw_q")
    dwk = _matmul(nkv, dkn, mode="tn", out_dtype=F32, tm=KV_RANK, tn=1024, tk=tks, name="dw_k")
    dwv = _matmul(nkv, dvm, mode="tn", out_dtype=F32, tm=KV_RANK, tn=1024, tk=tks, name="dw_v")
    dproj = jnp.concatenate([dqa, dka, dva, dga, dgb, dcq, dckv, dkr, jnp.zeros((s, 128), BF16)], axis=1)
    dwin = _matmul(h, dproj, mode="tn", out_dtype=F32, tm=1024, tn=1024, tk=tks, name="dw_in")
    gx, dpre = _pre_bwd(dproj, win, x, pre_w, dout)
    return err2, gx, dpre, dwin, dqw, dwq, dkvw, dwk, dwv, dwout, dpost


def _kernel_layouts(full):
    w_in, w_q_up, w_kv_up, w_out = full
    win = _perm_in(w_in)
    wq = jnp.pad(w_q_up.reshape(Q_RANK, HEADS, 192), ((0, 0), (0, 0), (0, 64))).reshape(Q_RANK, HEADS * MLA_QK_PAD)
    kv = w_kv_up.reshape(KV_RANK, HEADS, 256)
    wk = kv[:, :, :128].reshape(KV_RANK, SB_WIDTH)
    wv = kv[:, :, 128:].reshape(KV_RANK, SB_WIDTH)
    return win, wq, wk, wv, w_out


def _original_layouts(dwin, dwq, dwk, dwv):
    dwi = _unperm_in(dwin[:, :D_IN])
    dq = dwq.reshape(Q_RANK, HEADS, MLA_QK_PAD)[:, :, :192].reshape(Q_RANK, HEADS * 192)
    dkv = jnp.concatenate([dwk.reshape(KV_RANK, HEADS, 128), dwv.reshape(KV_RANK, HEADS, 128)], axis=2)
    return dwi, dq, dkv.reshape(KV_RANK, 2 * SB_WIDTH)


def kernel(x, positions, pre_norm_w, w_in, q_norm_w, w_q_up, kv_norm_w, w_kv_up, w_out, post_norm_w, loss_target, m_pre_norm_w, m_w_in, m_q_norm_w, m_w_q_up, m_kv_norm_w, m_w_kv_up, m_w_out, m_post_norm_w, v_pre_norm_w, v_w_in, v_q_norm_w, v_w_q_up, v_kv_norm_w, v_w_kv_up, v_w_out, v_post_norm_w):
    c = lax.axis_index("c")
    shards = (w_in[0], w_q_up[0], w_kv_up[0], w_out[0])
    gathered = _gather_weights(_pack_shard([w.astype(BF16) for w in shards]))
    g_in, g_q, g_kv, g_out = _unpack_shard(gathered)
    cat = lambda g: jnp.concatenate([g[0], g[1], g[2], g[3]], axis=1)
    full = (cat(g_in), cat(g_q), cat(g_kv), g_out.reshape(D_MODEL, D_MODEL))
    win, wq, wk, wv, wout = _kernel_layouts(full)

    err2, gx, dpre, dwin, dqw, dwq, dkvw, dwk, dwv, dwout, dpost = _local_step(
        x[0], positions[0], pre_norm_w, win, q_norm_w, wq, kv_norm_w, wk, wv, wout, post_norm_w, loss_target[0])
    loss = lax.psum(0.5 * jnp.sum(err2) / D_MODEL, ("x", "y", "c"))

    dwi, dq, dkv = _original_layouts(dwin, dwq, dwk, dwv)
    chunks = _grad_chunks(dwi, dq, dkv, dwout)
    mine = lax.dynamic_index_in_dim(chunks, c, axis=1, keepdims=False)
    pair = _pair_sum(mine, _swap_halves(chunks))
    parts = _exchange_chips(pair)
    wh, mh, vh = (_pack_halves(ws, c) for ws in (
        shards, (m_w_in[0], m_w_q_up[0], m_w_kv_up[0], m_w_out[0]), (v_w_in[0], v_w_q_up[0], v_w_kv_up[0], v_w_out[0])))
    both = _share_with_sibling(list(_sum_adamw(parts, wh, mh, vh)))
    big = [_unpack_halves(b) for b in both]

    small = _norm_allreduce_adamw(
        _pack_norm([dpre, dqw, dkvw, dpost]), _pack_norm([pre_norm_w, q_norm_w, kv_norm_w, post_norm_w]),
        _pack_norm([m_pre_norm_w, m_q_norm_w, m_kv_norm_w, m_post_norm_w]),
        _pack_norm([v_pre_norm_w, v_q_norm_w, v_kv_norm_w, v_post_norm_w]))
    small = [_unpack_norm(p) for p in small]

    def group(k):
        n, b = small[k], big[k]
        return (n[0], b[0][None], n[1], b[1][None], n[2], b[2][None], b[3][None], n[3])

    return (loss, gx[None], *group(0), *group(1), *group(2), *group(3))
```

```python
import functools
import math

import numpy as np
import jax
import jax.numpy as jnp
from jax import lax
from jax.experimental import pallas as pl
from jax.experimental.pallas import tpu as pltpu

F32 = jnp.float32
BF16 = jnp.bfloat16
MESH = pl.DeviceIdType.MESH

D_MODEL = 2048
HEADS = 8
HEAD_DIM = 128
SB_WIDTH = HEADS * HEAD_DIM
MLA_ROPE = 64
MLA_QK_PAD = 256
Q_RANK = 512
KV_RANK = 256
CHUNK = 64
EPS = 1e-6
ROPE_THETA = 10000.0
D_IN = 5952
D_INP = 6144
PB_W = 3072
PF_W = D_INP - PB_W
SB_SCALE = 1.0 / math.sqrt(HEAD_DIM)
MLA_SCALE = 1.0 / math.sqrt(HEAD_DIM + MLA_ROPE)
NEG = -1e30

ADAM_LR, ADAM_B1, ADAM_B2, ADAM_EPS, ADAM_WD, ADAM_STEP = 0.001, 0.9, 0.999, 1e-08, 0.01, 10
ADAM_C1 = 1.0 - ADAM_B1 ** ADAM_STEP
ADAM_C2 = 1.0 - ADAM_B2 ** ADAM_STEP

VMEM_LIMIT = 56 * 1024 * 1024
TQ = 1024
TK = 256


def _cp(sem=None, **kw):
    return pltpu.CompilerParams(dimension_semantics=sem, vmem_limit_bytes=VMEM_LIMIT, **kw)


def _dot(a, b, dims):
    return lax.dot_general(a, b, (dims, ((), ())), preferred_element_type=F32)


def _nn(a, b):
    return _dot(a, b, ((1,), (0,)))


def _nt(a, b):
    return _dot(a, b, ((1,), (1,)))


def _tn(a, b):
    return _dot(a, b, ((0,), (0,)))


def _split_dot(x, tri):
    hi = x.astype(BF16)
    lo = (x - hi.astype(F32)).astype(BF16)
    return _nn(hi, tri) + _nn(lo, tri)


def _rope(x, c, s1, s2):
    return x * c + pltpu.roll(x, 32, 1) * s1 + pltpu.roll(x, 96, 1) * s2


def _rope_t(x, c, s1, s2):
    return x * c - pltpu.roll(x, 32, 1) * s1 - pltpu.roll(x, 96, 1) * s2


def _silu_parts(g):
    sg = jax.nn.sigmoid(g)
    return g * sg, sg * (1.0 + g * (1.0 - sg))


def _matmul(a, b, *, mode, out_dtype, tm, tn, tk, name, n=None, b_off=0):
    if mode == "tn":
        kk, m = a.shape
        n = b.shape[1] if n is None else n
    else:
        m, kk = a.shape
        n = (b.shape[1] if mode == "nn" else b.shape[0]) if n is None else n
    nk = kk // tk
    a_spec = {"nn": pl.BlockSpec((tm, tk), lambda j, i, k: (i, k)),
              "nt": pl.BlockSpec((tm, tk), lambda j, i, k: (i, k)),
              "tn": pl.BlockSpec((tk, tm), lambda j, i, k: (k, i))}[mode]
    b_spec = {"nn": pl.BlockSpec((tk, tn), lambda j, i, k: (k, j + b_off)),
              "nt": pl.BlockSpec((tn, tk), lambda j, i, k: (j, k)),
              "tn": pl.BlockSpec((tk, tn), lambda j, i, k: (k, j))}[mode]
    dims = {"nn": ((1,), (0,)), "nt": ((1,), (1,)), "tn": ((0,), (0,))}[mode]

    def body(a_ref, b_ref, o_ref, acc_ref):
        k = pl.program_id(2)
        part = _dot(a_ref[...], b_ref[...], dims)
        if nk == 1:
            o_ref[...] = part.astype(out_dtype)
        else:
            @pl.when(k == 0)
            def _():
                acc_ref[...] = part

            @pl.when(k > 0)
            def _():
                acc_ref[...] += part

            @pl.when(k == nk - 1)
            def _():
                o_ref[...] = acc_ref[...].astype(out_dtype)

    return pl.pallas_call(
        body, name=name, grid=(n // tn, m // tm, nk),
        in_specs=[a_spec, b_spec], out_specs=pl.BlockSpec((tm, tn), lambda j, i, k: (i, j)),
        out_shape=jax.ShapeDtypeStruct((m, n), out_dtype),
        scratch_shapes=[pltpu.VMEM((tm, tn) if nk > 1 else (8, 128), F32)],
        compiler_params=_cp(("parallel", "parallel", "arbitrary")),
    )(a, b)


def _prenorm(x, w, tm=512):
    s, d = x.shape

    def body(x_ref, w_ref, h_ref):
        xv = x_ref[...]
        r = lax.rsqrt(jnp.mean(xv * xv, axis=1, keepdims=True) + EPS)
        h_ref[...] = ((xv * r) * w_ref[...]).astype(BF16)

    return pl.pallas_call(
        body, name="prenorm", grid=(s // tm,),
        in_specs=[pl.BlockSpec((tm, d), lambda i: (i, 0)), pl.BlockSpec((1, d), lambda i: (0, 0))],
        out_specs=pl.BlockSpec((tm, d), lambda i: (i, 0)),
        out_shape=jax.ShapeDtypeStruct((s, d), BF16),
        compiler_params=_cp(("parallel",)),
    )(x, w)


def _tri(n, strict_lower):
    row = lax.broadcasted_iota(jnp.int32, (n, n), 0)
    col = lax.broadcasted_iota(jnp.int32, (n, n), 1)
    return ((row > col) if strict_lower else (row < col)).astype(BF16)


def _sb_tile(q, k, mask):
    z = _nt(q, k) * SB_SCALE
    sp = jnp.maximum(z, 0.0) + jnp.log1p(jnp.exp(-jnp.abs(z)))
    lk = -sp
    if mask is not None:
        lk = jnp.where(mask, lk, 0.0)
    return z - sp, lk


def _tiles(s):
    return min(TQ, s), TK


def _band_masks(tq, tk, fn):
    row = lax.broadcasted_iota(jnp.int32, (tq, tk), 0)
    col = lax.broadcasted_iota(jnp.int32, (tq, tk), 1)
    return [fn(row, col + b * tk) for b in range(tq // tk)]


def _sb_fwd(projb):
    s = projb.shape[0]
    tq, tk = _tiles(s)
    nb = tq // tk

    def body(q_ref, k_ref, v_ref, o_ref, tot_ref, acc_ref, car_ref):
        i = pl.program_id(1)
        q = q_ref[...]
        tri = _tri(tk, True)
        masks = _band_masks(tq, tk, lambda r, c: c < r)
        acc_ref[...] = jnp.zeros_like(acc_ref)
        car_ref[...] = jnp.zeros_like(car_ref)

        def tile(j, mask):
            rows = pl.ds(pl.multiple_of(j * tk, tk), tk)
            ls, lk = _sb_tile(q, k_ref[rows, :], mask)
            a = jnp.exp(ls + _split_dot(lk, tri) + car_ref[...])
            if mask is not None:
                a = jnp.where(mask, a, 0.0)
            acc_ref[...] += _nn(a.astype(BF16), v_ref[rows, :])
            car_ref[...] += jnp.sum(lk, axis=1, keepdims=True)

        for b in reversed(range(nb)):
            tile(i * nb + b, masks[b])

        def step(jj, c):
            tile(i * nb - 1 - jj, None)
            return c

        lax.fori_loop(0, i * nb, step, 0)
        o_ref[...] = acc_ref[...]
        tot_ref[...] = jnp.broadcast_to(car_ref[...], (tq, HEAD_DIM))

    blk = pl.BlockSpec((tq, HEAD_DIM), lambda h, i: (i, h))
    return pl.pallas_call(
        body, name="sb_fwd", grid=(HEADS, s // tq),
        in_specs=[blk, pl.BlockSpec((s, HEAD_DIM), lambda h, i: (0, HEADS + h)),
                  pl.BlockSpec((s, HEAD_DIM), lambda h, i: (0, 2 * HEADS + h))],
        out_specs=[blk, blk],
        out_shape=[jax.ShapeDtypeStruct((s, SB_WIDTH), F32), jax.ShapeDtypeStruct((s, SB_WIDTH), F32)],
        scratch_shapes=[pltpu.VMEM((tq, HEAD_DIM), F32), pltpu.VMEM((tq, 1), F32)],
        compiler_params=_cp(("parallel", "arbitrary")),
    )(projb, projb, projb)


def _mla_prep(projf, qw, kvw, wq, wk, wv, rc, rs1, rs2, tm=512):
    s = projf.shape[0]

    def body(cq_ref, ckv_ref, kr_ref, qw_ref, kvw_ref, wq_ref, wk_ref, wv_ref, c_ref, s1_ref, s2_ref,
             nq_ref, nkv_ref, q_ref, k_ref, v_ref):
        c, s1, s2 = c_ref[...], s1_ref[...], s2_ref[...]
        cq = cq_ref[...]
        nq = ((cq * lax.rsqrt(jnp.mean(cq * cq, axis=1, keepdims=True) + EPS)) * qw_ref[...]).astype(BF16)
        nq_ref[...] = nq
        qf = _nn(nq, wq_ref[...])
        ckv = ckv_ref[...]
        nkv = ((ckv * lax.rsqrt(jnp.mean(ckv * ckv, axis=1, keepdims=True) + EPS)) * kvw_ref[...]).astype(BF16)
        nkv_ref[...] = nkv
        kn = _nn(nkv, wk_ref[...])
        v_ref[...] = _nn(nkv, wv_ref[...]).astype(BF16)
        krot = _rope(kr_ref[...], c, s1, s2).astype(BF16)
        for h in range(HEADS):
            lo = h * MLA_QK_PAD
            q_ref[:, lo:lo + 128] = qf[:, lo:lo + 128].astype(BF16)
            q_ref[:, lo + 128:lo + 256] = _rope(qf[:, lo + 128:lo + 256], c, s1, s2).astype(BF16)
            k_ref[:, lo:lo + 128] = kn[:, h * 128:(h + 1) * 128].astype(BF16)
            k_ref[:, lo + 128:lo + 256] = krot

    row = lambda w, b: pl.BlockSpec((tm, w), lambda i: (i, b))
    full = lambda a: pl.BlockSpec(a.shape, lambda i: (0, 0))
    return pl.pallas_call(
        body, name="mla_prep", grid=(s // tm,),
        in_specs=[row(Q_RANK, 2048 // Q_RANK), row(KV_RANK, 2560 // KV_RANK), row(128, 2816 // 128),
                  full(qw), full(kvw), full(wq), full(wk), full(wv), row(128, 0), row(128, 0), row(128, 0)],
        out_specs=[row(Q_RANK, 0), row(KV_RANK, 0), row(HEADS * MLA_QK_PAD, 0), row(HEADS * MLA_QK_PAD, 0),
                   row(SB_WIDTH, 0)],
        out_shape=[jax.ShapeDtypeStruct((s, Q_RANK), BF16), jax.ShapeDtypeStruct((s, KV_RANK), BF16),
                   jax.ShapeDtypeStruct((s, HEADS * MLA_QK_PAD), BF16),
                   jax.ShapeDtypeStruct((s, HEADS * MLA_QK_PAD), BF16),
                   jax.ShapeDtypeStruct((s, SB_WIDTH), BF16)],
        compiler_params=_cp(("parallel",)),
    )(projf, projf, projf, qw, kvw, wq, wk, wv, rc, rs1, rs2)


def _mla_mask(row, col):
    return (col // CHUNK) <= (row // CHUNK)


def _mla_fwd(qm, km, vm):
    s = qm.shape[0]
    tq, tk = _tiles(s)
    nb = tq // tk

    def body(q_ref, k_ref, v_ref, o_ref, lse_ref, acc_ref, m_ref, l_ref):
        i = pl.program_id(1)
        q = q_ref[...]
        masks = _band_masks(tq, tk, _mla_mask)
        acc_ref[...] = jnp.zeros_like(acc_ref)
        m_ref[...] = jnp.full_like(m_ref, NEG)
        l_ref[...] = jnp.zeros_like(l_ref)

        def tile(j, mask):
            rows = pl.ds(pl.multiple_of(j * tk, tk), tk)
            sc = _nt(q, k_ref[rows, :]) * MLA_SCALE
            if mask is not None:
                sc = jnp.where(mask, sc, NEG)
            m_old = m_ref[...]
            m_new = jnp.maximum(m_old, jnp.max(sc, axis=1, keepdims=True))
            p = jnp.exp(sc - m_new)
            alpha = jnp.exp(m_old - m_new)
            l_ref[...] = alpha * l_ref[...] + jnp.sum(p, axis=1, keepdims=True)
            acc_ref[...] = alpha * acc_ref[...] + _nn(p.astype(BF16), v_ref[rows, :])
            m_ref[...] = m_new

        for b in range(nb):
            tile(i * nb + b, masks[b])

        def step(j, c):
            tile(j, None)
            return c

        lax.fori_loop(0, i * nb, step, 0)
        o_ref[...] = acc_ref[...] / l_ref[...]
        lse_ref[...] = jnp.broadcast_to(m_ref[...] + jnp.log(l_ref[...]), (tq, HEAD_DIM))

    return pl.pallas_call(
        body, name="mla_fwd", grid=(HEADS, s // tq),
        in_specs=[pl.BlockSpec((tq, MLA_QK_PAD), lambda h, i: (i, h)),
                  pl.BlockSpec((s, MLA_QK_PAD), lambda h, i: (0, h)),
                  pl.BlockSpec((s, HEAD_DIM), lambda h, i: (0, h))],
        out_specs=[pl.BlockSpec((tq, HEAD_DIM), lambda h, i: (i, h)),
                   pl.BlockSpec((tq, HEAD_DIM), lambda h, i: (i, h))],
        out_shape=[jax.ShapeDtypeStruct((s, SB_WIDTH), F32), jax.ShapeDtypeStruct((s, SB_WIDTH), F32)],
        scratch_shapes=[pltpu.VMEM((tq, HEAD_DIM), F32), pltpu.VMEM((tq, 1), F32), pltpu.VMEM((tq, 1), F32)],
        compiler_params=_cp(("parallel", "arbitrary")),
    )(qm, km, vm)


def _out_post(oa, ob, projf, wout, x, target, pw, tm=256):
    s, d = x.shape

    def body(oa_ref, ob_ref, ga_ref, gb_ref, w_ref, x_ref, t_ref, pw_ref,
             mix_ref, dy_ref, dout_ref, loss_ref, dpw_ref):
        i = pl.program_id(0)
        sa, _ = _silu_parts(ga_ref[...])
        sb, _ = _silu_parts(gb_ref[...])
        mix_ref[:, :SB_WIDTH] = (oa_ref[...] * sa).astype(BF16)
        mix_ref[:, SB_WIDTH:] = (ob_ref[...] * sb).astype(BF16)
        y = _nn(mix_ref[...], w_ref[...])
        r = lax.rsqrt(jnp.mean(y * y, axis=1, keepdims=True) + EPS)
        yhat = y * r
        pwv = pw_ref[...]
        err = (x_ref[...] + yhat * pwv) - t_ref[...]
        dout = err * (1.0 / d)
        dout_ref[...] = dout
        g = dout * pwv
        dy_ref[...] = (r * (g - yhat * jnp.mean(g * yhat, axis=1, keepdims=True))).astype(BF16)

        @pl.when(i == 0)
        def _():
            loss_ref[...] = jnp.zeros_like(loss_ref)
            dpw_ref[...] = jnp.zeros_like(dpw_ref)

        loss_ref[...] += jnp.sum(err * err, axis=0, keepdims=True)
        dpw_ref[...] += jnp.sum(dout * yhat, axis=0, keepdims=True)

    row = lambda w, b: pl.BlockSpec((tm, w), lambda i: (i, b))
    vec = pl.BlockSpec((1, d), lambda i: (0, 0))
    return pl.pallas_call(
        body, name="out_post", grid=(s // tm,),
        in_specs=[row(SB_WIDTH, 0), row(SB_WIDTH, 0), row(SB_WIDTH, 0), row(SB_WIDTH, 1),
                  pl.BlockSpec(wout.shape, lambda i: (0, 0)), row(d, 0), row(d, 0), vec],
        out_specs=[row(d, 0), row(d, 0), row(d, 0), vec, vec],
        out_shape=[jax.ShapeDtypeStruct((s, d), BF16), jax.ShapeDtypeStruct((s, d), BF16),
                   jax.ShapeDtypeStruct((s, d), F32), jax.ShapeDtypeStruct((1, d), F32),
                   jax.ShapeDtypeStruct((1, d), F32)],
        compiler_params=_cp(("arbitrary",)),
    )(oa, ob, projf, projf, wout, x, target, pw)


def _sb_bwd(projb, projf, dmixed, oa, tot):
    s = projb.shape[0]
    tq, tk = _tiles(s)
    nb, nq = tq // tk, s // tq

    def body(q_ref, k_ref, v_ref, dm_ref, g_ref, o_ref, tot_ref, dq_ref, dk_ref, dv_ref, dg_ref,
             dka_ref, dva_ref, dqa_ref, pre_ref, cg_ref):
        i = pl.program_id(1)

        @pl.when(i == 0)
        def _():
            dka_ref[...] = jnp.zeros_like(dka_ref)
            dva_ref[...] = jnp.zeros_like(dva_ref)

        q = q_ref[...]
        silu, dsilu = _silu_parts(g_ref[...])
        dm = dm_ref[...]
        dg_ref[...] = (dm * o_ref[...] * dsilu).astype(BF16)
        do = (dm * silu).astype(BF16)
        tot = tot_ref[:, 0:1]
        tri = _tri(tk, True)
        tri2 = _tri(tk, False)
        masks = _band_masks(tq, tk, lambda r, c: c < r)
        dqa_ref[...] = jnp.zeros_like(dqa_ref)
        pre_ref[...] = jnp.zeros_like(pre_ref)
        cg_ref[...] = jnp.zeros_like(cg_ref)

        def tile(j, mask):
            rows = pl.ds(pl.multiple_of(j * tk, tk), tk)
            k = k_ref[rows, :]
            ls, lk = _sb_tile(q, k, mask)
            pre = pre_ref[...] + jnp.sum(lk, axis=1, keepdims=True)
            pre_ref[...] = pre
            a = jnp.exp(ls + _split_dot(lk, tri) + (tot - pre))
            if mask is not None:
                a = jnp.where(mask, a, 0.0)
            g = a * _nt(do, v_ref[rows, :])
            dva_ref[rows, :] += _tn(a.astype(BF16), do)
            cum = _split_dot(g, tri2) + cg_ref[...]
            beta = jnp.exp(ls)
            dz = (g * (1.0 - beta) - beta * cum) * SB_SCALE
            if mask is not None:
                dz = jnp.where(mask, dz, 0.0)
            dz = dz.astype(BF16)
            dqa_ref[...] += _nn(dz, k)
            dka_ref[rows, :] += _tn(dz, q)
            cg_ref[...] += jnp.sum(g, axis=1, keepdims=True)

        def step(j, c):
            tile(j, None)
            return c

        lax.fori_loop(0, i * nb, step, 0)
        for b in range(nb):
            tile(i * nb + b, masks[b])
        dq_ref[...] = dqa_ref[...].astype(BF16)

        @pl.when(i == nq - 1)
        def _():
            dk_ref[...] = dka_ref[...].astype(BF16)
            dv_ref[...] = dva_ref[...].astype(BF16)

    blk = lambda off: pl.BlockSpec((tq, HEAD_DIM), lambda h, i: (i, off + h))
    whole = lambda off: pl.BlockSpec((s, HEAD_DIM), lambda h, i: (0, off + h))
    o_sd = jax.ShapeDtypeStruct((s, SB_WIDTH), BF16)
    return pl.pallas_call(
        body, name="sb_bwd", grid=(HEADS, nq),
        in_specs=[blk(0), whole(HEADS), whole(2 * HEADS), blk(0), blk(0), blk(0), blk(0)],
        out_specs=[blk(0), whole(0), whole(0), blk(0)],
        out_shape=[o_sd, o_sd, o_sd, o_sd],
        scratch_shapes=[pltpu.VMEM((s, HEAD_DIM), F32), pltpu.VMEM((s, HEAD_DIM), F32),
                        pltpu.VMEM((tq, HEAD_DIM), F32), pltpu.VMEM((tq, 1), F32), pltpu.VMEM((tq, 1), F32)],
        compiler_params=_cp(("parallel", "arbitrary")),
    )(projb, projb, projb, dmixed, projf, oa, tot)


def _mla_bwd(qm, km, vm, projf, dmixed, ob, lse):
    s = qm.shape[0]
    tq, tk = _tiles(s)
    nb, nq = tq // tk, s // tq

    def body(q_ref, k_ref, v_ref, dm_ref, g_ref, o_ref, lse_ref, dq_ref, dk_ref, dv_ref, dg_ref,
             dva_ref, dqa_ref):
        i = pl.program_id(1)

        @pl.when(i == 0)
        def _():
            dk_ref[...] = jnp.zeros_like(dk_ref)
            dva_ref[...] = jnp.zeros_like(dva_ref)

        q = q_ref[...]
        silu, dsilu = _silu_parts(g_ref[...])
        dm = dm_ref[...]
        o = o_ref[...]
        dg_ref[...] = (dm * o * dsilu).astype(BF16)
        dof = dm * silu
        delta = jnp.sum(dof * o, axis=1, keepdims=True)
        do = dof.astype(BF16)
        lse_col = lse_ref[:, 0:1]
        masks = _band_masks(tq, tk, _mla_mask)
        dqa_ref[...] = jnp.zeros_like(dqa_ref)

        def tile(j, mask):
            rows = pl.ds(pl.multiple_of(j * tk, tk), tk)
            k = k_ref[rows, :]
            p = jnp.exp(_nt(q, k) * MLA_SCALE - lse_col)
            if mask is not None:
                p = jnp.where(mask, p, 0.0)
            ds = (p * (_nt(do, v_ref[rows, :]) - delta) * MLA_SCALE).astype(BF16)
            dva_ref[rows, :] += _tn(p.astype(BF16), do)
            dk_ref[rows, :] += _tn(ds, q)
            dqa_ref[...] += _nn(ds, k)

        def step(j, c):
            tile(j, None)
            return c

        lax.fori_loop(0, i * nb, step, 0)
        for b in range(nb):
            tile(i * nb + b, masks[b])
        dq_ref[...] = dqa_ref[...]

        @pl.when(i == nq - 1)
        def _():
            dv_ref[...] = dva_ref[...].astype(BF16)

    blk = lambda w, off: pl.BlockSpec((tq, w), lambda h, i: (i, off + h))
    whole = lambda w: pl.BlockSpec((s, w), lambda h, i: (0, h))
    return pl.pallas_call(
        body, name="mla_bwd", grid=(HEADS, nq),
        in_specs=[blk(MLA_QK_PAD, 0), whole(MLA_QK_PAD), whole(HEAD_DIM), blk(HEAD_DIM, HEADS),
                  blk(HEAD_DIM, HEADS), blk(HEAD_DIM, 0), blk(HEAD_DIM, 0)],
        out_specs=[blk(MLA_QK_PAD, 0), whole(MLA_QK_PAD), whole(HEAD_DIM), blk(HEAD_DIM, 0)],
        out_shape=[jax.ShapeDtypeStruct((s, HEADS * MLA_QK_PAD), F32),
                   jax.ShapeDtypeStruct((s, HEADS * MLA_QK_PAD), F32),
                   jax.ShapeDtypeStruct((s, SB_WIDTH), BF16), jax.ShapeDtypeStruct((s, SB_WIDTH), BF16)],
        scratch_shapes=[pltpu.VMEM((s, HEAD_DIM), F32), pltpu.VMEM((tq, MLA_QK_PAD), F32)],
        compiler_params=_cp(("parallel", "arbitrary")),
    )(qm, km, vm, dmixed, projf, ob, lse)


def _norm_bwd(x, w, dn):
    r = lax.rsqrt(jnp.mean(x * x, axis=1, keepdims=True) + EPS)
    xhat = x * r
    g = dn * w
    return r * (g - xhat * jnp.mean(g * xhat, axis=1, keepdims=True)), dn * xhat


def _mla_bwd_post(dqm, dkm, dvm, projf, qw, kvw, wq, wk, wv, rc, rs1, rs2, tm=256):
    s = dqm.shape[0]

    def body(dq_ref, dk_ref, dv_ref, cq_ref, ckv_ref, qw_ref, kvw_ref, wq_ref, wk_ref, wv_ref,
             c_ref, s1_ref, s2_ref, dqp_ref, dkn_ref, dcq_ref, dckv_ref, dkr_ref, dqw_ref, dkvw_ref):
        i = pl.program_id(0)
        c, s1, s2 = c_ref[...], s1_ref[...], s2_ref[...]
        drot = jnp.zeros((tm, 128), F32)
        for h in range(HEADS):
            lo = h * MLA_QK_PAD
            dqp_ref[:, lo:lo + 128] = dq_ref[:, lo:lo + 128].astype(BF16)
            dqp_ref[:, lo + 128:lo + 256] = _rope_t(dq_ref[:, lo + 128:lo + 256], c, s1, s2).astype(BF16)
            dkn_ref[:, h * 128:(h + 1) * 128] = dk_ref[:, lo:lo + 128].astype(BF16)
            drot = drot + dk_ref[:, lo + 128:lo + 256]
        dkr_ref[...] = _rope_t(drot, c, s1, s2).astype(BF16)
        dcq, dqw = _norm_bwd(cq_ref[...], qw_ref[...], _nt(dqp_ref[...], wq_ref[...]))
        dcq_ref[...] = dcq.astype(BF16)
        dnkv = _nt(dkn_ref[...], wk_ref[...]) + _nt(dv_ref[...], wv_ref[...])
        dckv, dkvw = _norm_bwd(ckv_ref[...], kvw_ref[...], dnkv)
        dckv_ref[...] = dckv.astype(BF16)

        @pl.when(i == 0)
        def _():
            dqw_ref[...] = jnp.zeros_like(dqw_ref)
            dkvw_ref[...] = jnp.zeros_like(dkvw_ref)

        dqw_ref[...] += jnp.sum(dqw, axis=0, keepdims=True)
        dkvw_ref[...] += jnp.sum(dkvw, axis=0, keepdims=True)

    row = lambda w, b: pl.BlockSpec((tm, w), lambda i: (i, b))
    full = lambda a: pl.BlockSpec(a.shape, lambda i: (0, 0))
    sd = jax.ShapeDtypeStruct
    return pl.pallas_call(
        body, name="mla_bwd_post", grid=(s // tm,),
        in_specs=[row(HEADS * MLA_QK_PAD, 0), row(HEADS * MLA_QK_PAD, 0), row(SB_WIDTH, 0),
                  row(Q_RANK, 2048 // Q_RANK), row(KV_RANK, 2560 // KV_RANK),
                  full(qw), full(kvw), full(wq), full(wk), full(wv), row(128, 0), row(128, 0), row(128, 0)],
        out_specs=[row(HEADS * MLA_QK_PAD, 0), row(SB_WIDTH, 0), row(Q_RANK, 0), row(KV_RANK, 0), row(128, 0),
                   pl.BlockSpec((1, Q_RANK), lambda i: (0, 0)), pl.BlockSpec((1, KV_RANK), lambda i: (0, 0))],
        out_shape=[sd((s, HEADS * MLA_QK_PAD), BF16), sd((s, SB_WIDTH), BF16), sd((s, Q_RANK), BF16),
                   sd((s, KV_RANK), BF16), sd((s, 128), BF16), sd((1, Q_RANK), F32), sd((1, KV_RANK), F32)],
        compiler_params=_cp(("arbitrary",)),
    )(dqm, dkm, dvm, projf, projf, qw, kvw, wq, wk, wv, rc, rs1, rs2)


def _pre_bwd(dproj, win, x, pw, dout, tm=512, tk=1024):
    s, d = x.shape
    nk = dproj.shape[1] // tk

    def body(dp_ref, w_ref, x_ref, pw_ref, do_ref, dx_ref, dpw_ref, acc_ref):
        i, k = pl.program_id(0), pl.program_id(1)
        part = _nt(dp_ref[...], w_ref[...])

        @pl.when(k == 0)
        def _():
            acc_ref[...] = part

        @pl.when(k > 0)
        def _():
            acc_ref[...] += part

        @pl.when(k == nk - 1)
        def _():
            dx, dw = _norm_bwd(x_ref[...], pw_ref[...], acc_ref[...])
            dx_ref[...] = do_ref[...] + dx

            @pl.when(i == 0)
            def _():
                dpw_ref[...] = jnp.zeros_like(dpw_ref)

            dpw_ref[...] += jnp.sum(dw, axis=0, keepdims=True)

    rowd = pl.BlockSpec((tm, d), lambda i, k: (i, 0))
    vec = pl.BlockSpec((1, d), lambda i, k: (0, 0))
    return pl.pallas_call(
        body, name="pre_bwd", grid=(s // tm, nk),
        in_specs=[pl.BlockSpec((tm, tk), lambda i, k: (i, k)), pl.BlockSpec((d, tk), lambda i, k: (0, k)),
                  rowd, vec, rowd],
        out_specs=[rowd, vec],
        out_shape=[jax.ShapeDtypeStruct((s, d), F32), jax.ShapeDtypeStruct((1, d), F32)],
        scratch_shapes=[pltpu.VMEM((tm, d), F32)],
        compiler_params=_cp(("arbitrary", "arbitrary")),
    )(dproj, win, x, pw, dout)


def _adamw(w, g, m, v):
    m = ADAM_B1 * m + (1.0 - ADAM_B1) * g
    v = ADAM_B2 * v + (1.0 - ADAM_B2) * (g * g)
    delta = -ADAM_LR * ((m / ADAM_C1) / (jnp.sqrt(v / ADAM_C2) + ADAM_EPS) + ADAM_WD * w)
    return delta, m, v


def _sum_adamw(core, own, got, w, m, v, tr=1728):
    n, r, lanes = got.shape

    def body(c_ref, o_ref, p_ref, w_ref, m_ref, v_ref, g_ref, d_ref, nm_ref, nv_ref):
        g = o_ref[...]
        for k in range(n):
            g = g + p_ref[k]
        g_ref[0] = g
        d_ref[0], nm_ref[0], nv_ref[0] = _adamw(w_ref[...], g, m_ref[...], v_ref[...])

    blk = pl.BlockSpec((tr, lanes), lambda i, c: (i, 0))
    out = pl.BlockSpec((1, tr, lanes), lambda i, c: (c[0], i, 0))
    sd = jax.ShapeDtypeStruct((2, r, lanes), F32)
    return pl.pallas_call(
        body, name="sum_adamw",
        grid_spec=pltpu.PrefetchScalarGridSpec(
            num_scalar_prefetch=1, grid=(r // tr,),
            in_specs=[blk, pl.BlockSpec((n, tr, lanes), lambda i, c: (0, i, 0)), blk, blk, blk],
            out_specs=[out, out, out, out]),
        out_shape=[sd, sd, sd, sd],
        compiler_params=_cp(("parallel",)),
    )(core, own, got, w, m, v)


def _pair_sum(mine, got, tr=1728):
    n, r, lanes = mine.shape

    def body(a_ref, b_ref, o_ref):
        o_ref[...] = a_ref[...] + b_ref[...]

    blk = pl.BlockSpec((n, tr, lanes), lambda i: (0, i, 0))
    return pl.pallas_call(
        body, name="pair_sum", grid=(r // tr,), in_specs=[blk, blk], out_specs=blk,
        out_shape=jax.ShapeDtypeStruct(mine.shape, F32), compiler_params=_cp(("parallel",)),
    )(mine, got)


def _place():
    return lax.axis_index("x"), lax.axis_index("y"), lax.axis_index("c")


ANY = pl.BlockSpec(memory_space=pl.ANY)


def _gather_weights(packed):
    r, lanes = packed.shape
    half = r // 2

    def body(w_ref, out_ref, send_sems, recv_sems):
        x, y, c = _place()
        chips = [(1 - x, y), (x, 1 - y), (1 - x, 1 - y)]

        def rows(chip, core):
            return out_ref.at[2 * chip[0] + chip[1], pl.ds(core * half, half), :]

        def copy(k, chip, core, to, src=None):
            return pltpu.make_async_remote_copy(
                src_ref=rows(chip, core) if src is None else src, dst_ref=rows(chip, core),
                send_sem=send_sems.at[k], recv_sem=recv_sems.at[k], device_id=to, device_id_type=MESH)

        first = [copy(j, (x, y), c, (*chip, c), src=w_ref.at[pl.ds(c * half, half), :])
                 for j, chip in enumerate(chips)]
        for cp in first:
            cp.start()
        passed = [copy(3 + j, chip, c, (x, y, 1 - c)) for j, chip in enumerate(chips)]
        for j, chip in enumerate(chips):
            copy(j, chip, c, (x, y, c)).wait_recv()
            passed[j].start()
        for j, chip in enumerate(chips):
            copy(3 + j, chip, 1 - c, (x, y, c)).wait_recv()
        for cp in first + passed:
            cp.wait_send()

    return pl.pallas_call(
        body, name="gather_weights", in_specs=[ANY], out_specs=ANY,
        out_shape=jax.ShapeDtypeStruct((4, r, lanes), packed.dtype),
        scratch_shapes=[pltpu.SemaphoreType.DMA((6,)), pltpu.SemaphoreType.DMA((6,))],
        compiler_params=pltpu.CompilerParams(has_side_effects=True),
    )(packed)


def _swap_halves(grads):
    n, _, r, lanes = grads.shape

    def body(g_ref, got_ref, send_sems, recv_sems):
        x, y, c = _place()
        copies = [pltpu.make_async_remote_copy(
            src_ref=g_ref.at[k, 1 - c], dst_ref=got_ref.at[k], send_sem=send_sems.at[k], recv_sem=recv_sems.at[k],
            device_id=(x, y, 1 - c), device_id_type=MESH) for k in range(n)]
        for cp in copies:
            cp.start()
        for cp in copies:
            cp.wait()

    return pl.pallas_call(
        body, name="swap_halves", in_specs=[ANY], out_specs=ANY,
        out_shape=jax.ShapeDtypeStruct((n, r, lanes), grads.dtype),
        scratch_shapes=[pltpu.SemaphoreType.DMA((n,)), pltpu.SemaphoreType.DMA((n,))],
        compiler_params=pltpu.CompilerParams(has_side_effects=True),
    )(grads)


def _exchange_chips(psum):
    n, r, lanes = psum.shape

    def body(p_ref, got_ref, send_sems, recv_sems):
        x, y, c = _place()
        chips = [(1 - x, y), (x, 1 - y), (1 - x, 1 - y)]
        copies = [pltpu.make_async_remote_copy(
            src_ref=p_ref.at[2 * chip[0] + chip[1]], dst_ref=got_ref.at[j],
            send_sem=send_sems.at[j], recv_sem=recv_sems.at[j], device_id=(*chip, c), device_id_type=MESH)
            for j, chip in enumerate(chips)]
        for cp in copies:
            cp.start()
        for cp in copies:
            cp.wait()

    return pl.pallas_call(
        body, name="exchange_chips", in_specs=[ANY], out_specs=ANY,
        out_shape=jax.ShapeDtypeStruct((3, r, lanes), psum.dtype),
        scratch_shapes=[pltpu.SemaphoreType.DMA((3,)), pltpu.SemaphoreType.DMA((3,))],
        compiler_params=pltpu.CompilerParams(has_side_effects=True),
    )(psum)


def _share_with_sibling(arrs):
    na = len(arrs)

    def body(*refs):
        outs = refs[na:2 * na]
        send_sems, recv_sems = refs[2 * na:]
        x, y, c = _place()
        copies = [pltpu.make_async_remote_copy(
            src_ref=outs[k].at[c], dst_ref=outs[k].at[c], send_sem=send_sems.at[k], recv_sem=recv_sems.at[k],
            device_id=(x, y, 1 - c), device_id_type=MESH) for k in range(na)]
        for cp in copies:
            cp.start()
        for cp in copies:
            cp.wait()

    return pl.pallas_call(
        body, name="share_with_sibling", in_specs=[ANY] * na, out_specs=[ANY] * na,
        out_shape=[jax.ShapeDtypeStruct(a.shape, a.dtype) for a in arrs],
        input_output_aliases={k: k for k in range(na)},
        scratch_shapes=[pltpu.SemaphoreType.DMA((na,)), pltpu.SemaphoreType.DMA((na,))],
        compiler_params=pltpu.CompilerParams(has_side_effects=True),
    )(*arrs)


def _norm_allreduce_adamw(part, w, m, v):
    r, lanes = part.shape

    def body(p_ref, w_ref, m_ref, v_ref, g_ref, d_ref, nm_ref, nv_ref, all_ref, send_sems, recv_sems):
        x, y, c = _place()
        me = 4 * x + 2 * y + c
        all_ref[me] = p_ref[...]
        copies = []
        for k in range(1, 8):
            peer = (x ^ (k >> 2), y ^ ((k >> 1) & 1), c ^ (k & 1))
            copies.append(pltpu.make_async_remote_copy(
                src_ref=p_ref, dst_ref=all_ref.at[me], send_sem=send_sems.at[k - 1], recv_sem=recv_sems.at[k - 1],
                device_id=peer, device_id_type=MESH))
        for cp in copies:
            cp.start()
        for cp in copies:
            cp.wait()
        g = all_ref[0]
        for k in range(1, 8):
            g = g + all_ref[k]
        g_ref[...] = g
        d_ref[...], nm_ref[...], nv_ref[...] = _adamw(w_ref[...], g, m_ref[...], v_ref[...])

    vm = pl.BlockSpec(memory_space=pltpu.VMEM)
    sd = jax.ShapeDtypeStruct((r, lanes), F32)
    return pl.pallas_call(
        body, name="norm_allreduce_adamw", in_specs=[vm] * 4, out_specs=[vm] * 4, out_shape=[sd] * 4,
        scratch_shapes=[pltpu.VMEM((8, r, lanes), F32), pltpu.SemaphoreType.DMA((7,)),
                        pltpu.SemaphoreType.DMA((7,))],
        compiler_params=pltpu.CompilerParams(has_side_effects=True),
    )(part, w, m, v)


W_SHAPES = {"w_in": (D_MODEL, D_IN // 4), "w_q_up": (Q_RANK, 1536 // 4), "w_kv_up": (KV_RANK, 2048 // 4),
            "w_out": (D_MODEL // 4, D_MODEL)}
W_NAMES = ("w_in", "w_q_up", "w_kv_up", "w_out")
LANES = 128


def _pack_shard(ws):
    return jnp.concatenate([w.reshape(-1, LANES) for w in ws], axis=0)


def _unpack_shard(packed):
    out, at = [], 0
    for name in W_NAMES:
        rows = W_SHAPES[name][0] * W_SHAPES[name][1] // LANES
        out.append(packed[..., at:at + rows, :].reshape(packed.shape[:-2] + W_SHAPES[name]))
        at += rows
    return out


def _pack_halves(ws, c):
    parts = []
    for w in ws:
        hr = w.shape[0] // 2
        parts.append(lax.dynamic_slice_in_dim(w, c * hr, hr, axis=0).reshape(-1, LANES))
    return jnp.concatenate(parts, axis=0)


def _unpack_halves(both):
    out, at = [], 0
    for name in W_NAMES:
        rws, cols = W_SHAPES[name]
        n = rws * cols // 2 // LANES
        out.append(both[:, at:at + n, :].reshape(rws, cols))
        at += n
    return out


def _perm_in(w):
    return jnp.concatenate([w[:, :4096], w[:, 4928:5952], w[:, 4096:4928],
                            jnp.zeros((w.shape[0], D_INP - D_IN), w.dtype)], axis=1)


def _unperm_in(w):
    return jnp.concatenate([w[:, :4096], w[:, 5120:5952], w[:, 4096:5120]], axis=1)


def _grad_chunks(dwin, dwq, dwkv, dwout):
    def per(g, axis):
        sh = jnp.stack(jnp.split(g, 4, axis=axis))
        return sh.reshape(4, 2, -1, LANES)
    return jnp.concatenate([per(dwin, 1), per(dwq, 1), per(dwkv, 1), per(dwout, 0)], axis=2)


NORM_NAMES = ("pre_norm_w", "q_norm_w", "kv_norm_w", "post_norm_w")
NORM_SIZES = (D_MODEL, Q_RANK, KV_RANK, D_MODEL)
NORM_ROWS = 40


def _pack_norm(vs):
    flat = jnp.concatenate([v.reshape(-1) for v in vs])
    return jnp.pad(flat, (0, NORM_ROWS * LANES - flat.shape[0])).reshape(NORM_ROWS, LANES)


def _unpack_norm(p):
    flat, out, at = p.reshape(-1), [], 0
    for n in NORM_SIZES:
        out.append(flat[at:at + n].reshape(1, n))
        at += n
    return out


def _rope_tables(positions):
    inv_freq = ROPE_THETA ** (-jnp.arange(0, MLA_ROPE, 2, dtype=F32) / MLA_ROPE)
    ang = positions.astype(F32)[:, None] * inv_freq
    cos, sin, z = jnp.cos(ang), jnp.sin(ang), jnp.zeros_like(ang)
    return (jnp.concatenate([cos, cos, z, z], axis=1), jnp.concatenate([z, sin, z, z], axis=1),
            jnp.concatenate([-sin, z, z, z], axis=1))


def _local_step(x, positions, pre_w, win, q_w, wq, kv_w, wk, wv, wout, post_w, target):
    s = x.shape[0]
    rc, rs1, rs2 = _rope_tables(positions)
    h = _prenorm(x, pre_w)
    projb = _matmul(h, win, mode="nn", out_dtype=BF16, tm=512, tn=1024, tk=D_MODEL, name="proj_b", n=PB_W)
    projf = _matmul(h, win, mode="nn", out_dtype=F32, tm=512, tn=1024, tk=D_MODEL, name="proj_f", n=PF_W,
                    b_off=PB_W // 1024)
    oa, tot = _sb_fwd(projb)
    nq, nkv, qm, km, vm = _mla_prep(projf, q_w, kv_w, wq, wk, wv, rc, rs1, rs2)
    ob, lse = _mla_fwd(qm, km, vm)
    mixed, dy, dout, err2, dpost = _out_post(oa, ob, projf, wout, x, target, post_w)

    dwout = _matmul(mixed, dy, mode="tn", out_dtype=F32, tm=1024, tn=1024, tk=min(1024, s), name="dw_out")
    dmixed = _matmul(dy, wout, mode="nt", out_dtype=F32, tm=512, tn=1024, tk=D_MODEL, name="d_mixed")
    dqa, dka, dva, dga = _sb_bwd(projb, projf, dmixed, oa, tot)
    dqm, dkm, dvm, dgb = _mla_bwd(qm, km, vm, projf, dmixed, ob, lse)
    dqp, dkn, dcq, dckv, dkr, dqw, dkvw = _mla_bwd_post(dqm, dkm, dvm, projf, q_w, kv_w, wq, wk, wv, rc, rs1, rs2)
    tks = min(1024, s)
    dwq = _matmul(nq, dqp, mode="tn", out_dtype=F32, tm=Q_RANK, tn=1024, tk=tks, name="dw_q")
    dwk = _matmul(nkv, dkn, mode="tn", out_dtype=F32, tm=KV_RANK, tn=1024, tk=tks, name="dw_k")
    dwv = _matmul(nkv, dvm, mode="tn", out_dtype=F32, tm=KV_RANK, tn=1024, tk=tks, name="dw_v")
    dproj = jnp.concatenate([dqa, dka, dva, dga, dgb, dcq, dckv, dkr, jnp.zeros((s, 128), BF16)], axis=1)
    dwin = _matmul(h, dproj, mode="tn", out_dtype=F32, tm=1024, tn=1024, tk=tks, name="dw_in")
    gx, dpre = _pre_bwd(dproj, win, x, pre_w, dout)
    return err2, gx, dpre, dwin, dqw, dwq, dkvw, dwk, dwv, dwout, dpost


def _kernel_layouts(full):
    w_in, w_q_up, w_kv_up, w_out = full
    win = _perm_in(w_in)
    wq = jnp.pad(w_q_up.reshape(Q_RANK, HEADS, 192), ((0, 0), (0, 0), (0, 64))).reshape(Q_RANK, HEADS * MLA_QK_PAD)
    kv = w_kv_up.reshape(KV_RANK, HEADS, 256)
    wk = kv[:, :, :128].reshape(KV_RANK, SB_WIDTH)
    wv = kv[:, :, 128:].reshape(KV_RANK, SB_WIDTH)
    return win, wq, wk, wv, w_out


def _original_layouts(dwin, dwq, dwk, dwv):
    dwi = _unperm_in(dwin[:, :D_IN])
    dq = dwq.reshape(Q_RANK, HEADS, MLA_QK_PAD)[:, :, :192].reshape(Q_RANK, HEADS * 192)
    dkv = jnp.concatenate([dwk.reshape(KV_RANK, HEADS, 128), dwv.reshape(KV_RANK, HEADS, 128)], axis=2)
    return dwi, dq, dkv.reshape(KV_RANK, 2 * SB_WIDTH)


def kernel(x, positions, pre_norm_w, w_in, q_norm_w, w_q_up, kv_norm_w, w_kv_up, w_out, post_norm_w, loss_target, m_pre_norm_w, m_w_in, m_q_norm_w, m_w_q_up, m_kv_norm_w, m_w_kv_up, m_w_out, m_post_norm_w, v_pre_norm_w, v_w_in, v_q_norm_w, v_w_q_up, v_kv_norm_w, v_w_kv_up, v_w_out, v_post_norm_w):
    c = lax.axis_index("c")
    chip = 2 * lax.axis_index("x") + lax.axis_index("y")
    shards = (w_in[0], w_q_up[0], w_kv_up[0], w_out[0])
    packed = _pack_shard([w.astype(BF16) for w in shards])
    gathered = lax.dynamic_update_slice(_gather_weights(packed), packed[None], (chip, 0, 0))
    g_in, g_q, g_kv, g_out = _unpack_shard(gathered)
    cat = lambda g: jnp.concatenate([g[0], g[1], g[2], g[3]], axis=1)
    full = (cat(g_in), cat(g_q), cat(g_kv), g_out.reshape(D_MODEL, D_MODEL))
    win, wq, wk, wv, wout = _kernel_layouts(full)

    err2, gx, dpre, dwin, dqw, dwq, dkvw, dwk, dwv, dwout, dpost = _local_step(
        x[0], positions[0], pre_norm_w, win, q_norm_w, wq, kv_norm_w, wk, wv, wout, post_norm_w, loss_target[0])
    loss = lax.psum(0.5 * jnp.sum(err2) / D_MODEL, ("x", "y", "c"))

    dwi, dq, dkv = _original_layouts(dwin, dwq, dwk, dwv)
    chunks = _grad_chunks(dwi, dq, dkv, dwout)
    mine = lax.dynamic_index_in_dim(chunks, c, axis=1, keepdims=False)
    pair = _pair_sum(mine, _swap_halves(chunks))
    own = lax.dynamic_index_in_dim(pair, chip, axis=0, keepdims=False)
    got = _exchange_chips(pair)
    wh, mh, vh = (_pack_halves(ws, c) for ws in (
        shards, (m_w_in[0], m_w_q_up[0], m_w_kv_up[0], m_w_out[0]), (v_w_in[0], v_w_q_up[0], v_w_kv_up[0], v_w_out[0])))
    both = _share_with_sibling(list(_sum_adamw(c.reshape(1), own, got, wh, mh, vh)))
    big = [_unpack_halves(b) for b in both]

    small = _norm_allreduce_adamw(
        _pack_norm([dpre, dqw, dkvw, dpost]), _pack_norm([pre_norm_w, q_norm_w, kv_norm_w, post_norm_w]),
        _pack_norm([m_pre_norm_w, m_q_norm_w, m_kv_norm_w, m_post_norm_w]),
        _pack_norm([v_pre_norm_w, v_q_norm_w, v_kv_norm_w, v_post_norm_w]))
    small = [_unpack_norm(p) for p in small]

    def group(k):
        n, b = small[k], big[k]
        return (n[0], b[0][None], n[1], b[1][None], n[2], b[2][None], b[3][None], n[3])

    return (loss, gx[None], *group(0), *group(1), *group(2), *group(3))
```

```python
import functools
import math

import numpy as np
import jax
import jax.numpy as jnp
from jax import lax
from jax.experimental import pallas as pl
from jax.experimental.pallas import tpu as pltpu

F32 = jnp.float32
BF16 = jnp.bfloat16
MESH = pl.DeviceIdType.MESH

D_MODEL = 2048
HEADS = 8
HEAD_DIM = 128
SB_WIDTH = HEADS * HEAD_DIM
MLA_ROPE = 64
MLA_QK_PAD = 256
Q_RANK = 512
KV_RANK = 256
CHUNK = 64
EPS = 1e-6
ROPE_THETA = 10000.0
D_IN = 5952
D_INP = 6144
PB_W = 3072
PF_W = D_INP - PB_W
SB_SCALE = 1.0 / math.sqrt(HEAD_DIM)
MLA_SCALE = 1.0 / math.sqrt(HEAD_DIM + MLA_ROPE)
NEG = -1e30

ADAM_LR, ADAM_B1, ADAM_B2, ADAM_EPS, ADAM_WD, ADAM_STEP = 0.001, 0.9, 0.999, 1e-08, 0.01, 10
ADAM_C1 = 1.0 - ADAM_B1 ** ADAM_STEP
ADAM_C2 = 1.0 - ADAM_B2 ** ADAM_STEP

VMEM_LIMIT = 56 * 1024 * 1024
TQ = 1024
TK = 256
MLA_TK = 512
UNROLL = 2


def _cp(sem=None, **kw):
    return pltpu.CompilerParams(dimension_semantics=sem, vmem_limit_bytes=VMEM_LIMIT, **kw)


def _dot(a, b, dims):
    return lax.dot_general(a, b, (dims, ((), ())), preferred_element_type=F32)


def _nn(a, b):
    return _dot(a, b, ((1,), (0,)))


def _nt(a, b):
    return _dot(a, b, ((1,), (1,)))


def _tn(a, b):
    return _dot(a, b, ((0,), (0,)))


def _rope(x, c, s1, s2):
    return x * c + pltpu.roll(x, 32, 1) * s1 + pltpu.roll(x, 96, 1) * s2


def _rope_t(x, c, s1, s2):
    return x * c - pltpu.roll(x, 32, 1) * s1 - pltpu.roll(x, 96, 1) * s2


def _silu_parts(g):
    sg = jax.nn.sigmoid(g)
    return g * sg, sg * (1.0 + g * (1.0 - sg))


def _matmul(a, b, *, mode, out_dtype, tm, tn, tk, name, n=None, b_off=0):
    if mode == "tn":
        kk, m = a.shape
        n = b.shape[1] if n is None else n
    else:
        m, kk = a.shape
        n = (b.shape[1] if mode == "nn" else b.shape[0]) if n is None else n
    nk = kk // tk
    a_spec = {"nn": pl.BlockSpec((tm, tk), lambda j, i, k: (i, k)),
              "nt": pl.BlockSpec((tm, tk), lambda j, i, k: (i, k)),
              "tn": pl.BlockSpec((tk, tm), lambda j, i, k: (k, i))}[mode]
    b_spec = {"nn": pl.BlockSpec((tk, tn), lambda j, i, k: (k, j + b_off)),
              "nt": pl.BlockSpec((tn, tk), lambda j, i, k: (j, k)),
              "tn": pl.BlockSpec((tk, tn), lambda j, i, k: (k, j))}[mode]
    dims = {"nn": ((1,), (0,)), "nt": ((1,), (1,)), "tn": ((0,), (0,))}[mode]

    def body(a_ref, b_ref, o_ref, acc_ref):
        k = pl.program_id(2)
        part = _dot(a_ref[...], b_ref[...], dims)
        if nk == 1:
            o_ref[...] = part.astype(out_dtype)
        else:
            @pl.when(k == 0)
            def _():
                acc_ref[...] = part

            @pl.when(k > 0)
            def _():
                acc_ref[...] += part

            @pl.when(k == nk - 1)
            def _():
                o_ref[...] = acc_ref[...].astype(out_dtype)

    return pl.pallas_call(
        body, name=name, grid=(n // tn, m // tm, nk),
        in_specs=[a_spec, b_spec], out_specs=pl.BlockSpec((tm, tn), lambda j, i, k: (i, j)),
        out_shape=jax.ShapeDtypeStruct((m, n), out_dtype),
        scratch_shapes=[pltpu.VMEM((tm, tn) if nk > 1 else (8, 128), F32)],
        compiler_params=_cp(("parallel", "parallel", "arbitrary")),
    )(a, b)


def _prenorm(x, w, tm=512):
    s, d = x.shape

    def body(x_ref, w_ref, h_ref):
        xv = x_ref[...]
        r = lax.rsqrt(jnp.mean(xv * xv, axis=1, keepdims=True) + EPS)
        h_ref[...] = ((xv * r) * w_ref[...]).astype(BF16)

    return pl.pallas_call(
        body, name="prenorm", grid=(s // tm,),
        in_specs=[pl.BlockSpec((tm, d), lambda i: (i, 0)), pl.BlockSpec((1, d), lambda i: (0, 0))],
        out_specs=pl.BlockSpec((tm, d), lambda i: (i, 0)),
        out_shape=jax.ShapeDtypeStruct((s, d), BF16),
        compiler_params=_cp(("parallel",)),
    )(x, w)


def _tri(n, cmp, value):
    row = lax.broadcasted_iota(jnp.int32, (n, n), 0)
    col = lax.broadcasted_iota(jnp.int32, (n, n), 1)
    return jnp.where(cmp(row, col), value, 0.0).astype(BF16)


def _tri_dot(tri, x):
    hi = x.astype(BF16)
    lo = (x - hi.astype(F32)).astype(BF16)
    return _nn(tri, hi) + _nn(tri, lo)


def _sb_tile(k, q, mask):
    z = _nt(k, q) * SB_SCALE
    sp = jnp.maximum(z, 0.0) + jnp.log(1.0 + jnp.exp(-jnp.abs(z)))
    if mask is not None:
        sp = jnp.where(mask, sp, 0.0)
    return z, sp


def _tiles(s, tk=TK):
    tq = min(TQ, s)
    return tq, tk, math.gcd(tq // tk, UNROLL)


def _band_masks(tq, tk, fn):
    key = lax.broadcasted_iota(jnp.int32, (tk, tq), 0)
    qry = lax.broadcasted_iota(jnp.int32, (tk, tq), 1)
    return [fn(qry, key + b * tk) for b in range(tq // tk)]


def _row_spec(tq):
    return pl.BlockSpec((8, tq), lambda h, i: (h, i))


def _sb_fwd(projb):
    s = projb.shape[0]
    tq, tk, unroll = _tiles(s)
    nb = tq // tk

    def body(q_ref, k_ref, v_ref, o_ref, tot_ref, acc_ref, car_ref):
        i = pl.program_id(1)
        q = q_ref[...]
        from_here = _tri(tk, lambda s_, j: j >= s_, -1.0)
        masks = _band_masks(tq, tk, lambda t, s_: s_ < t)
        acc_ref[...] = jnp.zeros_like(acc_ref)
        car_ref[...] = jnp.zeros_like(car_ref)

        def tile(j, mask):
            rows = pl.ds(pl.multiple_of(j * tk, tk), tk)
            z, sp = _sb_tile(k_ref[rows, :], q, mask)
            a = jnp.exp(z + _tri_dot(from_here, sp) + car_ref[...])
            if mask is not None:
                a = jnp.where(mask, a, 0.0)
            acc_ref[...] += _tn(v_ref[rows, :], a.astype(BF16))
            car_ref[...] -= jnp.sum(sp, axis=0, keepdims=True)

        for b in reversed(range(nb)):
            tile(i * nb + b, masks[b])

        def step(jj, c):
            for u in range(unroll):
                tile(i * nb - 1 - (jj * unroll + u), None)
            return c

        lax.fori_loop(0, i * nb // unroll, step, 0)
        o_ref[...] = acc_ref[...].T
        tot_ref[...] = jnp.broadcast_to(car_ref[...], (8, tq))

    blk = pl.BlockSpec((tq, HEAD_DIM), lambda h, i: (i, h))
    return pl.pallas_call(
        body, name="sb_fwd", grid=(HEADS, s // tq),
        in_specs=[blk, pl.BlockSpec((s, HEAD_DIM), lambda h, i: (0, HEADS + h)),
                  pl.BlockSpec((s, HEAD_DIM), lambda h, i: (0, 2 * HEADS + h))],
        out_specs=[blk, _row_spec(tq)],
        out_shape=[jax.ShapeDtypeStruct((s, SB_WIDTH), F32), jax.ShapeDtypeStruct((8 * HEADS, s), F32)],
        scratch_shapes=[pltpu.VMEM((HEAD_DIM, tq), F32), pltpu.VMEM((1, tq), F32)],
        compiler_params=_cp(("parallel", "arbitrary")),
    )(projb, projb, projb)


def _mla_prep(projf, qw, kvw, wq, wk, wv, rc, rs1, rs2, tm=512):
    s = projf.shape[0]

    def body(cq_ref, ckv_ref, kr_ref, qw_ref, kvw_ref, wq_ref, wk_ref, wv_ref, c_ref, s1_ref, s2_ref,
             nq_ref, nkv_ref, q_ref, k_ref, v_ref):
        c, s1, s2 = c_ref[...], s1_ref[...], s2_ref[...]
        cq = cq_ref[...]
        nq = ((cq * lax.rsqrt(jnp.mean(cq * cq, axis=1, keepdims=True) + EPS)) * qw_ref[...]).astype(BF16)
        nq_ref[...] = nq
        qf = _nn(nq, wq_ref[...])
        ckv = ckv_ref[...]
        nkv = ((ckv * lax.rsqrt(jnp.mean(ckv * ckv, axis=1, keepdims=True) + EPS)) * kvw_ref[...]).astype(BF16)
        nkv_ref[...] = nkv
        kn = _nn(nkv, wk_ref[...])
        v_ref[...] = _nn(nkv, wv_ref[...]).astype(BF16)
        krot = _rope(kr_ref[...], c, s1, s2).astype(BF16)
        for h in range(HEADS):
            lo = h * MLA_QK_PAD
            q_ref[:, lo:lo + 128] = qf[:, lo:lo + 128].astype(BF16)
            q_ref[:, lo + 128:lo + 256] = _rope(qf[:, lo + 128:lo + 256], c, s1, s2).astype(BF16)
            k_ref[:, lo:lo + 128] = kn[:, h * 128:(h + 1) * 128].astype(BF16)
            k_ref[:, lo + 128:lo + 256] = krot

    row = lambda w, b: pl.BlockSpec((tm, w), lambda i: (i, b))
    full = lambda a: pl.BlockSpec(a.shape, lambda i: (0, 0))
    return pl.pallas_call(
        body, name="mla_prep", grid=(s // tm,),
        in_specs=[row(Q_RANK, 2048 // Q_RANK), row(KV_RANK, 2560 // KV_RANK), row(128, 2816 // 128),
                  full(qw), full(kvw), full(wq), full(wk), full(wv), row(128, 0), row(128, 0), row(128, 0)],
        out_specs=[row(Q_RANK, 0), row(KV_RANK, 0), row(HEADS * MLA_QK_PAD, 0), row(HEADS * MLA_QK_PAD, 0),
                   row(SB_WIDTH, 0)],
        out_shape=[jax.ShapeDtypeStruct((s, Q_RANK), BF16), jax.ShapeDtypeStruct((s, KV_RANK), BF16),
                   jax.ShapeDtypeStruct((s, HEADS * MLA_QK_PAD), BF16),
                   jax.ShapeDtypeStruct((s, HEADS * MLA_QK_PAD), BF16),
                   jax.ShapeDtypeStruct((s, SB_WIDTH), BF16)],
        compiler_params=_cp(("parallel",)),
    )(projf, projf, projf, qw, kvw, wq, wk, wv, rc, rs1, rs2)


def _mla_mask(qry, key):
    return (key // CHUNK) <= (qry // CHUNK)


def _mla_fwd(qm, km, vm):
    s = qm.shape[0]
    tq, tk, unroll = _tiles(s, MLA_TK)
    nb = tq // tk

    def body(q_ref, k_ref, v_ref, o_ref, lse_ref, acc_ref, m_ref, l_ref):
        i = pl.program_id(1)
        q = q_ref[...]
        masks = _band_masks(tq, tk, _mla_mask)
        acc_ref[...] = jnp.zeros_like(acc_ref)
        m_ref[...] = jnp.full_like(m_ref, NEG)
        l_ref[...] = jnp.zeros_like(l_ref)

        def tile(j, mask):
            rows = pl.ds(pl.multiple_of(j * tk, tk), tk)
            sc = _nt(k_ref[rows, :], q) * MLA_SCALE
            if mask is not None:
                sc = jnp.where(mask, sc, NEG)
            m_old = m_ref[...]
            m_new = jnp.maximum(m_old, jnp.max(sc, axis=0, keepdims=True))
            p = jnp.exp(sc - m_new)
            alpha = jnp.exp(m_old - m_new)
            l_ref[...] = alpha * l_ref[...] + jnp.sum(p, axis=0, keepdims=True)
            acc_ref[...] = alpha * acc_ref[...] + _tn(v_ref[rows, :], p.astype(BF16))
            m_ref[...] = m_new

        for b in range(nb):
            tile(i * nb + b, masks[b])

        def step(jj, c):
            for u in range(unroll):
                tile(jj * unroll + u, None)
            return c

        lax.fori_loop(0, i * nb // unroll, step, 0)
        o_ref[...] = (acc_ref[...] / l_ref[...]).T
        lse_ref[...] = jnp.broadcast_to(m_ref[...] + jnp.log(l_ref[...]), (8, tq))

    return pl.pallas_call(
        body, name="mla_fwd", grid=(HEADS, s // tq),
        in_specs=[pl.BlockSpec((tq, MLA_QK_PAD), lambda h, i: (i, h)),
                  pl.BlockSpec((s, MLA_QK_PAD), lambda h, i: (0, h)),
                  pl.BlockSpec((s, HEAD_DIM), lambda h, i: (0, h))],
        out_specs=[pl.BlockSpec((tq, HEAD_DIM), lambda h, i: (i, h)), _row_spec(tq)],
        out_shape=[jax.ShapeDtypeStruct((s, SB_WIDTH), F32), jax.ShapeDtypeStruct((8 * HEADS, s), F32)],
        scratch_shapes=[pltpu.VMEM((HEAD_DIM, tq), F32), pltpu.VMEM((1, tq), F32), pltpu.VMEM((1, tq), F32)],
        compiler_params=_cp(("parallel", "arbitrary")),
    )(qm, km, vm)


def _out_post(oa, ob, projf, wout, x, target, pw, tm=256):
    s, d = x.shape

    def body(oa_ref, ob_ref, ga_ref, gb_ref, w_ref, x_ref, t_ref, pw_ref,
             mix_ref, dy_ref, dout_ref, loss_ref, dpw_ref):
        i = pl.program_id(0)
        sa, _ = _silu_parts(ga_ref[...])
        sb, _ = _silu_parts(gb_ref[...])
        mix_ref[:, :SB_WIDTH] = (oa_ref[...] * sa).astype(BF16)
        mix_ref[:, SB_WIDTH:] = (ob_ref[...] * sb).astype(BF16)
        y = _nn(mix_ref[...], w_ref[...])
        r = lax.rsqrt(jnp.mean(y * y, axis=1, keepdims=True) + EPS)
        yhat = y * r
        pwv = pw_ref[...]
        err = (x_ref[...] + yhat * pwv) - t_ref[...]
        dout = err * (1.0 / d)
        dout_ref[...] = dout
        g = dout * pwv
        dy_ref[...] = (r * (g - yhat * jnp.mean(g * yhat, axis=1, keepdims=True))).astype(BF16)

        @pl.when(i == 0)
        def _():
            loss_ref[...] = jnp.zeros_like(loss_ref)
            dpw_ref[...] = jnp.zeros_like(dpw_ref)

        loss_ref[...] += jnp.sum(err * err, axis=0, keepdims=True)
        dpw_ref[...] += jnp.sum(dout * yhat, axis=0, keepdims=True)

    row = lambda w, b: pl.BlockSpec((tm, w), lambda i: (i, b))
    vec = pl.BlockSpec((1, d), lambda i: (0, 0))
    return pl.pallas_call(
        body, name="out_post", grid=(s // tm,),
        in_specs=[row(SB_WIDTH, 0), row(SB_WIDTH, 0), row(SB_WIDTH, 0), row(SB_WIDTH, 1),
                  pl.BlockSpec(wout.shape, lambda i: (0, 0)), row(d, 0), row(d, 0), vec],
        out_specs=[row(d, 0), row(d, 0), row(d, 0), vec, vec],
        out_shape=[jax.ShapeDtypeStruct((s, d), BF16), jax.ShapeDtypeStruct((s, d), BF16),
                   jax.ShapeDtypeStruct((s, d), F32), jax.ShapeDtypeStruct((1, d), F32),
                   jax.ShapeDtypeStruct((1, d), F32)],
        compiler_params=_cp(("arbitrary",)),
    )(oa, ob, projf, projf, wout, x, target, pw)


def _sb_bwd(projb, projf, dmixed, oa, tot):
    s = projb.shape[0]
    tq, tk, unroll = _tiles(s)
    nb, nq = tq // tk, s // tq

    def body(q_ref, k_ref, v_ref, dm_ref, g_ref, o_ref, tot_ref, dq_ref, dk_ref, dv_ref, dg_ref,
             dka_ref, dva_ref, dqa_ref, pre_ref, cg_ref):
        i = pl.program_id(1)

        @pl.when(i == 0)
        def _():
            dka_ref[...] = jnp.zeros_like(dka_ref)
            dva_ref[...] = jnp.zeros_like(dva_ref)

        q = q_ref[...]
        silu, dsilu = _silu_parts(g_ref[...])
        dm = dm_ref[...]
        dg_ref[...] = (dm * o_ref[...] * dsilu).astype(BF16)
        do = (dm * silu).astype(BF16)
        tot = tot_ref[0:1, :]
        from_here = _tri(tk, lambda s_, j: j >= s_, -1.0)
        up_to_here = _tri(tk, lambda s_, j: j <= s_, 1.0)
        masks = _band_masks(tq, tk, lambda t, s_: s_ < t)
        dqa_ref[...] = jnp.zeros_like(dqa_ref)
        pre_ref[...] = jnp.zeros_like(pre_ref)
        cg_ref[...] = jnp.zeros_like(cg_ref)

        def tile(j, mask):
            rows = pl.ds(pl.multiple_of(j * tk, tk), tk)
            k = k_ref[rows, :]
            z, sp = _sb_tile(k, q, mask)
            pre = pre_ref[...] - jnp.sum(sp, axis=0, keepdims=True)
            pre_ref[...] = pre
            a = jnp.exp(z + _tri_dot(from_here, sp) + (tot - pre))
            if mask is not None:
                a = jnp.where(mask, a, 0.0)
            g = a * _nt(v_ref[rows, :], do)
            dva_ref[rows, :] += _nn(a.astype(BF16), do)
            cum = _nn(up_to_here, g.astype(BF16)) + cg_ref[...]
            dz = (g - jnp.exp(z - sp) * cum) * SB_SCALE
            if mask is not None:
                dz = jnp.where(mask, dz, 0.0)
            dz = dz.astype(BF16)
            dqa_ref[...] += _tn(k, dz)
            dka_ref[rows, :] += _nn(dz, q)
            cg_ref[...] += jnp.sum(g, axis=0, keepdims=True)

        def step(jj, c):
            for u in range(unroll):
                tile(jj * unroll + u, None)
            return c

        lax.fori_loop(0, i * nb // unroll, step, 0)
        for b in range(nb):
            tile(i * nb + b, masks[b])
        dq_ref[...] = dqa_ref[...].T.astype(BF16)

        @pl.when(i == nq - 1)
        def _():
            dk_ref[...] = dka_ref[...].astype(BF16)
            dv_ref[...] = dva_ref[...].astype(BF16)

    blk = lambda off: pl.BlockSpec((tq, HEAD_DIM), lambda h, i: (i, off + h))
    whole = lambda off: pl.BlockSpec((s, HEAD_DIM), lambda h, i: (0, off + h))
    o_sd = jax.ShapeDtypeStruct((s, SB_WIDTH), BF16)
    return pl.pallas_call(
        body, name="sb_bwd", grid=(HEADS, nq),
        in_specs=[blk(0), whole(HEADS), whole(2 * HEADS), blk(0), blk(0), blk(0), _row_spec(tq)],
        out_specs=[blk(0), whole(0), whole(0), blk(0)],
        out_shape=[o_sd, o_sd, o_sd, o_sd],
        scratch_shapes=[pltpu.VMEM((s, HEAD_DIM), F32), pltpu.VMEM((s, HEAD_DIM), F32),
                        pltpu.VMEM((HEAD_DIM, tq), F32), pltpu.VMEM((1, tq), F32), pltpu.VMEM((1, tq), F32)],
        compiler_params=_cp(("parallel", "arbitrary")),
    )(projb, projb, projb, dmixed, projf, oa, tot)


def _mla_bwd(qm, km, vm, projf, dmixed, ob, lse):
    s = qm.shape[0]
    tq, tk, unroll = _tiles(s, MLA_TK)
    nb, nq = tq // tk, s // tq

    def body(q_ref, k_ref, v_ref, dm_ref, g_ref, o_ref, lse_ref, dq_ref, dk_ref, dv_ref, dg_ref,
             dva_ref, dqa_ref):
        i = pl.program_id(1)

        @pl.when(i == 0)
        def _():
            dk_ref[...] = jnp.zeros_like(dk_ref)
            dva_ref[...] = jnp.zeros_like(dva_ref)

        q = q_ref[...]
        silu, dsilu = _silu_parts(g_ref[...])
        dm = dm_ref[...]
        o = o_ref[...]
        dg_ref[...] = (dm * o * dsilu).astype(BF16)
        dof = dm * silu
        delta = jnp.sum((dof * o).T, axis=0, keepdims=True)
        do = dof.astype(BF16)
        lse = lse_ref[0:1, :]
        masks = _band_masks(tq, tk, _mla_mask)
        dqa_ref[...] = jnp.zeros_like(dqa_ref)

        def tile(j, mask):
            rows = pl.ds(pl.multiple_of(j * tk, tk), tk)
            k = k_ref[rows, :]
            p = jnp.exp(_nt(k, q) * MLA_SCALE - lse)
            if mask is not None:
                p = jnp.where(mask, p, 0.0)
            ds = (p * (_nt(v_ref[rows, :], do) - delta) * MLA_SCALE).astype(BF16)
            dva_ref[rows, :] += _nn(p.astype(BF16), do)
            dk_ref[rows, :] += _nn(ds, q)
            dqa_ref[...] += _tn(k, ds)

        def step(jj, c):
            for u in range(unroll):
                tile(jj * unroll + u, None)
            return c

        lax.fori_loop(0, i * nb // unroll, step, 0)
        for b in range(nb):
            tile(i * nb + b, masks[b])
        dq_ref[...] = dqa_ref[...].T

        @pl.when(i == nq - 1)
        def _():
            dv_ref[...] = dva_ref[...].astype(BF16)

    blk = lambda w, off: pl.BlockSpec((tq, w), lambda h, i: (i, off + h))
    whole = lambda w: pl.BlockSpec((s, w), lambda h, i: (0, h))
    return pl.pallas_call(
        body, name="mla_bwd", grid=(HEADS, nq),
        in_specs=[blk(MLA_QK_PAD, 0), whole(MLA_QK_PAD), whole(HEAD_DIM), blk(HEAD_DIM, HEADS),
                  blk(HEAD_DIM, HEADS), blk(HEAD_DIM, 0), _row_spec(tq)],
        out_specs=[blk(MLA_QK_PAD, 0), whole(MLA_QK_PAD), whole(HEAD_DIM), blk(HEAD_DIM, 0)],
        out_shape=[jax.ShapeDtypeStruct((s, HEADS * MLA_QK_PAD), F32),
                   jax.ShapeDtypeStruct((s, HEADS * MLA_QK_PAD), F32),
                   jax.ShapeDtypeStruct((s, SB_WIDTH), BF16), jax.ShapeDtypeStruct((s, SB_WIDTH), BF16)],
        scratch_shapes=[pltpu.VMEM((s, HEAD_DIM), F32), pltpu.VMEM((MLA_QK_PAD, tq), F32)],
        compiler_params=_cp(("parallel", "arbitrary")),
    )(qm, km, vm, dmixed, projf, ob, lse)


def _norm_bwd(x, w, dn):
    r = lax.rsqrt(jnp.mean(x * x, axis=1, keepdims=True) + EPS)
    xhat = x * r
    g = dn * w
    return r * (g - xhat * jnp.mean(g * xhat, axis=1, keepdims=True)), dn * xhat


def _mla_bwd_post(dqm, dkm, dvm, projf, qw, kvw, wq, wk, wv, rc, rs1, rs2, tm=256):
    s = dqm.shape[0]

    def body(dq_ref, dk_ref, dv_ref, cq_ref, ckv_ref, qw_ref, kvw_ref, wq_ref, wk_ref, wv_ref,
             c_ref, s1_ref, s2_ref, dqp_ref, dkn_ref, dcq_ref, dckv_ref, dkr_ref, dqw_ref, dkvw_ref):
        i = pl.program_id(0)
        c, s1, s2 = c_ref[...], s1_ref[...], s2_ref[...]
        drot = jnp.zeros((tm, 128), F32)
        for h in range(HEADS):
            lo = h * MLA_QK_PAD
            dqp_ref[:, lo:lo + 128] = dq_ref[:, lo:lo + 128].astype(BF16)
            dqp_ref[:, lo + 128:lo + 256] = _rope_t(dq_ref[:, lo + 128:lo + 256], c, s1, s2).astype(BF16)
            dkn_ref[:, h * 128:(h + 1) * 128] = dk_ref[:, lo:lo + 128].astype(BF16)
            drot = drot + dk_ref[:, lo + 128:lo + 256]
        dkr_ref[...] = _rope_t(drot, c, s1, s2).astype(BF16)
        dcq, dqw = _norm_bwd(cq_ref[...], qw_ref[...], _nt(dqp_ref[...], wq_ref[...]))
        dcq_ref[...] = dcq.astype(BF16)
        dnkv = _nt(dkn_ref[...], wk_ref[...]) + _nt(dv_ref[...], wv_ref[...])
        dckv, dkvw = _norm_bwd(ckv_ref[...], kvw_ref[...], dnkv)
        dckv_ref[...] = dckv.astype(BF16)

        @pl.when(i == 0)
        def _():
            dqw_ref[...] = jnp.zeros_like(dqw_ref)
            dkvw_ref[...] = jnp.zeros_like(dkvw_ref)

        dqw_ref[...] += jnp.sum(dqw, axis=0, keepdims=True)
        dkvw_ref[...] += jnp.sum(dkvw, axis=0, keepdims=True)

    row = lambda w, b: pl.BlockSpec((tm, w), lambda i: (i, b))
    full = lambda a: pl.BlockSpec(a.shape, lambda i: (0, 0))
    sd = jax.ShapeDtypeStruct
    return pl.pallas_call(
        body, name="mla_bwd_post", grid=(s // tm,),
        in_specs=[row(HEADS * MLA_QK_PAD, 0), row(HEADS * MLA_QK_PAD, 0), row(SB_WIDTH, 0),
                  row(Q_RANK, 2048 // Q_RANK), row(KV_RANK, 2560 // KV_RANK),
                  full(qw), full(kvw), full(wq), full(wk), full(wv), row(128, 0), row(128, 0), row(128, 0)],
        out_specs=[row(HEADS * MLA_QK_PAD, 0), row(SB_WIDTH, 0), row(Q_RANK, 0), row(KV_RANK, 0), row(128, 0),
                   pl.BlockSpec((1, Q_RANK), lambda i: (0, 0)), pl.BlockSpec((1, KV_RANK), lambda i: (0, 0))],
        out_shape=[sd((s, HEADS * MLA_QK_PAD), BF16), sd((s, SB_WIDTH), BF16), sd((s, Q_RANK), BF16),
                   sd((s, KV_RANK), BF16), sd((s, 128), BF16), sd((1, Q_RANK), F32), sd((1, KV_RANK), F32)],
        compiler_params=_cp(("arbitrary",)),
    )(dqm, dkm, dvm, projf, projf, qw, kvw, wq, wk, wv, rc, rs1, rs2)


def _pre_bwd(dproj, win, x, pw, dout, tm=512, tk=1024):
    s, d = x.shape
    nk = dproj.shape[1] // tk

    def body(dp_ref, w_ref, x_ref, pw_ref, do_ref, dx_ref, dpw_ref, acc_ref):
        i, k = pl.program_id(0), pl.program_id(1)
        part = _nt(dp_ref[...], w_ref[...])

        @pl.when(k == 0)
        def _():
            acc_ref[...] = part

        @pl.when(k > 0)
        def _():
            acc_ref[...] += part

        @pl.when(k == nk - 1)
        def _():
            dx, dw = _norm_bwd(x_ref[...], pw_ref[...], acc_ref[...])
            dx_ref[...] = do_ref[...] + dx

            @pl.when(i == 0)
            def _():
                dpw_ref[...] = jnp.zeros_like(dpw_ref)

            dpw_ref[...] += jnp.sum(dw, axis=0, keepdims=True)

    rowd = pl.BlockSpec((tm, d), lambda i, k: (i, 0))
    vec = pl.BlockSpec((1, d), lambda i, k: (0, 0))
    return pl.pallas_call(
        body, name="pre_bwd", grid=(s // tm, nk),
        in_specs=[pl.BlockSpec((tm, tk), lambda i, k: (i, k)), pl.BlockSpec((d, tk), lambda i, k: (0, k)),
                  rowd, vec, rowd],
        out_specs=[rowd, vec],
        out_shape=[jax.ShapeDtypeStruct((s, d), F32), jax.ShapeDtypeStruct((1, d), F32)],
        scratch_shapes=[pltpu.VMEM((tm, d), F32)],
        compiler_params=_cp(("arbitrary", "arbitrary")),
    )(dproj, win, x, pw, dout)


def _adamw(w, g, m, v):
    m = ADAM_B1 * m + (1.0 - ADAM_B1) * g
    v = ADAM_B2 * v + (1.0 - ADAM_B2) * (g * g)
    delta = -ADAM_LR * ((m / ADAM_C1) / (jnp.sqrt(v / ADAM_C2) + ADAM_EPS) + ADAM_WD * w)
    return delta, m, v


def _sum_adamw(core, own, got, w, m, v, tr=1728):
    n, r, lanes = got.shape

    def body(c_ref, o_ref, p_ref, w_ref, m_ref, v_ref, g_ref, d_ref, nm_ref, nv_ref):
        g = o_ref[...]
        for k in range(n):
            g = g + p_ref[k].astype(F32)
        g_ref[0] = g
        d_ref[0], nm_ref[0], nv_ref[0] = _adamw(w_ref[...], g, m_ref[...], v_ref[...])

    blk = pl.BlockSpec((tr, lanes), lambda i, c: (i, 0))
    out = pl.BlockSpec((1, tr, lanes), lambda i, c: (c[0], i, 0))
    sd = jax.ShapeDtypeStruct((2, r, lanes), F32)
    return pl.pallas_call(
        body, name="sum_adamw",
        grid_spec=pltpu.PrefetchScalarGridSpec(
            num_scalar_prefetch=1, grid=(r // tr,),
            in_specs=[blk, pl.BlockSpec((n, tr, lanes), lambda i, c: (0, i, 0)), blk, blk, blk],
            out_specs=[out, out, out, out]),
        out_shape=[sd, sd, sd, sd],
        compiler_params=_cp(("parallel",)),
    )(core, own, got, w, m, v)


def _pair_sum(chip, mine, got, tr=1728):
    n, r, lanes = mine.shape

    def body(c_ref, a_ref, b_ref, oa_ref, ob_ref, all_ref, own_ref):
        all_ref[...] = (a_ref[...] + b_ref[...]).astype(BF16)
        own_ref[...] = oa_ref[0] + ob_ref[0]

    blk = pl.BlockSpec((n, tr, lanes), lambda i, c: (0, i, 0))
    one = pl.BlockSpec((1, tr, lanes), lambda i, c: (c[0], i, 0))
    return pl.pallas_call(
        body, name="pair_sum",
        grid_spec=pltpu.PrefetchScalarGridSpec(
            num_scalar_prefetch=1, grid=(r // tr,), in_specs=[blk, blk, one, one],
            out_specs=[blk, pl.BlockSpec((tr, lanes), lambda i, c: (i, 0))]),
        out_shape=[jax.ShapeDtypeStruct(mine.shape, BF16), jax.ShapeDtypeStruct((r, lanes), F32)],
        compiler_params=_cp(("parallel",)),
    )(chip, mine, got, mine, got)


def _place():
    return lax.axis_index("x"), lax.axis_index("y"), lax.axis_index("c")


ANY = pl.BlockSpec(memory_space=pl.ANY)


def _gather_weights(packed):
    r, lanes = packed.shape
    half = r // 2

    def body(w_ref, out_ref, send_sems, recv_sems):
        x, y, c = _place()
        chips = [(1 - x, y), (x, 1 - y), (1 - x, 1 - y)]

        def rows(chip, core):
            return out_ref.at[2 * chip[0] + chip[1], pl.ds(core * half, half), :]

        def copy(k, chip, core, to, src=None):
            return pltpu.make_async_remote_copy(
                src_ref=rows(chip, core) if src is None else src, dst_ref=rows(chip, core),
                send_sem=send_sems.at[k], recv_sem=recv_sems.at[k], device_id=to, device_id_type=MESH)

        first = [copy(j, (x, y), c, (*chip, c), src=w_ref.at[pl.ds(c * half, half), :])
                 for j, chip in enumerate(chips)]
        for cp in first:
            cp.start()
        passed = [copy(3 + j, chip, c, (x, y, 1 - c)) for j, chip in enumerate(chips)]
        for j, chip in enumerate(chips):
            copy(j, chip, c, (x, y, c)).wait_recv()
            passed[j].start()
        for j, chip in enumerate(chips):
            copy(3 + j, chip, 1 - c, (x, y, c)).wait_recv()
        for cp in first + passed:
            cp.wait_send()

    return pl.pallas_call(
        body, name="gather_weights", in_specs=[ANY], out_specs=ANY,
        out_shape=jax.ShapeDtypeStruct((4, r, lanes), packed.dtype),
        scratch_shapes=[pltpu.SemaphoreType.DMA((6,)), pltpu.SemaphoreType.DMA((6,))],
        compiler_params=pltpu.CompilerParams(has_side_effects=True),
    )(packed)


def _swap_halves(grads):
    n, _, r, lanes = grads.shape

    def body(g_ref, got_ref, send_sems, recv_sems):
        x, y, c = _place()
        copies = [pltpu.make_async_remote_copy(
            src_ref=g_ref.at[k, 1 - c], dst_ref=got_ref.at[k], send_sem=send_sems.at[k], recv_sem=recv_sems.at[k],
            device_id=(x, y, 1 - c), device_id_type=MESH) for k in range(n)]
        for cp in copies:
            cp.start()
        for cp in copies:
            cp.wait()

    return pl.pallas_call(
        body, name="swap_halves", in_specs=[ANY], out_specs=ANY,
        out_shape=jax.ShapeDtypeStruct((n, r, lanes), grads.dtype),
        scratch_shapes=[pltpu.SemaphoreType.DMA((n,)), pltpu.SemaphoreType.DMA((n,))],
        compiler_params=pltpu.CompilerParams(has_side_effects=True),
    )(grads)


def _exchange_chips(psum):
    n, r, lanes = psum.shape

    def body(p_ref, got_ref, send_sems, recv_sems):
        x, y, c = _place()
        chips = [(1 - x, y), (x, 1 - y), (1 - x, 1 - y)]
        copies = [pltpu.make_async_remote_copy(
            src_ref=p_ref.at[2 * chip[0] + chip[1]], dst_ref=got_ref.at[j],
            send_sem=send_sems.at[j], recv_sem=recv_sems.at[j], device_id=(*chip, c), device_id_type=MESH)
            for j, chip in enumerate(chips)]
        for cp in copies:
            cp.start()
        for cp in copies:
            cp.wait()

    return pl.pallas_call(
        body, name="exchange_chips", in_specs=[ANY], out_specs=ANY,
        out_shape=jax.ShapeDtypeStruct((3, r, lanes), psum.dtype),
        scratch_shapes=[pltpu.SemaphoreType.DMA((3,)), pltpu.SemaphoreType.DMA((3,))],
        compiler_params=pltpu.CompilerParams(has_side_effects=True),
    )(psum)


def _share_with_sibling(arrs):
    na = len(arrs)

    def body(*refs):
        outs = refs[na:2 * na]
        send_sems, recv_sems = refs[2 * na:]
        x, y, c = _place()
        copies = [pltpu.make_async_remote_copy(
            src_ref=outs[k].at[c], dst_ref=outs[k].at[c], send_sem=send_sems.at[k], recv_sem=recv_sems.at[k],
            device_id=(x, y, 1 - c), device_id_type=MESH) for k in range(na)]
        for cp in copies:
            cp.start()
        for cp in copies:
            cp.wait()

    return pl.pallas_call(
        body, name="share_with_sibling", in_specs=[ANY] * na, out_specs=[ANY] * na,
        out_shape=[jax.ShapeDtypeStruct(a.shape, a.dtype) for a in arrs],
        input_output_aliases={k: k for k in range(na)},
        scratch_shapes=[pltpu.SemaphoreType.DMA((na,)), pltpu.SemaphoreType.DMA((na,))],
        compiler_params=pltpu.CompilerParams(has_side_effects=True),
    )(*arrs)


def _norm_allreduce_adamw(part, w, m, v):
    r, lanes = part.shape

    def body(p_ref, w_ref, m_ref, v_ref, g_ref, d_ref, nm_ref, nv_ref, all_ref, send_sems, recv_sems):
        x, y, c = _place()
        me = 4 * x + 2 * y + c
        all_ref[me] = p_ref[...]
        copies = []
        for k in range(1, 8):
            peer = (x ^ (k >> 2), y ^ ((k >> 1) & 1), c ^ (k & 1))
            copies.append(pltpu.make_async_remote_copy(
                src_ref=p_ref, dst_ref=all_ref.at[me], send_sem=send_sems.at[k - 1], recv_sem=recv_sems.at[k - 1],
                device_id=peer, device_id_type=MESH))
        for cp in copies:
            cp.start()
        for cp in copies:
            cp.wait()
        g = all_ref[0]
        for k in range(1, 8):
            g = g + all_ref[k]
        g_ref[...] = g
        d_ref[...], nm_ref[...], nv_ref[...] = _adamw(w_ref[...], g, m_ref[...], v_ref[...])

    vm = pl.BlockSpec(memory_space=pltpu.VMEM)
    sd = jax.ShapeDtypeStruct((r, lanes), F32)
    return pl.pallas_call(
        body, name="norm_allreduce_adamw", in_specs=[vm] * 4, out_specs=[vm] * 4, out_shape=[sd] * 4,
        scratch_shapes=[pltpu.VMEM((8, r, lanes), F32), pltpu.SemaphoreType.DMA((7,)),
                        pltpu.SemaphoreType.DMA((7,))],
        compiler_params=pltpu.CompilerParams(has_side_effects=True),
    )(part, w, m, v)


W_SHAPES = {"w_in": (D_MODEL, D_IN // 4), "w_q_up": (Q_RANK, 1536 // 4), "w_kv_up": (KV_RANK, 2048 // 4),
            "w_out": (D_MODEL // 4, D_MODEL)}
W_NAMES = ("w_in", "w_q_up", "w_kv_up", "w_out")
LANES = 128


def _pack_shard(ws):
    return jnp.concatenate([w.reshape(-1, LANES) for w in ws], axis=0)


def _unpack_shard(packed):
    out, at = [], 0
    for name in W_NAMES:
        rows = W_SHAPES[name][0] * W_SHAPES[name][1] // LANES
        out.append(packed[..., at:at + rows, :].reshape(packed.shape[:-2] + W_SHAPES[name]))
        at += rows
    return out


def _pack_halves(ws, c):
    parts = []
    for w in ws:
        hr = w.shape[0] // 2
        parts.append(lax.dynamic_slice_in_dim(w, c * hr, hr, axis=0).reshape(-1, LANES))
    return jnp.concatenate(parts, axis=0)


def _unpack_halves(both):
    out, at = [], 0
    for name in W_NAMES:
        rws, cols = W_SHAPES[name]
        n = rws * cols // 2 // LANES
        out.append(both[:, at:at + n, :].reshape(rws, cols))
        at += n
    return out


def _perm_in(w):
    return jnp.concatenate([w[:, :4096], w[:, 4928:5952], w[:, 4096:4928],
                            jnp.zeros((w.shape[0], D_INP - D_IN), w.dtype)], axis=1)


def _unperm_in(w):
    return jnp.concatenate([w[:, :4096], w[:, 5120:5952], w[:, 4096:5120]], axis=1)


def _grad_chunks(dwin, dwq, dwkv, dwout):
    def per(g, axis):
        sh = jnp.stack(jnp.split(g, 4, axis=axis))
        return sh.reshape(4, 2, -1, LANES)
    return jnp.concatenate([per(dwin, 1), per(dwq, 1), per(dwkv, 1), per(dwout, 0)], axis=2)


NORM_NAMES = ("pre_norm_w", "q_norm_w", "kv_norm_w", "post_norm_w")
NORM_SIZES = (D_MODEL, Q_RANK, KV_RANK, D_MODEL)
NORM_ROWS = 40


def _pack_norm(vs):
    flat = jnp.concatenate([v.reshape(-1) for v in vs])
    return jnp.pad(flat, (0, NORM_ROWS * LANES - flat.shape[0])).reshape(NORM_ROWS, LANES)


def _unpack_norm(p):
    flat, out, at = p.reshape(-1), [], 0
    for n in NORM_SIZES:
        out.append(flat[at:at + n].reshape(1, n))
        at += n
    return out


def _rope_tables(positions):
    inv_freq = ROPE_THETA ** (-jnp.arange(0, MLA_ROPE, 2, dtype=F32) / MLA_ROPE)
    ang = positions.astype(F32)[:, None] * inv_freq
    cos, sin, z = jnp.cos(ang), jnp.sin(ang), jnp.zeros_like(ang)
    return (jnp.concatenate([cos, cos, z, z], axis=1), jnp.concatenate([z, sin, z, z], axis=1),
            jnp.concatenate([-sin, z, z, z], axis=1))


def _local_step(x, positions, pre_w, win, q_w, wq, kv_w, wk, wv, wout, post_w, target):
    s = x.shape[0]
    rc, rs1, rs2 = _rope_tables(positions)
    h = _prenorm(x, pre_w)
    projb = _matmul(h, win, mode="nn", out_dtype=BF16, tm=512, tn=1024, tk=D_MODEL, name="proj_b", n=PB_W)
    projf = _matmul(h, win, mode="nn", out_dtype=F32, tm=512, tn=1024, tk=D_MODEL, name="proj_f", n=PF_W,
                    b_off=PB_W // 1024)
    oa, tot = _sb_fwd(projb)
    nq, nkv, qm, km, vm = _mla_prep(projf, q_w, kv_w, wq, wk, wv, rc, rs1, rs2)
    ob, lse = _mla_fwd(qm, km, vm)
    mixed, dy, dout, err2, dpost = _out_post(oa, ob, projf, wout, x, target, post_w)

    dwout = _matmul(mixed, dy, mode="tn", out_dtype=F32, tm=1024, tn=1024, tk=min(1024, s), name="dw_out")
    dmixed = _matmul(dy, wout, mode="nt", out_dtype=F32, tm=512, tn=1024, tk=D_MODEL, name="d_mixed")
    dqa, dka, dva, dga = _sb_bwd(projb, projf, dmixed, oa, tot)
    dqm, dkm, dvm, dgb = _mla_bwd(qm, km, vm, projf, dmixed, ob, lse)
    dqp, dkn, dcq, dckv, dkr, dqw, dkvw = _mla_bwd_post(dqm, dkm, dvm, projf, q_w, kv_w, wq, wk, wv, rc, rs1, rs2)
    tks = min(1024, s)
    dwq = _matmul(nq, dqp, mode="tn", out_dtype=F32, tm=Q_RANK, tn=1024, tk=tks, name="dw_q")
    dwk = _matmul(nkv, dkn, mode="tn", out_dtype=F32, tm=KV_RANK, tn=1024, tk=tks, name="dw_k")
    dwv = _matmul(nkv, dvm, mode="tn", out_dtype=F32, tm=KV_RANK, tn=1024, tk=tks, name="dw_v")
    dproj = jnp.concatenate([dqa, dka, dva, dga, dgb, dcq, dckv, dkr, jnp.zeros((s, 128), BF16)], axis=1)
    dwin = _matmul(h, dproj, mode="tn", out_dtype=F32, tm=1024, tn=1024, tk=tks, name="dw_in")
    gx, dpre = _pre_bwd(dproj, win, x, pre_w, dout)
    return err2, gx, dpre, dwin, dqw, dwq, dkvw, dwk, dwv, dwout, dpost


def _kernel_layouts(full):
    w_in, w_q_up, w_kv_up, w_out = full
    win = _perm_in(w_in)
    wq = jnp.pad(w_q_up.reshape(Q_RANK, HEADS, 192), ((0, 0), (0, 0), (0, 64))).reshape(Q_RANK, HEADS * MLA_QK_PAD)
    kv = w_kv_up.reshape(KV_RANK, HEADS, 256)
    wk = kv[:, :, :128].reshape(KV_RANK, SB_WIDTH)
    wv = kv[:, :, 128:].reshape(KV_RANK, SB_WIDTH)
    return win, wq, wk, wv, w_out


def _original_layouts(dwin, dwq, dwk, dwv):
    dwi = _unperm_in(dwin[:, :D_IN])
    dq = dwq.reshape(Q_RANK, HEADS, MLA_QK_PAD)[:, :, :192].reshape(Q_RANK, HEADS * 192)
    dkv = jnp.concatenate([dwk.reshape(KV_RANK, HEADS, 128), dwv.reshape(KV_RANK, HEADS, 128)], axis=2)
    return dwi, dq, dkv.reshape(KV_RANK, 2 * SB_WIDTH)


def kernel(x, positions, pre_norm_w, w_in, q_norm_w, w_q_up, kv_norm_w, w_kv_up, w_out, post_norm_w, loss_target, m_pre_norm_w, m_w_in, m_q_norm_w, m_w_q_up, m_kv_norm_w, m_w_kv_up, m_w_out, m_post_norm_w, v_pre_norm_w, v_w_in, v_q_norm_w, v_w_q_up, v_kv_norm_w, v_w_kv_up, v_w_out, v_post_norm_w):
    c = lax.axis_index("c")
    chip = 2 * lax.axis_index("x") + lax.axis_index("y")
    shards = (w_in[0], w_q_up[0], w_kv_up[0], w_out[0])
    packed = _pack_shard([w.astype(BF16) for w in shards])
    gathered = lax.dynamic_update_slice(_gather_weights(packed), packed[None], (chip, 0, 0))
    g_in, g_q, g_kv, g_out = _unpack_shard(gathered)
    cat = lambda g: jnp.concatenate([g[0], g[1], g[2], g[3]], axis=1)
    full = (cat(g_in), cat(g_q), cat(g_kv), g_out.reshape(D_MODEL, D_MODEL))
    win, wq, wk, wv, wout = _kernel_layouts(full)

    err2, gx, dpre, dwin, dqw, dwq, dkvw, dwk, dwv, dwout, dpost = _local_step(
        x[0], positions[0], pre_norm_w, win, q_norm_w, wq, kv_norm_w, wk, wv, wout, post_norm_w, loss_target[0])
    loss = lax.psum(0.5 * jnp.sum(err2) / D_MODEL, ("x", "y", "c"))

    dwi, dq, dkv = _original_layouts(dwin, dwq, dwk, dwv)
    chunks = _grad_chunks(dwi, dq, dkv, dwout)
    mine = lax.dynamic_index_in_dim(chunks, c, axis=1, keepdims=False)
    pair, own = _pair_sum(chip.reshape(1), mine, _swap_halves(chunks))
    got = _exchange_chips(pair)
    wh, mh, vh = (_pack_halves(ws, c) for ws in (
        shards, (m_w_in[0], m_w_q_up[0], m_w_kv_up[0], m_w_out[0]), (v_w_in[0], v_w_q_up[0], v_w_kv_up[0], v_w_out[0])))
    both = _share_with_sibling(list(_sum_adamw(c.reshape(1), own, got, wh, mh, vh)))
    big = [_unpack_halves(b) for b in both]

    small = _norm_allreduce_adamw(
        _pack_norm([dpre, dqw, dkvw, dpost]), _pack_norm([pre_norm_w, q_norm_w, kv_norm_w, post_norm_w]),
        _pack_norm([m_pre_norm_w, m_q_norm_w, m_kv_norm_w, m_post_norm_w]),
        _pack_norm([v_pre_norm_w, v_q_norm_w, v_kv_norm_w, v_post_norm_w]))
    small = [_unpack_norm(p) for p in small]

    def group(k):
        n, b = small[k], big[k]
        return (n[0], b[0][None], n[1], b[1][None], n[2], b[2][None], b[3][None], n[3])

    return (loss, gx[None], *group(0), *group(1), *group(2), *group(3))
```

```python
import functools
import math

import numpy as np
import jax
import jax.numpy as jnp
from jax import lax
from jax.experimental import pallas as pl
from jax.experimental.pallas import tpu as pltpu

F32 = jnp.float32
BF16 = jnp.bfloat16
MESH = pl.DeviceIdType.MESH

D_MODEL = 2048
HEADS = 8
HEAD_DIM = 128
SB_WIDTH = HEADS * HEAD_DIM
MLA_ROPE = 64
MLA_QK_PAD = 256
Q_RANK = 512
KV_RANK = 256
CHUNK = 64
EPS = 1e-6
ROPE_THETA = 10000.0
D_IN = 5952
D_INP = 6144
PB_W = 3072
PF_W = D_INP - PB_W
SB_SCALE = 1.0 / math.sqrt(HEAD_DIM)
MLA_SCALE = 1.0 / math.sqrt(HEAD_DIM + MLA_ROPE)
NEG = -1e30

ADAM_LR, ADAM_B1, ADAM_B2, ADAM_EPS, ADAM_WD, ADAM_STEP = 0.001, 0.9, 0.999, 1e-08, 0.01, 10
ADAM_C1 = 1.0 - ADAM_B1 ** ADAM_STEP
ADAM_C2 = 1.0 - ADAM_B2 ** ADAM_STEP

VMEM_LIMIT = 56 * 1024 * 1024
TQ = 1024
TK = 256
MLA_TK = 512
UNROLL = 2


def _cp(sem=None, **kw):
    return pltpu.CompilerParams(dimension_semantics=sem, vmem_limit_bytes=VMEM_LIMIT, **kw)


def _dot(a, b, dims):
    return lax.dot_general(a, b, (dims, ((), ())), preferred_element_type=F32)


def _nn(a, b):
    return _dot(a, b, ((1,), (0,)))


def _nt(a, b):
    return _dot(a, b, ((1,), (1,)))


def _tn(a, b):
    return _dot(a, b, ((0,), (0,)))


def _rope(x, c, s1, s2):
    return x * c + pltpu.roll(x, 32, 1) * s1 + pltpu.roll(x, 96, 1) * s2


def _rope_t(x, c, s1, s2):
    return x * c - pltpu.roll(x, 32, 1) * s1 - pltpu.roll(x, 96, 1) * s2


def _silu_parts(g):
    sg = jax.nn.sigmoid(g)
    return g * sg, sg * (1.0 + g * (1.0 - sg))


def _matmul(a, b, *, mode, out_dtype, tm, tn, tk, name, n=None, b_off=0):
    if mode == "tn":
        kk, m = a.shape
        n = b.shape[1] if n is None else n
    else:
        m, kk = a.shape
        n = (b.shape[1] if mode == "nn" else b.shape[0]) if n is None else n
    nk = kk // tk
    a_spec = {"nn": pl.BlockSpec((tm, tk), lambda j, i, k: (i, k)),
              "nt": pl.BlockSpec((tm, tk), lambda j, i, k: (i, k)),
              "tn": pl.BlockSpec((tk, tm), lambda j, i, k: (k, i))}[mode]
    b_spec = {"nn": pl.BlockSpec((tk, tn), lambda j, i, k: (k, j + b_off)),
              "nt": pl.BlockSpec((tn, tk), lambda j, i, k: (j, k)),
              "tn": pl.BlockSpec((tk, tn), lambda j, i, k: (k, j))}[mode]
    dims = {"nn": ((1,), (0,)), "nt": ((1,), (1,)), "tn": ((0,), (0,))}[mode]

    def body(a_ref, b_ref, o_ref, acc_ref):
        k = pl.program_id(2)
        part = _dot(a_ref[...], b_ref[...], dims)
        if nk == 1:
            o_ref[...] = part.astype(out_dtype)
        else:
            @pl.when(k == 0)
            def _():
                acc_ref[...] = part

            @pl.when(k > 0)
            def _():
                acc_ref[...] += part

            @pl.when(k == nk - 1)
            def _():
                o_ref[...] = acc_ref[...].astype(out_dtype)

    return pl.pallas_call(
        body, name=name, grid=(n // tn, m // tm, nk),
        in_specs=[a_spec, b_spec], out_specs=pl.BlockSpec((tm, tn), lambda j, i, k: (i, j)),
        out_shape=jax.ShapeDtypeStruct((m, n), out_dtype),
        scratch_shapes=[pltpu.VMEM((tm, tn) if nk > 1 else (8, 128), F32)],
        compiler_params=_cp(("parallel", "parallel", "arbitrary")),
    )(a, b)


def _prenorm(x, w, tm=512):
    s, d = x.shape

    def body(x_ref, w_ref, h_ref):
        xv = x_ref[...]
        r = lax.rsqrt(jnp.mean(xv * xv, axis=1, keepdims=True) + EPS)
        h_ref[...] = ((xv * r) * w_ref[...]).astype(BF16)

    return pl.pallas_call(
        body, name="prenorm", grid=(s // tm,),
        in_specs=[pl.BlockSpec((tm, d), lambda i: (i, 0)), pl.BlockSpec((1, d), lambda i: (0, 0))],
        out_specs=pl.BlockSpec((tm, d), lambda i: (i, 0)),
        out_shape=jax.ShapeDtypeStruct((s, d), BF16),
        compiler_params=_cp(("parallel",)),
    )(x, w)


def _tri(n, cmp, value):
    row = lax.broadcasted_iota(jnp.int32, (n, n), 0)
    col = lax.broadcasted_iota(jnp.int32, (n, n), 1)
    return jnp.where(cmp(row, col), value, 0.0).astype(BF16)


def _sb_tile(k, q, mask):
    z = _nt(k, q) * SB_SCALE
    sp = jnp.maximum(z, 0.0) + jnp.log(1.0 + jnp.exp(-jnp.abs(z)))
    if mask is not None:
        sp = jnp.where(mask, sp, 0.0)
    return z, sp


def _tiles(s, tk=TK):
    tq = min(TQ, s)
    return tq, tk, math.gcd(tq // tk, UNROLL)


def _band_masks(tq, tk, fn):
    out = []
    for b in range(tq // tk):
        key = lax.broadcasted_iota(jnp.int32, (tk, tq - b * tk), 0) + b * tk
        qry = lax.broadcasted_iota(jnp.int32, (tk, tq - b * tk), 1) + b * tk
        out.append(fn(qry, key))
    return out


def _row_spec(tq):
    return pl.BlockSpec((8, tq), lambda h, i: (h, i))


def _sb_fwd(projb):
    s = projb.shape[0]
    tq, tk, unroll = _tiles(s)
    nb = tq // tk

    def body(q_ref, k_ref, v_ref, o_ref, tot_ref, acc_ref, car_ref):
        i = pl.program_id(1)
        q = q_ref[...]
        from_here = _tri(tk, lambda s_, j: j >= s_, -1.0)
        masks = _band_masks(tq, tk, lambda t, s_: s_ < t)
        acc_ref[...] = jnp.zeros_like(acc_ref)
        car_ref[...] = jnp.zeros_like(car_ref)

        def tile(j, mask, off):
            rows = pl.ds(pl.multiple_of(j * tk, tk), tk)
            z, sp = _sb_tile(k_ref[rows, :], q[off:, :], mask)
            a = jnp.exp(z + _nn(from_here, sp.astype(BF16)) + car_ref[:, off:])
            if mask is not None:
                a = jnp.where(mask, a, 0.0)
            acc_ref[:, off:] += _tn(v_ref[rows, :], a.astype(BF16))
            car_ref[:, off:] -= jnp.sum(sp, axis=0, keepdims=True)

        for b in reversed(range(nb)):
            tile(i * nb + b, masks[b], b * tk)

        def step(jj, c):
            for u in range(unroll):
                tile(i * nb - 1 - (jj * unroll + u), None, 0)
            return c

        lax.fori_loop(0, i * nb // unroll, step, 0)
        o_ref[...] = acc_ref[...].T
        tot_ref[...] = jnp.broadcast_to(car_ref[...], (8, tq))

    blk = pl.BlockSpec((tq, HEAD_DIM), lambda h, i: (i, h))
    return pl.pallas_call(
        body, name="sb_fwd", grid=(HEADS, s // tq),
        in_specs=[blk, pl.BlockSpec((s, HEAD_DIM), lambda h, i: (0, HEADS + h)),
                  pl.BlockSpec((s, HEAD_DIM), lambda h, i: (0, 2 * HEADS + h))],
        out_specs=[blk, _row_spec(tq)],
        out_shape=[jax.ShapeDtypeStruct((s, SB_WIDTH), F32), jax.ShapeDtypeStruct((8 * HEADS, s), F32)],
        scratch_shapes=[pltpu.VMEM((HEAD_DIM, tq), F32), pltpu.VMEM((1, tq), F32)],
        compiler_params=_cp(("parallel", "arbitrary")),
    )(projb, projb, projb)


def _mla_prep(projf, qw, kvw, wq, wk, wv, rc, rs1, rs2, tm=512):
    s = projf.shape[0]

    def body(cq_ref, ckv_ref, kr_ref, qw_ref, kvw_ref, wq_ref, wk_ref, wv_ref, c_ref, s1_ref, s2_ref,
             nq_ref, nkv_ref, q_ref, k_ref, v_ref):
        c, s1, s2 = c_ref[...], s1_ref[...], s2_ref[...]
        cq = cq_ref[...]
        nq = ((cq * lax.rsqrt(jnp.mean(cq * cq, axis=1, keepdims=True) + EPS)) * qw_ref[...]).astype(BF16)
        nq_ref[...] = nq
        qf = _nn(nq, wq_ref[...])
        ckv = ckv_ref[...]
        nkv = ((ckv * lax.rsqrt(jnp.mean(ckv * ckv, axis=1, keepdims=True) + EPS)) * kvw_ref[...]).astype(BF16)
        nkv_ref[...] = nkv
        kn = _nn(nkv, wk_ref[...])
        v_ref[...] = _nn(nkv, wv_ref[...]).astype(BF16)
        krot = _rope(kr_ref[...], c, s1, s2).astype(BF16)
        for h in range(HEADS):
            lo = h * MLA_QK_PAD
            q_ref[:, lo:lo + 128] = qf[:, lo:lo + 128].astype(BF16)
            q_ref[:, lo + 128:lo + 256] = _rope(qf[:, lo + 128:lo + 256], c, s1, s2).astype(BF16)
            k_ref[:, lo:lo + 128] = kn[:, h * 128:(h + 1) * 128].astype(BF16)
            k_ref[:, lo + 128:lo + 256] = krot

    row = lambda w, b: pl.BlockSpec((tm, w), lambda i: (i, b))
    full = lambda a: pl.BlockSpec(a.shape, lambda i: (0, 0))
    return pl.pallas_call(
        body, name="mla_prep", grid=(s // tm,),
        in_specs=[row(Q_RANK, 2048 // Q_RANK), row(KV_RANK, 2560 // KV_RANK), row(128, 2816 // 128),
                  full(qw), full(kvw), full(wq), full(wk), full(wv), row(128, 0), row(128, 0), row(128, 0)],
        out_specs=[row(Q_RANK, 0), row(KV_RANK, 0), row(HEADS * MLA_QK_PAD, 0), row(HEADS * MLA_QK_PAD, 0),
                   row(SB_WIDTH, 0)],
        out_shape=[jax.ShapeDtypeStruct((s, Q_RANK), BF16), jax.ShapeDtypeStruct((s, KV_RANK), BF16),
                   jax.ShapeDtypeStruct((s, HEADS * MLA_QK_PAD), BF16),
                   jax.ShapeDtypeStruct((s, HEADS * MLA_QK_PAD), BF16),
                   jax.ShapeDtypeStruct((s, SB_WIDTH), BF16)],
        compiler_params=_cp(("parallel",)),
    )(projf, projf, projf, qw, kvw, wq, wk, wv, rc, rs1, rs2)


def _mla_mask(qry, key):
    return (key // CHUNK) <= (qry // CHUNK)


def _mla_fwd(qm, km, vm):
    s = qm.shape[0]
    tq, tk, unroll = _tiles(s, MLA_TK)
    nb = tq // tk

    def body(q_ref, k_ref, v_ref, o_ref, lse_ref, acc_ref, m_ref, l_ref):
        i = pl.program_id(1)
        q = q_ref[...]
        masks = _band_masks(tq, tk, _mla_mask)
        acc_ref[...] = jnp.zeros_like(acc_ref)
        m_ref[...] = jnp.full_like(m_ref, NEG)
        l_ref[...] = jnp.zeros_like(l_ref)

        def tile(j, mask, off):
            rows = pl.ds(pl.multiple_of(j * tk, tk), tk)
            sc = _nt(k_ref[rows, :], q[off:, :]) * MLA_SCALE
            if mask is not None:
                sc = jnp.where(mask, sc, NEG)
            m_old = m_ref[:, off:]
            m_new = jnp.maximum(m_old, jnp.max(sc, axis=0, keepdims=True))
            p = jnp.exp(sc - m_new)
            alpha = jnp.exp(m_old - m_new)
            l_ref[:, off:] = alpha * l_ref[:, off:] + jnp.sum(p, axis=0, keepdims=True)
            acc_ref[:, off:] = alpha * acc_ref[:, off:] + _tn(v_ref[rows, :], p.astype(BF16))
            m_ref[:, off:] = m_new

        for b in range(nb):
            tile(i * nb + b, masks[b], b * tk)

        def step(jj, c):
            for u in range(unroll):
                tile(jj * unroll + u, None, 0)
            return c

        lax.fori_loop(0, i * nb // unroll, step, 0)
        o_ref[...] = (acc_ref[...] / l_ref[...]).T
        lse_ref[...] = jnp.broadcast_to(m_ref[...] + jnp.log(l_ref[...]), (8, tq))

    return pl.pallas_call(
        body, name="mla_fwd", grid=(HEADS, s // tq),
        in_specs=[pl.BlockSpec((tq, MLA_QK_PAD), lambda h, i: (i, h)),
                  pl.BlockSpec((s, MLA_QK_PAD), lambda h, i: (0, h)),
                  pl.BlockSpec((s, HEAD_DIM), lambda h, i: (0, h))],
        out_specs=[pl.BlockSpec((tq, HEAD_DIM), lambda h, i: (i, h)), _row_spec(tq)],
        out_shape=[jax.ShapeDtypeStruct((s, SB_WIDTH), F32), jax.ShapeDtypeStruct((8 * HEADS, s), F32)],
        scratch_shapes=[pltpu.VMEM((HEAD_DIM, tq), F32), pltpu.VMEM((1, tq), F32), pltpu.VMEM((1, tq), F32)],
        compiler_params=_cp(("parallel", "arbitrary")),
    )(qm, km, vm)


def _out_post(oa, ob, projf, wout, x, target, pw, tm=256):
    s, d = x.shape

    def body(oa_ref, ob_ref, ga_ref, gb_ref, w_ref, x_ref, t_ref, pw_ref,
             mix_ref, dy_ref, dout_ref, loss_ref, dpw_ref):
        i = pl.program_id(0)
        sa, _ = _silu_parts(ga_ref[...])
        sb, _ = _silu_parts(gb_ref[...])
        mix_ref[:, :SB_WIDTH] = (oa_ref[...] * sa).astype(BF16)
        mix_ref[:, SB_WIDTH:] = (ob_ref[...] * sb).astype(BF16)
        y = _nn(mix_ref[...], w_ref[...])
        r = lax.rsqrt(jnp.mean(y * y, axis=1, keepdims=True) + EPS)
        yhat = y * r
        pwv = pw_ref[...]
        err = (x_ref[...] + yhat * pwv) - t_ref[...]
        dout = err * (1.0 / d)
        dout_ref[...] = dout
        g = dout * pwv
        dy_ref[...] = (r * (g - yhat * jnp.mean(g * yhat, axis=1, keepdims=True))).astype(BF16)

        @pl.when(i == 0)
        def _():
            loss_ref[...] = jnp.zeros_like(loss_ref)
            dpw_ref[...] = jnp.zeros_like(dpw_ref)

        loss_ref[...] += jnp.sum(err * err, axis=0, keepdims=True)
        dpw_ref[...] += jnp.sum(dout * yhat, axis=0, keepdims=True)

    row = lambda w, b: pl.BlockSpec((tm, w), lambda i: (i, b))
    vec = pl.BlockSpec((1, d), lambda i: (0, 0))
    return pl.pallas_call(
        body, name="out_post", grid=(s // tm,),
        in_specs=[row(SB_WIDTH, 0), row(SB_WIDTH, 0), row(SB_WIDTH, 0), row(SB_WIDTH, 1),
                  pl.BlockSpec(wout.shape, lambda i: (0, 0)), row(d, 0), row(d, 0), vec],
        out_specs=[row(d, 0), row(d, 0), row(d, 0), vec, vec],
        out_shape=[jax.ShapeDtypeStruct((s, d), BF16), jax.ShapeDtypeStruct((s, d), BF16),
                   jax.ShapeDtypeStruct((s, d), F32), jax.ShapeDtypeStruct((1, d), F32),
                   jax.ShapeDtypeStruct((1, d), F32)],
        compiler_params=_cp(("arbitrary",)),
    )(oa, ob, projf, projf, wout, x, target, pw)


def _sb_bwd(projb, projf, dmixed, oa, tot):
    s = projb.shape[0]
    tq, tk, unroll = _tiles(s)
    nb, nq = tq // tk, s // tq

    def body(q_ref, k_ref, v_ref, dm_ref, g_ref, o_ref, tot_ref, dq_ref, dk_ref, dv_ref, dg_ref,
             dka_ref, dva_ref, dqa_ref, pre_ref, cg_ref):
        i = pl.program_id(1)

        @pl.when(i == 0)
        def _():
            dka_ref[...] = jnp.zeros_like(dka_ref)
            dva_ref[...] = jnp.zeros_like(dva_ref)

        q = q_ref[...]
        silu, dsilu = _silu_parts(g_ref[...])
        dm = dm_ref[...]
        dg_ref[...] = (dm * o_ref[...] * dsilu).astype(BF16)
        do = (dm * silu).astype(BF16)
        tot = tot_ref[0:1, :]
        from_here = _tri(tk, lambda s_, j: j >= s_, -1.0)
        up_to_here = _tri(tk, lambda s_, j: j <= s_, 1.0)
        masks = _band_masks(tq, tk, lambda t, s_: s_ < t)
        dqa_ref[...] = jnp.zeros_like(dqa_ref)
        pre_ref[...] = jnp.zeros_like(pre_ref)
        cg_ref[...] = jnp.zeros_like(cg_ref)

        def tile(j, mask, off):
            rows = pl.ds(pl.multiple_of(j * tk, tk), tk)
            k = k_ref[rows, :]
            z, sp = _sb_tile(k, q[off:, :], mask)
            pre = pre_ref[:, off:] - jnp.sum(sp, axis=0, keepdims=True)
            pre_ref[:, off:] = pre
            a = jnp.exp(z + _nn(from_here, sp.astype(BF16)) + (tot[:, off:] - pre))
            if mask is not None:
                a = jnp.where(mask, a, 0.0)
            g = a * _nt(v_ref[rows, :], do[off:, :])
            dva_ref[rows, :] += _nn(a.astype(BF16), do[off:, :])
            cum = _nn(up_to_here, g.astype(BF16)) + cg_ref[:, off:]
            dz = (g - jnp.exp(z - sp) * cum) * SB_SCALE
            if mask is not None:
                dz = jnp.where(mask, dz, 0.0)
            dz = dz.astype(BF16)
            dqa_ref[:, off:] += _tn(k, dz)
            dka_ref[rows, :] += _nn(dz, q[off:, :])
            cg_ref[:, off:] += jnp.sum(g, axis=0, keepdims=True)

        def step(jj, c):
            for u in range(unroll):
                tile(jj * unroll + u, None, 0)
            return c

        lax.fori_loop(0, i * nb // unroll, step, 0)
        for b in range(nb):
            tile(i * nb + b, masks[b], b * tk)
        dq_ref[...] = dqa_ref[...].T.astype(BF16)

        @pl.when(i == nq - 1)
        def _():
            dk_ref[...] = dka_ref[...].astype(BF16)
            dv_ref[...] = dva_ref[...].astype(BF16)

    blk = lambda off: pl.BlockSpec((tq, HEAD_DIM), lambda h, i: (i, off + h))
    whole = lambda off: pl.BlockSpec((s, HEAD_DIM), lambda h, i: (0, off + h))
    o_sd = jax.ShapeDtypeStruct((s, SB_WIDTH), BF16)
    return pl.pallas_call(
        body, name="sb_bwd", grid=(HEADS, nq),
        in_specs=[blk(0), whole(HEADS), whole(2 * HEADS), blk(0), blk(0), blk(0), _row_spec(tq)],
        out_specs=[blk(0), whole(0), whole(0), blk(0)],
        out_shape=[o_sd, o_sd, o_sd, o_sd],
        scratch_shapes=[pltpu.VMEM((s, HEAD_DIM), F32), pltpu.VMEM((s, HEAD_DIM), F32),
                        pltpu.VMEM((HEAD_DIM, tq), F32), pltpu.VMEM((1, tq), F32), pltpu.VMEM((1, tq), F32)],
        compiler_params=_cp(("parallel", "arbitrary")),
    )(projb, projb, projb, dmixed, projf, oa, tot)


def _mla_bwd(qm, km, vm, projf, dmixed, ob, lse):
    s = qm.shape[0]
    tq, tk, unroll = _tiles(s, MLA_TK)
    nb, nq = tq // tk, s // tq

    def body(q_ref, k_ref, v_ref, dm_ref, g_ref, o_ref, lse_ref, dq_ref, dk_ref, dv_ref, dg_ref,
             dva_ref, dqa_ref):
        i = pl.program_id(1)

        @pl.when(i == 0)
        def _():
            dk_ref[...] = jnp.zeros_like(dk_ref)
            dva_ref[...] = jnp.zeros_like(dva_ref)

        q = q_ref[...]
        silu, dsilu = _silu_parts(g_ref[...])
        dm = dm_ref[...]
        o = o_ref[...]
        dg_ref[...] = (dm * o * dsilu).astype(BF16)
        dof = dm * silu
        delta = jnp.sum((dof * o).T, axis=0, keepdims=True)
        do = dof.astype(BF16)
        lse = lse_ref[0:1, :]
        masks = _band_masks(tq, tk, _mla_mask)
        dqa_ref[...] = jnp.zeros_like(dqa_ref)

        def tile(j, mask, off):
            rows = pl.ds(pl.multiple_of(j * tk, tk), tk)
            k = k_ref[rows, :]
            p = jnp.exp(_nt(k, q[off:, :]) * MLA_SCALE - lse[:, off:])
            if mask is not None:
                p = jnp.where(mask, p, 0.0)
            ds = (p * (_nt(v_ref[rows, :], do[off:, :]) - delta[:, off:]) * MLA_SCALE).astype(BF16)
            dva_ref[rows, :] += _nn(p.astype(BF16), do[off:, :])
            dk_ref[rows, :] += _nn(ds, q[off:, :])
            dqa_ref[:, off:] += _tn(k, ds)

        def step(jj, c):
            for u in range(unroll):
                tile(jj * unroll + u, None, 0)
            return c

        lax.fori_loop(0, i * nb // unroll, step, 0)
        for b in range(nb):
            tile(i * nb + b, masks[b], b * tk)
        dq_ref[...] = dqa_ref[...].T

        @pl.when(i == nq - 1)
        def _():
            dv_ref[...] = dva_ref[...].astype(BF16)

    blk = lambda w, off: pl.BlockSpec((tq, w), lambda h, i: (i, off + h))
    whole = lambda w: pl.BlockSpec((s, w), lambda h, i: (0, h))
    return pl.pallas_call(
        body, name="mla_bwd", grid=(HEADS, nq),
        in_specs=[blk(MLA_QK_PAD, 0), whole(MLA_QK_PAD), whole(HEAD_DIM), blk(HEAD_DIM, HEADS),
                  blk(HEAD_DIM, HEADS), blk(HEAD_DIM, 0), _row_spec(tq)],
        out_specs=[blk(MLA_QK_PAD, 0), whole(MLA_QK_PAD), whole(HEAD_DIM), blk(HEAD_DIM, 0)],
        out_shape=[jax.ShapeDtypeStruct((s, HEADS * MLA_QK_PAD), F32),
                   jax.ShapeDtypeStruct((s, HEADS * MLA_QK_PAD), F32),
                   jax.ShapeDtypeStruct((s, SB_WIDTH), BF16), jax.ShapeDtypeStruct((s, SB_WIDTH), BF16)],
        scratch_shapes=[pltpu.VMEM((s, HEAD_DIM), F32), pltpu.VMEM((MLA_QK_PAD, tq), F32)],
        compiler_params=_cp(("parallel", "arbitrary")),
    )(qm, km, vm, dmixed, projf, ob, lse)


def _norm_bwd(x, w, dn):
    r = lax.rsqrt(jnp.mean(x * x, axis=1, keepdims=True) + EPS)
    xhat = x * r
    g = dn * w
    return r * (g - xhat * jnp.mean(g * xhat, axis=1, keepdims=True)), dn * xhat


def _mla_bwd_post(dqm, dkm, dvm, projf, qw, kvw, wq, wk, wv, rc, rs1, rs2, tm=256):
    s = dqm.shape[0]

    def body(dq_ref, dk_ref, dv_ref, cq_ref, ckv_ref, qw_ref, kvw_ref, wq_ref, wk_ref, wv_ref,
             c_ref, s1_ref, s2_ref, dqp_ref, dkn_ref, dcq_ref, dckv_ref, dkr_ref, dqw_ref, dkvw_ref):
        i = pl.program_id(0)
        c, s1, s2 = c_ref[...], s1_ref[...], s2_ref[...]
        drot = jnp.zeros((tm, 128), F32)
        for h in range(HEADS):
            lo = h * MLA_QK_PAD
            dqp_ref[:, lo:lo + 128] = dq_ref[:, lo:lo + 128].astype(BF16)
            dqp_ref[:, lo + 128:lo + 256] = _rope_t(dq_ref[:, lo + 128:lo + 256], c, s1, s2).astype(BF16)
            dkn_ref[:, h * 128:(h + 1) * 128] = dk_ref[:, lo:lo + 128].astype(BF16)
            drot = drot + dk_ref[:, lo + 128:lo + 256]
        dkr_ref[...] = _rope_t(drot, c, s1, s2).astype(BF16)
        dcq, dqw = _norm_bwd(cq_ref[...], qw_ref[...], _nt(dqp_ref[...], wq_ref[...]))
        dcq_ref[...] = dcq.astype(BF16)
        dnkv = _nt(dkn_ref[...], wk_ref[...]) + _nt(dv_ref[...], wv_ref[...])
        dckv, dkvw = _norm_bwd(ckv_ref[...], kvw_ref[...], dnkv)
        dckv_ref[...] = dckv.astype(BF16)

        @pl.when(i == 0)
        def _():
            dqw_ref[...] = jnp.zeros_like(dqw_ref)
            dkvw_ref[...] = jnp.zeros_like(dkvw_ref)

        dqw_ref[...] += jnp.sum(dqw, axis=0, keepdims=True)
        dkvw_ref[...] += jnp.sum(dkvw, axis=0, keepdims=True)

    row = lambda w, b: pl.BlockSpec((tm, w), lambda i: (i, b))
    full = lambda a: pl.BlockSpec(a.shape, lambda i: (0, 0))
    sd = jax.ShapeDtypeStruct
    return pl.pallas_call(
        body, name="mla_bwd_post", grid=(s // tm,),
        in_specs=[row(HEADS * MLA_QK_PAD, 0), row(HEADS * MLA_QK_PAD, 0), row(SB_WIDTH, 0),
                  row(Q_RANK, 2048 // Q_RANK), row(KV_RANK, 2560 // KV_RANK),
                  full(qw), full(kvw), full(wq), full(wk), full(wv), row(128, 0), row(128, 0), row(128, 0)],
        out_specs=[row(HEADS * MLA_QK_PAD, 0), row(SB_WIDTH, 0), row(Q_RANK, 0), row(KV_RANK, 0), row(128, 0),
                   pl.BlockSpec((1, Q_RANK), lambda i: (0, 0)), pl.BlockSpec((1, KV_RANK), lambda i: (0, 0))],
        out_shape=[sd((s, HEADS * MLA_QK_PAD), BF16), sd((s, SB_WIDTH), BF16), sd((s, Q_RANK), BF16),
                   sd((s, KV_RANK), BF16), sd((s, 128), BF16), sd((1, Q_RANK), F32), sd((1, KV_RANK), F32)],
        compiler_params=_cp(("arbitrary",)),
    )(dqm, dkm, dvm, projf, projf, qw, kvw, wq, wk, wv, rc, rs1, rs2)


def _pre_bwd(dproj, win, x, pw, dout, tm=512, tk=1024):
    s, d = x.shape
    nk = dproj.shape[1] // tk

    def body(dp_ref, w_ref, x_ref, pw_ref, do_ref, dx_ref, dpw_ref, acc_ref):
        i, k = pl.program_id(0), pl.program_id(1)
        part = _nt(dp_ref[...], w_ref[...])

        @pl.when(k == 0)
        def _():
            acc_ref[...] = part

        @pl.when(k > 0)
        def _():
            acc_ref[...] += part

        @pl.when(k == nk - 1)
        def _():
            dx, dw = _norm_bwd(x_ref[...], pw_ref[...], acc_ref[...])
            dx_ref[...] = do_ref[...] + dx

            @pl.when(i == 0)
            def _():
                dpw_ref[...] = jnp.zeros_like(dpw_ref)

            dpw_ref[...] += jnp.sum(dw, axis=0, keepdims=True)

    rowd = pl.BlockSpec((tm, d), lambda i, k: (i, 0))
    vec = pl.BlockSpec((1, d), lambda i, k: (0, 0))
    return pl.pallas_call(
        body, name="pre_bwd", grid=(s // tm, nk),
        in_specs=[pl.BlockSpec((tm, tk), lambda i, k: (i, k)), pl.BlockSpec((d, tk), lambda i, k: (0, k)),
                  rowd, vec, rowd],
        out_specs=[rowd, vec],
        out_shape=[jax.ShapeDtypeStruct((s, d), F32), jax.ShapeDtypeStruct((1, d), F32)],
        scratch_shapes=[pltpu.VMEM((tm, d), F32)],
        compiler_params=_cp(("arbitrary", "arbitrary")),
    )(dproj, win, x, pw, dout)


def _adamw(w, g, m, v):
    m = ADAM_B1 * m + (1.0 - ADAM_B1) * g
    v = ADAM_B2 * v + (1.0 - ADAM_B2) * (g * g)
    delta = -ADAM_LR * ((m / ADAM_C1) / (jnp.sqrt(v / ADAM_C2) + ADAM_EPS) + ADAM_WD * w)
    return delta, m, v


def _sum_adamw(core, own, got, w, m, v, tr=1728):
    n, r, lanes = got.shape

    def body(c_ref, o_ref, p_ref, w_ref, m_ref, v_ref, g_ref, d_ref, nm_ref, nv_ref):
        g = o_ref[...]
        for k in range(n):
            g = g + p_ref[k].astype(F32)
        g_ref[0] = g
        d_ref[0], nm_ref[0], nv_ref[0] = _adamw(w_ref[...], g, m_ref[...], v_ref[...])

    blk = pl.BlockSpec((tr, lanes), lambda i, c: (i, 0))
    out = pl.BlockSpec((1, tr, lanes), lambda i, c: (c[0], i, 0))
    sd = jax.ShapeDtypeStruct((2, r, lanes), F32)
    return pl.pallas_call(
        body, name="sum_adamw",
        grid_spec=pltpu.PrefetchScalarGridSpec(
            num_scalar_prefetch=1, grid=(r // tr,),
            in_specs=[blk, pl.BlockSpec((n, tr, lanes), lambda i, c: (0, i, 0)), blk, blk, blk],
            out_specs=[out, out, out, out]),
        out_shape=[sd, sd, sd, sd],
        compiler_params=_cp(("parallel",)),
    )(core, own, got, w, m, v)


def _pair_sum(chip, mine, got, tr=1728):
    n, r, lanes = mine.shape

    def body(c_ref, a_ref, b_ref, oa_ref, ob_ref, all_ref, own_ref):
        all_ref[...] = (a_ref[...] + b_ref[...]).astype(BF16)
        own_ref[...] = oa_ref[0] + ob_ref[0]

    blk = pl.BlockSpec((n, tr, lanes), lambda i, c: (0, i, 0))
    one = pl.BlockSpec((1, tr, lanes), lambda i, c: (c[0], i, 0))
    return pl.pallas_call(
        body, name="pair_sum",
        grid_spec=pltpu.PrefetchScalarGridSpec(
            num_scalar_prefetch=1, grid=(r // tr,), in_specs=[blk, blk, one, one],
            out_specs=[blk, pl.BlockSpec((tr, lanes), lambda i, c: (i, 0))]),
        out_shape=[jax.ShapeDtypeStruct(mine.shape, BF16), jax.ShapeDtypeStruct((r, lanes), F32)],
        compiler_params=_cp(("parallel",)),
    )(chip, mine, got, mine, got)


def _place():
    return lax.axis_index("x"), lax.axis_index("y"), lax.axis_index("c")


ANY = pl.BlockSpec(memory_space=pl.ANY)


def _gather_weights(packed):
    r, lanes = packed.shape
    half = r // 2

    def body(w_ref, out_ref, send_sems, recv_sems):
        x, y, c = _place()
        chips = [(1 - x, y), (x, 1 - y), (1 - x, 1 - y)]

        def rows(chip, core):
            return out_ref.at[2 * chip[0] + chip[1], pl.ds(core * half, half), :]

        def copy(k, chip, core, to, src=None):
            return pltpu.make_async_remote_copy(
                src_ref=rows(chip, core) if src is None else src, dst_ref=rows(chip, core),
                send_sem=send_sems.at[k], recv_sem=recv_sems.at[k], device_id=to, device_id_type=MESH)

        first = [copy(j, (x, y), c, (*chip, c), src=w_ref.at[pl.ds(c * half, half), :])
                 for j, chip in enumerate(chips)]
        for cp in first:
            cp.start()
        passed = [copy(3 + j, chip, c, (x, y, 1 - c)) for j, chip in enumerate(chips)]
        for j, chip in enumerate(chips):
            copy(j, chip, c, (x, y, c)).wait_recv()
            passed[j].start()
        for j, chip in enumerate(chips):
            copy(3 + j, chip, 1 - c, (x, y, c)).wait_recv()
        for cp in first + passed:
            cp.wait_send()

    return pl.pallas_call(
        body, name="gather_weights", in_specs=[ANY], out_specs=ANY,
        out_shape=jax.ShapeDtypeStruct((4, r, lanes), packed.dtype),
        scratch_shapes=[pltpu.SemaphoreType.DMA((6,)), pltpu.SemaphoreType.DMA((6,))],
        compiler_params=pltpu.CompilerParams(has_side_effects=True),
    )(packed)


def _swap_halves(grads):
    n, _, r, lanes = grads.shape

    def body(g_ref, got_ref, send_sems, recv_sems):
        x, y, c = _place()
        copies = [pltpu.make_async_remote_copy(
            src_ref=g_ref.at[k, 1 - c], dst_ref=got_ref.at[k], send_sem=send_sems.at[k], recv_sem=recv_sems.at[k],
            device_id=(x, y, 1 - c), device_id_type=MESH) for k in range(n)]
        for cp in copies:
            cp.start()
        for cp in copies:
            cp.wait()

    return pl.pallas_call(
        body, name="swap_halves", in_specs=[ANY], out_specs=ANY,
        out_shape=jax.ShapeDtypeStruct((n, r, lanes), grads.dtype),
        scratch_shapes=[pltpu.SemaphoreType.DMA((n,)), pltpu.SemaphoreType.DMA((n,))],
        compiler_params=pltpu.CompilerParams(has_side_effects=True),
    )(grads)


def _exchange_chips(psum):
    n, r, lanes = psum.shape

    def body(p_ref, got_ref, send_sems, recv_sems):
        x, y, c = _place()
        chips = [(1 - x, y), (x, 1 - y), (1 - x, 1 - y)]
        copies = [pltpu.make_async_remote_copy(
            src_ref=p_ref.at[2 * chip[0] + chip[1]], dst_ref=got_ref.at[j],
            send_sem=send_sems.at[j], recv_sem=recv_sems.at[j], device_id=(*chip, c), device_id_type=MESH)
            for j, chip in enumerate(chips)]
        for cp in copies:
            cp.start()
        for cp in copies:
            cp.wait()

    return pl.pallas_call(
        body, name="exchange_chips", in_specs=[ANY], out_specs=ANY,
        out_shape=jax.ShapeDtypeStruct((3, r, lanes), psum.dtype),
        scratch_shapes=[pltpu.SemaphoreType.DMA((3,)), pltpu.SemaphoreType.DMA((3,))],
        compiler_params=pltpu.CompilerParams(has_side_effects=True),
    )(psum)


def _share_with_sibling(arrs):
    na = len(arrs)

    def body(*refs):
        outs = refs[na:2 * na]
        send_sems, recv_sems = refs[2 * na:]
        x, y, c = _place()
        copies = [pltpu.make_async_remote_copy(
            src_ref=outs[k].at[c], dst_ref=outs[k].at[c], send_sem=send_sems.at[k], recv_sem=recv_sems.at[k],
            device_id=(x, y, 1 - c), device_id_type=MESH) for k in range(na)]
        for cp in copies:
            cp.start()
        for cp in copies:
            cp.wait()

    return pl.pallas_call(
        body, name="share_with_sibling", in_specs=[ANY] * na, out_specs=[ANY] * na,
        out_shape=[jax.ShapeDtypeStruct(a.shape, a.dtype) for a in arrs],
        input_output_aliases={k: k for k in range(na)},
        scratch_shapes=[pltpu.SemaphoreType.DMA((na,)), pltpu.SemaphoreType.DMA((na,))],
        compiler_params=pltpu.CompilerParams(has_side_effects=True),
    )(*arrs)


def _norm_allreduce_adamw(part, w, m, v):
    r, lanes = part.shape

    def body(p_ref, w_ref, m_ref, v_ref, g_ref, d_ref, nm_ref, nv_ref, all_ref, send_sems, recv_sems):
        x, y, c = _place()
        me = 4 * x + 2 * y + c
        all_ref[me] = p_ref[...]
        copies = []
        for k in range(1, 8):
            peer = (x ^ (k >> 2), y ^ ((k >> 1) & 1), c ^ (k & 1))
            copies.append(pltpu.make_async_remote_copy(
                src_ref=p_ref, dst_ref=all_ref.at[me], send_sem=send_sems.at[k - 1], recv_sem=recv_sems.at[k - 1],
                device_id=peer, device_id_type=MESH))
        for cp in copies:
            cp.start()
        for cp in copies:
            cp.wait()
        g = all_ref[0]
        for k in range(1, 8):
            g = g + all_ref[k]
        g_ref[...] = g
        d_ref[...], nm_ref[...], nv_ref[...] = _adamw(w_ref[...], g, m_ref[...], v_ref[...])

    vm = pl.BlockSpec(memory_space=pltpu.VMEM)
    sd = jax.ShapeDtypeStruct((r, lanes), F32)
    return pl.pallas_call(
        body, name="norm_allreduce_adamw", in_specs=[vm] * 4, out_specs=[vm] * 4, out_shape=[sd] * 4,
        scratch_shapes=[pltpu.VMEM((8, r, lanes), F32), pltpu.SemaphoreType.DMA((7,)),
                        pltpu.SemaphoreType.DMA((7,))],
        compiler_params=pltpu.CompilerParams(has_side_effects=True),
    )(part, w, m, v)


W_SHAPES = {"w_in": (D_MODEL, D_IN // 4), "w_q_up": (Q_RANK, 1536 // 4), "w_kv_up": (KV_RANK, 2048 // 4),
            "w_out": (D_MODEL // 4, D_MODEL)}
W_NAMES = ("w_in", "w_q_up", "w_kv_up", "w_out")
LANES = 128


def _pack_shard(ws):
    return jnp.concatenate([w.reshape(-1, LANES) for w in ws], axis=0)


def _unpack_shard(packed):
    out, at = [], 0
    for name in W_NAMES:
        rows = W_SHAPES[name][0] * W_SHAPES[name][1] // LANES
        out.append(packed[..., at:at + rows, :].reshape(packed.shape[:-2] + W_SHAPES[name]))
        at += rows
    return out


def _pack_halves(ws, c):
    parts = []
    for w in ws:
        hr = w.shape[0] // 2
        parts.append(lax.dynamic_slice_in_dim(w, c * hr, hr, axis=0).reshape(-1, LANES))
    return jnp.concatenate(parts, axis=0)


def _unpack_halves(both):
    out, at = [], 0
    for name in W_NAMES:
        rws, cols = W_SHAPES[name]
        n = rws * cols // 2 // LANES
        out.append(both[:, at:at + n, :].reshape(rws, cols))
        at += n
    return out


def _perm_in(w):
    return jnp.concatenate([w[:, :4096], w[:, 4928:5952], w[:, 4096:4928],
                            jnp.zeros((w.shape[0], D_INP - D_IN), w.dtype)], axis=1)


def _unperm_in(w):
    return jnp.concatenate([w[:, :4096], w[:, 5120:5952], w[:, 4096:5120]], axis=1)


def _grad_chunks(dwin, dwq, dwkv, dwout):
    def per(g, axis):
        sh = jnp.stack(jnp.split(g, 4, axis=axis))
        return sh.reshape(4, 2, -1, LANES)
    return jnp.concatenate([per(dwin, 1), per(dwq, 1), per(dwkv, 1), per(dwout, 0)], axis=2)


NORM_NAMES = ("pre_norm_w", "q_norm_w", "kv_norm_w", "post_norm_w")
NORM_SIZES = (D_MODEL, Q_RANK, KV_RANK, D_MODEL)
NORM_ROWS = 40


def _pack_norm(vs):
    flat = jnp.concatenate([v.reshape(-1) for v in vs])
    return jnp.pad(flat, (0, NORM_ROWS * LANES - flat.shape[0])).reshape(NORM_ROWS, LANES)


def _unpack_norm(p):
    flat, out, at = p.reshape(-1), [], 0
    for n in NORM_SIZES:
        out.append(flat[at:at + n].reshape(1, n))
        at += n
    return out


def _rope_tables(positions):
    inv_freq = ROPE_THETA ** (-jnp.arange(0, MLA_ROPE, 2, dtype=F32) / MLA_ROPE)
    ang = positions.astype(F32)[:, None] * inv_freq
    cos, sin, z = jnp.cos(ang), jnp.sin(ang), jnp.zeros_like(ang)
    return (jnp.concatenate([cos, cos, z, z], axis=1), jnp.concatenate([z, sin, z, z], axis=1),
            jnp.concatenate([-sin, z, z, z], axis=1))


def _local_step(x, positions, pre_w, win, q_w, wq, kv_w, wk, wv, wout, post_w, target):
    s = x.shape[0]
    rc, rs1, rs2 = _rope_tables(positions)
    h = _prenorm(x, pre_w)
    projb = _matmul(h, win, mode="nn", out_dtype=BF16, tm=512, tn=1024, tk=D_MODEL, name="proj_b", n=PB_W)
    projf = _matmul(h, win, mode="nn", out_dtype=F32, tm=512, tn=1024, tk=D_MODEL, name="proj_f", n=PF_W,
                    b_off=PB_W // 1024)
    oa, tot = _sb_fwd(projb)
    nq, nkv, qm, km, vm = _mla_prep(projf, q_w, kv_w, wq, wk, wv, rc, rs1, rs2)
    ob, lse = _mla_fwd(qm, km, vm)
    mixed, dy, dout, err2, dpost = _out_post(oa, ob, projf, wout, x, target, post_w)

    dwout = _matmul(mixed, dy, mode="tn", out_dtype=F32, tm=1024, tn=1024, tk=min(4096, s), name="dw_out")
    dmixed = _matmul(dy, wout, mode="nt", out_dtype=F32, tm=512, tn=1024, tk=D_MODEL, name="d_mixed")
    dqa, dka, dva, dga = _sb_bwd(projb, projf, dmixed, oa, tot)
    dqm, dkm, dvm, dgb = _mla_bwd(qm, km, vm, projf, dmixed, ob, lse)
    dqp, dkn, dcq, dckv, dkr, dqw, dkvw = _mla_bwd_post(dqm, dkm, dvm, projf, q_w, kv_w, wq, wk, wv, rc, rs1, rs2)
    tks = min(4096, s)
    dwq = _matmul(nq, dqp, mode="tn", out_dtype=F32, tm=Q_RANK, tn=1024, tk=tks, name="dw_q")
    dwk = _matmul(nkv, dkn, mode="tn", out_dtype=F32, tm=KV_RANK, tn=1024, tk=tks, name="dw_k")
    dwv = _matmul(nkv, dvm, mode="tn", out_dtype=F32, tm=KV_RANK, tn=1024, tk=tks, name="dw_v")
    dproj = jnp.concatenate([dqa, dka, dva, dga, dgb, dcq, dckv, dkr, jnp.zeros((s, 128), BF16)], axis=1)
    dwin = _matmul(h, dproj, mode="tn", out_dtype=F32, tm=1024, tn=1024, tk=tks, name="dw_in")
    gx, dpre = _pre_bwd(dproj, win, x, pre_w, dout)
    return err2, gx, dpre, dwin, dqw, dwq, dkvw, dwk, dwv, dwout, dpost


def _kernel_layouts(full):
    w_in, w_q_up, w_kv_up, w_out = full
    win = _perm_in(w_in)
    wq = jnp.pad(w_q_up.reshape(Q_RANK, HEADS, 192), ((0, 0), (0, 0), (0, 64))).reshape(Q_RANK, HEADS * MLA_QK_PAD)
    kv = w_kv_up.reshape(KV_RANK, HEADS, 256)
    wk = kv[:, :, :128].reshape(KV_RANK, SB_WIDTH)
    wv = kv[:, :, 128:].reshape(KV_RANK, SB_WIDTH)
    return win, wq, wk, wv, w_out


def _original_layouts(dwin, dwq, dwk, dwv):
    dwi = _unperm_in(dwin[:, :D_IN])
    dq = dwq.reshape(Q_RANK, HEADS, MLA_QK_PAD)[:, :, :192].reshape(Q_RANK, HEADS * 192)
    dkv = jnp.concatenate([dwk.reshape(KV_RANK, HEADS, 128), dwv.reshape(KV_RANK, HEADS, 128)], axis=2)
    return dwi, dq, dkv.reshape(KV_RANK, 2 * SB_WIDTH)


def kernel(x, positions, pre_norm_w, w_in, q_norm_w, w_q_up, kv_norm_w, w_kv_up, w_out, post_norm_w, loss_target, m_pre_norm_w, m_w_in, m_q_norm_w, m_w_q_up, m_kv_norm_w, m_w_kv_up, m_w_out, m_post_norm_w, v_pre_norm_w, v_w_in, v_q_norm_w, v_w_q_up, v_kv_norm_w, v_w_kv_up, v_w_out, v_post_norm_w):
    c = lax.axis_index("c")
    chip = 2 * lax.axis_index("x") + lax.axis_index("y")
    shards = (w_in[0], w_q_up[0], w_kv_up[0], w_out[0])
    packed = _pack_shard([w.astype(BF16) for w in shards])
    gathered = lax.dynamic_update_slice(_gather_weights(packed), packed[None], (chip, 0, 0))
    g_in, g_q, g_kv, g_out = _unpack_shard(gathered)
    cat = lambda g: jnp.concatenate([g[0], g[1], g[2], g[3]], axis=1)
    full = (cat(g_in), cat(g_q), cat(g_kv), g_out.reshape(D_MODEL, D_MODEL))
    win, wq, wk, wv, wout = _kernel_layouts(full)

    err2, gx, dpre, dwin, dqw, dwq, dkvw, dwk, dwv, dwout, dpost = _local_step(
        x[0], positions[0], pre_norm_w, win, q_norm_w, wq, kv_norm_w, wk, wv, wout, post_norm_w, loss_target[0])
    loss = lax.psum(0.5 * jnp.sum(err2) / D_MODEL, ("x", "y", "c"))

    dwi, dq, dkv = _original_layouts(dwin, dwq, dwk, dwv)
    chunks = _grad_chunks(dwi, dq, dkv, dwout)
    mine = lax.dynamic_index_in_dim(chunks, c, axis=1, keepdims=False)
    pair, own = _pair_sum(chip.reshape(1), mine, _swap_halves(chunks))
    got = _exchange_chips(pair)
    wh, mh, vh = (_pack_halves(ws, c) for ws in (
        shards, (m_w_in[0], m_w_q_up[0], m_w_kv_up[0], m_w_out[0]), (v_w_in[0], v_w_q_up[0], v_w_kv_up[0], v_w_out[0])))
    both = _share_with_sibling(list(_sum_adamw(c.reshape(1), own, got, wh, mh, vh)))
    big = [_unpack_halves(b) for b in both]

    small = _norm_allreduce_adamw(
        _pack_norm([dpre, dqw, dkvw, dpost]), _pack_norm([pre_norm_w, q_norm_w, kv_norm_w, post_norm_w]),
        _pack_norm([m_pre_norm_w, m_q_norm_w, m_kv_norm_w, m_post_norm_w]),
        _pack_norm([v_pre_norm_w, v_q_norm_w, v_kv_norm_w, v_post_norm_w]))
    small = [_unpack_norm(p) for p in small]

    def group(k):
        n, b = small[k], big[k]
        return (n[0], b[0][None], n[1], b[1][None], n[2], b[2][None], b[3][None], n[3])

    return (loss, gx[None], *group(0), *group(1), *group(2), *group(3))
```

```python
import functools
import math

import numpy as np
import jax
import jax.numpy as jnp
from jax import lax
from jax.experimental import pallas as pl
from jax.experimental.pallas import tpu as pltpu

F32 = jnp.float32
BF16 = jnp.bfloat16
MESH = pl.DeviceIdType.MESH

D_MODEL = 2048
HEADS = 8
HEAD_DIM = 128
SB_WIDTH = HEADS * HEAD_DIM
MLA_ROPE = 64
MLA_QK_PAD = 256
Q_RANK = 512
KV_RANK = 256
CHUNK = 64
EPS = 1e-6
ROPE_THETA = 10000.0
D_IN = 5952
D_INP = 6144
PB_W = 3072
PF_W = D_INP - PB_W
SB_SCALE = 1.0 / math.sqrt(HEAD_DIM)
MLA_SCALE = 1.0 / math.sqrt(HEAD_DIM + MLA_ROPE)
NEG = -1e30

ADAM_LR, ADAM_B1, ADAM_B2, ADAM_EPS, ADAM_WD, ADAM_STEP = 0.001, 0.9, 0.999, 1e-08, 0.01, 10
ADAM_C1 = 1.0 - ADAM_B1 ** ADAM_STEP
ADAM_C2 = 1.0 - ADAM_B2 ** ADAM_STEP

VMEM_LIMIT = 56 * 1024 * 1024
TQ = 1024
TK = 256
MLA_TK = 512
UNROLL = 2


def _cp(sem=None, **kw):
    return pltpu.CompilerParams(dimension_semantics=sem, vmem_limit_bytes=VMEM_LIMIT, **kw)


def _dot(a, b, dims):
    return lax.dot_general(a, b, (dims, ((), ())), preferred_element_type=F32)


def _nn(a, b):
    return _dot(a, b, ((1,), (0,)))


def _nt(a, b):
    return _dot(a, b, ((1,), (1,)))


def _tn(a, b):
    return _dot(a, b, ((0,), (0,)))


def _rope(x, c, s1, s2):
    return x * c + pltpu.roll(x, 32, 1) * s1 + pltpu.roll(x, 96, 1) * s2


def _rope_t(x, c, s1, s2):
    return x * c - pltpu.roll(x, 32, 1) * s1 - pltpu.roll(x, 96, 1) * s2


def _silu_parts(g):
    sg = jax.nn.sigmoid(g)
    return g * sg, sg * (1.0 + g * (1.0 - sg))


def _matmul(a, b, *, mode, out_dtype, tm, tn, tk, name, n=None, b_off=0):
    if mode == "tn":
        kk, m = a.shape
        n = b.shape[1] if n is None else n
    else:
        m, kk = a.shape
        n = (b.shape[1] if mode == "nn" else b.shape[0]) if n is None else n
    nk = kk // tk
    a_spec = {"nn": pl.BlockSpec((tm, tk), lambda j, i, k: (i, k)),
              "nt": pl.BlockSpec((tm, tk), lambda j, i, k: (i, k)),
              "tn": pl.BlockSpec((tk, tm), lambda j, i, k: (k, i))}[mode]
    b_spec = {"nn": pl.BlockSpec((tk, tn), lambda j, i, k: (k, j + b_off)),
              "nt": pl.BlockSpec((tn, tk), lambda j, i, k: (j, k)),
              "tn": pl.BlockSpec((tk, tn), lambda j, i, k: (k, j))}[mode]
    dims = {"nn": ((1,), (0,)), "nt": ((1,), (1,)), "tn": ((0,), (0,))}[mode]

    def body(a_ref, b_ref, o_ref, acc_ref):
        k = pl.program_id(2)
        part = _dot(a_ref[...], b_ref[...], dims)
        if nk == 1:
            o_ref[...] = part.astype(out_dtype)
        else:
            @pl.when(k == 0)
            def _():
                acc_ref[...] = part

            @pl.when(k > 0)
            def _():
                acc_ref[...] += part

            @pl.when(k == nk - 1)
            def _():
                o_ref[...] = acc_ref[...].astype(out_dtype)

    return pl.pallas_call(
        body, name=name, grid=(n // tn, m // tm, nk),
        in_specs=[a_spec, b_spec], out_specs=pl.BlockSpec((tm, tn), lambda j, i, k: (i, j)),
        out_shape=jax.ShapeDtypeStruct((m, n), out_dtype),
        scratch_shapes=[pltpu.VMEM((tm, tn) if nk > 1 else (8, 128), F32)],
        compiler_params=_cp(("parallel", "parallel", "arbitrary")),
    )(a, b)


def _prenorm(x, w, tm=512):
    s, d = x.shape

    def body(x_ref, w_ref, h_ref):
        xv = x_ref[...]
        r = lax.rsqrt(jnp.mean(xv * xv, axis=1, keepdims=True) + EPS)
        h_ref[...] = ((xv * r) * w_ref[...]).astype(BF16)

    return pl.pallas_call(
        body, name="prenorm", grid=(s // tm,),
        in_specs=[pl.BlockSpec((tm, d), lambda i: (i, 0)), pl.BlockSpec((1, d), lambda i: (0, 0))],
        out_specs=pl.BlockSpec((tm, d), lambda i: (i, 0)),
        out_shape=jax.ShapeDtypeStruct((s, d), BF16),
        compiler_params=_cp(("parallel",)),
    )(x, w)


def _tri(n, cmp, value):
    row = lax.broadcasted_iota(jnp.int32, (n, n), 0)
    col = lax.broadcasted_iota(jnp.int32, (n, n), 1)
    return jnp.where(cmp(row, col), value, 0.0).astype(BF16)


def _sb_tile(k, q, mask):
    z = _nt(k, q) * SB_SCALE
    sp = jnp.maximum(z, 0.0) + jnp.log(1.0 + jnp.exp(-jnp.abs(z)))
    if mask is not None:
        sp = jnp.where(mask, sp, 0.0)
    return z, sp


def _tiles(s, tk=TK):
    tq = min(TQ, s)
    return tq, tk, math.gcd(tq // tk, UNROLL)


def _band_masks(tq, tk, fn):
    out = []
    for b in range(tq // tk):
        key = lax.broadcasted_iota(jnp.int32, (tk, tq - b * tk), 0) + b * tk
        qry = lax.broadcasted_iota(jnp.int32, (tk, tq - b * tk), 1) + b * tk
        out.append(fn(qry, key))
    return out


def _row_spec(tq):
    return pl.BlockSpec((8, tq), lambda h, i: (h, i))


def _sb_fwd(projb):
    s = projb.shape[0]
    tq, tk, unroll = _tiles(s)
    nb = tq // tk

    def body(q_ref, k_ref, v_ref, o_ref, tot_ref, acc_ref, car_ref):
        i = pl.program_id(1)
        q = q_ref[...]
        from_here = _tri(tk, lambda s_, j: j >= s_, -1.0)
        masks = _band_masks(tq, tk, lambda t, s_: s_ < t)
        acc_ref[...] = jnp.zeros_like(acc_ref)
        car_ref[...] = jnp.zeros_like(car_ref)

        def tile(j, mask, off):
            rows = pl.ds(pl.multiple_of(j * tk, tk), tk)
            z, sp = _sb_tile(k_ref[rows, :], q[off:, :], mask)
            a = jnp.exp(z + _nn(from_here, sp.astype(BF16)) + car_ref[:, off:])
            if mask is not None:
                a = jnp.where(mask, a, 0.0)
            acc_ref[:, off:] += _tn(v_ref[rows, :], a.astype(BF16))
            car_ref[:, off:] -= jnp.sum(sp, axis=0, keepdims=True)

        for b in reversed(range(nb)):
            tile(i * nb + b, masks[b], b * tk)

        def step(jj, c):
            for u in range(unroll):
                tile(i * nb - 1 - (jj * unroll + u), None, 0)
            return c

        lax.fori_loop(0, i * nb // unroll, step, 0)
        o_ref[...] = acc_ref[...].T
        tot_ref[...] = jnp.broadcast_to(car_ref[...], (8, tq))

    blk = pl.BlockSpec((tq, HEAD_DIM), lambda h, i: (i, h))
    return pl.pallas_call(
        body, name="sb_fwd", grid=(HEADS, s // tq),
        in_specs=[blk, pl.BlockSpec((s, HEAD_DIM), lambda h, i: (0, HEADS + h)),
                  pl.BlockSpec((s, HEAD_DIM), lambda h, i: (0, 2 * HEADS + h))],
        out_specs=[blk, _row_spec(tq)],
        out_shape=[jax.ShapeDtypeStruct((s, SB_WIDTH), F32), jax.ShapeDtypeStruct((8 * HEADS, s), F32)],
        scratch_shapes=[pltpu.VMEM((HEAD_DIM, tq), F32), pltpu.VMEM((1, tq), F32)],
        compiler_params=_cp(("parallel", "arbitrary")),
    )(projb, projb, projb)


def _mla_prep(projf, qw, kvw, wq, wk, wv, rc, rs1, rs2, tm=512):
    s = projf.shape[0]

    def body(cq_ref, ckv_ref, kr_ref, qw_ref, kvw_ref, wq_ref, wk_ref, wv_ref, c_ref, s1_ref, s2_ref,
             nq_ref, nkv_ref, q_ref, k_ref, v_ref):
        c, s1, s2 = c_ref[...], s1_ref[...], s2_ref[...]
        cq = cq_ref[...]
        nq = ((cq * lax.rsqrt(jnp.mean(cq * cq, axis=1, keepdims=True) + EPS)) * qw_ref[...]).astype(BF16)
        nq_ref[...] = nq
        qf = _nn(nq, wq_ref[...])
        ckv = ckv_ref[...]
        nkv = ((ckv * lax.rsqrt(jnp.mean(ckv * ckv, axis=1, keepdims=True) + EPS)) * kvw_ref[...]).astype(BF16)
        nkv_ref[...] = nkv
        kn = _nn(nkv, wk_ref[...])
        v_ref[...] = _nn(nkv, wv_ref[...]).astype(BF16)
        krot = _rope(kr_ref[...], c, s1, s2).astype(BF16)
        for h in range(HEADS):
            lo = h * MLA_QK_PAD
            q_ref[:, lo:lo + 128] = qf[:, lo:lo + 128].astype(BF16)
            q_ref[:, lo + 128:lo + 256] = _rope(qf[:, lo + 128:lo + 256], c, s1, s2).astype(BF16)
            k_ref[:, lo:lo + 128] = kn[:, h * 128:(h + 1) * 128].astype(BF16)
            k_ref[:, lo + 128:lo + 256] = krot

    row = lambda w, b: pl.BlockSpec((tm, w), lambda i: (i, b))
    full = lambda a: pl.BlockSpec(a.shape, lambda i: (0, 0))
    return pl.pallas_call(
        body, name="mla_prep", grid=(s // tm,),
        in_specs=[row(Q_RANK, 2048 // Q_RANK), row(KV_RANK, 2560 // KV_RANK), row(128, 2816 // 128),
                  full(qw), full(kvw), full(wq), full(wk), full(wv), row(128, 0), row(128, 0), row(128, 0)],
        out_specs=[row(Q_RANK, 0), row(KV_RANK, 0), row(HEADS * MLA_QK_PAD, 0), row(HEADS * MLA_QK_PAD, 0),
                   row(SB_WIDTH, 0)],
        out_shape=[jax.ShapeDtypeStruct((s, Q_RANK), BF16), jax.ShapeDtypeStruct((s, KV_RANK), BF16),
                   jax.ShapeDtypeStruct((s, HEADS * MLA_QK_PAD), BF16),
                   jax.ShapeDtypeStruct((s, HEADS * MLA_QK_PAD), BF16),
                   jax.ShapeDtypeStruct((s, SB_WIDTH), BF16)],
        compiler_params=_cp(("parallel",)),
    )(projf, projf, projf, qw, kvw, wq, wk, wv, rc, rs1, rs2)


def _mla_mask(qry, key):
    return (key // CHUNK) <= (qry // CHUNK)


def _mla_fwd(qm, km, vm):
    s = qm.shape[0]
    tq, tk, unroll = _tiles(s, MLA_TK)
    nb = tq // tk

    def body(q_ref, k_ref, v_ref, o_ref, lse_ref, acc_ref, m_ref, l_ref):
        i = pl.program_id(1)
        q = q_ref[...]
        masks = _band_masks(tq, tk, _mla_mask)
        acc_ref[...] = jnp.zeros_like(acc_ref)
        m_ref[...] = jnp.full_like(m_ref, NEG)
        l_ref[...] = jnp.zeros_like(l_ref)

        def tile(j, mask, off):
            rows = pl.ds(pl.multiple_of(j * tk, tk), tk)
            sc = _nt(k_ref[rows, :], q[off:, :]) * MLA_SCALE
            if mask is not None:
                sc = jnp.where(mask, sc, NEG)
            m_old = m_ref[:, off:]
            m_new = jnp.maximum(m_old, jnp.max(sc, axis=0, keepdims=True))
            p = jnp.exp(sc - m_new)
            alpha = jnp.exp(m_old - m_new)
            l_ref[:, off:] = alpha * l_ref[:, off:] + jnp.sum(p, axis=0, keepdims=True)
            acc_ref[:, off:] = alpha * acc_ref[:, off:] + _tn(v_ref[rows, :], p.astype(BF16))
            m_ref[:, off:] = m_new

        for b in range(nb):
            tile(i * nb + b, masks[b], b * tk)

        def step(jj, c):
            for u in range(unroll):
                tile(jj * unroll + u, None, 0)
            return c

        lax.fori_loop(0, i * nb // unroll, step, 0)
        o_ref[...] = (acc_ref[...] / l_ref[...]).T
        lse_ref[...] = jnp.broadcast_to(m_ref[...] + jnp.log(l_ref[...]), (8, tq))

    return pl.pallas_call(
        body, name="mla_fwd", grid=(HEADS, s // tq),
        in_specs=[pl.BlockSpec((tq, MLA_QK_PAD), lambda h, i: (i, h)),
                  pl.BlockSpec((s, MLA_QK_PAD), lambda h, i: (0, h)),
                  pl.BlockSpec((s, HEAD_DIM), lambda h, i: (0, h))],
        out_specs=[pl.BlockSpec((tq, HEAD_DIM), lambda h, i: (i, h)), _row_spec(tq)],
        out_shape=[jax.ShapeDtypeStruct((s, SB_WIDTH), F32), jax.ShapeDtypeStruct((8 * HEADS, s), F32)],
        scratch_shapes=[pltpu.VMEM((HEAD_DIM, tq), F32), pltpu.VMEM((1, tq), F32), pltpu.VMEM((1, tq), F32)],
        compiler_params=_cp(("parallel", "arbitrary")),
    )(qm, km, vm)


def _out_post(oa, ob, projf, wout, x, target, pw, tm=256):
    s, d = x.shape

    def body(oa_ref, ob_ref, ga_ref, gb_ref, w_ref, x_ref, t_ref, pw_ref,
             mix_ref, dy_ref, dout_ref, loss_ref, dpw_ref):
        i = pl.program_id(0)
        sa, _ = _silu_parts(ga_ref[...])
        sb, _ = _silu_parts(gb_ref[...])
        mix_ref[:, :SB_WIDTH] = (oa_ref[...] * sa).astype(BF16)
        mix_ref[:, SB_WIDTH:] = (ob_ref[...] * sb).astype(BF16)
        y = _nn(mix_ref[...], w_ref[...])
        r = lax.rsqrt(jnp.mean(y * y, axis=1, keepdims=True) + EPS)
        yhat = y * r
        pwv = pw_ref[...]
        err = (x_ref[...] + yhat * pwv) - t_ref[...]
        dout = err * (1.0 / d)
        dout_ref[...] = dout
        g = dout * pwv
        dy_ref[...] = (r * (g - yhat * jnp.mean(g * yhat, axis=1, keepdims=True))).astype(BF16)

        @pl.when(i == 0)
        def _():
            loss_ref[...] = jnp.zeros_like(loss_ref)
            dpw_ref[...] = jnp.zeros_like(dpw_ref)

        loss_ref[...] += jnp.sum(err * err, axis=0, keepdims=True)
        dpw_ref[...] += jnp.sum(dout * yhat, axis=0, keepdims=True)

    row = lambda w, b: pl.BlockSpec((tm, w), lambda i: (i, b))
    vec = pl.BlockSpec((1, d), lambda i: (0, 0))
    return pl.pallas_call(
        body, name="out_post", grid=(s // tm,),
        in_specs=[row(SB_WIDTH, 0), row(SB_WIDTH, 0), row(SB_WIDTH, 0), row(SB_WIDTH, 1),
                  pl.BlockSpec(wout.shape, lambda i: (0, 0)), row(d, 0), row(d, 0), vec],
        out_specs=[row(d, 0), row(d, 0), row(d, 0), vec, vec],
        out_shape=[jax.ShapeDtypeStruct((s, d), BF16), jax.ShapeDtypeStruct((s, d), BF16),
                   jax.ShapeDtypeStruct((s, d), F32), jax.ShapeDtypeStruct((1, d), F32),
                   jax.ShapeDtypeStruct((1, d), F32)],
        compiler_params=_cp(("arbitrary",)),
    )(oa, ob, projf, projf, wout, x, target, pw)


def _sb_bwd(projb, projf, dmixed, oa, tot):
    s = projb.shape[0]
    tq, tk, unroll = _tiles(s)
    nb, nq = tq // tk, s // tq

    def body(q_ref, k_ref, v_ref, dm_ref, g_ref, o_ref, tot_ref, dq_ref, dk_ref, dv_ref, dg_ref,
             dka_ref, dva_ref, dqa_ref, pre_ref, cg_ref):
        i = pl.program_id(1)

        @pl.when(i == 0)
        def _():
            dka_ref[...] = jnp.zeros_like(dka_ref)
            dva_ref[...] = jnp.zeros_like(dva_ref)

        q = q_ref[...]
        silu, dsilu = _silu_parts(g_ref[...])
        dm = dm_ref[...]
        dg_ref[...] = (dm * o_ref[...] * dsilu).astype(BF16)
        do = (dm * silu).astype(BF16)
        tot = tot_ref[0:1, :]
        from_here = _tri(tk, lambda s_, j: j >= s_, -1.0)
        up_to_here = _tri(tk, lambda s_, j: j <= s_, 1.0)
        masks = _band_masks(tq, tk, lambda t, s_: s_ < t)
        dqa_ref[...] = jnp.zeros_like(dqa_ref)
        pre_ref[...] = jnp.zeros_like(pre_ref)
        cg_ref[...] = jnp.zeros_like(cg_ref)

        def tile(j, mask, off):
            rows = pl.ds(pl.multiple_of(j * tk, tk), tk)
            k = k_ref[rows, :]
            z, sp = _sb_tile(k, q[off:, :], mask)
            pre = pre_ref[:, off:] - jnp.sum(sp, axis=0, keepdims=True)
            pre_ref[:, off:] = pre
            a = jnp.exp(z + _nn(from_here, sp.astype(BF16)) + (tot[:, off:] - pre))
            if mask is not None:
                a = jnp.where(mask, a, 0.0)
            g = a * _nt(v_ref[rows, :], do[off:, :])
            dva_ref[rows, :] += _nn(a.astype(BF16), do[off:, :])
            cum = _nn(up_to_here, g.astype(BF16)) + cg_ref[:, off:]
            dz = (g - jnp.exp(z - sp) * cum) * SB_SCALE
            if mask is not None:
                dz = jnp.where(mask, dz, 0.0)
            dz = dz.astype(BF16)
            dqa_ref[:, off:] += _tn(k, dz)
            dka_ref[rows, :] += _nn(dz, q[off:, :])
            cg_ref[:, off:] += jnp.sum(g, axis=0, keepdims=True)

        def step(jj, c):
            for u in range(unroll):
                tile(jj * unroll + u, None, 0)
            return c

        lax.fori_loop(0, i * nb // unroll, step, 0)
        for b in range(nb):
            tile(i * nb + b, masks[b], b * tk)
        dq_ref[...] = dqa_ref[...].T.astype(BF16)

        @pl.when(i == nq - 1)
        def _():
            dk_ref[...] = dka_ref[...].astype(BF16)
            dv_ref[...] = dva_ref[...].astype(BF16)

    blk = lambda off: pl.BlockSpec((tq, HEAD_DIM), lambda h, i: (i, off + h))
    whole = lambda off: pl.BlockSpec((s, HEAD_DIM), lambda h, i: (0, off + h))
    o_sd = jax.ShapeDtypeStruct((s, SB_WIDTH), BF16)
    return pl.pallas_call(
        body, name="sb_bwd", grid=(HEADS, nq),
        in_specs=[blk(0), whole(HEADS), whole(2 * HEADS), blk(0), blk(0), blk(0), _row_spec(tq)],
        out_specs=[blk(0), whole(0), whole(0), blk(0)],
        out_shape=[o_sd, o_sd, o_sd, o_sd],
        scratch_shapes=[pltpu.VMEM((s, HEAD_DIM), F32), pltpu.VMEM((s, HEAD_DIM), F32),
                        pltpu.VMEM((HEAD_DIM, tq), F32), pltpu.VMEM((1, tq), F32), pltpu.VMEM((1, tq), F32)],
        compiler_params=_cp(("parallel", "arbitrary")),
    )(projb, projb, projb, dmixed, projf, oa, tot)


def _mla_bwd(qm, km, vm, projf, dmixed, ob, lse):
    s = qm.shape[0]
    tq, tk, unroll = _tiles(s, MLA_TK)
    nb, nq = tq // tk, s // tq

    def body(q_ref, k_ref, v_ref, dm_ref, g_ref, o_ref, lse_ref, dq_ref, dk_ref, dv_ref, dg_ref,
             dva_ref, dqa_ref):
        i = pl.program_id(1)

        @pl.when(i == 0)
        def _():
            dk_ref[...] = jnp.zeros_like(dk_ref)
            dva_ref[...] = jnp.zeros_like(dva_ref)

        q = q_ref[...]
        silu, dsilu = _silu_parts(g_ref[...])
        dm = dm_ref[...]
        o = o_ref[...]
        dg_ref[...] = (dm * o * dsilu).astype(BF16)
        dof = dm * silu
        delta = jnp.sum((dof * o).T, axis=0, keepdims=True)
        do = dof.astype(BF16)
        lse = lse_ref[0:1, :]
        masks = _band_masks(tq, tk, _mla_mask)
        dqa_ref[...] = jnp.zeros_like(dqa_ref)

        def tile(j, mask, off):
            rows = pl.ds(pl.multiple_of(j * tk, tk), tk)
            k = k_ref[rows, :]
            p = jnp.exp(_nt(k, q[off:, :]) * MLA_SCALE - lse[:, off:])
            if mask is not None:
                p = jnp.where(mask, p, 0.0)
            ds = (p * (_nt(v_ref[rows, :], do[off:, :]) - delta[:, off:]) * MLA_SCALE).astype(BF16)
            dva_ref[rows, :] += _nn(p.astype(BF16), do[off:, :])
            dk_ref[rows, :] += _nn(ds, q[off:, :])
            dqa_ref[:, off:] += _tn(k, ds)

        def step(jj, c):
            for u in range(unroll):
                tile(jj * unroll + u, None, 0)
            return c

        lax.fori_loop(0, i * nb // unroll, step, 0)
        for b in range(nb):
            tile(i * nb + b, masks[b], b * tk)
        dq_ref[...] = dqa_ref[...].T

        @pl.when(i == nq - 1)
        def _():
            dv_ref[...] = dva_ref[...].astype(BF16)

    blk = lambda w, off: pl.BlockSpec((tq, w), lambda h, i: (i, off + h))
    whole = lambda w: pl.BlockSpec((s, w), lambda h, i: (0, h))
    return pl.pallas_call(
        body, name="mla_bwd", grid=(HEADS, nq),
        in_specs=[blk(MLA_QK_PAD, 0), whole(MLA_QK_PAD), whole(HEAD_DIM), blk(HEAD_DIM, HEADS),
                  blk(HEAD_DIM, HEADS), blk(HEAD_DIM, 0), _row_spec(tq)],
        out_specs=[blk(MLA_QK_PAD, 0), whole(MLA_QK_PAD), whole(HEAD_DIM), blk(HEAD_DIM, 0)],
        out_shape=[jax.ShapeDtypeStruct((s, HEADS * MLA_QK_PAD), F32),
                   jax.ShapeDtypeStruct((s, HEADS * MLA_QK_PAD), F32),
                   jax.ShapeDtypeStruct((s, SB_WIDTH), BF16), jax.ShapeDtypeStruct((s, SB_WIDTH), BF16)],
        scratch_shapes=[pltpu.VMEM((s, HEAD_DIM), F32), pltpu.VMEM((MLA_QK_PAD, tq), F32)],
        compiler_params=_cp(("parallel", "arbitrary")),
    )(qm, km, vm, dmixed, projf, ob, lse)


def _norm_bwd(x, w, dn):
    r = lax.rsqrt(jnp.mean(x * x, axis=1, keepdims=True) + EPS)
    xhat = x * r
    g = dn * w
    return r * (g - xhat * jnp.mean(g * xhat, axis=1, keepdims=True)), dn * xhat


def _mla_bwd_post(dqm, dkm, dvm, projf, qw, kvw, wq, wk, wv, rc, rs1, rs2, tm=256):
    s = dqm.shape[0]

    def body(dq_ref, dk_ref, dv_ref, cq_ref, ckv_ref, qw_ref, kvw_ref, wq_ref, wk_ref, wv_ref,
             c_ref, s1_ref, s2_ref, dqp_ref, dkn_ref, dcq_ref, dckv_ref, dkr_ref, dqw_ref, dkvw_ref):
        i = pl.program_id(0)
        c, s1, s2 = c_ref[...], s1_ref[...], s2_ref[...]
        drot = jnp.zeros((tm, 128), F32)
        for h in range(HEADS):
            lo = h * MLA_QK_PAD
            dqp_ref[:, lo:lo + 128] = dq_ref[:, lo:lo + 128].astype(BF16)
            dqp_ref[:, lo + 128:lo + 256] = _rope_t(dq_ref[:, lo + 128:lo + 256], c, s1, s2).astype(BF16)
            dkn_ref[:, h * 128:(h + 1) * 128] = dk_ref[:, lo:lo + 128].astype(BF16)
            drot = drot + dk_ref[:, lo + 128:lo + 256]
        dkr_ref[...] = _rope_t(drot, c, s1, s2).astype(BF16)
        dcq, dqw = _norm_bwd(cq_ref[...], qw_ref[...], _nt(dqp_ref[...], wq_ref[...]))
        dcq_ref[...] = dcq.astype(BF16)
        dnkv = _nt(dkn_ref[...], wk_ref[...]) + _nt(dv_ref[...], wv_ref[...])
        dckv, dkvw = _norm_bwd(ckv_ref[...], kvw_ref[...], dnkv)
        dckv_ref[...] = dckv.astype(BF16)

        @pl.when(i == 0)
        def _():
            dqw_ref[...] = jnp.zeros_like(dqw_ref)
            dkvw_ref[...] = jnp.zeros_like(dkvw_ref)

        dqw_ref[...] += jnp.sum(dqw, axis=0, keepdims=True)
        dkvw_ref[...] += jnp.sum(dkvw, axis=0, keepdims=True)

    row = lambda w, b: pl.BlockSpec((tm, w), lambda i: (i, b))
    full = lambda a: pl.BlockSpec(a.shape, lambda i: (0, 0))
    sd = jax.ShapeDtypeStruct
    return pl.pallas_call(
        body, name="mla_bwd_post", grid=(s // tm,),
        in_specs=[row(HEADS * MLA_QK_PAD, 0), row(HEADS * MLA_QK_PAD, 0), row(SB_WIDTH, 0),
                  row(Q_RANK, 2048 // Q_RANK), row(KV_RANK, 2560 // KV_RANK),
                  full(qw), full(kvw), full(wq), full(wk), full(wv), row(128, 0), row(128, 0), row(128, 0)],
        out_specs=[row(HEADS * MLA_QK_PAD, 0), row(SB_WIDTH, 0), row(Q_RANK, 0), row(KV_RANK, 0), row(128, 0),
                   pl.BlockSpec((1, Q_RANK), lambda i: (0, 0)), pl.BlockSpec((1, KV_RANK), lambda i: (0, 0))],
        out_shape=[sd((s, HEADS * MLA_QK_PAD), BF16), sd((s, SB_WIDTH), BF16), sd((s, Q_RANK), BF16),
                   sd((s, KV_RANK), BF16), sd((s, 128), BF16), sd((1, Q_RANK), F32), sd((1, KV_RANK), F32)],
        compiler_params=_cp(("arbitrary",)),
    )(dqm, dkm, dvm, projf, projf, qw, kvw, wq, wk, wv, rc, rs1, rs2)


def _pre_bwd(dproj, win, x, pw, dout, tm=512, tk=1024):
    s, d = x.shape
    nk = dproj.shape[1] // tk

    def body(dp_ref, w_ref, x_ref, pw_ref, do_ref, dx_ref, dpw_ref, acc_ref):
        i, k = pl.program_id(0), pl.program_id(1)
        part = _nt(dp_ref[...], w_ref[...])

        @pl.when(k == 0)
        def _():
            acc_ref[...] = part

        @pl.when(k > 0)
        def _():
            acc_ref[...] += part

        @pl.when(k == nk - 1)
        def _():
            dx, dw = _norm_bwd(x_ref[...], pw_ref[...], acc_ref[...])
            dx_ref[...] = do_ref[...] + dx

            @pl.when(i == 0)
            def _():
                dpw_ref[...] = jnp.zeros_like(dpw_ref)

            dpw_ref[...] += jnp.sum(dw, axis=0, keepdims=True)

    rowd = pl.BlockSpec((tm, d), lambda i, k: (i, 0))
    vec = pl.BlockSpec((1, d), lambda i, k: (0, 0))
    return pl.pallas_call(
        body, name="pre_bwd", grid=(s // tm, nk),
        in_specs=[pl.BlockSpec((tm, tk), lambda i, k: (i, k)), pl.BlockSpec((d, tk), lambda i, k: (0, k)),
                  rowd, vec, rowd],
        out_specs=[rowd, vec],
        out_shape=[jax.ShapeDtypeStruct((s, d), F32), jax.ShapeDtypeStruct((1, d), F32)],
        scratch_shapes=[pltpu.VMEM((tm, d), F32)],
        compiler_params=_cp(("arbitrary", "arbitrary")),
    )(dproj, win, x, pw, dout)


def _adamw(w, g, m, v):
    m = ADAM_B1 * m + (1.0 - ADAM_B1) * g
    v = ADAM_B2 * v + (1.0 - ADAM_B2) * (g * g)
    delta = -ADAM_LR * ((m / ADAM_C1) / (jnp.sqrt(v / ADAM_C2) + ADAM_EPS) + ADAM_WD * w)
    return delta, m, v


def _row_block(rows):
    return math.gcd(rows, 128)


def _sum_adamw(core, own, got, w, m, v, name):
    n, hr, cols = got.shape
    tr = _row_block(hr)
    nblk = hr // tr

    def body(c_ref, o_ref, p_ref, w_ref, m_ref, v_ref, g_ref, d_ref, nm_ref, nv_ref):
        g = o_ref[...]
        for k in range(n):
            g = g + p_ref[k].astype(F32)
        g_ref[...] = g
        d_ref[...], nm_ref[...], nv_ref[...] = _adamw(w_ref[...], g, m_ref[...], v_ref[...])

    mine = pl.BlockSpec((tr, cols), lambda i, c: (c[0] * nblk + i, 0))
    sd = jax.ShapeDtypeStruct((2 * hr, cols), F32)
    return pl.pallas_call(
        body, name=name,
        grid_spec=pltpu.PrefetchScalarGridSpec(
            num_scalar_prefetch=1, grid=(nblk,),
            in_specs=[pl.BlockSpec((tr, cols), lambda i, c: (i, 0)),
                      pl.BlockSpec((n, tr, cols), lambda i, c: (0, i, 0)), mine, mine, mine],
            out_specs=[mine, mine, mine, mine]),
        out_shape=[sd, sd, sd, sd],
        compiler_params=_cp(("parallel",)),
    )(core, own, got, w, m, v)


def _pair_sum(place, grads, got, name):
    n, _, hr, cols = grads.shape
    tr = _row_block(hr)

    def body(p_ref, a_ref, b_ref, oa_ref, ob_ref, all_ref, own_ref):
        all_ref[...] = (a_ref[:, 0] + b_ref[...]).astype(BF16)
        own_ref[...] = oa_ref[0, 0] + ob_ref[0]

    return pl.pallas_call(
        body, name=name,
        grid_spec=pltpu.PrefetchScalarGridSpec(
            num_scalar_prefetch=1, grid=(hr // tr,),
            in_specs=[pl.BlockSpec((n, 1, tr, cols), lambda i, p: (0, p[0], i, 0)),
                      pl.BlockSpec((n, tr, cols), lambda i, p: (0, i, 0)),
                      pl.BlockSpec((1, 1, tr, cols), lambda i, p: (p[1], p[0], i, 0)),
                      pl.BlockSpec((1, tr, cols), lambda i, p: (p[1], i, 0))],
            out_specs=[pl.BlockSpec((n, tr, cols), lambda i, p: (0, i, 0)),
                       pl.BlockSpec((tr, cols), lambda i, p: (i, 0))]),
        out_shape=[jax.ShapeDtypeStruct((n, hr, cols), BF16), jax.ShapeDtypeStruct((hr, cols), F32)],
        compiler_params=_cp(("parallel",)),
    )(place, grads, got, grads, got)


def _place():
    return lax.axis_index("x"), lax.axis_index("y"), lax.axis_index("c")


ANY = pl.BlockSpec(memory_space=pl.ANY)
W_NAMES = ("w_in", "w_q_up", "w_kv_up", "w_out")
NW = len(W_NAMES)


def _comm_call(body, name, ins, out_shape, n_copies, aliases=None):
    return pl.pallas_call(
        body, name=name, in_specs=[ANY] * len(ins), out_specs=[ANY] * len(out_shape), out_shape=out_shape,
        input_output_aliases=aliases or {},
        scratch_shapes=[pltpu.SemaphoreType.DMA((n_copies,)), pltpu.SemaphoreType.DMA((n_copies,))],
        compiler_params=pltpu.CompilerParams(has_side_effects=True),
    )(*ins)


def _gather_weights(shards):
    def body(*refs):
        ws, outs = refs[:NW], refs[NW:2 * NW]
        send_sems, recv_sems = refs[2 * NW:]
        x, y, c = _place()
        chips = [(1 - x, y), (x, 1 - y), (1 - x, 1 - y)]

        def rows(w, chip, core):
            half = ws[w].shape[0] // 2
            return outs[w].at[2 * chip[0] + chip[1], pl.ds(core * half, half), :]

        def copy(w, k, chip, core, to, own=False):
            half = ws[w].shape[0] // 2
            return pltpu.make_async_remote_copy(
                src_ref=ws[w].at[pl.ds(core * half, half), :] if own else rows(w, chip, core),
                dst_ref=rows(w, chip, core), send_sem=send_sems.at[6 * w + k], recv_sem=recv_sems.at[6 * w + k],
                device_id=to, device_id_type=MESH)

        first = [copy(w, j, (x, y), c, (*chip, c), own=True) for w in range(NW) for j, chip in enumerate(chips)]
        for cp in first:
            cp.start()
        passed = []
        for w in range(NW):
            for j, chip in enumerate(chips):
                copy(w, j, chip, c, (x, y, c)).wait_recv()
                passed.append(copy(w, 3 + j, chip, c, (x, y, 1 - c)))
                passed[-1].start()
        for w in range(NW):
            for j, chip in enumerate(chips):
                copy(w, 3 + j, chip, 1 - c, (x, y, c)).wait_recv()
        for cp in first + passed:
            cp.wait_send()

    return _comm_call(body, "gather_weights", shards,
                      [jax.ShapeDtypeStruct((4,) + w.shape, w.dtype) for w in shards], 6 * NW)


def _swap_halves(grads):
    def body(*refs):
        gs, gots = refs[:NW], refs[NW:2 * NW]
        send_sems, recv_sems = refs[2 * NW:]
        x, y, c = _place()
        copies = [pltpu.make_async_remote_copy(
            src_ref=gs[w].at[k, 1 - c], dst_ref=gots[w].at[k], send_sem=send_sems.at[4 * w + k],
            recv_sem=recv_sems.at[4 * w + k], device_id=(x, y, 1 - c), device_id_type=MESH)
            for w in range(NW) for k in range(4)]
        for cp in copies:
            cp.start()
        for cp in copies:
            cp.wait()

    return _comm_call(body, "swap_halves", grads,
                      [jax.ShapeDtypeStruct((4,) + g.shape[2:], g.dtype) for g in grads], 4 * NW)


def _exchange_chips(psums):
    def body(*refs):
        ps, gots = refs[:NW], refs[NW:2 * NW]
        send_sems, recv_sems = refs[2 * NW:]
        x, y, c = _place()
        chips = [(1 - x, y), (x, 1 - y), (1 - x, 1 - y)]
        copies = [pltpu.make_async_remote_copy(
            src_ref=ps[w].at[2 * chip[0] + chip[1]], dst_ref=gots[w].at[j], send_sem=send_sems.at[3 * w + j],
            recv_sem=recv_sems.at[3 * w + j], device_id=(*chip, c), device_id_type=MESH)
            for w in range(NW) for j, chip in enumerate(chips)]
        for cp in copies:
            cp.start()
        for cp in copies:
            cp.wait()

    return _comm_call(body, "exchange_chips", psums,
                      [jax.ShapeDtypeStruct((3,) + p.shape[1:], p.dtype) for p in psums], 3 * NW)


def _share_with_sibling(arrs):
    na = len(arrs)

    def body(*refs):
        outs = refs[na:2 * na]
        send_sems, recv_sems = refs[2 * na:]
        x, y, c = _place()

        def half(ref):
            hr = ref.shape[0] // 2
            return ref.at[pl.ds(c * hr, hr), :]

        copies = [pltpu.make_async_remote_copy(
            src_ref=half(outs[k]), dst_ref=half(outs[k]), send_sem=send_sems.at[k], recv_sem=recv_sems.at[k],
            device_id=(x, y, 1 - c), device_id_type=MESH) for k in range(na)]
        for cp in copies:
            cp.start()
        for cp in copies:
            cp.wait()

    return _comm_call(body, "share_with_sibling", arrs, [jax.ShapeDtypeStruct(a.shape, a.dtype) for a in arrs],
                      na, aliases={k: k for k in range(na)})


def _norm_allreduce_adamw(part, w, m, v):
    r, lanes = part.shape

    def body(p_ref, w_ref, m_ref, v_ref, g_ref, d_ref, nm_ref, nv_ref, all_ref, send_sems, recv_sems):
        x, y, c = _place()
        me = 4 * x + 2 * y + c
        all_ref[me] = p_ref[...]
        copies = []
        for k in range(1, 8):
            peer = (x ^ (k >> 2), y ^ ((k >> 1) & 1), c ^ (k & 1))
            copies.append(pltpu.make_async_remote_copy(
                src_ref=p_ref, dst_ref=all_ref.at[me], send_sem=send_sems.at[k - 1], recv_sem=recv_sems.at[k - 1],
                device_id=peer, device_id_type=MESH))
        for cp in copies:
            cp.start()
        for cp in copies:
            cp.wait()
        g = all_ref[0]
        for k in range(1, 8):
            g = g + all_ref[k]
        g_ref[...] = g
        d_ref[...], nm_ref[...], nv_ref[...] = _adamw(w_ref[...], g, m_ref[...], v_ref[...])

    vm = pl.BlockSpec(memory_space=pltpu.VMEM)
    sd = jax.ShapeDtypeStruct((r, lanes), F32)
    return pl.pallas_call(
        body, name="norm_allreduce_adamw", in_specs=[vm] * 4, out_specs=[vm] * 4, out_shape=[sd] * 4,
        scratch_shapes=[pltpu.VMEM((8, r, lanes), F32), pltpu.SemaphoreType.DMA((7,)),
                        pltpu.SemaphoreType.DMA((7,))],
        compiler_params=pltpu.CompilerParams(has_side_effects=True),
    )(part, w, m, v)


LANES = 128


def _perm_in(w):
    return jnp.concatenate([w[:, :4096], w[:, 4928:5952], w[:, 4096:4928],
                            jnp.zeros((w.shape[0], D_INP - D_IN), w.dtype)], axis=1)


def _unperm_in(w):
    return jnp.concatenate([w[:, :4096], w[:, 5120:5952], w[:, 4096:5120]], axis=1)


def _by_chip_and_half(g, axis):
    sh = jnp.stack(jnp.split(g, 4, axis=axis))
    return sh.reshape(4, 2, sh.shape[1] // 2, sh.shape[2])


NORM_NAMES = ("pre_norm_w", "q_norm_w", "kv_norm_w", "post_norm_w")
NORM_SIZES = (D_MODEL, Q_RANK, KV_RANK, D_MODEL)
NORM_ROWS = 40


def _pack_norm(vs):
    flat = jnp.concatenate([v.reshape(-1) for v in vs])
    return jnp.pad(flat, (0, NORM_ROWS * LANES - flat.shape[0])).reshape(NORM_ROWS, LANES)


def _unpack_norm(p):
    flat, out, at = p.reshape(-1), [], 0
    for n in NORM_SIZES:
        out.append(flat[at:at + n].reshape(1, n))
        at += n
    return out


def _rope_tables(positions):
    inv_freq = ROPE_THETA ** (-jnp.arange(0, MLA_ROPE, 2, dtype=F32) / MLA_ROPE)
    ang = positions.astype(F32)[:, None] * inv_freq
    cos, sin, z = jnp.cos(ang), jnp.sin(ang), jnp.zeros_like(ang)
    return (jnp.concatenate([cos, cos, z, z], axis=1), jnp.concatenate([z, sin, z, z], axis=1),
            jnp.concatenate([-sin, z, z, z], axis=1))


def _local_step(x, positions, pre_w, win, q_w, wq, kv_w, wk, wv, wout, post_w, target):
    s = x.shape[0]
    rc, rs1, rs2 = _rope_tables(positions)
    h = _prenorm(x, pre_w)
    projb = _matmul(h, win, mode="nn", out_dtype=BF16, tm=512, tn=1024, tk=D_MODEL, name="proj_b", n=PB_W)
    projf = _matmul(h, win, mode="nn", out_dtype=F32, tm=512, tn=1024, tk=D_MODEL, name="proj_f", n=PF_W,
                    b_off=PB_W // 1024)
    oa, tot = _sb_fwd(projb)
    nq, nkv, qm, km, vm = _mla_prep(projf, q_w, kv_w, wq, wk, wv, rc, rs1, rs2)
    ob, lse = _mla_fwd(qm, km, vm)
    mixed, dy, dout, err2, dpost = _out_post(oa, ob, projf, wout, x, target, post_w)

    dwout = _matmul(mixed, dy, mode="tn", out_dtype=F32, tm=1024, tn=1024, tk=min(4096, s), name="dw_out")
    dmixed = _matmul(dy, wout, mode="nt", out_dtype=F32, tm=512, tn=1024, tk=D_MODEL, name="d_mixed")
    dqa, dka, dva, dga = _sb_bwd(projb, projf, dmixed, oa, tot)
    dqm, dkm, dvm, dgb = _mla_bwd(qm, km, vm, projf, dmixed, ob, lse)
    dqp, dkn, dcq, dckv, dkr, dqw, dkvw = _mla_bwd_post(dqm, dkm, dvm, projf, q_w, kv_w, wq, wk, wv, rc, rs1, rs2)
    tks = min(4096, s)
    dwq = _matmul(nq, dqp, mode="tn", out_dtype=F32, tm=Q_RANK, tn=1024, tk=tks, name="dw_q")
    dwk = _matmul(nkv, dkn, mode="tn", out_dtype=F32, tm=KV_RANK, tn=1024, tk=tks, name="dw_k")
    dwv = _matmul(nkv, dvm, mode="tn", out_dtype=F32, tm=KV_RANK, tn=1024, tk=tks, name="dw_v")
    dproj = jnp.concatenate([dqa, dka, dva, dga, dgb, dcq, dckv, dkr, jnp.zeros((s, 128), BF16)], axis=1)
    dwin = _matmul(h, dproj, mode="tn", out_dtype=F32, tm=1024, tn=1024, tk=tks, name="dw_in")
    gx, dpre = _pre_bwd(dproj, win, x, pre_w, dout)
    return err2, gx, dpre, dwin, dqw, dwq, dkvw, dwk, dwv, dwout, dpost


def _kernel_layouts(full):
    w_in, w_q_up, w_kv_up, w_out = full
    win = _perm_in(w_in)
    wq = jnp.pad(w_q_up.reshape(Q_RANK, HEADS, 192), ((0, 0), (0, 0), (0, 64))).reshape(Q_RANK, HEADS * MLA_QK_PAD)
    kv = w_kv_up.reshape(KV_RANK, HEADS, 256)
    wk = kv[:, :, :128].reshape(KV_RANK, SB_WIDTH)
    wv = kv[:, :, 128:].reshape(KV_RANK, SB_WIDTH)
    return win, wq, wk, wv, w_out


def _original_layouts(dwin, dwq, dwk, dwv):
    dwi = _unperm_in(dwin[:, :D_IN])
    dq = dwq.reshape(Q_RANK, HEADS, MLA_QK_PAD)[:, :, :192].reshape(Q_RANK, HEADS * 192)
    dkv = jnp.concatenate([dwk.reshape(KV_RANK, HEADS, 128), dwv.reshape(KV_RANK, HEADS, 128)], axis=2)
    return dwi, dq, dkv.reshape(KV_RANK, 2 * SB_WIDTH)


def kernel(x, positions, pre_norm_w, w_in, q_norm_w, w_q_up, kv_norm_w, w_kv_up, w_out, post_norm_w, loss_target, m_pre_norm_w, m_w_in, m_q_norm_w, m_w_q_up, m_kv_norm_w, m_w_kv_up, m_w_out, m_post_norm_w, v_pre_norm_w, v_w_in, v_q_norm_w, v_w_q_up, v_kv_norm_w, v_w_kv_up, v_w_out, v_post_norm_w):
    c = lax.axis_index("c")
    chip = 2 * lax.axis_index("x") + lax.axis_index("y")
    shards = (w_in[0], w_q_up[0], w_kv_up[0], w_out[0])
    mine16 = [w.astype(BF16) for w in shards]
    others = _gather_weights(mine16)
    slot = lambda w, k: jnp.where(chip == k, mine16[w], others[w][k])
    full = tuple(jnp.concatenate([slot(w, k) for k in range(4)], axis=ax) for w, ax in ((0, 1), (1, 1), (2, 1), (3, 0)))
    win, wq, wk, wv, wout = _kernel_layouts(full)

    err2, gx, dpre, dwin, dqw, dwq, dkvw, dwk, dwv, dwout, dpost = _local_step(
        x[0], positions[0], pre_norm_w, win, q_norm_w, wq, kv_norm_w, wk, wv, wout, post_norm_w, loss_target[0])
    loss = lax.psum(0.5 * jnp.sum(err2) / D_MODEL, ("x", "y", "c"))

    dwi, dq, dkv = _original_layouts(dwin, dwq, dwk, dwv)
    grads = [_by_chip_and_half(g, ax) for g, ax in ((dwi, 1), (dq, 1), (dkv, 1), (dwout, 0))]
    place = jnp.stack([c, chip])
    halves = _swap_halves(grads)
    pairs = [_pair_sum(place, grads[w], halves[w], "pair_sum_" + W_NAMES[w]) for w in range(NW)]
    gots = _exchange_chips([p[0] for p in pairs])
    ms = (m_w_in[0], m_w_q_up[0], m_w_kv_up[0], m_w_out[0])
    vs = (v_w_in[0], v_w_q_up[0], v_w_kv_up[0], v_w_out[0])
    done = [_sum_adamw(c.reshape(1), pairs[w][1], gots[w], shards[w], ms[w], vs[w], "sum_adamw_" + W_NAMES[w])
            for w in range(NW)]
    shared = _share_with_sibling([done[w][k] for k in range(4) for w in range(NW)])
    big = [shared[NW * k:NW * (k + 1)] for k in range(4)]

    small = _norm_allreduce_adamw(
        _pack_norm([dpre, dqw, dkvw, dpost]), _pack_norm([pre_norm_w, q_norm_w, kv_norm_w, post_norm_w]),
        _pack_norm([m_pre_norm_w, m_q_norm_w, m_kv_norm_w, m_post_norm_w]),
        _pack_norm([v_pre_norm_w, v_q_norm_w, v_kv_norm_w, v_post_norm_w]))
    small = [_unpack_norm(p) for p in small]

    def group(k):
        n, b = small[k], big[k]
        return (n[0], b[0][None], n[1], b[1][None], n[2], b[2][None], b[3][None], n[3])

    return (loss, gx[None], *group(0), *group(1), *group(2), *group(3))
```

```python
import functools
import math

import numpy as np
import jax
import jax.numpy as jnp
from jax import lax
from jax.experimental import pallas as pl
from jax.experimental.pallas import tpu as pltpu

F32 = jnp.float32
BF16 = jnp.bfloat16
MESH = pl.DeviceIdType.MESH

D_MODEL = 2048
HEADS = 8
HEAD_DIM = 128
SB_WIDTH = HEADS * HEAD_DIM
MLA_ROPE = 64
MLA_QK_PAD = 256
Q_RANK = 512
KV_RANK = 256
CHUNK = 64
EPS = 1e-6
ROPE_THETA = 10000.0
D_IN = 5952
D_INP = 6144
PB_W = 3072
PF_W = D_INP - PB_W
SB_SCALE = 1.0 / math.sqrt(HEAD_DIM)
MLA_SCALE = 1.0 / math.sqrt(HEAD_DIM + MLA_ROPE)
NEG = -1e30

ADAM_LR, ADAM_B1, ADAM_B2, ADAM_EPS, ADAM_WD, ADAM_STEP = 0.001, 0.9, 0.999, 1e-08, 0.01, 10
ADAM_C1 = 1.0 - ADAM_B1 ** ADAM_STEP
ADAM_C2 = 1.0 - ADAM_B2 ** ADAM_STEP

ANY = pl.BlockSpec(memory_space=pl.ANY)
VMEM_LIMIT = 56 * 1024 * 1024
TQ = 1024
TK = 256
MLA_TK = 512
UNROLL = 2


def _cp(sem=None, **kw):
    return pltpu.CompilerParams(dimension_semantics=sem, vmem_limit_bytes=VMEM_LIMIT, **kw)


def _dot(a, b, dims):
    return lax.dot_general(a, b, (dims, ((), ())), preferred_element_type=F32)


def _nn(a, b):
    return _dot(a, b, ((1,), (0,)))


def _nt(a, b):
    return _dot(a, b, ((1,), (1,)))


def _tn(a, b):
    return _dot(a, b, ((0,), (0,)))


def _rope(x, c, s1, s2):
    return x * c + pltpu.roll(x, 32, 1) * s1 + pltpu.roll(x, 96, 1) * s2


def _rope_t(x, c, s1, s2):
    return x * c - pltpu.roll(x, 32, 1) * s1 - pltpu.roll(x, 96, 1) * s2


def _silu_parts(g):
    sg = jax.nn.sigmoid(g)
    return g * sg, sg * (1.0 + g * (1.0 - sg))


def _matmul(a, b, *, mode, out_dtype, tm, tn, tk, name, n=None, b_off=0):
    if mode == "tn":
        kk, m = a.shape
        n = b.shape[1] if n is None else n
    else:
        m, kk = a.shape
        n = (b.shape[1] if mode == "nn" else b.shape[0]) if n is None else n
    nk = kk // tk
    a_spec = {"nn": pl.BlockSpec((tm, tk), lambda j, i, k: (i, k)),
              "nt": pl.BlockSpec((tm, tk), lambda j, i, k: (i, k)),
              "tn": pl.BlockSpec((tk, tm), lambda j, i, k: (k, i))}[mode]
    b_spec = {"nn": pl.BlockSpec((tk, tn), lambda j, i, k: (k, j + b_off)),
              "nt": pl.BlockSpec((tn, tk), lambda j, i, k: (j, k)),
              "tn": pl.BlockSpec((tk, tn), lambda j, i, k: (k, j))}[mode]
    dims = {"nn": ((1,), (0,)), "nt": ((1,), (1,)), "tn": ((0,), (0,))}[mode]

    def body(a_ref, b_ref, o_ref, acc_ref):
        k = pl.program_id(2)
        part = _dot(a_ref[...], b_ref[...], dims)
        if nk == 1:
            o_ref[...] = part.astype(out_dtype)
        else:
            @pl.when(k == 0)
            def _():
                acc_ref[...] = part

            @pl.when(k > 0)
            def _():
                acc_ref[...] += part

            @pl.when(k == nk - 1)
            def _():
                o_ref[...] = acc_ref[...].astype(out_dtype)

    return pl.pallas_call(
        body, name=name, grid=(n // tn, m // tm, nk),
        in_specs=[a_spec, b_spec], out_specs=pl.BlockSpec((tm, tn), lambda j, i, k: (i, j)),
        out_shape=jax.ShapeDtypeStruct((m, n), out_dtype),
        scratch_shapes=[pltpu.VMEM((tm, tn) if nk > 1 else (8, 128), F32)],
        compiler_params=_cp(("parallel", "parallel", "arbitrary")),
    )(a, b)


def _prenorm(x, w, tm=512):
    s, d = x.shape

    def body(x_ref, w_ref, h_ref):
        xv = x_ref[...]
        r = lax.rsqrt(jnp.mean(xv * xv, axis=1, keepdims=True) + EPS)
        h_ref[...] = ((xv * r) * w_ref[...]).astype(BF16)

    return pl.pallas_call(
        body, name="prenorm", grid=(s // tm,),
        in_specs=[pl.BlockSpec((tm, d), lambda i: (i, 0)), pl.BlockSpec((1, d), lambda i: (0, 0))],
        out_specs=pl.BlockSpec((tm, d), lambda i: (i, 0)),
        out_shape=jax.ShapeDtypeStruct((s, d), BF16),
        compiler_params=_cp(("parallel",)),
    )(x, w)


def _tri(n, cmp, value):
    row = lax.broadcasted_iota(jnp.int32, (n, n), 0)
    col = lax.broadcasted_iota(jnp.int32, (n, n), 1)
    return jnp.where(cmp(row, col), value, 0.0).astype(BF16)


def _softplus(z, mask):
    sp = jnp.maximum(z, 0.0) + jnp.log(1.0 + jnp.exp(-jnp.abs(z)))
    return sp if mask is None else jnp.where(mask, sp, 0.0)


def _tiles(s, tk=TK):
    tq = min(TQ, s)
    return tq, tk, math.gcd(tq // tk, UNROLL)


def _band_masks(tq, tk, fn):
    out = []
    for b in range(tq // tk):
        key = lax.broadcasted_iota(jnp.int32, (tk, tq - b * tk), 0) + b * tk
        qry = lax.broadcasted_iota(jnp.int32, (tk, tq - b * tk), 1) + b * tk
        out.append(fn(qry, key))
    return out


def _row_spec(tq):
    return pl.BlockSpec((8, tq), lambda h, i: (h, i))


def _sb_store_shape(s, tq, tk):
    nb, nq = tq // tk, s // tq
    return (HEADS, nb * nq * (nq + 1) // 2, tk, tq)


def _sb_fwd(projb):
    s = projb.shape[0]
    tq, tk, _ = _tiles(s)
    nb = tq // tk

    def body(q_ref, k_ref, v_ref, o_ref, a_st, b_st, acc_ref, car_ref, z_ref, a_buf, b_buf, sems):
        h, i = pl.program_id(0), pl.program_id(1)
        q = q_ref[...]
        from_here = _tri(tk, lambda s_, j: j >= s_, -1.0)
        masks = _band_masks(tq, tk, lambda t, s_: s_ < t)
        acc_ref[...] = jnp.zeros_like(acc_ref)
        car_ref[...] = jnp.zeros_like(car_ref)
        first = nb * (i * (i + 1) // 2)

        def stores(st, g):
            at = pl.ds(first + g * nb, nb)
            return (pltpu.make_async_copy(a_buf.at[st], a_st.at[h, at], sems.at[st, 0]),
                    pltpu.make_async_copy(b_buf.at[st], b_st.at[h, at], sems.at[st, 1]))

        def tile(j, mask, off, st, u):
            rows = pl.ds(pl.multiple_of(j * tk, tk), tk)
            z_ref[:, off:] = _nt(k_ref[rows, :], q[off:, :]) * SB_SCALE
            z = z_ref[:, off:]
            sp = _softplus(z, mask)
            a = jnp.exp(z + _nn(from_here, sp.astype(BF16)) + car_ref[:, off:])
            beta = jnp.exp(z - sp)
            if mask is not None:
                a = jnp.where(mask, a, 0.0)
                beta = jnp.where(mask, beta, 0.0)
            a = a.astype(BF16)
            if off:
                a_buf[st, u, :, :off] = jnp.zeros((tk, off), BF16)
                b_buf[st, u, :, :off] = jnp.zeros((tk, off), BF16)
            a_buf[st, u, :, off:] = a
            b_buf[st, u, :, off:] = beta.astype(BF16)
            acc_ref[:, off:] += _tn(v_ref[rows, :], a)
            car_ref[:, off:] -= jnp.sum(sp, axis=0, keepdims=True)

        for b in reversed(range(nb)):
            tile(i * nb + b, masks[b], b * tk, 0, b)
        for cp in stores(0, i):
            cp.start()

        def step(jj, c):
            st, g = (jj + 1) % 2, i - 1 - jj

            @pl.when(jj >= 1)
            def _():
                for cp in stores(st, g):
                    cp.wait()

            for u in reversed(range(nb)):
                tile(g * nb + u, None, 0, st, u)
            for cp in stores(st, g):
                cp.start()
            return c

        lax.fori_loop(0, i, step, 0)
        for cp in stores(0, 0):
            cp.wait()

        @pl.when(i >= 1)
        def _():
            for cp in stores(1, 0):
                cp.wait()

        o_ref[...] = acc_ref[...].T

    blk = pl.BlockSpec((tq, HEAD_DIM), lambda h, i: (i, h))
    st = jax.ShapeDtypeStruct(_sb_store_shape(s, tq, tk), BF16)
    return pl.pallas_call(
        body, name="sb_fwd", grid=(HEADS, s // tq),
        in_specs=[blk, pl.BlockSpec((s, HEAD_DIM), lambda h, i: (0, HEADS + h)),
                  pl.BlockSpec((s, HEAD_DIM), lambda h, i: (0, 2 * HEADS + h))],
        out_specs=[blk, ANY, ANY],
        out_shape=[jax.ShapeDtypeStruct((s, SB_WIDTH), F32), st, st],
        scratch_shapes=[pltpu.VMEM((HEAD_DIM, tq), F32), pltpu.VMEM((1, tq), F32), pltpu.VMEM((tk, tq), F32),
                        pltpu.VMEM((2, nb, tk, tq), BF16), pltpu.VMEM((2, nb, tk, tq), BF16),
                        pltpu.SemaphoreType.DMA((2, 2))],
        compiler_params=_cp(("arbitrary", "arbitrary")),
    )(projb, projb, projb)


def _mla_prep(projf, qw, kvw, wq, wk, wv, rc, rs1, rs2, tm=512):
    s = projf.shape[0]

    def body(cq_ref, ckv_ref, kr_ref, qw_ref, kvw_ref, wq_ref, wk_ref, wv_ref, c_ref, s1_ref, s2_ref,
             nq_ref, nkv_ref, q_ref, k_ref, v_ref):
        c, s1, s2 = c_ref[...], s1_ref[...], s2_ref[...]
        cq = cq_ref[...]
        nq = ((cq * lax.rsqrt(jnp.mean(cq * cq, axis=1, keepdims=True) + EPS)) * qw_ref[...]).astype(BF16)
        nq_ref[...] = nq
        qf = _nn(nq, wq_ref[...])
        ckv = ckv_ref[...]
        nkv = ((ckv * lax.rsqrt(jnp.mean(ckv * ckv, axis=1, keepdims=True) + EPS)) * kvw_ref[...]).astype(BF16)
        nkv_ref[...] = nkv
        kn = _nn(nkv, wk_ref[...])
        v_ref[...] = _nn(nkv, wv_ref[...]).astype(BF16)
        krot = _rope(kr_ref[...], c, s1, s2).astype(BF16)
        for h in range(HEADS):
            lo = h * MLA_QK_PAD
            q_ref[:, lo:lo + 128] = qf[:, lo:lo + 128].astype(BF16)
            q_ref[:, lo + 128:lo + 256] = _rope(qf[:, lo + 128:lo + 256], c, s1, s2).astype(BF16)
            k_ref[:, lo:lo + 128] = kn[:, h * 128:(h + 1) * 128].astype(BF16)
            k_ref[:, lo + 128:lo + 256] = krot

    row = lambda w, b: pl.BlockSpec((tm, w), lambda i: (i, b))
    full = lambda a: pl.BlockSpec(a.shape, lambda i: (0, 0))
    return pl.pallas_call(
        body, name="mla_prep", grid=(s // tm,),
        in_specs=[row(Q_RANK, 2048 // Q_RANK), row(KV_RANK, 2560 // KV_RANK), row(128, 2816 // 128),
                  full(qw), full(kvw), full(wq), full(wk), full(wv), row(128, 0), row(128, 0), row(128, 0)],
        out_specs=[row(Q_RANK, 0), row(KV_RANK, 0), row(HEADS * MLA_QK_PAD, 0), row(HEADS * MLA_QK_PAD, 0),
                   row(SB_WIDTH, 0)],
        out_shape=[jax.ShapeDtypeStruct((s, Q_RANK), BF16), jax.ShapeDtypeStruct((s, KV_RANK), BF16),
                   jax.ShapeDtypeStruct((s, HEADS * MLA_QK_PAD), BF16),
                   jax.ShapeDtypeStruct((s, HEADS * MLA_QK_PAD), BF16),
                   jax.ShapeDtypeStruct((s, SB_WIDTH), BF16)],
        compiler_params=_cp(("parallel",)),
    )(projf, projf, projf, qw, kvw, wq, wk, wv, rc, rs1, rs2)


def _mla_mask(qry, key):
    return (key // CHUNK) <= (qry // CHUNK)


def _mla_fwd(qm, km, vm):
    s = qm.shape[0]
    tq, tk, unroll = _tiles(s, MLA_TK)
    nb = tq // tk

    def body(q_ref, k_ref, v_ref, o_ref, lse_ref, acc_ref, m_ref, l_ref):
        i = pl.program_id(1)
        q = q_ref[...]
        masks = _band_masks(tq, tk, _mla_mask)
        acc_ref[...] = jnp.zeros_like(acc_ref)
        m_ref[...] = jnp.full_like(m_ref, NEG)
        l_ref[...] = jnp.zeros_like(l_ref)

        def tile(j, mask, off):
            rows = pl.ds(pl.multiple_of(j * tk, tk), tk)
            sc = _nt(k_ref[rows, :], q[off:, :]) * MLA_SCALE
            if mask is not None:
                sc = jnp.where(mask, sc, NEG)
            m_old = m_ref[:, off:]
            m_new = jnp.maximum(m_old, jnp.max(sc, axis=0, keepdims=True))
            p = jnp.exp(sc - m_new)
            alpha = jnp.exp(m_old - m_new)
            l_ref[:, off:] = alpha * l_ref[:, off:] + jnp.sum(p, axis=0, keepdims=True)
            acc_ref[:, off:] = alpha * acc_ref[:, off:] + _tn(v_ref[rows, :], p.astype(BF16))
            m_ref[:, off:] = m_new

        for b in range(nb):
            tile(i * nb + b, masks[b], b * tk)

        def step(jj, c):
            for u in range(unroll):
                tile(jj * unroll + u, None, 0)
            return c

        lax.fori_loop(0, i * nb // unroll, step, 0)
        o_ref[...] = (acc_ref[...] / l_ref[...]).T
        lse_ref[...] = jnp.broadcast_to(m_ref[...] + jnp.log(l_ref[...]), (8, tq))

    return pl.pallas_call(
        body, name="mla_fwd", grid=(HEADS, s // tq),
        in_specs=[pl.BlockSpec((tq, MLA_QK_PAD), lambda h, i: (i, h)),
                  pl.BlockSpec((s, MLA_QK_PAD), lambda h, i: (0, h)),
                  pl.BlockSpec((s, HEAD_DIM), lambda h, i: (0, h))],
        out_specs=[pl.BlockSpec((tq, HEAD_DIM), lambda h, i: (i, h)), _row_spec(tq)],
        out_shape=[jax.ShapeDtypeStruct((s, SB_WIDTH), F32), jax.ShapeDtypeStruct((8 * HEADS, s), F32)],
        scratch_shapes=[pltpu.VMEM((HEAD_DIM, tq), F32), pltpu.VMEM((1, tq), F32), pltpu.VMEM((1, tq), F32)],
        compiler_params=_cp(("parallel", "arbitrary")),
    )(qm, km, vm)


def _out_post(oa, ob, projf, wout, x, target, pw, tm=256):
    s, d = x.shape

    def body(oa_ref, ob_ref, ga_ref, gb_ref, w_ref, x_ref, t_ref, pw_ref,
             mix_ref, dy_ref, dout_ref, loss_ref, dpw_ref):
        i = pl.program_id(0)
        sa, _ = _silu_parts(ga_ref[...])
        sb, _ = _silu_parts(gb_ref[...])
        mix_ref[:, :SB_WIDTH] = (oa_ref[...] * sa).astype(BF16)
        mix_ref[:, SB_WIDTH:] = (ob_ref[...] * sb).astype(BF16)
        y = _nn(mix_ref[...], w_ref[...])
        r = lax.rsqrt(jnp.mean(y * y, axis=1, keepdims=True) + EPS)
        yhat = y * r
        pwv = pw_ref[...]
        err = (x_ref[...] + yhat * pwv) - t_ref[...]
        dout = err * (1.0 / d)
        dout_ref[...] = dout
        g = dout * pwv
        dy_ref[...] = (r * (g - yhat * jnp.mean(g * yhat, axis=1, keepdims=True))).astype(BF16)

        @pl.when(i == 0)
        def _():
            loss_ref[...] = jnp.zeros_like(loss_ref)
            dpw_ref[...] = jnp.zeros_like(dpw_ref)

        loss_ref[...] += jnp.sum(err * err, axis=0, keepdims=True)
        dpw_ref[...] += jnp.sum(dout * yhat, axis=0, keepdims=True)

    row = lambda w, b: pl.BlockSpec((tm, w), lambda i: (i, b))
    vec = pl.BlockSpec((1, d), lambda i: (0, 0))
    return pl.pallas_call(
        body, name="out_post", grid=(s // tm,),
        in_specs=[row(SB_WIDTH, 0), row(SB_WIDTH, 0), row(SB_WIDTH, 0), row(SB_WIDTH, 1),
                  pl.BlockSpec(wout.shape, lambda i: (0, 0)), row(d, 0), row(d, 0), vec],
        out_specs=[row(d, 0), row(d, 0), row(d, 0), vec, vec],
        out_shape=[jax.ShapeDtypeStruct((s, d), BF16), jax.ShapeDtypeStruct((s, d), BF16),
                   jax.ShapeDtypeStruct((s, d), F32), jax.ShapeDtypeStruct((1, d), F32),
                   jax.ShapeDtypeStruct((1, d), F32)],
        compiler_params=_cp(("arbitrary",)),
    )(oa, ob, projf, projf, wout, x, target, pw)


def _sb_bwd(projb, projf, dmixed, oa, a_st, b_st):
    s = projb.shape[0]
    tq, tk, _ = _tiles(s)
    nb, nq = tq // tk, s // tq

    def body(q_ref, k_ref, v_ref, dm_ref, g_ref, o_ref, a_st, b_st, dq_ref, dk_ref, dv_ref, dg_ref,
             dka_ref, dva_ref, dqa_ref, cg_ref, a_buf, b_buf, sems):
        h, i = pl.program_id(0), pl.program_id(1)

        @pl.when(i == 0)
        def _():
            dka_ref[...] = jnp.zeros_like(dka_ref)
            dva_ref[...] = jnp.zeros_like(dva_ref)

        q = q_ref[...]
        silu, dsilu = _silu_parts(g_ref[...])
        dm = dm_ref[...]
        dg_ref[...] = (dm * o_ref[...] * dsilu).astype(BF16)
        do = (dm * silu).astype(BF16)
        up_to_here = _tri(tk, lambda s_, j: j <= s_, 1.0)
        dqa_ref[...] = jnp.zeros_like(dqa_ref)
        cg_ref[...] = jnp.zeros_like(cg_ref)
        first = nb * (i * (i + 1) // 2)

        def loads(st, g):
            at = pl.ds(first + g * nb, nb)
            return (pltpu.make_async_copy(a_st.at[h, at], a_buf.at[st], sems.at[st, 0]),
                    pltpu.make_async_copy(b_st.at[h, at], b_buf.at[st], sems.at[st, 1]))

        def tile(j, off, st, u):
            rows = pl.ds(pl.multiple_of(j * tk, tk), tk)
            k = k_ref[rows, :]
            a = a_buf[st, u, :, off:]
            g = a.astype(F32) * _nt(v_ref[rows, :], do[off:, :])
            dva_ref[rows, :] += _nn(a, do[off:, :])
            cum = _nn(up_to_here, g.astype(BF16)) + cg_ref[:, off:]
            dz = ((g - b_buf[st, u, :, off:].astype(F32) * cum) * SB_SCALE).astype(BF16)
            dqa_ref[:, off:] += _tn(k, dz)
            dka_ref[rows, :] += _nn(dz, q[off:, :])
            cg_ref[:, off:] += jnp.sum(g, axis=0, keepdims=True)

        for cp in loads(0, 0):
            cp.start()

        def step(g, c):
            st = g % 2
            for cp in loads(1 - st, g + 1):
                cp.start()
            for cp in loads(st, g):
                cp.wait()
            for u in range(nb):
                tile(g * nb + u, 0, st, u)
            return c

        lax.fori_loop(0, i, step, 0)
        for cp in loads(i % 2, i):
            cp.wait()
        for b in range(nb):
            tile(i * nb + b, b * tk, i % 2, b)
        dq_ref[...] = dqa_ref[...].T.astype(BF16)

        @pl.when(i == nq - 1)
        def _():
            dk_ref[...] = dka_ref[...].astype(BF16)
            dv_ref[...] = dva_ref[...].astype(BF16)

    blk = lambda off: pl.BlockSpec((tq, HEAD_DIM), lambda h, i: (i, off + h))
    whole = lambda off: pl.BlockSpec((s, HEAD_DIM), lambda h, i: (0, off + h))
    o_sd = jax.ShapeDtypeStruct((s, SB_WIDTH), BF16)
    return pl.pallas_call(
        body, name="sb_bwd", grid=(HEADS, nq),
        in_specs=[blk(0), whole(HEADS), whole(2 * HEADS), blk(0), blk(0), blk(0), ANY, ANY],
        out_specs=[blk(0), whole(0), whole(0), blk(0)],
        out_shape=[o_sd, o_sd, o_sd, o_sd],
        scratch_shapes=[pltpu.VMEM((s, HEAD_DIM), F32), pltpu.VMEM((s, HEAD_DIM), F32),
                        pltpu.VMEM((HEAD_DIM, tq), F32), pltpu.VMEM((1, tq), F32),
                        pltpu.VMEM((2, nb, tk, tq), BF16), pltpu.VMEM((2, nb, tk, tq), BF16),
                        pltpu.SemaphoreType.DMA((2, 2))],
        compiler_params=_cp(("arbitrary", "arbitrary")),
    )(projb, projb, projb, dmixed, projf, oa, a_st, b_st)


def _mla_bwd(qm, km, vm, projf, dmixed, ob, lse):
    s = qm.shape[0]
    tq, tk, unroll = _tiles(s, MLA_TK)
    nb, nq = tq // tk, s // tq

    def body(q_ref, k_ref, v_ref, dm_ref, g_ref, o_ref, lse_ref, dq_ref, dk_ref, dv_ref, dg_ref,
             dva_ref, dqa_ref):
        i = pl.program_id(1)

        @pl.when(i == 0)
        def _():
            dk_ref[...] = jnp.zeros_like(dk_ref)
            dva_ref[...] = jnp.zeros_like(dva_ref)

        q = q_ref[...]
        silu, dsilu = _silu_parts(g_ref[...])
        dm = dm_ref[...]
        o = o_ref[...]
        dg_ref[...] = (dm * o * dsilu).astype(BF16)
        dof = dm * silu
        delta = jnp.sum((dof * o).T, axis=0, keepdims=True)
        do = dof.astype(BF16)
        lse = lse_ref[0:1, :]
        masks = _band_masks(tq, tk, _mla_mask)
        dqa_ref[...] = jnp.zeros_like(dqa_ref)

        def tile(j, mask, off):
            rows = pl.ds(pl.multiple_of(j * tk, tk), tk)
            k = k_ref[rows, :]
            p = jnp.exp(_nt(k, q[off:, :]) * MLA_SCALE - lse[:, off:])
            if mask is not None:
                p = jnp.where(mask, p, 0.0)
            ds = (p * (_nt(v_ref[rows, :], do[off:, :]) - delta[:, off:]) * MLA_SCALE).astype(BF16)
            dva_ref[rows, :] += _nn(p.astype(BF16), do[off:, :])
            dk_ref[rows, :] += _nn(ds, q[off:, :])
            dqa_ref[:, off:] += _tn(k, ds)

        def step(jj, c):
            for u in range(unroll):
                tile(jj * unroll + u, None, 0)
            return c

        lax.fori_loop(0, i * nb // unroll, step, 0)
        for b in range(nb):
            tile(i * nb + b, masks[b], b * tk)
        dq_ref[...] = dqa_ref[...].T

        @pl.when(i == nq - 1)
        def _():
            dv_ref[...] = dva_ref[...].astype(BF16)

    blk = lambda w, off: pl.BlockSpec((tq, w), lambda h, i: (i, off + h))
    whole = lambda w: pl.BlockSpec((s, w), lambda h, i: (0, h))
    return pl.pallas_call(
        body, name="mla_bwd", grid=(HEADS, nq),
        in_specs=[blk(MLA_QK_PAD, 0), whole(MLA_QK_PAD), whole(HEAD_DIM), blk(HEAD_DIM, HEADS),
                  blk(HEAD_DIM, HEADS), blk(HEAD_DIM, 0), _row_spec(tq)],
        out_specs=[blk(MLA_QK_PAD, 0), whole(MLA_QK_PAD), whole(HEAD_DIM), blk(HEAD_DIM, 0)],
        out_shape=[jax.ShapeDtypeStruct((s, HEADS * MLA_QK_PAD), F32),
                   jax.ShapeDtypeStruct((s, HEADS * MLA_QK_PAD), F32),
                   jax.ShapeDtypeStruct((s, SB_WIDTH), BF16), jax.ShapeDtypeStruct((s, SB_WIDTH), BF16)],
        scratch_shapes=[pltpu.VMEM((s, HEAD_DIM), F32), pltpu.VMEM((MLA_QK_PAD, tq), F32)],
        compiler_params=_cp(("parallel", "arbitrary")),
    )(qm, km, vm, dmixed, projf, ob, lse)


def _norm_bwd(x, w, dn):
    r = lax.rsqrt(jnp.mean(x * x, axis=1, keepdims=True) + EPS)
    xhat = x * r
    g = dn * w
    return r * (g - xhat * jnp.mean(g * xhat, axis=1, keepdims=True)), dn * xhat


def _mla_bwd_post(dqm, dkm, dvm, projf, qw, kvw, wq, wk, wv, rc, rs1, rs2, tm=256):
    s = dqm.shape[0]

    def body(dq_ref, dk_ref, dv_ref, cq_ref, ckv_ref, qw_ref, kvw_ref, wq_ref, wk_ref, wv_ref,
             c_ref, s1_ref, s2_ref, dqp_ref, dkn_ref, dcq_ref, dckv_ref, dkr_ref, dqw_ref, dkvw_ref):
        i = pl.program_id(0)
        c, s1, s2 = c_ref[...], s1_ref[...], s2_ref[...]
        drot = jnp.zeros((tm, 128), F32)
        for h in range(HEADS):
            lo = h * MLA_QK_PAD
            dqp_ref[:, lo:lo + 128] = dq_ref[:, lo:lo + 128].astype(BF16)
            dqp_ref[:, lo + 128:lo + 256] = _rope_t(dq_ref[:, lo + 128:lo + 256], c, s1, s2).astype(BF16)
            dkn_ref[:, h * 128:(h + 1) * 128] = dk_ref[:, lo:lo + 128].astype(BF16)
            drot = drot + dk_ref[:, lo + 128:lo + 256]
        dkr_ref[...] = _rope_t(drot, c, s1, s2).astype(BF16)
        dcq, dqw = _norm_bwd(cq_ref[...], qw_ref[...], _nt(dqp_ref[...], wq_ref[...]))
        dcq_ref[...] = dcq.astype(BF16)
        dnkv = _nt(dkn_ref[...], wk_ref[...]) + _nt(dv_ref[...], wv_ref[...])
        dckv, dkvw = _norm_bwd(ckv_ref[...], kvw_ref[...], dnkv)
        dckv_ref[...] = dckv.astype(BF16)

        @pl.when(i == 0)
        def _():
            dqw_ref[...] = jnp.zeros_like(dqw_ref)
            dkvw_ref[...] = jnp.zeros_like(dkvw_ref)

        dqw_ref[...] += jnp.sum(dqw, axis=0, keepdims=True)
        dkvw_ref[...] += jnp.sum(dkvw, axis=0, keepdims=True)

    row = lambda w, b: pl.BlockSpec((tm, w), lambda i: (i, b))
    full = lambda a: pl.BlockSpec(a.shape, lambda i: (0, 0))
    sd = jax.ShapeDtypeStruct
    return pl.pallas_call(
        body, name="mla_bwd_post", grid=(s // tm,),
        in_specs=[row(HEADS * MLA_QK_PAD, 0), row(HEADS * MLA_QK_PAD, 0), row(SB_WIDTH, 0),
                  row(Q_RANK, 2048 // Q_RANK), row(KV_RANK, 2560 // KV_RANK),
                  full(qw), full(kvw), full(wq), full(wk), full(wv), row(128, 0), row(128, 0), row(128, 0)],
        out_specs=[row(HEADS * MLA_QK_PAD, 0), row(SB_WIDTH, 0), row(Q_RANK, 0), row(KV_RANK, 0), row(128, 0),
                   pl.BlockSpec((1, Q_RANK), lambda i: (0, 0)), pl.BlockSpec((1, KV_RANK), lambda i: (0, 0))],
        out_shape=[sd((s, HEADS * MLA_QK_PAD), BF16), sd((s, SB_WIDTH), BF16), sd((s, Q_RANK), BF16),
                   sd((s, KV_RANK), BF16), sd((s, 128), BF16), sd((1, Q_RANK), F32), sd((1, KV_RANK), F32)],
        compiler_params=_cp(("arbitrary",)),
    )(dqm, dkm, dvm, projf, projf, qw, kvw, wq, wk, wv, rc, rs1, rs2)


def _pre_bwd(dproj, win, x, pw, dout, tm=512, tk=1024):
    s, d = x.shape
    nk = dproj.shape[1] // tk

    def body(dp_ref, w_ref, x_ref, pw_ref, do_ref, dx_ref, dpw_ref, acc_ref):
        i, k = pl.program_id(0), pl.program_id(1)
        part = _nt(dp_ref[...], w_ref[...])

        @pl.when(k == 0)
        def _():
            acc_ref[...] = part

        @pl.when(k > 0)
        def _():
            acc_ref[...] += part

        @pl.when(k == nk - 1)
        def _():
            dx, dw = _norm_bwd(x_ref[...], pw_ref[...], acc_ref[...])
            dx_ref[...] = do_ref[...] + dx

            @pl.when(i == 0)
            def _():
                dpw_ref[...] = jnp.zeros_like(dpw_ref)

            dpw_ref[...] += jnp.sum(dw, axis=0, keepdims=True)

    rowd = pl.BlockSpec((tm, d), lambda i, k: (i, 0))
    vec = pl.BlockSpec((1, d), lambda i, k: (0, 0))
    return pl.pallas_call(
        body, name="pre_bwd", grid=(s // tm, nk),
        in_specs=[pl.BlockSpec((tm, tk), lambda i, k: (i, k)), pl.BlockSpec((d, tk), lambda i, k: (0, k)),
                  rowd, vec, rowd],
        out_specs=[rowd, vec],
        out_shape=[jax.ShapeDtypeStruct((s, d), F32), jax.ShapeDtypeStruct((1, d), F32)],
        scratch_shapes=[pltpu.VMEM((tm, d), F32)],
        compiler_params=_cp(("arbitrary", "arbitrary")),
    )(dproj, win, x, pw, dout)


def _adamw(w, g, m, v):
    m = ADAM_B1 * m + (1.0 - ADAM_B1) * g
    v = ADAM_B2 * v + (1.0 - ADAM_B2) * (g * g)
    delta = -ADAM_LR * ((m / ADAM_C1) / (jnp.sqrt(v / ADAM_C2) + ADAM_EPS) + ADAM_WD * w)
    return delta, m, v


def _row_block(rows):
    return math.gcd(rows, 128)


def _sum_adamw(core, own, got, w, m, v, name):
    n, hr, cols = got.shape
    tr = _row_block(hr)
    nblk = hr // tr

    def body(c_ref, o_ref, p_ref, w_ref, m_ref, v_ref, g_ref, d_ref, nm_ref, nv_ref):
        g = o_ref[...]
        for k in range(n):
            g = g + p_ref[k].astype(F32)
        g_ref[...] = g
        d_ref[...], nm_ref[...], nv_ref[...] = _adamw(w_ref[...], g, m_ref[...], v_ref[...])

    mine = pl.BlockSpec((tr, cols), lambda i, c: (c[0] * nblk + i, 0))
    sd = jax.ShapeDtypeStruct((2 * hr, cols), F32)
    return pl.pallas_call(
        body, name=name,
        grid_spec=pltpu.PrefetchScalarGridSpec(
            num_scalar_prefetch=1, grid=(nblk,),
            in_specs=[pl.BlockSpec((tr, cols), lambda i, c: (i, 0)),
                      pl.BlockSpec((n, tr, cols), lambda i, c: (0, i, 0)), mine, mine, mine],
            out_specs=[mine, mine, mine, mine]),
        out_shape=[sd, sd, sd, sd],
        compiler_params=_cp(("parallel",)),
    )(core, own, got, w, m, v)


def _pair_sum(place, grads, got, name):
    n, _, hr, cols = grads.shape
    tr = _row_block(hr)

    def body(p_ref, a_ref, b_ref, oa_ref, ob_ref, all_ref, own_ref):
        all_ref[...] = (a_ref[:, 0] + b_ref[...]).astype(BF16)
        own_ref[...] = oa_ref[0, 0] + ob_ref[0]

    return pl.pallas_call(
        body, name=name,
        grid_spec=pltpu.PrefetchScalarGridSpec(
            num_scalar_prefetch=1, grid=(hr // tr,),
            in_specs=[pl.BlockSpec((n, 1, tr, cols), lambda i, p: (0, p[0], i, 0)),
                      pl.BlockSpec((n, tr, cols), lambda i, p: (0, i, 0)),
                      pl.BlockSpec((1, 1, tr, cols), lambda i, p: (p[1], p[0], i, 0)),
                      pl.BlockSpec((1, tr, cols), lambda i, p: (p[1], i, 0))],
            out_specs=[pl.BlockSpec((n, tr, cols), lambda i, p: (0, i, 0)),
                       pl.BlockSpec((tr, cols), lambda i, p: (i, 0))]),
        out_shape=[jax.ShapeDtypeStruct((n, hr, cols), BF16), jax.ShapeDtypeStruct((hr, cols), F32)],
        compiler_params=_cp(("parallel",)),
    )(place, grads, got, grads, got)


def _place():
    return lax.axis_index("x"), lax.axis_index("y"), lax.axis_index("c")


W_NAMES = ("w_in", "w_q_up", "w_kv_up", "w_out")
NW = len(W_NAMES)


def _comm_call(body, name, ins, out_shape, n_copies, aliases=None):
    return pl.pallas_call(
        body, name=name, in_specs=[ANY] * len(ins), out_specs=[ANY] * len(out_shape), out_shape=out_shape,
        input_output_aliases=aliases or {},
        scratch_shapes=[pltpu.SemaphoreType.DMA((n_copies,)), pltpu.SemaphoreType.DMA((n_copies,))],
        compiler_params=pltpu.CompilerParams(has_side_effects=True),
    )(*ins)


def _gather_weights(shards):
    def body(*refs):
        ws, outs = refs[:NW], refs[NW:2 * NW]
        send_sems, recv_sems = refs[2 * NW:]
        x, y, c = _place()
        chips = [(1 - x, y), (x, 1 - y), (1 - x, 1 - y)]

        def rows(w, chip, core):
            half = ws[w].shape[0] // 2
            return outs[w].at[2 * chip[0] + chip[1], pl.ds(core * half, half), :]

        def copy(w, k, chip, core, to, own=False):
            half = ws[w].shape[0] // 2
            return pltpu.make_async_remote_copy(
                src_ref=ws[w].at[pl.ds(core * half, half), :] if own else rows(w, chip, core),
                dst_ref=rows(w, chip, core), send_sem=send_sems.at[6 * w + k], recv_sem=recv_sems.at[6 * w + k],
                device_id=to, device_id_type=MESH)

        first = [copy(w, j, (x, y), c, (*chip, c), own=True) for w in range(NW) for j, chip in enumerate(chips)]
        for cp in first:
            cp.start()
        passed = []
        for w in range(NW):
            for j, chip in enumerate(chips):
                copy(w, j, chip, c, (x, y, c)).wait_recv()
                passed.append(copy(w, 3 + j, chip, c, (x, y, 1 - c)))
                passed[-1].start()
        for w in range(NW):
            for j, chip in enumerate(chips):
                copy(w, 3 + j, chip, 1 - c, (x, y, c)).wait_recv()
        for cp in first + passed:
            cp.wait_send()

    return _comm_call(body, "gather_weights", shards,
                      [jax.ShapeDtypeStruct((4,) + w.shape, w.dtype) for w in shards], 6 * NW)


def _swap_halves(grads):
    def body(*refs):
        gs, gots = refs[:NW], refs[NW:2 * NW]
        send_sems, recv_sems = refs[2 * NW:]
        x, y, c = _place()
        copies = [pltpu.make_async_remote_copy(
            src_ref=gs[w].at[k, 1 - c], dst_ref=gots[w].at[k], send_sem=send_sems.at[4 * w + k],
            recv_sem=recv_sems.at[4 * w + k], device_id=(x, y, 1 - c), device_id_type=MESH)
            for w in range(NW) for k in range(4)]
        for cp in copies:
            cp.start()
        for cp in copies:
            cp.wait()

    return _comm_call(body, "swap_halves", grads,
                      [jax.ShapeDtypeStruct((4,) + g.shape[2:], g.dtype) for g in grads], 4 * NW)


def _exchange_chips(psums):
    def body(*refs):
        ps, gots = refs[:NW], refs[NW:2 * NW]
        send_sems, recv_sems = refs[2 * NW:]
        x, y, c = _place()
        chips = [(1 - x, y), (x, 1 - y), (1 - x, 1 - y)]
        copies = [pltpu.make_async_remote_copy(
            src_ref=ps[w].at[2 * chip[0] + chip[1]], dst_ref=gots[w].at[j], send_sem=send_sems.at[3 * w + j],
            recv_sem=recv_sems.at[3 * w + j], device_id=(*chip, c), device_id_type=MESH)
            for w in range(NW) for j, chip in enumerate(chips)]
        for cp in copies:
            cp.start()
        for cp in copies:
            cp.wait()

    return _comm_call(body, "exchange_chips", psums,
                      [jax.ShapeDtypeStruct((3,) + p.shape[1:], p.dtype) for p in psums], 3 * NW)


def _share_with_sibling(arrs):
    na = len(arrs)

    def body(*refs):
        outs = refs[na:2 * na]
        send_sems, recv_sems = refs[2 * na:]
        x, y, c = _place()

        def half(ref):
            hr = ref.shape[0] // 2
            return ref.at[pl.ds(c * hr, hr), :]

        copies = [pltpu.make_async_remote_copy(
            src_ref=half(outs[k]), dst_ref=half(outs[k]), send_sem=send_sems.at[k], recv_sem=recv_sems.at[k],
            device_id=(x, y, 1 - c), device_id_type=MESH) for k in range(na)]
        for cp in copies:
            cp.start()
        for cp in copies:
            cp.wait()

    return _comm_call(body, "share_with_sibling", arrs, [jax.ShapeDtypeStruct(a.shape, a.dtype) for a in arrs],
                      na, aliases={k: k for k in range(na)})


def _norm_allreduce_adamw(part, w, m, v):
    r, lanes = part.shape

    def body(p_ref, w_ref, m_ref, v_ref, g_ref, d_ref, nm_ref, nv_ref, all_ref, send_sems, recv_sems):
        x, y, c = _place()
        me = 4 * x + 2 * y + c
        all_ref[me] = p_ref[...]
        copies = []
        for k in range(1, 8):
            peer = (x ^ (k >> 2), y ^ ((k >> 1) & 1), c ^ (k & 1))
            copies.append(pltpu.make_async_remote_copy(
                src_ref=p_ref, dst_ref=all_ref.at[me], send_sem=send_sems.at[k - 1], recv_sem=recv_sems.at[k - 1],
                device_id=peer, device_id_type=MESH))
        for cp in copies:
            cp.start()
        for cp in copies:
            cp.wait()
        g = all_ref[0]
        for k in range(1, 8):
            g = g + all_ref[k]
        g_ref[...] = g
        d_ref[...], nm_ref[...], nv_ref[...] = _adamw(w_ref[...], g, m_ref[...], v_ref[...])

    vm = pl.BlockSpec(memory_space=pltpu.VMEM)
    sd = jax.ShapeDtypeStruct((r, lanes), F32)
    return pl.pallas_call(
        body, name="norm_allreduce_adamw", in_specs=[vm] * 4, out_specs=[vm] * 4, out_shape=[sd] * 4,
        scratch_shapes=[pltpu.VMEM((8, r, lanes), F32), pltpu.SemaphoreType.DMA((7,)),
                        pltpu.SemaphoreType.DMA((7,))],
        compiler_params=pltpu.CompilerParams(has_side_effects=True),
    )(part, w, m, v)


LANES = 128


def _perm_in(w):
    return jnp.concatenate([w[:, :4096], w[:, 4928:5952], w[:, 4096:4928],
                            jnp.zeros((w.shape[0], D_INP - D_IN), w.dtype)], axis=1)


def _unperm_in(w):
    return jnp.concatenate([w[:, :4096], w[:, 5120:5952], w[:, 4096:5120]], axis=1)


def _by_chip_and_half(g, axis):
    sh = jnp.stack(jnp.split(g, 4, axis=axis))
    return sh.reshape(4, 2, sh.shape[1] // 2, sh.shape[2])


NORM_NAMES = ("pre_norm_w", "q_norm_w", "kv_norm_w", "post_norm_w")
NORM_SIZES = (D_MODEL, Q_RANK, KV_RANK, D_MODEL)
NORM_ROWS = 40


def _pack_norm(vs):
    flat = jnp.concatenate([v.reshape(-1) for v in vs])
    return jnp.pad(flat, (0, NORM_ROWS * LANES - flat.shape[0])).reshape(NORM_ROWS, LANES)


def _unpack_norm(p):
    flat, out, at = p.reshape(-1), [], 0
    for n in NORM_SIZES:
        out.append(flat[at:at + n].reshape(1, n))
        at += n
    return out


def _rope_tables(positions):
    inv_freq = ROPE_THETA ** (-jnp.arange(0, MLA_ROPE, 2, dtype=F32) / MLA_ROPE)
    ang = positions.astype(F32)[:, None] * inv_freq
    cos, sin, z = jnp.cos(ang), jnp.sin(ang), jnp.zeros_like(ang)
    return (jnp.concatenate([cos, cos, z, z], axis=1), jnp.concatenate([z, sin, z, z], axis=1),
            jnp.concatenate([-sin, z, z, z], axis=1))


def _local_step(x, positions, pre_w, win, q_w, wq, kv_w, wk, wv, wout, post_w, target):
    s = x.shape[0]
    rc, rs1, rs2 = _rope_tables(positions)
    h = _prenorm(x, pre_w)
    projb = _matmul(h, win, mode="nn", out_dtype=BF16, tm=512, tn=1024, tk=D_MODEL, name="proj_b", n=PB_W)
    projf = _matmul(h, win, mode="nn", out_dtype=F32, tm=512, tn=1024, tk=D_MODEL, name="proj_f", n=PF_W,
                    b_off=PB_W // 1024)
    oa, a_st, b_st = _sb_fwd(projb)
    nq, nkv, qm, km, vm = _mla_prep(projf, q_w, kv_w, wq, wk, wv, rc, rs1, rs2)
    ob, lse = _mla_fwd(qm, km, vm)
    mixed, dy, dout, err2, dpost = _out_post(oa, ob, projf, wout, x, target, post_w)

    dwout = _matmul(mixed, dy, mode="tn", out_dtype=F32, tm=1024, tn=1024, tk=min(4096, s), name="dw_out")
    dmixed = _matmul(dy, wout, mode="nt", out_dtype=F32, tm=512, tn=1024, tk=D_MODEL, name="d_mixed")
    dqa, dka, dva, dga = _sb_bwd(projb, projf, dmixed, oa, a_st, b_st)
    dqm, dkm, dvm, dgb = _mla_bwd(qm, km, vm, projf, dmixed, ob, lse)
    dqp, dkn, dcq, dckv, dkr, dqw, dkvw = _mla_bwd_post(dqm, dkm, dvm, projf, q_w, kv_w, wq, wk, wv, rc, rs1, rs2)
    tks = min(4096, s)
    dwq = _matmul(nq, dqp, mode="tn", out_dtype=F32, tm=Q_RANK, tn=1024, tk=tks, name="dw_q")
    dwk = _matmul(nkv, dkn, mode="tn", out_dtype=F32, tm=KV_RANK, tn=1024, tk=tks, name="dw_k")
    dwv = _matmul(nkv, dvm, mode="tn", out_dtype=F32, tm=KV_RANK, tn=1024, tk=tks, name="dw_v")
    dproj = jnp.concatenate([dqa, dka, dva, dga, dgb, dcq, dckv, dkr, jnp.zeros((s, 128), BF16)], axis=1)
    dwin = _matmul(h, dproj, mode="tn", out_dtype=F32, tm=1024, tn=1024, tk=tks, name="dw_in")
    gx, dpre = _pre_bwd(dproj, win, x, pre_w, dout)
    return err2, gx, dpre, dwin, dqw, dwq, dkvw, dwk, dwv, dwout, dpost


def _kernel_layouts(full):
    w_in, w_q_up, w_kv_up, w_out = full
    win = _perm_in(w_in)
    wq = jnp.pad(w_q_up.reshape(Q_RANK, HEADS, 192), ((0, 0), (0, 0), (0, 64))).reshape(Q_RANK, HEADS * MLA_QK_PAD)
    kv = w_kv_up.reshape(KV_RANK, HEADS, 256)
    wk = kv[:, :, :128].reshape(KV_RANK, SB_WIDTH)
    wv = kv[:, :, 128:].reshape(KV_RANK, SB_WIDTH)
    return win, wq, wk, wv, w_out


def _original_layouts(dwin, dwq, dwk, dwv):
    dwi = _unperm_in(dwin[:, :D_IN])
    dq = dwq.reshape(Q_RANK, HEADS, MLA_QK_PAD)[:, :, :192].reshape(Q_RANK, HEADS * 192)
    dkv = jnp.concatenate([dwk.reshape(KV_RANK, HEADS, 128), dwv.reshape(KV_RANK, HEADS, 128)], axis=2)
    return dwi, dq, dkv.reshape(KV_RANK, 2 * SB_WIDTH)


def kernel(x, positions, pre_norm_w, w_in, q_norm_w, w_q_up, kv_norm_w, w_kv_up, w_out, post_norm_w, loss_target, m_pre_norm_w, m_w_in, m_q_norm_w, m_w_q_up, m_kv_norm_w, m_w_kv_up, m_w_out, m_post_norm_w, v_pre_norm_w, v_w_in, v_q_norm_w, v_w_q_up, v_kv_norm_w, v_w_kv_up, v_w_out, v_post_norm_w):
    c = lax.axis_index("c")
    chip = 2 * lax.axis_index("x") + lax.axis_index("y")
    shards = (w_in[0], w_q_up[0], w_kv_up[0], w_out[0])
    mine16 = [w.astype(BF16) for w in shards]
    others = _gather_weights(mine16)
    slot = lambda w, k: jnp.where(chip == k, mine16[w], others[w][k])
    full = tuple(jnp.concatenate([slot(w, k) for k in range(4)], axis=ax) for w, ax in ((0, 1), (1, 1), (2, 1), (3, 0)))
    win, wq, wk, wv, wout = _kernel_layouts(full)

    err2, gx, dpre, dwin, dqw, dwq, dkvw, dwk, dwv, dwout, dpost = _local_step(
        x[0], positions[0], pre_norm_w, win, q_norm_w, wq, kv_norm_w, wk, wv, wout, post_norm_w, loss_target[0])
    loss = lax.psum(0.5 * jnp.sum(err2) / D_MODEL, ("x", "y", "c"))

    dwi, dq, dkv = _original_layouts(dwin, dwq, dwk, dwv)
    grads = [_by_chip_and_half(g, ax) for g, ax in ((dwi, 1), (dq, 1), (dkv, 1), (dwout, 0))]
    place = jnp.stack([c, chip])
    halves = _swap_halves(grads)
    pairs = [_pair_sum(place, grads[w], halves[w], "pair_sum_" + W_NAMES[w]) for w in range(NW)]
    gots = _exchange_chips([p[0] for p in pairs])
    ms = (m_w_in[0], m_w_q_up[0], m_w_kv_up[0], m_w_out[0])
    vs = (v_w_in[0], v_w_q_up[0], v_w_kv_up[0], v_w_out[0])
    done = [_sum_adamw(c.reshape(1), pairs[w][1], gots[w], shards[w], ms[w], vs[w], "sum_adamw_" + W_NAMES[w])
            for w in range(NW)]
    shared = _share_with_sibling([done[w][k] for k in range(4) for w in range(NW)])
    big = [shared[NW * k:NW * (k + 1)] for k in range(4)]

    small = _norm_allreduce_adamw(
        _pack_norm([dpre, dqw, dkvw, dpost]), _pack_norm([pre_norm_w, q_norm_w, kv_norm_w, post_norm_w]),
        _pack_norm([m_pre_norm_w, m_q_norm_w, m_kv_norm_w, m_post_norm_w]),
        _pack_norm([v_pre_norm_w, v_q_norm_w, v_kv_norm_w, v_post_norm_w]))
    small = [_unpack_norm(p) for p in small]

    def group(k):
        n, b = small[k], big[k]
        return (n[0], b[0][None], n[1], b[1][None], n[2], b[2][None], b[3][None], n[3])

    return (loss, gx[None], *group(0), *group(1), *group(2), *group(3))
```

```python
import functools
import math

import numpy as np
import jax
import jax.numpy as jnp
from jax import lax
from jax.experimental import pallas as pl
from jax.experimental.pallas import tpu as pltpu

F32 = jnp.float32
BF16 = jnp.bfloat16
MESH = pl.DeviceIdType.MESH

D_MODEL = 2048
HEADS = 8
HEAD_DIM = 128
SB_WIDTH = HEADS * HEAD_DIM
MLA_ROPE = 64
MLA_QK_PAD = 256
Q_RANK = 512
KV_RANK = 256
CHUNK = 64
EPS = 1e-6
ROPE_THETA = 10000.0
D_IN = 5952
D_INP = 6144
PB_W = 3072
PF_W = D_INP - PB_W
SB_SCALE = 1.0 / math.sqrt(HEAD_DIM)
MLA_SCALE = 1.0 / math.sqrt(HEAD_DIM + MLA_ROPE)
NEG = -1e30

ADAM_LR, ADAM_B1, ADAM_B2, ADAM_EPS, ADAM_WD, ADAM_STEP = 0.001, 0.9, 0.999, 1e-08, 0.01, 10
ADAM_C1 = 1.0 - ADAM_B1 ** ADAM_STEP
ADAM_C2 = 1.0 - ADAM_B2 ** ADAM_STEP

ANY = pl.BlockSpec(memory_space=pl.ANY)
VMEM_LIMIT = 56 * 1024 * 1024
TQ = 1024
TK = 256
MLA_TK = 512
UNROLL = 2


def _cp(sem=None, **kw):
    return pltpu.CompilerParams(dimension_semantics=sem, vmem_limit_bytes=VMEM_LIMIT, **kw)


def _dot(a, b, dims):
    return lax.dot_general(a, b, (dims, ((), ())), preferred_element_type=F32)


def _nn(a, b):
    return _dot(a, b, ((1,), (0,)))


def _nt(a, b):
    return _dot(a, b, ((1,), (1,)))


def _tn(a, b):
    return _dot(a, b, ((0,), (0,)))


def _rope(x, c, s1, s2):
    return x * c + pltpu.roll(x, 32, 1) * s1 + pltpu.roll(x, 96, 1) * s2


def _rope_t(x, c, s1, s2):
    return x * c - pltpu.roll(x, 32, 1) * s1 - pltpu.roll(x, 96, 1) * s2


def _silu_parts(g):
    sg = jax.nn.sigmoid(g)
    return g * sg, sg * (1.0 + g * (1.0 - sg))


def _matmul(a, b, *, mode, out_dtype, tm, tn, tk, name, n=None, b_off=0):
    if mode == "tn":
        kk, m = a.shape
        n = b.shape[1] if n is None else n
    else:
        m, kk = a.shape
        n = (b.shape[1] if mode == "nn" else b.shape[0]) if n is None else n
    nk = kk // tk
    a_spec = {"nn": pl.BlockSpec((tm, tk), lambda j, i, k: (i, k)),
              "nt": pl.BlockSpec((tm, tk), lambda j, i, k: (i, k)),
              "tn": pl.BlockSpec((tk, tm), lambda j, i, k: (k, i))}[mode]
    b_spec = {"nn": pl.BlockSpec((tk, tn), lambda j, i, k: (k, j + b_off)),
              "nt": pl.BlockSpec((tn, tk), lambda j, i, k: (j, k)),
              "tn": pl.BlockSpec((tk, tn), lambda j, i, k: (k, j))}[mode]
    dims = {"nn": ((1,), (0,)), "nt": ((1,), (1,)), "tn": ((0,), (0,))}[mode]

    def body(a_ref, b_ref, o_ref, acc_ref):
        k = pl.program_id(2)
        part = _dot(a_ref[...], b_ref[...], dims)
        if nk == 1:
            o_ref[...] = part.astype(out_dtype)
        else:
            @pl.when(k == 0)
            def _():
                acc_ref[...] = part

            @pl.when(k > 0)
            def _():
                acc_ref[...] += part

            @pl.when(k == nk - 1)
            def _():
                o_ref[...] = acc_ref[...].astype(out_dtype)

    return pl.pallas_call(
        body, name=name, grid=(n // tn, m // tm, nk),
        in_specs=[a_spec, b_spec], out_specs=pl.BlockSpec((tm, tn), lambda j, i, k: (i, j)),
        out_shape=jax.ShapeDtypeStruct((m, n), out_dtype),
        scratch_shapes=[pltpu.VMEM((tm, tn) if nk > 1 else (8, 128), F32)],
        compiler_params=_cp(("parallel", "parallel", "arbitrary")),
    )(a, b)


def _prenorm(x, w, tm=512):
    s, d = x.shape

    def body(x_ref, w_ref, h_ref):
        xv = x_ref[...]
        r = lax.rsqrt(jnp.mean(xv * xv, axis=1, keepdims=True) + EPS)
        h_ref[...] = ((xv * r) * w_ref[...]).astype(BF16)

    return pl.pallas_call(
        body, name="prenorm", grid=(s // tm,),
        in_specs=[pl.BlockSpec((tm, d), lambda i: (i, 0)), pl.BlockSpec((1, d), lambda i: (0, 0))],
        out_specs=pl.BlockSpec((tm, d), lambda i: (i, 0)),
        out_shape=jax.ShapeDtypeStruct((s, d), BF16),
        compiler_params=_cp(("parallel",)),
    )(x, w)


def _tri(n, cmp, value):
    row = lax.broadcasted_iota(jnp.int32, (n, n), 0)
    col = lax.broadcasted_iota(jnp.int32, (n, n), 1)
    return jnp.where(cmp(row, col), value, 0.0).astype(BF16)


def _softplus(z, mask):
    sp = jnp.maximum(z, 0.0) + jnp.log(1.0 + jnp.exp(-jnp.abs(z)))
    return sp if mask is None else jnp.where(mask, sp, 0.0)


def _tiles(s, tk=TK):
    tq = min(TQ, s)
    return tq, tk, math.gcd(tq // tk, UNROLL)


def _band_masks(tq, tk, fn):
    out = []
    for b in range(tq // tk):
        key = lax.broadcasted_iota(jnp.int32, (tk, tq - b * tk), 0) + b * tk
        qry = lax.broadcasted_iota(jnp.int32, (tk, tq - b * tk), 1) + b * tk
        out.append(fn(qry, key))
    return out


def _row_spec(tq):
    return pl.BlockSpec((8, tq), lambda h, i: (h, i))


def _sb_store_shape(s, tq, tk):
    nb, nq = tq // tk, s // tq
    return (HEADS, nb * nq * (nq + 1) // 2, tk, tq)


def _sb_fwd(projb):
    s = projb.shape[0]
    tq, tk, _ = _tiles(s)
    nb = tq // tk

    def body(q_ref, k_ref, v_ref, o_ref, a_st, b_st, acc_ref, car_ref, z_ref, a_buf, b_buf, sems):
        h, i = pl.program_id(0), pl.program_id(1)
        q = q_ref[...]
        from_here = _tri(tk, lambda s_, j: j >= s_, -1.0)
        masks = _band_masks(tq, tk, lambda t, s_: s_ < t)
        acc_ref[...] = jnp.zeros_like(acc_ref)
        car_ref[...] = jnp.zeros_like(car_ref)
        first = nb * (i * (i + 1) // 2)

        def stores(st, g):
            at = pl.ds(first + g * nb, nb)
            return (pltpu.make_async_copy(a_buf.at[st], a_st.at[h, at], sems.at[st, 0]),
                    pltpu.make_async_copy(b_buf.at[st], b_st.at[h, at], sems.at[st, 1]))

        def tile(j, mask, off, st, u):
            rows = pl.ds(pl.multiple_of(j * tk, tk), tk)
            z_ref[:, off:] = _nt(k_ref[rows, :], q[off:, :]) * SB_SCALE
            z = z_ref[:, off:]
            sp = _softplus(z, mask)
            a = jnp.exp(z + _nn(from_here, sp.astype(BF16)) + car_ref[:, off:])
            beta = jnp.exp(z - sp)
            if mask is not None:
                a = jnp.where(mask, a, 0.0)
                beta = jnp.where(mask, beta, 0.0)
            a = a.astype(BF16)
            if off:
                a_buf[st, u, :, :off] = jnp.zeros((tk, off), BF16)
                b_buf[st, u, :, :off] = jnp.zeros((tk, off), BF16)
            a_buf[st, u, :, off:] = a
            b_buf[st, u, :, off:] = beta.astype(BF16)
            acc_ref[:, off:] += _tn(v_ref[rows, :], a)
            car_ref[:, off:] -= jnp.sum(sp, axis=0, keepdims=True)

        for b in reversed(range(nb)):
            tile(i * nb + b, masks[b], b * tk, 0, b)
        for cp in stores(0, i):
            cp.start()

        def step(jj, c):
            st, g = (jj + 1) % 2, i - 1 - jj

            @pl.when(jj >= 1)
            def _():
                for cp in stores(st, g):
                    cp.wait()

            for u in reversed(range(nb)):
                tile(g * nb + u, None, 0, st, u)
            for cp in stores(st, g):
                cp.start()
            return c

        lax.fori_loop(0, i, step, 0)
        for cp in stores(0, 0):
            cp.wait()

        @pl.when(i >= 1)
        def _():
            for cp in stores(1, 0):
                cp.wait()

        o_ref[...] = acc_ref[...].T

    blk = pl.BlockSpec((tq, HEAD_DIM), lambda h, i: (i, h))
    st = jax.ShapeDtypeStruct(_sb_store_shape(s, tq, tk), BF16)
    return pl.pallas_call(
        body, name="sb_fwd", grid=(HEADS, s // tq),
        in_specs=[blk, pl.BlockSpec((s, HEAD_DIM), lambda h, i: (0, HEADS + h)),
                  pl.BlockSpec((s, HEAD_DIM), lambda h, i: (0, 2 * HEADS + h))],
        out_specs=[blk, ANY, ANY],
        out_shape=[jax.ShapeDtypeStruct((s, SB_WIDTH), F32), st, st],
        scratch_shapes=[pltpu.VMEM((HEAD_DIM, tq), F32), pltpu.VMEM((1, tq), F32), pltpu.VMEM((tk, tq), F32),
                        pltpu.VMEM((2, nb, tk, tq), BF16), pltpu.VMEM((2, nb, tk, tq), BF16),
                        pltpu.SemaphoreType.DMA((2, 2))],
        compiler_params=_cp(("arbitrary", "arbitrary")),
    )(projb, projb, projb)


def _mla_prep(projf, qw, kvw, wq, wk, wv, rc, rs1, rs2, tm=512):
    s = projf.shape[0]

    def body(cq_ref, ckv_ref, kr_ref, qw_ref, kvw_ref, wq_ref, wk_ref, wv_ref, c_ref, s1_ref, s2_ref,
             nq_ref, nkv_ref, q_ref, k_ref, v_ref):
        c, s1, s2 = c_ref[...], s1_ref[...], s2_ref[...]
        cq = cq_ref[...]
        nq = ((cq * lax.rsqrt(jnp.mean(cq * cq, axis=1, keepdims=True) + EPS)) * qw_ref[...]).astype(BF16)
        nq_ref[...] = nq
        qf = _nn(nq, wq_ref[...])
        ckv = ckv_ref[...]
        nkv = ((ckv * lax.rsqrt(jnp.mean(ckv * ckv, axis=1, keepdims=True) + EPS)) * kvw_ref[...]).astype(BF16)
        nkv_ref[...] = nkv
        kn = _nn(nkv, wk_ref[...])
        v_ref[...] = _nn(nkv, wv_ref[...]).astype(BF16)
        krot = _rope(kr_ref[...], c, s1, s2).astype(BF16)
        for h in range(HEADS):
            lo = h * MLA_QK_PAD
            q_ref[:, lo:lo + 128] = qf[:, lo:lo + 128].astype(BF16)
            q_ref[:, lo + 128:lo + 256] = _rope(qf[:, lo + 128:lo + 256], c, s1, s2).astype(BF16)
            k_ref[:, lo:lo + 128] = kn[:, h * 128:(h + 1) * 128].astype(BF16)
            k_ref[:, lo + 128:lo + 256] = krot

    row = lambda w, b: pl.BlockSpec((tm, w), lambda i: (i, b))
    full = lambda a: pl.BlockSpec(a.shape, lambda i: (0, 0))
    return pl.pallas_call(
        body, name="mla_prep", grid=(s // tm,),
        in_specs=[row(Q_RANK, 2048 // Q_RANK), row(KV_RANK, 2560 // KV_RANK), row(128, 2816 // 128),
                  full(qw), full(kvw), full(wq), full(wk), full(wv), row(128, 0), row(128, 0), row(128, 0)],
        out_specs=[row(Q_RANK, 0), row(KV_RANK, 0), row(HEADS * MLA_QK_PAD, 0), row(HEADS * MLA_QK_PAD, 0),
                   row(SB_WIDTH, 0)],
        out_shape=[jax.ShapeDtypeStruct((s, Q_RANK), BF16), jax.ShapeDtypeStruct((s, KV_RANK), BF16),
                   jax.ShapeDtypeStruct((s, HEADS * MLA_QK_PAD), BF16),
                   jax.ShapeDtypeStruct((s, HEADS * MLA_QK_PAD), BF16),
                   jax.ShapeDtypeStruct((s, SB_WIDTH), BF16)],
        compiler_params=_cp(("parallel",)),
    )(projf, projf, projf, qw, kvw, wq, wk, wv, rc, rs1, rs2)


def _mla_mask(qry, key):
    return (key // CHUNK) <= (qry // CHUNK)


def _mla_fwd(qm, km, vm):
    s = qm.shape[0]
    tq, tk, unroll = _tiles(s, MLA_TK)
    nb = tq // tk

    def body(q_ref, k_ref, v_ref, o_ref, lse_ref, acc_ref, m_ref, l_ref):
        i = pl.program_id(1)
        q = q_ref[...]
        masks = _band_masks(tq, tk, _mla_mask)
        acc_ref[...] = jnp.zeros_like(acc_ref)
        m_ref[...] = jnp.full_like(m_ref, NEG)
        l_ref[...] = jnp.zeros_like(l_ref)

        def tile(j, mask, off):
            rows = pl.ds(pl.multiple_of(j * tk, tk), tk)
            sc = _nt(k_ref[rows, :], q[off:, :]) * MLA_SCALE
            if mask is not None:
                sc = jnp.where(mask, sc, NEG)
            m_old = m_ref[:, off:]
            m_new = jnp.maximum(m_old, jnp.max(sc, axis=0, keepdims=True))
            p = jnp.exp(sc - m_new)
            alpha = jnp.exp(m_old - m_new)
            l_ref[:, off:] = alpha * l_ref[:, off:] + jnp.sum(p, axis=0, keepdims=True)
            acc_ref[:, off:] = alpha * acc_ref[:, off:] + _tn(v_ref[rows, :], p.astype(BF16))
            m_ref[:, off:] = m_new

        for b in range(nb):
            tile(i * nb + b, masks[b], b * tk)

        def step(jj, c):
            for u in range(unroll):
                tile(jj * unroll + u, None, 0)
            return c

        lax.fori_loop(0, i * nb // unroll, step, 0)
        o_ref[...] = (acc_ref[...] / l_ref[...]).T
        lse_ref[...] = jnp.broadcast_to(m_ref[...] + jnp.log(l_ref[...]), (8, tq))

    return pl.pallas_call(
        body, name="mla_fwd", grid=(HEADS, s // tq),
        in_specs=[pl.BlockSpec((tq, MLA_QK_PAD), lambda h, i: (i, h)),
                  pl.BlockSpec((s, MLA_QK_PAD), lambda h, i: (0, h)),
                  pl.BlockSpec((s, HEAD_DIM), lambda h, i: (0, h))],
        out_specs=[pl.BlockSpec((tq, HEAD_DIM), lambda h, i: (i, h)), _row_spec(tq)],
        out_shape=[jax.ShapeDtypeStruct((s, SB_WIDTH), F32), jax.ShapeDtypeStruct((8 * HEADS, s), F32)],
        scratch_shapes=[pltpu.VMEM((HEAD_DIM, tq), F32), pltpu.VMEM((1, tq), F32), pltpu.VMEM((1, tq), F32)],
        compiler_params=_cp(("parallel", "arbitrary")),
    )(qm, km, vm)


def _out_post(oa, ob, projf, wout, x, target, pw, tm=256):
    s, d = x.shape

    def body(oa_ref, ob_ref, ga_ref, gb_ref, w_ref, x_ref, t_ref, pw_ref,
             mix_ref, dy_ref, dout_ref, loss_ref, dpw_ref):
        i = pl.program_id(0)
        sa, _ = _silu_parts(ga_ref[...])
        sb, _ = _silu_parts(gb_ref[...])
        mix_ref[:, :SB_WIDTH] = (oa_ref[...] * sa).astype(BF16)
        mix_ref[:, SB_WIDTH:] = (ob_ref[...] * sb).astype(BF16)
        y = _nn(mix_ref[...], w_ref[...])
        r = lax.rsqrt(jnp.mean(y * y, axis=1, keepdims=True) + EPS)
        yhat = y * r
        pwv = pw_ref[...]
        err = (x_ref[...] + yhat * pwv) - t_ref[...]
        dout = err * (1.0 / d)
        dout_ref[...] = dout
        g = dout * pwv
        dy_ref[...] = (r * (g - yhat * jnp.mean(g * yhat, axis=1, keepdims=True))).astype(BF16)

        @pl.when(i == 0)
        def _():
            loss_ref[...] = jnp.zeros_like(loss_ref)
            dpw_ref[...] = jnp.zeros_like(dpw_ref)

        loss_ref[...] += jnp.sum(err * err, axis=0, keepdims=True)
        dpw_ref[...] += jnp.sum(dout * yhat, axis=0, keepdims=True)

    row = lambda w, b: pl.BlockSpec((tm, w), lambda i: (i, b))
    vec = pl.BlockSpec((1, d), lambda i: (0, 0))
    return pl.pallas_call(
        body, name="out_post", grid=(s // tm,),
        in_specs=[row(SB_WIDTH, 0), row(SB_WIDTH, 0), row(SB_WIDTH, 0), row(SB_WIDTH, 1),
                  pl.BlockSpec(wout.shape, lambda i: (0, 0)), row(d, 0), row(d, 0), vec],
        out_specs=[row(d, 0), row(d, 0), row(d, 0), vec, vec],
        out_shape=[jax.ShapeDtypeStruct((s, d), BF16), jax.ShapeDtypeStruct((s, d), BF16),
                   jax.ShapeDtypeStruct((s, d), F32), jax.ShapeDtypeStruct((1, d), F32),
                   jax.ShapeDtypeStruct((1, d), F32)],
        compiler_params=_cp(("arbitrary",)),
    )(oa, ob, projf, projf, wout, x, target, pw)


def _sb_bwd(projb, projf, dmixed, oa, a_st, b_st):
    s = projb.shape[0]
    tq, tk, _ = _tiles(s)
    nb, nq = tq // tk, s // tq

    def body(q_ref, k_ref, v_ref, dm_ref, g_ref, o_ref, a_st, b_st, dq_ref, dk_ref, dv_ref, dg_ref,
             dka_ref, dva_ref, dqa_ref, cg_ref, a_buf, b_buf, sems):
        h, i = pl.program_id(0), pl.program_id(1)

        @pl.when(i == 0)
        def _():
            dka_ref[...] = jnp.zeros_like(dka_ref)
            dva_ref[...] = jnp.zeros_like(dva_ref)

        q = q_ref[...]
        silu, dsilu = _silu_parts(g_ref[...])
        dm = dm_ref[...]
        dg_ref[...] = (dm * o_ref[...] * dsilu).astype(BF16)
        do = (dm * silu).astype(BF16)
        up_to_here = _tri(tk, lambda s_, j: j <= s_, 1.0)
        dqa_ref[...] = jnp.zeros_like(dqa_ref)
        cg_ref[...] = jnp.zeros_like(cg_ref)
        first = nb * (i * (i + 1) // 2)

        def loads(st, g):
            at = pl.ds(first + g * nb, nb)
            return (pltpu.make_async_copy(a_st.at[h, at], a_buf.at[st], sems.at[st, 0]),
                    pltpu.make_async_copy(b_st.at[h, at], b_buf.at[st], sems.at[st, 1]))

        def tile(j, off, st, u):
            rows = pl.ds(pl.multiple_of(j * tk, tk), tk)
            k = k_ref[rows, :]
            a = a_buf[st, u, :, off:]
            g = a.astype(F32) * _nt(v_ref[rows, :], do[off:, :])
            dva_ref[rows, :] += _nn(a, do[off:, :])
            cum = _nn(up_to_here, g.astype(BF16)) + cg_ref[:, off:]
            dz = ((g - b_buf[st, u, :, off:].astype(F32) * cum) * SB_SCALE).astype(BF16)
            dqa_ref[:, off:] += _tn(k, dz)
            dka_ref[rows, :] += _nn(dz, q[off:, :])
            cg_ref[:, off:] += jnp.sum(g, axis=0, keepdims=True)

        for cp in loads(0, 0):
            cp.start()

        def step(g, c):
            st = g % 2
            for cp in loads(1 - st, g + 1):
                cp.start()
            for cp in loads(st, g):
                cp.wait()
            for u in range(nb):
                tile(g * nb + u, 0, st, u)
            return c

        lax.fori_loop(0, i, step, 0)
        for cp in loads(i % 2, i):
            cp.wait()
        for b in range(nb):
            tile(i * nb + b, b * tk, i % 2, b)
        dq_ref[...] = dqa_ref[...].T.astype(BF16)

        @pl.when(i == nq - 1)
        def _():
            dk_ref[...] = dka_ref[...].astype(BF16)
            dv_ref[...] = dva_ref[...].astype(BF16)

    blk = lambda off: pl.BlockSpec((tq, HEAD_DIM), lambda h, i: (i, off + h))
    whole = lambda off: pl.BlockSpec((s, HEAD_DIM), lambda h, i: (0, off + h))
    o_sd = jax.ShapeDtypeStruct((s, SB_WIDTH), BF16)
    return pl.pallas_call(
        body, name="sb_bwd", grid=(HEADS, nq),
        in_specs=[blk(0), whole(HEADS), whole(2 * HEADS), blk(0), blk(0), blk(0), ANY, ANY],
        out_specs=[blk(0), whole(0), whole(0), blk(0)],
        out_shape=[o_sd, o_sd, o_sd, o_sd],
        scratch_shapes=[pltpu.VMEM((s, HEAD_DIM), F32), pltpu.VMEM((s, HEAD_DIM), F32),
                        pltpu.VMEM((HEAD_DIM, tq), F32), pltpu.VMEM((1, tq), F32),
                        pltpu.VMEM((2, nb, tk, tq), BF16), pltpu.VMEM((2, nb, tk, tq), BF16),
                        pltpu.SemaphoreType.DMA((2, 2))],
        compiler_params=_cp(("arbitrary", "arbitrary")),
    )(projb, projb, projb, dmixed, projf, oa, a_st, b_st)


def _mla_bwd(qm, km, vm, projf, dmixed, ob, lse):
    s = qm.shape[0]
    tq, tk, unroll = _tiles(s, MLA_TK)
    nb, nq = tq // tk, s // tq

    def body(q_ref, k_ref, v_ref, dm_ref, g_ref, o_ref, lse_ref, dq_ref, dk_ref, dv_ref, dg_ref,
             dva_ref, dqa_ref):
        i = pl.program_id(1)

        @pl.when(i == 0)
        def _():
            dk_ref[...] = jnp.zeros_like(dk_ref)
            dva_ref[...] = jnp.zeros_like(dva_ref)

        q = q_ref[...]
        silu, dsilu = _silu_parts(g_ref[...])
        dm = dm_ref[...]
        o = o_ref[...]
        dg_ref[...] = (dm * o * dsilu).astype(BF16)
        dof = dm * silu
        delta = jnp.sum((dof * o).T, axis=0, keepdims=True)
        do = dof.astype(BF16)
        lse = lse_ref[0:1, :]
        masks = _band_masks(tq, tk, _mla_mask)
        dqa_ref[...] = jnp.zeros_like(dqa_ref)

        def tile(j, mask, off):
            rows = pl.ds(pl.multiple_of(j * tk, tk), tk)
            k = k_ref[rows, :]
            p = jnp.exp(_nt(k, q[off:, :]) * MLA_SCALE - lse[:, off:])
            if mask is not None:
                p = jnp.where(mask, p, 0.0)
            ds = (p * (_nt(v_ref[rows, :], do[off:, :]) - delta[:, off:]) * MLA_SCALE).astype(BF16)
            dva_ref[rows, :] += _nn(p.astype(BF16), do[off:, :])
            dk_ref[rows, :] += _nn(ds, q[off:, :])
            dqa_ref[:, off:] += _tn(k, ds)

        def step(jj, c):
            for u in range(unroll):
                tile(jj * unroll + u, None, 0)
            return c

        lax.fori_loop(0, i * nb // unroll, step, 0)
        for b in range(nb):
            tile(i * nb + b, masks[b], b * tk)
        dq_ref[...] = dqa_ref[...].T

        @pl.when(i == nq - 1)
        def _():
            dv_ref[...] = dva_ref[...].astype(BF16)

    blk = lambda w, off: pl.BlockSpec((tq, w), lambda h, i: (i, off + h))
    whole = lambda w: pl.BlockSpec((s, w), lambda h, i: (0, h))
    return pl.pallas_call(
        body, name="mla_bwd", grid=(HEADS, nq),
        in_specs=[blk(MLA_QK_PAD, 0), whole(MLA_QK_PAD), whole(HEAD_DIM), blk(HEAD_DIM, HEADS),
                  blk(HEAD_DIM, HEADS), blk(HEAD_DIM, 0), _row_spec(tq)],
        out_specs=[blk(MLA_QK_PAD, 0), whole(MLA_QK_PAD), whole(HEAD_DIM), blk(HEAD_DIM, 0)],
        out_shape=[jax.ShapeDtypeStruct((s, HEADS * MLA_QK_PAD), F32),
                   jax.ShapeDtypeStruct((s, HEADS * MLA_QK_PAD), F32),
                   jax.ShapeDtypeStruct((s, SB_WIDTH), BF16), jax.ShapeDtypeStruct((s, SB_WIDTH), BF16)],
        scratch_shapes=[pltpu.VMEM((s, HEAD_DIM), F32), pltpu.VMEM((MLA_QK_PAD, tq), F32)],
        compiler_params=_cp(("parallel", "arbitrary")),
    )(qm, km, vm, dmixed, projf, ob, lse)


def _norm_bwd(x, w, dn):
    r = lax.rsqrt(jnp.mean(x * x, axis=1, keepdims=True) + EPS)
    xhat = x * r
    g = dn * w
    return r * (g - xhat * jnp.mean(g * xhat, axis=1, keepdims=True)), dn * xhat


def _mla_bwd_post(dqm, dkm, dvm, projf, qw, kvw, wq, wk, wv, rc, rs1, rs2, tm=256):
    s = dqm.shape[0]

    def body(dq_ref, dk_ref, dv_ref, cq_ref, ckv_ref, qw_ref, kvw_ref, wq_ref, wk_ref, wv_ref,
             c_ref, s1_ref, s2_ref, dqp_ref, dkn_ref, dcq_ref, dckv_ref, dkr_ref, dqw_ref, dkvw_ref):
        i = pl.program_id(0)
        c, s1, s2 = c_ref[...], s1_ref[...], s2_ref[...]
        drot = jnp.zeros((tm, 128), F32)
        for h in range(HEADS):
            lo = h * MLA_QK_PAD
            dqp_ref[:, lo:lo + 128] = dq_ref[:, lo:lo + 128].astype(BF16)
            dqp_ref[:, lo + 128:lo + 256] = _rope_t(dq_ref[:, lo + 128:lo + 256], c, s1, s2).astype(BF16)
            dkn_ref[:, h * 128:(h + 1) * 128] = dk_ref[:, lo:lo + 128].astype(BF16)
            drot = drot + dk_ref[:, lo + 128:lo + 256]
        dkr_ref[...] = _rope_t(drot, c, s1, s2).astype(BF16)
        dcq, dqw = _norm_bwd(cq_ref[...], qw_ref[...], _nt(dqp_ref[...], wq_ref[...]))
        dcq_ref[...] = dcq.astype(BF16)
        dnkv = _nt(dkn_ref[...], wk_ref[...]) + _nt(dv_ref[...], wv_ref[...])
        dckv, dkvw = _norm_bwd(ckv_ref[...], kvw_ref[...], dnkv)
        dckv_ref[...] = dckv.astype(BF16)

        @pl.when(i == 0)
        def _():
            dqw_ref[...] = jnp.zeros_like(dqw_ref)
            dkvw_ref[...] = jnp.zeros_like(dkvw_ref)

        dqw_ref[...] += jnp.sum(dqw, axis=0, keepdims=True)
        dkvw_ref[...] += jnp.sum(dkvw, axis=0, keepdims=True)

    row = lambda w, b: pl.BlockSpec((tm, w), lambda i: (i, b))
    full = lambda a: pl.BlockSpec(a.shape, lambda i: (0, 0))
    sd = jax.ShapeDtypeStruct
    return pl.pallas_call(
        body, name="mla_bwd_post", grid=(s // tm,),
        in_specs=[row(HEADS * MLA_QK_PAD, 0), row(HEADS * MLA_QK_PAD, 0), row(SB_WIDTH, 0),
                  row(Q_RANK, 2048 // Q_RANK), row(KV_RANK, 2560 // KV_RANK),
                  full(qw), full(kvw), full(wq), full(wk), full(wv), row(128, 0), row(128, 0), row(128, 0)],
        out_specs=[row(HEADS * MLA_QK_PAD, 0), row(SB_WIDTH, 0), row(Q_RANK, 0), row(KV_RANK, 0), row(128, 0),
                   pl.BlockSpec((1, Q_RANK), lambda i: (0, 0)), pl.BlockSpec((1, KV_RANK), lambda i: (0, 0))],
        out_shape=[sd((s, HEADS * MLA_QK_PAD), BF16), sd((s, SB_WIDTH), BF16), sd((s, Q_RANK), BF16),
                   sd((s, KV_RANK), BF16), sd((s, 128), BF16), sd((1, Q_RANK), F32), sd((1, KV_RANK), F32)],
        compiler_params=_cp(("arbitrary",)),
    )(dqm, dkm, dvm, projf, projf, qw, kvw, wq, wk, wv, rc, rs1, rs2)


def _pre_bwd(dproj, win, x, pw, dout, psums, tm=256):
    s, d = x.shape
    kk = dproj.shape[1]
    nw, nsteps = len(psums), s // tm

    def body(*refs):
        dp_ref, w_ref, x_ref, pw_ref, do_ref = refs[:5]
        ps = refs[5:5 + nw]
        dx_ref, dpw_ref = refs[5 + nw:7 + nw]
        gots = refs[7 + nw:7 + 2 * nw]
        send_sems, recv_sems = refs[7 + 2 * nw:]
        i = pl.program_id(0)

        def copies():
            if not nw:
                return []
            px, py, pc = _place()
            chips = [(1 - px, py), (px, 1 - py), (1 - px, 1 - py)]
            return [pltpu.make_async_remote_copy(
                src_ref=ps[w].at[2 * chip[0] + chip[1]], dst_ref=gots[w].at[j], send_sem=send_sems.at[3 * w + j],
                recv_sem=recv_sems.at[3 * w + j], device_id=(*chip, pc), device_id_type=MESH)
                for w in range(nw) for j, chip in enumerate(chips)]

        @pl.when(i == 0)
        def _():
            dpw_ref[...] = jnp.zeros_like(dpw_ref)
            for cp in copies():
                cp.start()

        dx, dw = _norm_bwd(x_ref[...], pw_ref[...], _nt(dp_ref[...], w_ref[...]))
        dx_ref[...] = do_ref[...] + dx
        dpw_ref[...] += jnp.sum(dw, axis=0, keepdims=True)

        @pl.when(i == nsteps - 1)
        def _():
            for cp in copies():
                cp.wait()

    rowd = pl.BlockSpec((tm, d), lambda i: (i, 0))
    vec = pl.BlockSpec((1, d), lambda i: (0, 0))
    out = pl.pallas_call(
        body, name="pre_bwd", grid=(nsteps,),
        in_specs=[pl.BlockSpec((tm, kk), lambda i: (i, 0)),
                  pl.BlockSpec((d, kk), lambda i: (0, 0), pipeline_mode=pl.Buffered(1)), rowd, vec, rowd] + [ANY] * nw,
        out_specs=[rowd, vec] + [ANY] * nw,
        out_shape=[jax.ShapeDtypeStruct((s, d), F32), jax.ShapeDtypeStruct((1, d), F32)]
        + [jax.ShapeDtypeStruct((3,) + p.shape[1:], p.dtype) for p in psums],
        scratch_shapes=[pltpu.SemaphoreType.DMA((max(3 * nw, 1),)), pltpu.SemaphoreType.DMA((max(3 * nw, 1),))],
        compiler_params=_cp(("arbitrary",), has_side_effects=True),
    )(dproj, win, x, pw, dout, *psums)
    return out[0], out[1], list(out[2:])


def _adamw(w, g, m, v):
    m = ADAM_B1 * m + (1.0 - ADAM_B1) * g
    v = ADAM_B2 * v + (1.0 - ADAM_B2) * (g * g)
    delta = -ADAM_LR * ((m / ADAM_C1) / (jnp.sqrt(v / ADAM_C2) + ADAM_EPS) + ADAM_WD * w)
    return delta, m, v


def _row_block(rows):
    return math.gcd(rows, 128)


def _sum_adamw(core, own, got, w, m, v, name):
    n, hr, cols = got.shape
    tr = _row_block(hr)
    nblk = hr // tr

    def body(c_ref, o_ref, p_ref, w_ref, m_ref, v_ref, g_ref, d_ref, nm_ref, nv_ref):
        g = o_ref[...]
        for k in range(n):
            g = g + p_ref[k].astype(F32)
        g_ref[...] = g
        d_ref[...], nm_ref[...], nv_ref[...] = _adamw(w_ref[...], g, m_ref[...], v_ref[...])

    mine = pl.BlockSpec((tr, cols), lambda i, c: (c[0] * nblk + i, 0))
    sd = jax.ShapeDtypeStruct((2 * hr, cols), F32)
    return pl.pallas_call(
        body, name=name,
        grid_spec=pltpu.PrefetchScalarGridSpec(
            num_scalar_prefetch=1, grid=(nblk,),
            in_specs=[pl.BlockSpec((tr, cols), lambda i, c: (i, 0)),
                      pl.BlockSpec((n, tr, cols), lambda i, c: (0, i, 0)), mine, mine, mine],
            out_specs=[mine, mine, mine, mine]),
        out_shape=[sd, sd, sd, sd],
        compiler_params=_cp(("parallel",)),
    )(core, own, got, w, m, v)


def _pair_sum(place, grads, got, name):
    n, _, hr, cols = grads.shape
    tr = _row_block(hr)

    def body(p_ref, a_ref, b_ref, oa_ref, ob_ref, all_ref, own_ref):
        all_ref[...] = (a_ref[:, 0] + b_ref[...]).astype(BF16)
        own_ref[...] = oa_ref[0, 0] + ob_ref[0]

    return pl.pallas_call(
        body, name=name,
        grid_spec=pltpu.PrefetchScalarGridSpec(
            num_scalar_prefetch=1, grid=(hr // tr,),
            in_specs=[pl.BlockSpec((n, 1, tr, cols), lambda i, p: (0, p[0], i, 0)),
                      pl.BlockSpec((n, tr, cols), lambda i, p: (0, i, 0)),
                      pl.BlockSpec((1, 1, tr, cols), lambda i, p: (p[1], p[0], i, 0)),
                      pl.BlockSpec((1, tr, cols), lambda i, p: (p[1], i, 0))],
            out_specs=[pl.BlockSpec((n, tr, cols), lambda i, p: (0, i, 0)),
                       pl.BlockSpec((tr, cols), lambda i, p: (i, 0))]),
        out_shape=[jax.ShapeDtypeStruct((n, hr, cols), BF16), jax.ShapeDtypeStruct((hr, cols), F32)],
        compiler_params=_cp(("parallel",)),
    )(place, grads, got, grads, got)


def _place():
    return lax.axis_index("x"), lax.axis_index("y"), lax.axis_index("c")


W_NAMES = ("w_in", "w_q_up", "w_kv_up", "w_out")
NW = len(W_NAMES)


def _comm_call(body, name, ins, out_shape, n_copies, aliases=None):
    return pl.pallas_call(
        body, name=name, in_specs=[ANY] * len(ins), out_specs=[ANY] * len(out_shape), out_shape=out_shape,
        input_output_aliases=aliases or {},
        scratch_shapes=[pltpu.SemaphoreType.DMA((n_copies,)), pltpu.SemaphoreType.DMA((n_copies,))],
        compiler_params=pltpu.CompilerParams(has_side_effects=True),
    )(*ins)


def _gather_weights(shards):
    def body(*refs):
        ws, outs = refs[:NW], refs[NW:2 * NW]
        send_sems, recv_sems = refs[2 * NW:]
        x, y, c = _place()
        chips = [(1 - x, y), (x, 1 - y), (1 - x, 1 - y)]

        def rows(w, chip, core):
            half = ws[w].shape[0] // 2
            return outs[w].at[2 * chip[0] + chip[1], pl.ds(core * half, half), :]

        def copy(w, k, chip, core, to, own=False):
            half = ws[w].shape[0] // 2
            return pltpu.make_async_remote_copy(
                src_ref=ws[w].at[pl.ds(core * half, half), :] if own else rows(w, chip, core),
                dst_ref=rows(w, chip, core), send_sem=send_sems.at[6 * w + k], recv_sem=recv_sems.at[6 * w + k],
                device_id=to, device_id_type=MESH)

        first = [copy(w, j, (x, y), c, (*chip, c), own=True) for w in range(NW) for j, chip in enumerate(chips)]
        for cp in first:
            cp.start()
        passed = []
        for w in range(NW):
            for j, chip in enumerate(chips):
                copy(w, j, chip, c, (x, y, c)).wait_recv()
                passed.append(copy(w, 3 + j, chip, c, (x, y, 1 - c)))
                passed[-1].start()
        for w in range(NW):
            for j, chip in enumerate(chips):
                copy(w, 3 + j, chip, 1 - c, (x, y, c)).wait_recv()
        for cp in first + passed:
            cp.wait_send()

    return _comm_call(body, "gather_weights", shards,
                      [jax.ShapeDtypeStruct((4,) + w.shape, w.dtype) for w in shards], 6 * NW)


def _swap_halves(grads):
    def body(*refs):
        gs, gots = refs[:NW], refs[NW:2 * NW]
        send_sems, recv_sems = refs[2 * NW:]
        x, y, c = _place()
        copies = [pltpu.make_async_remote_copy(
            src_ref=gs[w].at[k, 1 - c], dst_ref=gots[w].at[k], send_sem=send_sems.at[4 * w + k],
            recv_sem=recv_sems.at[4 * w + k], device_id=(x, y, 1 - c), device_id_type=MESH)
            for w in range(NW) for k in range(4)]
        for cp in copies:
            cp.start()
        for cp in copies:
            cp.wait()

    return _comm_call(body, "swap_halves", grads,
                      [jax.ShapeDtypeStruct((4,) + g.shape[2:], g.dtype) for g in grads], 4 * NW)


def _share_with_sibling(arrs):
    na = len(arrs)

    def body(*refs):
        outs = refs[na:2 * na]
        send_sems, recv_sems = refs[2 * na:]
        x, y, c = _place()

        def half(ref):
            hr = ref.shape[0] // 2
            return ref.at[pl.ds(c * hr, hr), :]

        copies = [pltpu.make_async_remote_copy(
            src_ref=half(outs[k]), dst_ref=half(outs[k]), send_sem=send_sems.at[k], recv_sem=recv_sems.at[k],
            device_id=(x, y, 1 - c), device_id_type=MESH) for k in range(na)]
        for cp in copies:
            cp.start()
        for cp in copies:
            cp.wait()

    return _comm_call(body, "share_with_sibling", arrs, [jax.ShapeDtypeStruct(a.shape, a.dtype) for a in arrs],
                      na, aliases={k: k for k in range(na)})


def _norm_allreduce_adamw(part, w, m, v):
    r, lanes = part.shape

    def body(p_ref, w_ref, m_ref, v_ref, g_ref, d_ref, nm_ref, nv_ref, all_ref, send_sems, recv_sems):
        x, y, c = _place()
        me = 4 * x + 2 * y + c
        all_ref[me] = p_ref[...]
        copies = []
        for k in range(1, 8):
            peer = (x ^ (k >> 2), y ^ ((k >> 1) & 1), c ^ (k & 1))
            copies.append(pltpu.make_async_remote_copy(
                src_ref=p_ref, dst_ref=all_ref.at[me], send_sem=send_sems.at[k - 1], recv_sem=recv_sems.at[k - 1],
                device_id=peer, device_id_type=MESH))
        for cp in copies:
            cp.start()
        for cp in copies:
            cp.wait()
        g = all_ref[0]
        for k in range(1, 8):
            g = g + all_ref[k]
        g_ref[...] = g
        d_ref[...], nm_ref[...], nv_ref[...] = _adamw(w_ref[...], g, m_ref[...], v_ref[...])

    vm = pl.BlockSpec(memory_space=pltpu.VMEM)
    sd = jax.ShapeDtypeStruct((r, lanes), F32)
    return pl.pallas_call(
        body, name="norm_allreduce_adamw", in_specs=[vm] * 4, out_specs=[vm] * 4, out_shape=[sd] * 4,
        scratch_shapes=[pltpu.VMEM((8, r, lanes), F32), pltpu.SemaphoreType.DMA((7,)),
                        pltpu.SemaphoreType.DMA((7,))],
        compiler_params=pltpu.CompilerParams(has_side_effects=True),
    )(part, w, m, v)


LANES = 128


def _perm_in(w):
    return jnp.concatenate([w[:, :4096], w[:, 4928:5952], w[:, 4096:4928],
                            jnp.zeros((w.shape[0], D_INP - D_IN), w.dtype)], axis=1)


def _unperm_in(w):
    return jnp.concatenate([w[:, :4096], w[:, 5120:5952], w[:, 4096:5120]], axis=1)


def _by_chip_and_half(g, axis):
    rows, cols = g.shape
    if axis == 0:
        return g.reshape(4, 2, rows // 8, cols)
    return g.reshape(rows, 4, cols // 4).transpose(1, 0, 2).reshape(4, 2, rows // 2, cols // 4)


NORM_NAMES = ("pre_norm_w", "q_norm_w", "kv_norm_w", "post_norm_w")
NORM_SIZES = (D_MODEL, Q_RANK, KV_RANK, D_MODEL)
NORM_ROWS = 40


def _pack_norm(vs):
    flat = jnp.concatenate([v.reshape(-1) for v in vs])
    return jnp.pad(flat, (0, NORM_ROWS * LANES - flat.shape[0])).reshape(NORM_ROWS, LANES)


def _unpack_norm(p):
    flat, out, at = p.reshape(-1), [], 0
    for n in NORM_SIZES:
        out.append(flat[at:at + n].reshape(1, n))
        at += n
    return out


def _rope_tables(positions):
    inv_freq = ROPE_THETA ** (-jnp.arange(0, MLA_ROPE, 2, dtype=F32) / MLA_ROPE)
    ang = positions.astype(F32)[:, None] * inv_freq
    cos, sin, z = jnp.cos(ang), jnp.sin(ang), jnp.zeros_like(ang)
    return (jnp.concatenate([cos, cos, z, z], axis=1), jnp.concatenate([z, sin, z, z], axis=1),
            jnp.concatenate([-sin, z, z, z], axis=1))


def _local_step(x, positions, pre_w, win, q_w, wq, kv_w, wk, wv, wout, post_w, target):
    s = x.shape[0]
    rc, rs1, rs2 = _rope_tables(positions)
    h = _prenorm(x, pre_w)
    projb = _matmul(h, win, mode="nn", out_dtype=BF16, tm=512, tn=1024, tk=D_MODEL, name="proj_b", n=PB_W)
    projf = _matmul(h, win, mode="nn", out_dtype=F32, tm=512, tn=1024, tk=D_MODEL, name="proj_f", n=PF_W,
                    b_off=PB_W // 1024)
    oa, a_st, b_st = _sb_fwd(projb)
    nq, nkv, qm, km, vm = _mla_prep(projf, q_w, kv_w, wq, wk, wv, rc, rs1, rs2)
    ob, lse = _mla_fwd(qm, km, vm)
    mixed, dy, dout, err2, dpost = _out_post(oa, ob, projf, wout, x, target, post_w)

    dwout = _matmul(mixed, dy, mode="tn", out_dtype=F32, tm=1024, tn=1024, tk=min(4096, s), name="dw_out")
    dmixed = _matmul(dy, wout, mode="nt", out_dtype=F32, tm=512, tn=1024, tk=D_MODEL, name="d_mixed")
    dqa, dka, dva, dga = _sb_bwd(projb, projf, dmixed, oa, a_st, b_st)
    dqm, dkm, dvm, dgb = _mla_bwd(qm, km, vm, projf, dmixed, ob, lse)
    dqp, dkn, dcq, dckv, dkr, dqw, dkvw = _mla_bwd_post(dqm, dkm, dvm, projf, q_w, kv_w, wq, wk, wv, rc, rs1, rs2)
    tks = min(4096, s)
    dwq = _matmul(nq, dqp, mode="tn", out_dtype=F32, tm=Q_RANK, tn=1024, tk=tks, name="dw_q")
    dwk = _matmul(nkv, dkn, mode="tn", out_dtype=F32, tm=KV_RANK, tn=1024, tk=tks, name="dw_k")
    dwv = _matmul(nkv, dvm, mode="tn", out_dtype=F32, tm=KV_RANK, tn=1024, tk=tks, name="dw_v")
    dproj = jnp.concatenate([dqa, dka, dva, dga, dgb, dcq, dckv, dkr, jnp.zeros((s, 128), BF16)], axis=1)
    dwin = _matmul(h, dproj, mode="tn", out_dtype=F32, tm=1024, tn=1024, tk=tks, name="dw_in")
    return err2, dproj, dout, dwin, dqw, dwq, dkvw, dwk, dwv, dwout, dpost


def _kernel_layouts(full):
    w_in, w_q_up, w_kv_up, w_out = full
    win = _perm_in(w_in)
    wq = jnp.pad(w_q_up.reshape(Q_RANK, HEADS, 192), ((0, 0), (0, 0), (0, 64))).reshape(Q_RANK, HEADS * MLA_QK_PAD)
    kv = w_kv_up.reshape(KV_RANK, HEADS, 256)
    wk = kv[:, :, :128].reshape(KV_RANK, SB_WIDTH)
    wv = kv[:, :, 128:].reshape(KV_RANK, SB_WIDTH)
    return win, wq, wk, wv, w_out


def _original_layouts(dwin, dwq, dwk, dwv):
    dwi = _unperm_in(dwin[:, :D_IN])
    dq = dwq.reshape(Q_RANK, HEADS, MLA_QK_PAD)[:, :, :192].reshape(Q_RANK, HEADS * 192)
    dkv = jnp.concatenate([dwk.reshape(KV_RANK, HEADS, 128), dwv.reshape(KV_RANK, HEADS, 128)], axis=2)
    return dwi, dq, dkv.reshape(KV_RANK, 2 * SB_WIDTH)


def kernel(x, positions, pre_norm_w, w_in, q_norm_w, w_q_up, kv_norm_w, w_kv_up, w_out, post_norm_w, loss_target, m_pre_norm_w, m_w_in, m_q_norm_w, m_w_q_up, m_kv_norm_w, m_w_kv_up, m_w_out, m_post_norm_w, v_pre_norm_w, v_w_in, v_q_norm_w, v_w_q_up, v_kv_norm_w, v_w_kv_up, v_w_out, v_post_norm_w):
    c = lax.axis_index("c")
    chip = 2 * lax.axis_index("x") + lax.axis_index("y")
    shards = (w_in[0], w_q_up[0], w_kv_up[0], w_out[0])
    mine16 = [w.astype(BF16) for w in shards]
    others = _gather_weights(mine16)
    slot = lambda w, k: jnp.where(chip == k, mine16[w], others[w][k])
    full = tuple(jnp.concatenate([slot(w, k) for k in range(4)], axis=ax) for w, ax in ((0, 1), (1, 1), (2, 1), (3, 0)))
    win, wq, wk, wv, wout = _kernel_layouts(full)

    err2, dproj, dout, dwin, dqw, dwq, dkvw, dwk, dwv, dwout, dpost = _local_step(
        x[0], positions[0], pre_norm_w, win, q_norm_w, wq, kv_norm_w, wk, wv, wout, post_norm_w, loss_target[0])
    loss = lax.psum(0.5 * jnp.sum(err2) / D_MODEL, ("x", "y", "c"))

    dwi, dq, dkv = _original_layouts(dwin, dwq, dwk, dwv)
    grads = [_by_chip_and_half(g, ax) for g, ax in ((dwi, 1), (dq, 1), (dkv, 1), (dwout, 0))]
    place = jnp.stack([c, chip])
    halves = _swap_halves(grads)
    pairs = [_pair_sum(place, grads[w], halves[w], "pair_sum_" + W_NAMES[w]) for w in range(NW)]
    gx, dpre, gots = _pre_bwd(dproj, win, x[0], pre_norm_w, dout, [p[0] for p in pairs])
    ms = (m_w_in[0], m_w_q_up[0], m_w_kv_up[0], m_w_out[0])
    vs = (v_w_in[0], v_w_q_up[0], v_w_kv_up[0], v_w_out[0])
    done = [_sum_adamw(c.reshape(1), pairs[w][1], gots[w], shards[w], ms[w], vs[w], "sum_adamw_" + W_NAMES[w])
            for w in range(NW)]
    shared = _share_with_sibling([done[w][k] for k in range(4) for w in range(NW)])
    big = [shared[NW * k:NW * (k + 1)] for k in range(4)]

    small = _norm_allreduce_adamw(
        _pack_norm([dpre, dqw, dkvw, dpost]), _pack_norm([pre_norm_w, q_norm_w, kv_norm_w, post_norm_w]),
        _pack_norm([m_pre_norm_w, m_q_norm_w, m_kv_norm_w, m_post_norm_w]),
        _pack_norm([v_pre_norm_w, v_q_norm_w, v_kv_norm_w, v_post_norm_w]))
    small = [_unpack_norm(p) for p in small]

    def group(k):
        n, b = small[k], big[k]
        return (n[0], b[0][None], n[1], b[1][None], n[2], b[2][None], b[3][None], n[3])

    return (loss, gx[None], *group(0), *group(1), *group(2), *group(3))
```

```python
import functools
import math

import numpy as np
import jax
import jax.numpy as jnp
from jax import lax
from jax.experimental import pallas as pl
from jax.experimental.pallas import tpu as pltpu

F32 = jnp.float32
BF16 = jnp.bfloat16
MESH = pl.DeviceIdType.MESH

D_MODEL = 2048
HEADS = 8
HEAD_DIM = 128
SB_WIDTH = HEADS * HEAD_DIM
MLA_ROPE = 64
MLA_QK_PAD = 256
Q_RANK = 512
KV_RANK = 256
CHUNK = 64
EPS = 1e-6
ROPE_THETA = 10000.0
D_IN = 5952
D_INP = 6144
PB_W = 3072
PF_W = D_INP - PB_W
SB_SCALE = 1.0 / math.sqrt(HEAD_DIM)
MLA_SCALE = 1.0 / math.sqrt(HEAD_DIM + MLA_ROPE)
NEG = -1e30

ADAM_LR, ADAM_B1, ADAM_B2, ADAM_EPS, ADAM_WD, ADAM_STEP = 0.001, 0.9, 0.999, 1e-08, 0.01, 10
ADAM_C1 = 1.0 - ADAM_B1 ** ADAM_STEP
ADAM_C2 = 1.0 - ADAM_B2 ** ADAM_STEP

ANY = pl.BlockSpec(memory_space=pl.ANY)
VMEM_LIMIT = 56 * 1024 * 1024
TQ = 1024
TK = 256
MLA_TK = 512
UNROLL = 2


def _cp(sem=None, **kw):
    return pltpu.CompilerParams(dimension_semantics=sem, vmem_limit_bytes=VMEM_LIMIT, **kw)


def _dot(a, b, dims):
    return lax.dot_general(a, b, (dims, ((), ())), preferred_element_type=F32)


def _nn(a, b):
    return _dot(a, b, ((1,), (0,)))


def _nt(a, b):
    return _dot(a, b, ((1,), (1,)))


def _tn(a, b):
    return _dot(a, b, ((0,), (0,)))


def _rope(x, c, s1, s2):
    return x * c + pltpu.roll(x, 32, 1) * s1 + pltpu.roll(x, 96, 1) * s2


def _rope_t(x, c, s1, s2):
    return x * c - pltpu.roll(x, 32, 1) * s1 - pltpu.roll(x, 96, 1) * s2


def _silu_parts(g):
    sg = jax.nn.sigmoid(g)
    return g * sg, sg * (1.0 + g * (1.0 - sg))


def _matmul(a, b, *, mode, out_dtype, tm, tn, tk, name, n=None, b_off=0):
    if mode == "tn":
        kk, m = a.shape
        n = b.shape[1] if n is None else n
    else:
        m, kk = a.shape
        n = (b.shape[1] if mode == "nn" else b.shape[0]) if n is None else n
    nk = kk // tk
    a_spec = {"nn": pl.BlockSpec((tm, tk), lambda j, i, k: (i, k)),
              "nt": pl.BlockSpec((tm, tk), lambda j, i, k: (i, k)),
              "tn": pl.BlockSpec((tk, tm), lambda j, i, k: (k, i))}[mode]
    b_spec = {"nn": pl.BlockSpec((tk, tn), lambda j, i, k: (k, j + b_off)),
              "nt": pl.BlockSpec((tn, tk), lambda j, i, k: (j, k)),
              "tn": pl.BlockSpec((tk, tn), lambda j, i, k: (k, j))}[mode]
    dims = {"nn": ((1,), (0,)), "nt": ((1,), (1,)), "tn": ((0,), (0,))}[mode]

    def body(a_ref, b_ref, o_ref, acc_ref):
        k = pl.program_id(2)
        part = _dot(a_ref[...], b_ref[...], dims)
        if nk == 1:
            o_ref[...] = part.astype(out_dtype)
        else:
            @pl.when(k == 0)
            def _():
                acc_ref[...] = part

            @pl.when(k > 0)
            def _():
                acc_ref[...] += part

            @pl.when(k == nk - 1)
            def _():
                o_ref[...] = acc_ref[...].astype(out_dtype)

    return pl.pallas_call(
        body, name=name, grid=(n // tn, m // tm, nk),
        in_specs=[a_spec, b_spec], out_specs=pl.BlockSpec((tm, tn), lambda j, i, k: (i, j)),
        out_shape=jax.ShapeDtypeStruct((m, n), out_dtype),
        scratch_shapes=[pltpu.VMEM((tm, tn) if nk > 1 else (8, 128), F32)],
        compiler_params=_cp(("parallel", "parallel", "arbitrary")),
    )(a, b)


def _dw_by_shard(h, dp4, tm=1024, tk=2048):
    s, d = h.shape
    n, _, cols = dp4.shape
    tk = min(tk, s)
    nk = s // tk

    def body(a_ref, b_ref, o_ref, acc_ref):
        k = pl.program_id(2)
        part = _tn(a_ref[...], b_ref[...])

        @pl.when(k == 0)
        def _():
            acc_ref[...] = part

        @pl.when(k > 0)
        def _():
            acc_ref[...] += part

        @pl.when(k == nk - 1)
        def _():
            o_ref[...] = acc_ref[...]

    return pl.pallas_call(
        body, name="dw_in", grid=(n, d // tm, nk),
        in_specs=[pl.BlockSpec((tk, tm), lambda j, i, k: (k, i)),
                  pl.BlockSpec((None, tk, cols), lambda j, i, k: (j, k, 0))],
        out_specs=pl.BlockSpec((None, tm, cols), lambda j, i, k: (j, i, 0)),
        out_shape=jax.ShapeDtypeStruct((n, d, cols), F32),
        scratch_shapes=[pltpu.VMEM((tm, cols), F32)],
        compiler_params=_cp(("parallel", "parallel", "arbitrary")),
    )(h, dp4)


def _prenorm(x, w, tm=512):
    s, d = x.shape

    def body(x_ref, w_ref, h_ref):
        xv = x_ref[...]
        r = lax.rsqrt(jnp.mean(xv * xv, axis=1, keepdims=True) + EPS)
        h_ref[...] = ((xv * r) * w_ref[...]).astype(BF16)

    return pl.pallas_call(
        body, name="prenorm", grid=(s // tm,),
        in_specs=[pl.BlockSpec((tm, d), lambda i: (i, 0)), pl.BlockSpec((1, d), lambda i: (0, 0))],
        out_specs=pl.BlockSpec((tm, d), lambda i: (i, 0)),
        out_shape=jax.ShapeDtypeStruct((s, d), BF16),
        compiler_params=_cp(("parallel",)),
    )(x, w)


def _tri(n, cmp, value):
    row = lax.broadcasted_iota(jnp.int32, (n, n), 0)
    col = lax.broadcasted_iota(jnp.int32, (n, n), 1)
    return jnp.where(cmp(row, col), value, 0.0).astype(BF16)


def _softplus(z, mask):
    sp = jnp.maximum(z, 0.0) + jnp.log(1.0 + jnp.exp(-jnp.abs(z)))
    return sp if mask is None else jnp.where(mask, sp, 0.0)


def _tiles(s, tk=TK):
    tq = min(TQ, s)
    return tq, tk, math.gcd(tq // tk, UNROLL)


def _band_masks(tq, tk, fn):
    out = []
    for b in range(tq // tk):
        key = lax.broadcasted_iota(jnp.int32, (tk, tq - b * tk), 0) + b * tk
        qry = lax.broadcasted_iota(jnp.int32, (tk, tq - b * tk), 1) + b * tk
        out.append(fn(qry, key))
    return out


def _row_spec(tq):
    return pl.BlockSpec((8, tq), lambda h, i: (h, i))


def _sb_store_shape(s, tq, tk):
    nb, nq = tq // tk, s // tq
    return (HEADS, nb * nq * (nq + 1) // 2, tk, tq)


def _sb_fwd(projb):
    s = projb.shape[0]
    tq, tk, _ = _tiles(s)
    nb = tq // tk

    def body(q_ref, k_ref, v_ref, o_ref, a_st, b_st, acc_ref, car_ref, z_ref, a_buf, b_buf, sems):
        h, i = pl.program_id(0), pl.program_id(1)
        q = q_ref[...]
        from_here = _tri(tk, lambda s_, j: j >= s_, -1.0)
        masks = _band_masks(tq, tk, lambda t, s_: s_ < t)
        acc_ref[...] = jnp.zeros_like(acc_ref)
        car_ref[...] = jnp.zeros_like(car_ref)
        first = nb * (i * (i + 1) // 2)

        def stores(st, g):
            at = pl.ds(first + g * nb, nb)
            return (pltpu.make_async_copy(a_buf.at[st], a_st.at[h, at], sems.at[st, 0]),
                    pltpu.make_async_copy(b_buf.at[st], b_st.at[h, at], sems.at[st, 1]))

        def tile(j, mask, off, st, u):
            rows = pl.ds(pl.multiple_of(j * tk, tk), tk)
            z_ref[:, off:] = _nt(k_ref[rows, :], q[off:, :]) * SB_SCALE
            z = z_ref[:, off:]
            sp = _softplus(z, mask)
            a = jnp.exp(z + _nn(from_here, sp.astype(BF16)) + car_ref[:, off:])
            beta = jnp.exp(z - sp)
            if mask is not None:
                a = jnp.where(mask, a, 0.0)
                beta = jnp.where(mask, beta, 0.0)
            a = a.astype(BF16)
            if off:
                a_buf[st, u, :, :off] = jnp.zeros((tk, off), BF16)
                b_buf[st, u, :, :off] = jnp.zeros((tk, off), BF16)
            a_buf[st, u, :, off:] = a
            b_buf[st, u, :, off:] = beta.astype(BF16)
            acc_ref[:, off:] += _tn(v_ref[rows, :], a)
            car_ref[:, off:] -= jnp.sum(sp, axis=0, keepdims=True)

        for b in reversed(range(nb)):
            tile(i * nb + b, masks[b], b * tk, 0, b)
        for cp in stores(0, i):
            cp.start()

        def step(jj, c):
            st, g = (jj + 1) % 2, i - 1 - jj

            @pl.when(jj >= 1)
            def _():
                for cp in stores(st, g):
                    cp.wait()

            for u in reversed(range(nb)):
                tile(g * nb + u, None, 0, st, u)
            for cp in stores(st, g):
                cp.start()
            return c

        lax.fori_loop(0, i, step, 0)
        for cp in stores(0, 0):
            cp.wait()

        @pl.when(i >= 1)
        def _():
            for cp in stores(1, 0):
                cp.wait()

        o_ref[...] = acc_ref[...].T

    blk = pl.BlockSpec((tq, HEAD_DIM), lambda h, i: (i, h))
    st = jax.ShapeDtypeStruct(_sb_store_shape(s, tq, tk), BF16)
    return pl.pallas_call(
        body, name="sb_fwd", grid=(HEADS, s // tq),
        in_specs=[blk, pl.BlockSpec((s, HEAD_DIM), lambda h, i: (0, HEADS + h)),
                  pl.BlockSpec((s, HEAD_DIM), lambda h, i: (0, 2 * HEADS + h))],
        out_specs=[blk, ANY, ANY],
        out_shape=[jax.ShapeDtypeStruct((s, SB_WIDTH), F32), st, st],
        scratch_shapes=[pltpu.VMEM((HEAD_DIM, tq), F32), pltpu.VMEM((1, tq), F32), pltpu.VMEM((tk, tq), F32),
                        pltpu.VMEM((2, nb, tk, tq), BF16), pltpu.VMEM((2, nb, tk, tq), BF16),
                        pltpu.SemaphoreType.DMA((2, 2))],
        compiler_params=_cp(("arbitrary", "arbitrary")),
    )(projb, projb, projb)


def _mla_prep(projf, qw, kvw, wq, wk, wv, rc, rs1, rs2, tm=512):
    s = projf.shape[0]

    def body(cq_ref, ckv_ref, kr_ref, qw_ref, kvw_ref, wq_ref, wk_ref, wv_ref, c_ref, s1_ref, s2_ref,
             nq_ref, nkv_ref, q_ref, k_ref, v_ref):
        c, s1, s2 = c_ref[...], s1_ref[...], s2_ref[...]
        cq = cq_ref[...]
        nq = ((cq * lax.rsqrt(jnp.mean(cq * cq, axis=1, keepdims=True) + EPS)) * qw_ref[...]).astype(BF16)
        nq_ref[...] = nq
        qf = _nn(nq, wq_ref[...])
        ckv = ckv_ref[...]
        nkv = ((ckv * lax.rsqrt(jnp.mean(ckv * ckv, axis=1, keepdims=True) + EPS)) * kvw_ref[...]).astype(BF16)
        nkv_ref[...] = nkv
        kn = _nn(nkv, wk_ref[...])
        v_ref[...] = _nn(nkv, wv_ref[...]).astype(BF16)
        krot = _rope(kr_ref[...], c, s1, s2).astype(BF16)
        for h in range(HEADS):
            lo = h * MLA_QK_PAD
            q_ref[:, lo:lo + 128] = qf[:, lo:lo + 128].astype(BF16)
            q_ref[:, lo + 128:lo + 256] = _rope(qf[:, lo + 128:lo + 256], c, s1, s2).astype(BF16)
            k_ref[:, lo:lo + 128] = kn[:, h * 128:(h + 1) * 128].astype(BF16)
            k_ref[:, lo + 128:lo + 256] = krot

    row = lambda w, b: pl.BlockSpec((tm, w), lambda i: (i, b))
    full = lambda a: pl.BlockSpec(a.shape, lambda i: (0, 0))
    return pl.pallas_call(
        body, name="mla_prep", grid=(s // tm,),
        in_specs=[row(Q_RANK, 2048 // Q_RANK), row(KV_RANK, 2560 // KV_RANK), row(128, 2816 // 128),
                  full(qw), full(kvw), full(wq), full(wk), full(wv), row(128, 0), row(128, 0), row(128, 0)],
        out_specs=[row(Q_RANK, 0), row(KV_RANK, 0), row(HEADS * MLA_QK_PAD, 0), row(HEADS * MLA_QK_PAD, 0),
                   row(SB_WIDTH, 0)],
        out_shape=[jax.ShapeDtypeStruct((s, Q_RANK), BF16), jax.ShapeDtypeStruct((s, KV_RANK), BF16),
                   jax.ShapeDtypeStruct((s, HEADS * MLA_QK_PAD), BF16),
                   jax.ShapeDtypeStruct((s, HEADS * MLA_QK_PAD), BF16),
                   jax.ShapeDtypeStruct((s, SB_WIDTH), BF16)],
        compiler_params=_cp(("parallel",)),
    )(projf, projf, projf, qw, kvw, wq, wk, wv, rc, rs1, rs2)


def _mla_mask(qry, key):
    return (key // CHUNK) <= (qry // CHUNK)


def _mla_fwd(qm, km, vm):
    s = qm.shape[0]
    tq, tk, unroll = _tiles(s, MLA_TK)
    nb = tq // tk

    def body(q_ref, k_ref, v_ref, o_ref, lse_ref, acc_ref, m_ref, l_ref):
        i = pl.program_id(1)
        q = q_ref[...]
        masks = _band_masks(tq, tk, _mla_mask)
        acc_ref[...] = jnp.zeros_like(acc_ref)
        m_ref[...] = jnp.full_like(m_ref, NEG)
        l_ref[...] = jnp.zeros_like(l_ref)

        def tile(j, mask, off):
            rows = pl.ds(pl.multiple_of(j * tk, tk), tk)
            sc = _nt(k_ref[rows, :], q[off:, :]) * MLA_SCALE
            if mask is not None:
                sc = jnp.where(mask, sc, NEG)
            m_old = m_ref[:, off:]
            m_new = jnp.maximum(m_old, jnp.max(sc, axis=0, keepdims=True))
            p = jnp.exp(sc - m_new)
            alpha = jnp.exp(m_old - m_new)
            l_ref[:, off:] = alpha * l_ref[:, off:] + jnp.sum(p, axis=0, keepdims=True)
            acc_ref[:, off:] = alpha * acc_ref[:, off:] + _tn(v_ref[rows, :], p.astype(BF16))
            m_ref[:, off:] = m_new

        for b in range(nb):
            tile(i * nb + b, masks[b], b * tk)

        def step(jj, c):
            for u in range(unroll):
                tile(jj * unroll + u, None, 0)
            return c

        lax.fori_loop(0, i * nb // unroll, step, 0)
        o_ref[...] = (acc_ref[...] / l_ref[...]).T
        lse_ref[...] = jnp.broadcast_to(m_ref[...] + jnp.log(l_ref[...]), (8, tq))

    return pl.pallas_call(
        body, name="mla_fwd", grid=(HEADS, s // tq),
        in_specs=[pl.BlockSpec((tq, MLA_QK_PAD), lambda h, i: (i, h)),
                  pl.BlockSpec((s, MLA_QK_PAD), lambda h, i: (0, h)),
                  pl.BlockSpec((s, HEAD_DIM), lambda h, i: (0, h))],
        out_specs=[pl.BlockSpec((tq, HEAD_DIM), lambda h, i: (i, h)), _row_spec(tq)],
        out_shape=[jax.ShapeDtypeStruct((s, SB_WIDTH), F32), jax.ShapeDtypeStruct((8 * HEADS, s), F32)],
        scratch_shapes=[pltpu.VMEM((HEAD_DIM, tq), F32), pltpu.VMEM((1, tq), F32), pltpu.VMEM((1, tq), F32)],
        compiler_params=_cp(("parallel", "arbitrary")),
    )(qm, km, vm)


def _out_post(oa, ob, projf, wout, x, target, pw, tm=256):
    s, d = x.shape

    def body(oa_ref, ob_ref, ga_ref, gb_ref, w_ref, x_ref, t_ref, pw_ref,
             mix_ref, dy_ref, dout_ref, loss_ref, dpw_ref):
        i = pl.program_id(0)
        sa, _ = _silu_parts(ga_ref[...])
        sb, _ = _silu_parts(gb_ref[...])
        mix_ref[:, :SB_WIDTH] = (oa_ref[...] * sa).astype(BF16)
        mix_ref[:, SB_WIDTH:] = (ob_ref[...] * sb).astype(BF16)
        y = _nn(mix_ref[...], w_ref[...])
        r = lax.rsqrt(jnp.mean(y * y, axis=1, keepdims=True) + EPS)
        yhat = y * r
        pwv = pw_ref[...]
        err = (x_ref[...] + yhat * pwv) - t_ref[...]
        dout = err * (1.0 / d)
        dout_ref[...] = dout
        g = dout * pwv
        dy_ref[...] = (r * (g - yhat * jnp.mean(g * yhat, axis=1, keepdims=True))).astype(BF16)

        @pl.when(i == 0)
        def _():
            loss_ref[...] = jnp.zeros_like(loss_ref)
            dpw_ref[...] = jnp.zeros_like(dpw_ref)

        loss_ref[...] += jnp.sum(err * err, axis=0, keepdims=True)
        dpw_ref[...] += jnp.sum(dout * yhat, axis=0, keepdims=True)

    row = lambda w, b: pl.BlockSpec((tm, w), lambda i: (i, b))
    vec = pl.BlockSpec((1, d), lambda i: (0, 0))
    return pl.pallas_call(
        body, name="out_post", grid=(s // tm,),
        in_specs=[row(SB_WIDTH, 0), row(SB_WIDTH, 0), row(SB_WIDTH, 0), row(SB_WIDTH, 1),
                  pl.BlockSpec(wout.shape, lambda i: (0, 0)), row(d, 0), row(d, 0), vec],
        out_specs=[row(d, 0), row(d, 0), row(d, 0), vec, vec],
        out_shape=[jax.ShapeDtypeStruct((s, d), BF16), jax.ShapeDtypeStruct((s, d), BF16),
                   jax.ShapeDtypeStruct((s, d), F32), jax.ShapeDtypeStruct((1, d), F32),
                   jax.ShapeDtypeStruct((1, d), F32)],
        compiler_params=_cp(("arbitrary",)),
    )(oa, ob, projf, projf, wout, x, target, pw)


def _sb_bwd(projb, projf, dmixed, oa, a_st, b_st):
    s = projb.shape[0]
    tq, tk, _ = _tiles(s)
    nb, nq = tq // tk, s // tq

    def body(q_ref, k_ref, v_ref, dm_ref, g_ref, o_ref, a_st, b_st, dq_ref, dk_ref, dv_ref, dg_ref,
             dka_ref, dva_ref, dqa_ref, cg_ref, a_buf, b_buf, sems):
        h, i = pl.program_id(0), pl.program_id(1)

        @pl.when(i == 0)
        def _():
            dka_ref[...] = jnp.zeros_like(dka_ref)
            dva_ref[...] = jnp.zeros_like(dva_ref)

        q = q_ref[...]
        silu, dsilu = _silu_parts(g_ref[...])
        dm = dm_ref[...]
        dg_ref[...] = (dm * o_ref[...] * dsilu).astype(BF16)
        do = (dm * silu).astype(BF16)
        up_to_here = _tri(tk, lambda s_, j: j <= s_, 1.0)
        dqa_ref[...] = jnp.zeros_like(dqa_ref)
        cg_ref[...] = jnp.zeros_like(cg_ref)
        first = nb * (i * (i + 1) // 2)

        def loads(st, g):
            at = pl.ds(first + g * nb, nb)
            return (pltpu.make_async_copy(a_st.at[h, at], a_buf.at[st], sems.at[st, 0]),
                    pltpu.make_async_copy(b_st.at[h, at], b_buf.at[st], sems.at[st, 1]))

        def tile(j, off, st, u):
            rows = pl.ds(pl.multiple_of(j * tk, tk), tk)
            k = k_ref[rows, :]
            a = a_buf[st, u, :, off:]
            g = a.astype(F32) * _nt(v_ref[rows, :], do[off:, :])
            dva_ref[rows, :] += _nn(a, do[off:, :])
            cum = _nn(up_to_here, g.astype(BF16)) + cg_ref[:, off:]
            dz = ((g - b_buf[st, u, :, off:].astype(F32) * cum) * SB_SCALE).astype(BF16)
            dqa_ref[:, off:] += _tn(k, dz)
            dka_ref[rows, :] += _nn(dz, q[off:, :])
            cg_ref[:, off:] += jnp.sum(g, axis=0, keepdims=True)

        for cp in loads(0, 0):
            cp.start()

        def step(g, c):
            st = g % 2
            for cp in loads(1 - st, g + 1):
                cp.start()
            for cp in loads(st, g):
                cp.wait()
            for u in range(nb):
                tile(g * nb + u, 0, st, u)
            return c

        lax.fori_loop(0, i, step, 0)
        for cp in loads(i % 2, i):
            cp.wait()
        for b in range(nb):
            tile(i * nb + b, b * tk, i % 2, b)
        dq_ref[...] = dqa_ref[...].T.astype(BF16)

        @pl.when(i == nq - 1)
        def _():
            dk_ref[...] = dka_ref[...].astype(BF16)
            dv_ref[...] = dva_ref[...].astype(BF16)

    blk = lambda off: pl.BlockSpec((tq, HEAD_DIM), lambda h, i: (i, off + h))
    whole = lambda off: pl.BlockSpec((s, HEAD_DIM), lambda h, i: (0, off + h))
    o_sd = jax.ShapeDtypeStruct((s, SB_WIDTH), BF16)
    return pl.pallas_call(
        body, name="sb_bwd", grid=(HEADS, nq),
        in_specs=[blk(0), whole(HEADS), whole(2 * HEADS), blk(0), blk(0), blk(0), ANY, ANY],
        out_specs=[blk(0), whole(0), whole(0), blk(0)],
        out_shape=[o_sd, o_sd, o_sd, o_sd],
        scratch_shapes=[pltpu.VMEM((s, HEAD_DIM), F32), pltpu.VMEM((s, HEAD_DIM), F32),
                        pltpu.VMEM((HEAD_DIM, tq), F32), pltpu.VMEM((1, tq), F32),
                        pltpu.VMEM((2, nb, tk, tq), BF16), pltpu.VMEM((2, nb, tk, tq), BF16),
                        pltpu.SemaphoreType.DMA((2, 2))],
        compiler_params=_cp(("arbitrary", "arbitrary")),
    )(projb, projb, projb, dmixed, projf, oa, a_st, b_st)


def _mla_bwd(qm, km, vm, projf, dmixed, ob, lse):
    s = qm.shape[0]
    tq, tk, unroll = _tiles(s, MLA_TK)
    nb, nq = tq // tk, s // tq

    def body(q_ref, k_ref, v_ref, dm_ref, g_ref, o_ref, lse_ref, dq_ref, dk_ref, dv_ref, dg_ref,
             dva_ref, dqa_ref):
        i = pl.program_id(1)

        @pl.when(i == 0)
        def _():
            dk_ref[...] = jnp.zeros_like(dk_ref)
            dva_ref[...] = jnp.zeros_like(dva_ref)

        q = q_ref[...]
        silu, dsilu = _silu_parts(g_ref[...])
        dm = dm_ref[...]
        o = o_ref[...]
        dg_ref[...] = (dm * o * dsilu).astype(BF16)
        dof = dm * silu
        delta = jnp.sum((dof * o).T, axis=0, keepdims=True)
        do = dof.astype(BF16)
        lse = lse_ref[0:1, :]
        masks = _band_masks(tq, tk, _mla_mask)
        dqa_ref[...] = jnp.zeros_like(dqa_ref)

        def tile(j, mask, off):
            rows = pl.ds(pl.multiple_of(j * tk, tk), tk)
            k = k_ref[rows, :]
            p = jnp.exp(_nt(k, q[off:, :]) * MLA_SCALE - lse[:, off:])
            if mask is not None:
                p = jnp.where(mask, p, 0.0)
            ds = (p * (_nt(v_ref[rows, :], do[off:, :]) - delta[:, off:]) * MLA_SCALE).astype(BF16)
            dva_ref[rows, :] += _nn(p.astype(BF16), do[off:, :])
            dk_ref[rows, :] += _nn(ds, q[off:, :])
            dqa_ref[:, off:] += _tn(k, ds)

        def step(jj, c):
            for u in range(unroll):
                tile(jj * unroll + u, None, 0)
            return c

        lax.fori_loop(0, i * nb // unroll, step, 0)
        for b in range(nb):
            tile(i * nb + b, masks[b], b * tk)
        dq_ref[...] = dqa_ref[...].T

        @pl.when(i == nq - 1)
        def _():
            dv_ref[...] = dva_ref[...].astype(BF16)

    blk = lambda w, off: pl.BlockSpec((tq, w), lambda h, i: (i, off + h))
    whole = lambda w: pl.BlockSpec((s, w), lambda h, i: (0, h))
    return pl.pallas_call(
        body, name="mla_bwd", grid=(HEADS, nq),
        in_specs=[blk(MLA_QK_PAD, 0), whole(MLA_QK_PAD), whole(HEAD_DIM), blk(HEAD_DIM, HEADS),
                  blk(HEAD_DIM, HEADS), blk(HEAD_DIM, 0), _row_spec(tq)],
        out_specs=[blk(MLA_QK_PAD, 0), whole(MLA_QK_PAD), whole(HEAD_DIM), blk(HEAD_DIM, 0)],
        out_shape=[jax.ShapeDtypeStruct((s, HEADS * MLA_QK_PAD), F32),
                   jax.ShapeDtypeStruct((s, HEADS * MLA_QK_PAD), F32),
                   jax.ShapeDtypeStruct((s, SB_WIDTH), BF16), jax.ShapeDtypeStruct((s, SB_WIDTH), BF16)],
        scratch_shapes=[pltpu.VMEM((s, HEAD_DIM), F32), pltpu.VMEM((MLA_QK_PAD, tq), F32)],
        compiler_params=_cp(("parallel", "arbitrary")),
    )(qm, km, vm, dmixed, projf, ob, lse)


def _norm_bwd(x, w, dn):
    r = lax.rsqrt(jnp.mean(x * x, axis=1, keepdims=True) + EPS)
    xhat = x * r
    g = dn * w
    return r * (g - xhat * jnp.mean(g * xhat, axis=1, keepdims=True)), dn * xhat


def _mla_bwd_post(dqm, dkm, dvm, projf, qw, kvw, wq, wk, wv, rc, rs1, rs2, tm=256):
    s = dqm.shape[0]

    def body(dq_ref, dk_ref, dv_ref, cq_ref, ckv_ref, qw_ref, kvw_ref, wq_ref, wk_ref, wv_ref,
             c_ref, s1_ref, s2_ref, dqp_ref, dkn_ref, dlat_ref, dqw_ref, dkvw_ref):
        i = pl.program_id(0)
        c, s1, s2 = c_ref[...], s1_ref[...], s2_ref[...]
        drot = jnp.zeros((tm, 128), F32)
        for h in range(HEADS):
            lo = h * MLA_QK_PAD
            dqp_ref[:, lo:lo + 128] = dq_ref[:, lo:lo + 128].astype(BF16)
            dqp_ref[:, lo + 128:lo + 256] = _rope_t(dq_ref[:, lo + 128:lo + 256], c, s1, s2).astype(BF16)
            dkn_ref[:, h * 128:(h + 1) * 128] = dk_ref[:, lo:lo + 128].astype(BF16)
            drot = drot + dk_ref[:, lo + 128:lo + 256]
        dlat_ref[:, Q_RANK + KV_RANK:Q_RANK + KV_RANK + 128] = _rope_t(drot, c, s1, s2).astype(BF16)
        dlat_ref[:, Q_RANK + KV_RANK + 128:] = jnp.zeros((tm, 128), BF16)
        dcq, dqw = _norm_bwd(cq_ref[...], qw_ref[...], _nt(dqp_ref[...], wq_ref[...]))
        dlat_ref[:, :Q_RANK] = dcq.astype(BF16)
        dnkv = _nt(dkn_ref[...], wk_ref[...]) + _nt(dv_ref[...], wv_ref[...])
        dckv, dkvw = _norm_bwd(ckv_ref[...], kvw_ref[...], dnkv)
        dlat_ref[:, Q_RANK:Q_RANK + KV_RANK] = dckv.astype(BF16)

        @pl.when(i == 0)
        def _():
            dqw_ref[...] = jnp.zeros_like(dqw_ref)
            dkvw_ref[...] = jnp.zeros_like(dkvw_ref)

        dqw_ref[...] += jnp.sum(dqw, axis=0, keepdims=True)
        dkvw_ref[...] += jnp.sum(dkvw, axis=0, keepdims=True)

    row = lambda w, b: pl.BlockSpec((tm, w), lambda i: (i, b))
    full = lambda a: pl.BlockSpec(a.shape, lambda i: (0, 0))
    sd = jax.ShapeDtypeStruct
    return pl.pallas_call(
        body, name="mla_bwd_post", grid=(s // tm,),
        in_specs=[row(HEADS * MLA_QK_PAD, 0), row(HEADS * MLA_QK_PAD, 0), row(SB_WIDTH, 0),
                  row(Q_RANK, 2048 // Q_RANK), row(KV_RANK, 2560 // KV_RANK),
                  full(qw), full(kvw), full(wq), full(wk), full(wv), row(128, 0), row(128, 0), row(128, 0)],
        out_specs=[row(HEADS * MLA_QK_PAD, 0), row(SB_WIDTH, 0), row(1024, 0),
                   pl.BlockSpec((1, Q_RANK), lambda i: (0, 0)), pl.BlockSpec((1, KV_RANK), lambda i: (0, 0))],
        out_shape=[sd((s, HEADS * MLA_QK_PAD), BF16), sd((s, SB_WIDTH), BF16), sd((s, 1024), BF16),
                   sd((1, Q_RANK), F32), sd((1, KV_RANK), F32)],
        compiler_params=_cp(("arbitrary",)),
    )(dqm, dkm, dvm, projf, projf, qw, kvw, wq, wk, wv, rc, rs1, rs2)


def _pre_bwd(pieces, win, x, pw, dout, psums, tm=256):
    s, d = x.shape
    npc, pw_ = len(pieces), pieces[0].shape[1]
    nw, nsteps = len(psums), s // tm

    def body(*refs):
        dps, (w_ref, x_ref, pw_ref, do_ref), refs = refs[:npc], refs[npc:npc + 4], refs[npc + 4:]
        ps, (dx_ref, dpw_ref), gots = refs[:nw], refs[nw:nw + 2], refs[nw + 2:2 * nw + 2]
        send_sems, recv_sems = refs[2 * nw + 2:]
        i = pl.program_id(0)

        def copies():
            if not nw:
                return []
            px, py, pc = _place()
            chips = [(1 - px, py), (px, 1 - py), (1 - px, 1 - py)]
            return [pltpu.make_async_remote_copy(
                src_ref=ps[w].at[2 * chip[0] + chip[1]], dst_ref=gots[w].at[j], send_sem=send_sems.at[3 * w + j],
                recv_sem=recv_sems.at[3 * w + j], device_id=(*chip, pc), device_id_type=MESH)
                for w in range(nw) for j, chip in enumerate(chips)]

        @pl.when(i == 0)
        def _():
            dpw_ref[...] = jnp.zeros_like(dpw_ref)
            for cp in copies():
                cp.start()

        dh = _nt(dps[0][...], w_ref[:, :pw_])
        for p in range(1, npc):
            dh = dh + _nt(dps[p][...], w_ref[:, p * pw_:(p + 1) * pw_])
        dx, dw = _norm_bwd(x_ref[...], pw_ref[...], dh)
        dx_ref[...] = do_ref[...] + dx
        dpw_ref[...] += jnp.sum(dw, axis=0, keepdims=True)

        @pl.when(i == nsteps - 1)
        def _():
            for cp in copies():
                cp.wait()

    rowd = pl.BlockSpec((tm, d), lambda i: (i, 0))
    vec = pl.BlockSpec((1, d), lambda i: (0, 0))
    out = pl.pallas_call(
        body, name="pre_bwd", grid=(nsteps,),
        in_specs=[pl.BlockSpec((tm, pw_), lambda i: (i, 0))] * npc
        + [pl.BlockSpec(win.shape, lambda i: (0, 0), pipeline_mode=pl.Buffered(1)), rowd, vec, rowd] + [ANY] * nw,
        out_specs=[rowd, vec] + [ANY] * nw,
        out_shape=[jax.ShapeDtypeStruct((s, d), F32), jax.ShapeDtypeStruct((1, d), F32)]
        + [jax.ShapeDtypeStruct((3,) + p.shape[1:], p.dtype) for p in psums],
        scratch_shapes=[pltpu.SemaphoreType.DMA((max(3 * nw, 1),)), pltpu.SemaphoreType.DMA((max(3 * nw, 1),))],
        compiler_params=_cp(("arbitrary",), has_side_effects=True),
    )(*pieces, win, x, pw, dout, *psums)
    return out[0], out[1], list(out[2:])


def _adamw(w, g, m, v):
    m = ADAM_B1 * m + (1.0 - ADAM_B1) * g
    v = ADAM_B2 * v + (1.0 - ADAM_B2) * (g * g)
    delta = -ADAM_LR * ((m / ADAM_C1) / (jnp.sqrt(v / ADAM_C2) + ADAM_EPS) + ADAM_WD * w)
    return delta, m, v


def _row_block(rows):
    return math.gcd(rows, 128)


def _sum_parts(core, own, got, name):
    n, hr, cols = got.shape
    tr = _row_block(hr)
    nblk = hr // tr

    def body(c_ref, o_ref, p_ref, g_ref):
        g = o_ref[...]
        for k in range(n):
            g = g + p_ref[k].astype(F32)
        g_ref[...] = g

    return pl.pallas_call(
        body, name=name,
        grid_spec=pltpu.PrefetchScalarGridSpec(
            num_scalar_prefetch=1, grid=(nblk,),
            in_specs=[pl.BlockSpec((tr, cols), lambda i, c: (i, 0)),
                      pl.BlockSpec((n, tr, cols), lambda i, c: (0, i, 0))],
            out_specs=pl.BlockSpec((tr, cols), lambda i, c: (c[0] * nblk + i, 0))),
        out_shape=jax.ShapeDtypeStruct((2 * hr, cols), F32),
        compiler_params=_cp(("parallel",)),
    )(core, own, got)


def _adamw_shard(g, w, m, v, name):
    rows, cols = g.shape
    tr = _row_block(rows)

    def body(g_ref, w_ref, m_ref, v_ref, go_ref, d_ref, nm_ref, nv_ref):
        g_ = g_ref[...]
        go_ref[...] = g_
        d_ref[...], nm_ref[...], nv_ref[...] = _adamw(w_ref[...], g_, m_ref[...], v_ref[...])

    blk = pl.BlockSpec((tr, cols), lambda i: (i, 0))
    sd = jax.ShapeDtypeStruct((rows, cols), F32)
    return pl.pallas_call(
        body, name=name, grid=(rows // tr,), in_specs=[blk] * 4, out_specs=[blk] * 4, out_shape=[sd] * 4,
        compiler_params=_cp(("parallel",)),
    )(g, w, m, v)


def _pair_sum(place, grads, got, name):
    n, _, hr, cols = grads.shape
    tr = _row_block(hr)

    def body(p_ref, a_ref, b_ref, oa_ref, ob_ref, all_ref, own_ref):
        all_ref[...] = (a_ref[:, 0] + b_ref[...]).astype(BF16)
        own_ref[...] = oa_ref[0, 0] + ob_ref[0]

    return pl.pallas_call(
        body, name=name,
        grid_spec=pltpu.PrefetchScalarGridSpec(
            num_scalar_prefetch=1, grid=(hr // tr,),
            in_specs=[pl.BlockSpec((n, 1, tr, cols), lambda i, p: (0, p[0], i, 0)),
                      pl.BlockSpec((n, tr, cols), lambda i, p: (0, i, 0)),
                      pl.BlockSpec((1, 1, tr, cols), lambda i, p: (p[1], p[0], i, 0)),
                      pl.BlockSpec((1, tr, cols), lambda i, p: (p[1], i, 0))],
            out_specs=[pl.BlockSpec((n, tr, cols), lambda i, p: (0, i, 0)),
                       pl.BlockSpec((tr, cols), lambda i, p: (i, 0))]),
        out_shape=[jax.ShapeDtypeStruct((n, hr, cols), BF16), jax.ShapeDtypeStruct((hr, cols), F32)],
        compiler_params=_cp(("parallel",)),
    )(place, grads, got, grads, got)


def _place():
    return lax.axis_index("x"), lax.axis_index("y"), lax.axis_index("c")


W_NAMES = ("w_in", "w_q_up", "w_kv_up", "w_out")
NW = len(W_NAMES)


def _comm_call(body, name, ins, out_shape, n_copies, aliases=None):
    return pl.pallas_call(
        body, name=name, in_specs=[ANY] * len(ins), out_specs=[ANY] * len(out_shape), out_shape=out_shape,
        input_output_aliases=aliases or {},
        scratch_shapes=[pltpu.SemaphoreType.DMA((n_copies,)), pltpu.SemaphoreType.DMA((n_copies,))],
        compiler_params=pltpu.CompilerParams(has_side_effects=True),
    )(*ins)


def _gather_weights(shards):
    def body(*refs):
        ws, outs = refs[:NW], refs[NW:2 * NW]
        send_sems, recv_sems = refs[2 * NW:]
        x, y, c = _place()
        chips = [(1 - x, y), (x, 1 - y), (1 - x, 1 - y)]

        def rows(w, chip, core):
            half = ws[w].shape[0] // 2
            return outs[w].at[2 * chip[0] + chip[1], pl.ds(core * half, half), :]

        def copy(w, k, chip, core, to, own=False):
            half = ws[w].shape[0] // 2
            return pltpu.make_async_remote_copy(
                src_ref=ws[w].at[pl.ds(core * half, half), :] if own else rows(w, chip, core),
                dst_ref=rows(w, chip, core), send_sem=send_sems.at[6 * w + k], recv_sem=recv_sems.at[6 * w + k],
                device_id=to, device_id_type=MESH)

        first = [copy(w, j, (x, y), c, (*chip, c), own=True) for w in range(NW) for j, chip in enumerate(chips)]
        for cp in first:
            cp.start()
        passed = []
        for w in range(NW):
            for j, chip in enumerate(chips):
                copy(w, j, chip, c, (x, y, c)).wait_recv()
                passed.append(copy(w, 3 + j, chip, c, (x, y, 1 - c)))
                passed[-1].start()
        for w in range(NW):
            for j, chip in enumerate(chips):
                copy(w, 3 + j, chip, 1 - c, (x, y, c)).wait_recv()
        for cp in first + passed:
            cp.wait_send()

    return _comm_call(body, "gather_weights", shards,
                      [jax.ShapeDtypeStruct((4,) + w.shape, w.dtype) for w in shards], 6 * NW)


def _swap_halves(grads):
    def body(*refs):
        gs, gots = refs[:NW], refs[NW:2 * NW]
        send_sems, recv_sems = refs[2 * NW:]
        x, y, c = _place()
        copies = [pltpu.make_async_remote_copy(
            src_ref=gs[w].at[k, 1 - c], dst_ref=gots[w].at[k], send_sem=send_sems.at[4 * w + k],
            recv_sem=recv_sems.at[4 * w + k], device_id=(x, y, 1 - c), device_id_type=MESH)
            for w in range(NW) for k in range(4)]
        for cp in copies:
            cp.start()
        for cp in copies:
            cp.wait()

    return _comm_call(body, "swap_halves", grads,
                      [jax.ShapeDtypeStruct((4,) + g.shape[2:], g.dtype) for g in grads], 4 * NW)


def _share_with_sibling(arrs):
    na = len(arrs)

    def body(*refs):
        outs = refs[na:2 * na]
        send_sems, recv_sems = refs[2 * na:]
        x, y, c = _place()

        def half(ref):
            hr = ref.shape[0] // 2
            return ref.at[pl.ds(c * hr, hr), :]

        copies = [pltpu.make_async_remote_copy(
            src_ref=half(outs[k]), dst_ref=half(outs[k]), send_sem=send_sems.at[k], recv_sem=recv_sems.at[k],
            device_id=(x, y, 1 - c), device_id_type=MESH) for k in range(na)]
        for cp in copies:
            cp.start()
        for cp in copies:
            cp.wait()

    return _comm_call(body, "share_with_sibling", arrs, [jax.ShapeDtypeStruct(a.shape, a.dtype) for a in arrs],
                      na, aliases={k: k for k in range(na)})


def _norm_allreduce_adamw(part, w, m, v):
    r, lanes = part.shape

    def body(p_ref, w_ref, m_ref, v_ref, g_ref, d_ref, nm_ref, nv_ref, all_ref, send_sems, recv_sems):
        x, y, c = _place()
        me = 4 * x + 2 * y + c
        all_ref[me] = p_ref[...]
        copies = []
        for k in range(1, 8):
            peer = (x ^ (k >> 2), y ^ ((k >> 1) & 1), c ^ (k & 1))
            copies.append(pltpu.make_async_remote_copy(
                src_ref=p_ref, dst_ref=all_ref.at[me], send_sem=send_sems.at[k - 1], recv_sem=recv_sems.at[k - 1],
                device_id=peer, device_id_type=MESH))
        for cp in copies:
            cp.start()
        for cp in copies:
            cp.wait()
        g = all_ref[0]
        for k in range(1, 8):
            g = g + all_ref[k]
        g_ref[...] = g
        d_ref[...], nm_ref[...], nv_ref[...] = _adamw(w_ref[...], g, m_ref[...], v_ref[...])

    vm = pl.BlockSpec(memory_space=pltpu.VMEM)
    sd = jax.ShapeDtypeStruct((r, lanes), F32)
    return pl.pallas_call(
        body, name="norm_allreduce_adamw", in_specs=[vm] * 4, out_specs=[vm] * 4, out_shape=[sd] * 4,
        scratch_shapes=[pltpu.VMEM((8, r, lanes), F32), pltpu.SemaphoreType.DMA((7,)),
                        pltpu.SemaphoreType.DMA((7,))],
        compiler_params=pltpu.CompilerParams(has_side_effects=True),
    )(part, w, m, v)


LANES = 128


def _perm_in(shards):
    s0, s1, s2, s3 = shards
    w = D_IN // 4
    return jnp.concatenate([s0, s1, s2[:, :4096 - 2 * w], s3[:, 4928 - 3 * w:], s2[:, 4096 - 2 * w:],
                            s3[:, :4928 - 3 * w], jnp.zeros((s0.shape[0], D_INP - D_IN), s0.dtype)], axis=1)


def _by_chip_and_half(g, axis):
    rows, cols = g.shape
    if axis == 0:
        return g.reshape(4, 2, rows // 8, cols)
    return g.reshape(rows, 4, cols // 4).transpose(1, 0, 2).reshape(4, 2, rows // 2, cols // 4)


NORM_NAMES = ("pre_norm_w", "q_norm_w", "kv_norm_w", "post_norm_w")
NORM_SIZES = (D_MODEL, Q_RANK, KV_RANK, D_MODEL)
NORM_ROWS = 40


def _pack_norm(vs):
    flat = jnp.concatenate([v.reshape(-1) for v in vs])
    return jnp.pad(flat, (0, NORM_ROWS * LANES - flat.shape[0])).reshape(NORM_ROWS, LANES)


def _unpack_norm(p):
    flat, out, at = p.reshape(-1), [], 0
    for n in NORM_SIZES:
        out.append(flat[at:at + n].reshape(1, n))
        at += n
    return out


def _rope_tables(positions):
    inv_freq = ROPE_THETA ** (-jnp.arange(0, MLA_ROPE, 2, dtype=F32) / MLA_ROPE)
    ang = positions.astype(F32)[:, None] * inv_freq
    cos, sin, z = jnp.cos(ang), jnp.sin(ang), jnp.zeros_like(ang)
    return (jnp.concatenate([cos, cos, z, z], axis=1), jnp.concatenate([z, sin, z, z], axis=1),
            jnp.concatenate([-sin, z, z, z], axis=1))


def _local_step(x, positions, pre_w, win, q_w, wq, kv_w, wk, wv, wout, post_w, target):
    s = x.shape[0]
    rc, rs1, rs2 = _rope_tables(positions)
    h = _prenorm(x, pre_w)
    projb = _matmul(h, win, mode="nn", out_dtype=BF16, tm=512, tn=1024, tk=D_MODEL, name="proj_b", n=PB_W)
    projf = _matmul(h, win, mode="nn", out_dtype=F32, tm=512, tn=1024, tk=D_MODEL, name="proj_f", n=PF_W,
                    b_off=PB_W // 1024)
    oa, a_st, b_st = _sb_fwd(projb)
    nq, nkv, qm, km, vm = _mla_prep(projf, q_w, kv_w, wq, wk, wv, rc, rs1, rs2)
    ob, lse = _mla_fwd(qm, km, vm)
    mixed, dy, dout, err2, dpost = _out_post(oa, ob, projf, wout, x, target, post_w)

    dwout = _matmul(mixed, dy, mode="tn", out_dtype=F32, tm=1024, tn=1024, tk=min(4096, s), name="dw_out")
    dmixed = _matmul(dy, wout, mode="nt", out_dtype=F32, tm=512, tn=1024, tk=D_MODEL, name="d_mixed")
    dqa, dka, dva, dga = _sb_bwd(projb, projf, dmixed, oa, a_st, b_st)
    dqm, dkm, dvm, dgb = _mla_bwd(qm, km, vm, projf, dmixed, ob, lse)
    dqp, dkn, dlat, dqw, dkvw = _mla_bwd_post(dqm, dkm, dvm, projf, q_w, kv_w, wq, wk, wv, rc, rs1, rs2)
    tks = min(4096, s)
    dwq = _matmul(nq, dqp, mode="tn", out_dtype=F32, tm=Q_RANK, tn=1024, tk=tks, name="dw_q")
    dwk = _matmul(nkv, dkn, mode="tn", out_dtype=F32, tm=KV_RANK, tn=1024, tk=tks, name="dw_k")
    dwv = _matmul(nkv, dvm, mode="tn", out_dtype=F32, tm=KV_RANK, tn=1024, tk=tks, name="dw_v")
    lat = Q_RANK + KV_RANK + MLA_ROPE
    orig = jnp.concatenate([dqa, dka, dva, dga, dlat[:, :lat], dgb], axis=1)
    dwin = _dw_by_shard(h, jnp.stack(jnp.split(orig, 4, axis=1)))
    return err2, (dqa, dka, dva, dga, dgb, dlat), dout, dwin, dqw, dwq, dkvw, dwk, dwv, dwout, dpost


def _kernel_layouts(in_shards, w_q_up, w_kv_up, w_out):
    win = _perm_in(in_shards)
    wq = jnp.pad(w_q_up.reshape(Q_RANK, HEADS, 192), ((0, 0), (0, 0), (0, 64))).reshape(Q_RANK, HEADS * MLA_QK_PAD)
    kv = w_kv_up.reshape(KV_RANK, HEADS, 256)
    wk = kv[:, :, :128].reshape(KV_RANK, SB_WIDTH)
    wv = kv[:, :, 128:].reshape(KV_RANK, SB_WIDTH)
    return win, wq, wk, wv, w_out


def _original_layouts(dwq, dwk, dwv):
    dq = dwq.reshape(Q_RANK, HEADS, MLA_QK_PAD)[:, :, :192].reshape(Q_RANK, HEADS * 192)
    dkv = jnp.concatenate([dwk.reshape(KV_RANK, HEADS, 128), dwv.reshape(KV_RANK, HEADS, 128)], axis=2)
    return dq, dkv.reshape(KV_RANK, 2 * SB_WIDTH)


def kernel(x, positions, pre_norm_w, w_in, q_norm_w, w_q_up, kv_norm_w, w_kv_up, w_out, post_norm_w, loss_target, m_pre_norm_w, m_w_in, m_q_norm_w, m_w_q_up, m_kv_norm_w, m_w_kv_up, m_w_out, m_post_norm_w, v_pre_norm_w, v_w_in, v_q_norm_w, v_w_q_up, v_kv_norm_w, v_w_kv_up, v_w_out, v_post_norm_w):
    c = lax.axis_index("c")
    chip = 2 * lax.axis_index("x") + lax.axis_index("y")
    shards = (w_in[0], w_q_up[0], w_kv_up[0], w_out[0])
    mine16 = [w.astype(BF16) for w in shards]
    others = _gather_weights(mine16)
    slot = lambda w, k: jnp.where(chip == k, mine16[w], others[w][k])
    cat = lambda w, ax: jnp.concatenate([slot(w, k) for k in range(4)], axis=ax)
    win, wq, wk, wv, wout = _kernel_layouts([slot(0, k) for k in range(4)], cat(1, 1), cat(2, 1), cat(3, 0))

    err2, pieces, dout, dwin, dqw, dwq, dkvw, dwk, dwv, dwout, dpost = _local_step(
        x[0], positions[0], pre_norm_w, win, q_norm_w, wq, kv_norm_w, wk, wv, wout, post_norm_w, loss_target[0])
    loss = lax.psum(0.5 * jnp.sum(err2) / D_MODEL, ("x", "y", "c"))

    dq, dkv = _original_layouts(dwq, dwk, dwv)
    grads = [dwin.reshape(4, 2, D_MODEL // 2, D_IN // 4)] + [
        _by_chip_and_half(g, ax) for g, ax in ((dq, 1), (dkv, 1), (dwout, 0))]
    place = jnp.stack([c, chip])
    halves = _swap_halves(grads)
    pairs = [_pair_sum(place, grads[w], halves[w], "pair_sum_" + W_NAMES[w]) for w in range(NW)]
    gx, dpre, gots = _pre_bwd(pieces, win, x[0], pre_norm_w, dout, [p[0] for p in pairs])
    sums = _share_with_sibling([_sum_parts(c.reshape(1), pairs[w][1], gots[w], "sum_parts_" + W_NAMES[w])
                                for w in range(NW)])
    ms = (m_w_in[0], m_w_q_up[0], m_w_kv_up[0], m_w_out[0])
    vs = (v_w_in[0], v_w_q_up[0], v_w_kv_up[0], v_w_out[0])
    done = [_adamw_shard(sums[w], shards[w], ms[w], vs[w], "adamw_" + W_NAMES[w]) for w in range(NW)]
    big = [[done[w][k] for w in range(NW)] for k in range(4)]

    small = _norm_allreduce_adamw(
        _pack_norm([dpre, dqw, dkvw, dpost]), _pack_norm([pre_norm_w, q_norm_w, kv_norm_w, post_norm_w]),
        _pack_norm([m_pre_norm_w, m_q_norm_w, m_kv_norm_w, m_post_norm_w]),
        _pack_norm([v_pre_norm_w, v_q_norm_w, v_kv_norm_w, v_post_norm_w]))
    small = [_unpack_norm(p) for p in small]

    def group(k):
        n, b = small[k], big[k]
        return (n[0], b[0][None], n[1], b[1][None], n[2], b[2][None], b[3][None], n[3])

    return (loss, gx[None], *group(0), *group(1), *group(2), *group(3))
```

```python
import functools
import math

import numpy as np
import jax
import jax.numpy as jnp
from jax import lax
from jax.experimental import pallas as pl
from jax.experimental.pallas import tpu as pltpu

F32 = jnp.float32
BF16 = jnp.bfloat16
MESH = pl.DeviceIdType.MESH

D_MODEL = 2048
HEADS = 8
HEAD_DIM = 128
SB_WIDTH = HEADS * HEAD_DIM
MLA_ROPE = 64
MLA_QK_PAD = 256
Q_RANK = 512
KV_RANK = 256
CHUNK = 64
EPS = 1e-6
ROPE_THETA = 10000.0
D_IN = 5952
D_INP = 6144
PB_W = 3072
PF_W = D_INP - PB_W
SB_SCALE = 1.0 / math.sqrt(HEAD_DIM)
MLA_SCALE = 1.0 / math.sqrt(HEAD_DIM + MLA_ROPE)
NEG = -1e30

ADAM_LR, ADAM_B1, ADAM_B2, ADAM_EPS, ADAM_WD, ADAM_STEP = 0.001, 0.9, 0.999, 1e-08, 0.01, 10
ADAM_C1 = 1.0 - ADAM_B1 ** ADAM_STEP
ADAM_C2 = 1.0 - ADAM_B2 ** ADAM_STEP

ANY = pl.BlockSpec(memory_space=pl.ANY)
VMEM_LIMIT = 56 * 1024 * 1024
TQ = 1024
TK = 256
MLA_TK = 512
MLA_FWD_TK = 1024
UNROLL = 2


def _cp(sem=None, **kw):
    return pltpu.CompilerParams(dimension_semantics=sem, vmem_limit_bytes=VMEM_LIMIT, **kw)


def _dot(a, b, dims):
    return lax.dot_general(a, b, (dims, ((), ())), preferred_element_type=F32)


def _nn(a, b):
    return _dot(a, b, ((1,), (0,)))


def _nt(a, b):
    return _dot(a, b, ((1,), (1,)))


def _tn(a, b):
    return _dot(a, b, ((0,), (0,)))


def _rope(x, c, s1, s2):
    return x * c + pltpu.roll(x, 32, 1) * s1 + pltpu.roll(x, 96, 1) * s2


def _rope_t(x, c, s1, s2):
    return x * c - pltpu.roll(x, 32, 1) * s1 - pltpu.roll(x, 96, 1) * s2


def _silu_parts(g):
    sg = jax.nn.sigmoid(g)
    return g * sg, sg * (1.0 + g * (1.0 - sg))


def _matmul(a, b, *, mode, out_dtype, tm, tn, tk, name, n=None, b_off=0):
    if mode == "tn":
        kk, m = a.shape
        n = b.shape[1] if n is None else n
    else:
        m, kk = a.shape
        n = (b.shape[1] if mode == "nn" else b.shape[0]) if n is None else n
    nk = kk // tk
    a_spec = {"nn": pl.BlockSpec((tm, tk), lambda j, i, k: (i, k)),
              "nt": pl.BlockSpec((tm, tk), lambda j, i, k: (i, k)),
              "tn": pl.BlockSpec((tk, tm), lambda j, i, k: (k, i))}[mode]
    b_spec = {"nn": pl.BlockSpec((tk, tn), lambda j, i, k: (k, j + b_off)),
              "nt": pl.BlockSpec((tn, tk), lambda j, i, k: (j, k)),
              "tn": pl.BlockSpec((tk, tn), lambda j, i, k: (k, j))}[mode]
    dims = {"nn": ((1,), (0,)), "nt": ((1,), (1,)), "tn": ((0,), (0,))}[mode]

    def body(a_ref, b_ref, o_ref, acc_ref):
        k = pl.program_id(2)
        part = _dot(a_ref[...], b_ref[...], dims)
        if nk == 1:
            o_ref[...] = part.astype(out_dtype)
        else:
            @pl.when(k == 0)
            def _():
                acc_ref[...] = part

            @pl.when(k > 0)
            def _():
                acc_ref[...] += part

            @pl.when(k == nk - 1)
            def _():
                o_ref[...] = acc_ref[...].astype(out_dtype)

    return pl.pallas_call(
        body, name=name, grid=(n // tn, m // tm, nk),
        in_specs=[a_spec, b_spec], out_specs=pl.BlockSpec((tm, tn), lambda j, i, k: (i, j)),
        out_shape=jax.ShapeDtypeStruct((m, n), out_dtype),
        scratch_shapes=[pltpu.VMEM((tm, tn) if nk > 1 else (8, 128), F32)],
        compiler_params=_cp(("parallel", "parallel", "arbitrary")),
    )(a, b)


def _dw_by_shard(h, dpad, cols, tm=1024, tk=2048):
    s, d = h.shape
    cpad = dpad.shape[1] // 4
    tk = min(tk, s)
    nk = s // tk

    def body(a_ref, b_ref, o_ref, acc_ref):
        k = pl.program_id(2)
        part = _tn(a_ref[...], b_ref[...])

        @pl.when(k == 0)
        def _():
            acc_ref[...] = part

        @pl.when(k > 0)
        def _():
            acc_ref[...] += part

        @pl.when(k == nk - 1)
        def _():
            o_ref[...] = acc_ref[:, :cols]

    return pl.pallas_call(
        body, name="dw_in", grid=(4, d // tm, nk),
        in_specs=[pl.BlockSpec((tk, tm), lambda j, i, k: (k, i)), pl.BlockSpec((tk, cpad), lambda j, i, k: (k, j))],
        out_specs=pl.BlockSpec((None, tm, cols), lambda j, i, k: (j, i, 0)),
        out_shape=jax.ShapeDtypeStruct((4, d, cols), F32),
        scratch_shapes=[pltpu.VMEM((tm, cpad), F32)],
        compiler_params=_cp(("parallel", "parallel", "arbitrary")),
    )(h, dpad)


def _prenorm(x, w, tm=512):
    s, d = x.shape

    def body(x_ref, w_ref, h_ref):
        xv = x_ref[...]
        r = lax.rsqrt(jnp.mean(xv * xv, axis=1, keepdims=True) + EPS)
        h_ref[...] = ((xv * r) * w_ref[...]).astype(BF16)

    return pl.pallas_call(
        body, name="prenorm", grid=(s // tm,),
        in_specs=[pl.BlockSpec((tm, d), lambda i: (i, 0)), pl.BlockSpec((1, d), lambda i: (0, 0))],
        out_specs=pl.BlockSpec((tm, d), lambda i: (i, 0)),
        out_shape=jax.ShapeDtypeStruct((s, d), BF16),
        compiler_params=_cp(("parallel",)),
    )(x, w)


def _tri(n, cmp, value):
    row = lax.broadcasted_iota(jnp.int32, (n, n), 0)
    col = lax.broadcasted_iota(jnp.int32, (n, n), 1)
    return jnp.where(cmp(row, col), value, 0.0).astype(BF16)


def _softplus(z, mask):
    sp = jnp.maximum(z, 0.0) + jnp.log(1.0 + jnp.exp(-jnp.abs(z)))
    return sp if mask is None else jnp.where(mask, sp, 0.0)


def _tiles(s, tk=TK):
    tq = min(TQ, s)
    return tq, tk, math.gcd(tq // tk, UNROLL)


def _band_masks(tq, tk, fn):
    out = []
    for b in range(tq // tk):
        key = lax.broadcasted_iota(jnp.int32, (tk, tq - b * tk), 0) + b * tk
        qry = lax.broadcasted_iota(jnp.int32, (tk, tq - b * tk), 1) + b * tk
        out.append(fn(qry, key))
    return out


def _row_spec(tq):
    return pl.BlockSpec((8, tq), lambda h, i: (h, i))


def _sb_store_shape(s, tq, tk):
    nb, nq = tq // tk, s // tq
    return (HEADS, nb * nq * (nq + 1) // 2, tk, tq)


def _sb_fwd(projb):
    s = projb.shape[0]
    tq, tk, _ = _tiles(s)
    nb = tq // tk

    def body(q_ref, k_ref, v_ref, o_ref, a_st, b_st, acc_ref, car_ref, z_ref, a_buf, b_buf, sems):
        h, i = pl.program_id(0), pl.program_id(1)
        q = q_ref[...]
        from_here = _tri(tk, lambda s_, j: j >= s_, -1.0)
        masks = _band_masks(tq, tk, lambda t, s_: s_ < t)
        acc_ref[...] = jnp.zeros_like(acc_ref)
        car_ref[...] = jnp.zeros_like(car_ref)
        first = nb * (i * (i + 1) // 2)

        def stores(st, g):
            at = pl.ds(first + g * nb, nb)
            return (pltpu.make_async_copy(a_buf.at[st], a_st.at[h, at], sems.at[st, 0]),
                    pltpu.make_async_copy(b_buf.at[st], b_st.at[h, at], sems.at[st, 1]))

        def tile(j, mask, off, st, u):
            rows = pl.ds(pl.multiple_of(j * tk, tk), tk)
            z_ref[:, off:] = _nt(k_ref[rows, :], q[off:, :]) * SB_SCALE
            z = z_ref[:, off:]
            sp = _softplus(z, mask)
            a = jnp.exp(z + _nn(from_here, sp.astype(BF16)) + car_ref[:, off:])
            beta = jnp.exp(z - sp)
            if mask is not None:
                a = jnp.where(mask, a, 0.0)
                beta = jnp.where(mask, beta, 0.0)
            a = a.astype(BF16)
            if off:
                a_buf[st, u, :, :off] = jnp.zeros((tk, off), BF16)
                b_buf[st, u, :, :off] = jnp.zeros((tk, off), BF16)
            a_buf[st, u, :, off:] = a
            b_buf[st, u, :, off:] = beta.astype(BF16)
            acc_ref[:, off:] += _tn(v_ref[rows, :], a)
            car_ref[:, off:] -= jnp.sum(sp, axis=0, keepdims=True)

        for b in reversed(range(nb)):
            tile(i * nb + b, masks[b], b * tk, 0, b)
        for cp in stores(0, i):
            cp.start()

        def step(jj, c):
            st, g = (jj + 1) % 2, i - 1 - jj

            @pl.when(jj >= 1)
            def _():
                for cp in stores(st, g):
                    cp.wait()

            for u in reversed(range(nb)):
                tile(g * nb + u, None, 0, st, u)
            for cp in stores(st, g):
                cp.start()
            return c

        lax.fori_loop(0, i, step, 0)
        for cp in stores(0, 0):
            cp.wait()

        @pl.when(i >= 1)
        def _():
            for cp in stores(1, 0):
                cp.wait()

        o_ref[...] = acc_ref[...].T

    blk = pl.BlockSpec((tq, HEAD_DIM), lambda h, i: (i, h))
    st = jax.ShapeDtypeStruct(_sb_store_shape(s, tq, tk), BF16)
    return pl.pallas_call(
        body, name="sb_fwd", grid=(HEADS, s // tq),
        in_specs=[blk, pl.BlockSpec((s, HEAD_DIM), lambda h, i: (0, HEADS + h)),
                  pl.BlockSpec((s, HEAD_DIM), lambda h, i: (0, 2 * HEADS + h))],
        out_specs=[blk, ANY, ANY],
        out_shape=[jax.ShapeDtypeStruct((s, SB_WIDTH), F32), st, st],
        scratch_shapes=[pltpu.VMEM((HEAD_DIM, tq), F32), pltpu.VMEM((1, tq), F32), pltpu.VMEM((tk, tq), F32),
                        pltpu.VMEM((2, nb, tk, tq), BF16), pltpu.VMEM((2, nb, tk, tq), BF16),
                        pltpu.SemaphoreType.DMA((2, 2))],
        compiler_params=_cp(("arbitrary", "arbitrary")),
    )(projb, projb, projb)


def _mla_prep(projf, qw, kvw, wq, wk, wv, rc, rs1, rs2, tm=512):
    s = projf.shape[0]

    def body(cq_ref, ckv_ref, kr_ref, qw_ref, kvw_ref, wq_ref, wk_ref, wv_ref, c_ref, s1_ref, s2_ref,
             nq_ref, nkv_ref, q_ref, k_ref, v_ref):
        c, s1, s2 = c_ref[...], s1_ref[...], s2_ref[...]
        cq = cq_ref[...]
        nq = ((cq * lax.rsqrt(jnp.mean(cq * cq, axis=1, keepdims=True) + EPS)) * qw_ref[...]).astype(BF16)
        nq_ref[...] = nq
        qf = _nn(nq, wq_ref[...])
        ckv = ckv_ref[...]
        nkv = ((ckv * lax.rsqrt(jnp.mean(ckv * ckv, axis=1, keepdims=True) + EPS)) * kvw_ref[...]).astype(BF16)
        nkv_ref[...] = nkv
        kn = _nn(nkv, wk_ref[...])
        v_ref[...] = _nn(nkv, wv_ref[...]).astype(BF16)
        krot = _rope(kr_ref[...], c, s1, s2).astype(BF16)
        for h in range(HEADS):
            lo = h * MLA_QK_PAD
            q_ref[:, lo:lo + 128] = qf[:, lo:lo + 128].astype(BF16)
            q_ref[:, lo + 128:lo + 256] = _rope(qf[:, lo + 128:lo + 256], c, s1, s2).astype(BF16)
            k_ref[:, lo:lo + 128] = kn[:, h * 128:(h + 1) * 128].astype(BF16)
            k_ref[:, lo + 128:lo + 256] = krot

    row = lambda w, b: pl.BlockSpec((tm, w), lambda i: (i, b))
    full = lambda a: pl.BlockSpec(a.shape, lambda i: (0, 0))
    return pl.pallas_call(
        body, name="mla_prep", grid=(s // tm,),
        in_specs=[row(Q_RANK, 2048 // Q_RANK), row(KV_RANK, 2560 // KV_RANK), row(128, 2816 // 128),
                  full(qw), full(kvw), full(wq), full(wk), full(wv), row(128, 0), row(128, 0), row(128, 0)],
        out_specs=[row(Q_RANK, 0), row(KV_RANK, 0), row(HEADS * MLA_QK_PAD, 0), row(HEADS * MLA_QK_PAD, 0),
                   row(SB_WIDTH, 0)],
        out_shape=[jax.ShapeDtypeStruct((s, Q_RANK), BF16), jax.ShapeDtypeStruct((s, KV_RANK), BF16),
                   jax.ShapeDtypeStruct((s, HEADS * MLA_QK_PAD), BF16),
                   jax.ShapeDtypeStruct((s, HEADS * MLA_QK_PAD), BF16),
                   jax.ShapeDtypeStruct((s, SB_WIDTH), BF16)],
        compiler_params=_cp(("parallel",)),
    )(projf, projf, projf, qw, kvw, wq, wk, wv, rc, rs1, rs2)


def _mla_mask(qry, key):
    return (key // CHUNK) <= (qry // CHUNK)


def _mla_fwd(qm, km, vm):
    s = qm.shape[0]
    tq, tk, unroll = _tiles(s, min(MLA_FWD_TK, s))
    nb = tq // tk

    def body(q_ref, k_ref, v_ref, o_ref, lse_ref, acc_ref, m_ref, l_ref):
        i = pl.program_id(1)
        q = q_ref[...]
        masks = _band_masks(tq, tk, _mla_mask)
        acc_ref[...] = jnp.zeros_like(acc_ref)
        m_ref[...] = jnp.full_like(m_ref, NEG)
        l_ref[...] = jnp.zeros_like(l_ref)

        def tile(j, mask, off):
            rows = pl.ds(pl.multiple_of(j * tk, tk), tk)
            sc = _nt(k_ref[rows, :], q[off:, :]) * MLA_SCALE
            if mask is not None:
                sc = jnp.where(mask, sc, NEG)
            m_old = m_ref[:, off:]
            m_new = jnp.maximum(m_old, jnp.max(sc, axis=0, keepdims=True))
            p = jnp.exp(sc - m_new)
            alpha = jnp.exp(m_old - m_new)
            l_ref[:, off:] = alpha * l_ref[:, off:] + jnp.sum(p, axis=0, keepdims=True)
            acc_ref[:, off:] = alpha * acc_ref[:, off:] + _tn(v_ref[rows, :], p.astype(BF16))
            m_ref[:, off:] = m_new

        for b in range(nb):
            tile(i * nb + b, masks[b], b * tk)

        def step(jj, c):
            for u in range(unroll):
                tile(jj * unroll + u, None, 0)
            return c

        lax.fori_loop(0, i * nb // unroll, step, 0)
        o_ref[...] = (acc_ref[...] / l_ref[...]).T
        lse_ref[...] = jnp.broadcast_to(m_ref[...] + jnp.log(l_ref[...]), (8, tq))

    return pl.pallas_call(
        body, name="mla_fwd", grid=(HEADS, s // tq),
        in_specs=[pl.BlockSpec((tq, MLA_QK_PAD), lambda h, i: (i, h)),
                  pl.BlockSpec((s, MLA_QK_PAD), lambda h, i: (0, h)),
                  pl.BlockSpec((s, HEAD_DIM), lambda h, i: (0, h))],
        out_specs=[pl.BlockSpec((tq, HEAD_DIM), lambda h, i: (i, h)), _row_spec(tq)],
        out_shape=[jax.ShapeDtypeStruct((s, SB_WIDTH), F32), jax.ShapeDtypeStruct((8 * HEADS, s), F32)],
        scratch_shapes=[pltpu.VMEM((HEAD_DIM, tq), F32), pltpu.VMEM((1, tq), F32), pltpu.VMEM((1, tq), F32)],
        compiler_params=_cp(("parallel", "arbitrary")),
    )(qm, km, vm)


def _out_post(oa, ob, projf, wout, x, target, pw, tm=256):
    s, d = x.shape

    def body(oa_ref, ob_ref, ga_ref, gb_ref, w_ref, x_ref, t_ref, pw_ref,
             mix_ref, dy_ref, dout_ref, loss_ref, dpw_ref):
        i = pl.program_id(0)
        sa, _ = _silu_parts(ga_ref[...])
        sb, _ = _silu_parts(gb_ref[...])
        mix_ref[:, :SB_WIDTH] = (oa_ref[...] * sa).astype(BF16)
        mix_ref[:, SB_WIDTH:] = (ob_ref[...] * sb).astype(BF16)
        y = _nn(mix_ref[...], w_ref[...])
        r = lax.rsqrt(jnp.mean(y * y, axis=1, keepdims=True) + EPS)
        yhat = y * r
        pwv = pw_ref[...]
        err = (x_ref[...] + yhat * pwv) - t_ref[...]
        dout = err * (1.0 / d)
        dout_ref[...] = dout
        g = dout * pwv
        dy_ref[...] = (r * (g - yhat * jnp.mean(g * yhat, axis=1, keepdims=True))).astype(BF16)

        @pl.when(i == 0)
        def _():
            loss_ref[...] = jnp.zeros_like(loss_ref)
            dpw_ref[...] = jnp.zeros_like(dpw_ref)

        loss_ref[...] += jnp.sum(err * err, axis=0, keepdims=True)
        dpw_ref[...] += jnp.sum(dout * yhat, axis=0, keepdims=True)

    row = lambda w, b: pl.BlockSpec((tm, w), lambda i: (i, b))
    vec = pl.BlockSpec((1, d), lambda i: (0, 0))
    return pl.pallas_call(
        body, name="out_post", grid=(s // tm,),
        in_specs=[row(SB_WIDTH, 0), row(SB_WIDTH, 0), row(SB_WIDTH, 0), row(SB_WIDTH, 1),
                  pl.BlockSpec(wout.shape, lambda i: (0, 0)), row(d, 0), row(d, 0), vec],
        out_specs=[row(d, 0), row(d, 0), row(d, 0), vec, vec],
        out_shape=[jax.ShapeDtypeStruct((s, d), BF16), jax.ShapeDtypeStruct((s, d), BF16),
                   jax.ShapeDtypeStruct((s, d), F32), jax.ShapeDtypeStruct((1, d), F32),
                   jax.ShapeDtypeStruct((1, d), F32)],
        compiler_params=_cp(("arbitrary",)),
    )(oa, ob, projf, projf, wout, x, target, pw)


def _sb_bwd(projb, projf, dmixed, oa, a_st, b_st):
    s = projb.shape[0]
    tq, tk, _ = _tiles(s)
    nb, nq = tq // tk, s // tq

    def body(q_ref, k_ref, v_ref, dm_ref, g_ref, o_ref, a_st, b_st, dq_ref, dk_ref, dv_ref, dg_ref,
             dka_ref, dva_ref, dqa_ref, cg_ref, a_buf, b_buf, sems):
        h, i = pl.program_id(0), pl.program_id(1)

        @pl.when(i == 0)
        def _():
            dka_ref[...] = jnp.zeros_like(dka_ref)
            dva_ref[...] = jnp.zeros_like(dva_ref)

        q = q_ref[...]
        silu, dsilu = _silu_parts(g_ref[...])
        dm = dm_ref[...]
        dg_ref[...] = (dm * o_ref[...] * dsilu).astype(BF16)
        do = (dm * silu).astype(BF16)
        up_to_here = _tri(tk, lambda s_, j: j <= s_, 1.0)
        dqa_ref[...] = jnp.zeros_like(dqa_ref)
        cg_ref[...] = jnp.zeros_like(cg_ref)
        first = nb * (i * (i + 1) // 2)

        def loads(st, g):
            at = pl.ds(first + g * nb, nb)
            return (pltpu.make_async_copy(a_st.at[h, at], a_buf.at[st], sems.at[st, 0]),
                    pltpu.make_async_copy(b_st.at[h, at], b_buf.at[st], sems.at[st, 1]))

        def tile(j, off, st, u):
            rows = pl.ds(pl.multiple_of(j * tk, tk), tk)
            k = k_ref[rows, :]
            a = a_buf[st, u, :, off:]
            g = a.astype(F32) * _nt(v_ref[rows, :], do[off:, :])
            dva_ref[rows, :] += _nn(a, do[off:, :])
            cum = _nn(up_to_here, g.astype(BF16)) + cg_ref[:, off:]
            dz = ((g - b_buf[st, u, :, off:].astype(F32) * cum) * SB_SCALE).astype(BF16)
            dqa_ref[:, off:] += _tn(k, dz)
            dka_ref[rows, :] += _nn(dz, q[off:, :])
            cg_ref[:, off:] += jnp.sum(g, axis=0, keepdims=True)

        for cp in loads(0, 0):
            cp.start()

        def step(g, c):
            st = g % 2
            for cp in loads(1 - st, g + 1):
                cp.start()
            for cp in loads(st, g):
                cp.wait()
            for u in range(nb):
                tile(g * nb + u, 0, st, u)
            return c

        lax.fori_loop(0, i, step, 0)
        for cp in loads(i % 2, i):
            cp.wait()
        for b in range(nb):
            tile(i * nb + b, b * tk, i % 2, b)
        dq_ref[...] = dqa_ref[...].T.astype(BF16)

        @pl.when(i == nq - 1)
        def _():
            dk_ref[...] = dka_ref[...].astype(BF16)
            dv_ref[...] = dva_ref[...].astype(BF16)

    blk = lambda off: pl.BlockSpec((tq, HEAD_DIM), lambda h, i: (i, off + h))
    whole = lambda off: pl.BlockSpec((s, HEAD_DIM), lambda h, i: (0, off + h))
    o_sd = jax.ShapeDtypeStruct((s, SB_WIDTH), BF16)
    return pl.pallas_call(
        body, name="sb_bwd", grid=(HEADS, nq),
        in_specs=[blk(0), whole(HEADS), whole(2 * HEADS), blk(0), blk(0), blk(0), ANY, ANY],
        out_specs=[blk(0), whole(0), whole(0), blk(0)],
        out_shape=[o_sd, o_sd, o_sd, o_sd],
        scratch_shapes=[pltpu.VMEM((s, HEAD_DIM), F32), pltpu.VMEM((s, HEAD_DIM), F32),
                        pltpu.VMEM((HEAD_DIM, tq), F32), pltpu.VMEM((1, tq), F32),
                        pltpu.VMEM((2, nb, tk, tq), BF16), pltpu.VMEM((2, nb, tk, tq), BF16),
                        pltpu.SemaphoreType.DMA((2, 2))],
        compiler_params=_cp(("arbitrary", "arbitrary")),
    )(projb, projb, projb, dmixed, projf, oa, a_st, b_st)


def _mla_bwd(qm, km, vm, projf, dmixed, ob, lse):
    s = qm.shape[0]
    tq, tk, unroll = _tiles(s, MLA_TK)
    nb, nq = tq // tk, s // tq

    def body(q_ref, k_ref, v_ref, dm_ref, g_ref, o_ref, lse_ref, dq_ref, dk_ref, dv_ref, dg_ref,
             dva_ref, dqa_ref):
        i = pl.program_id(1)

        @pl.when(i == 0)
        def _():
            dk_ref[...] = jnp.zeros_like(dk_ref)
            dva_ref[...] = jnp.zeros_like(dva_ref)

        q = q_ref[...]
        silu, dsilu = _silu_parts(g_ref[...])
        dm = dm_ref[...]
        o = o_ref[...]
        dg_ref[...] = (dm * o * dsilu).astype(BF16)
        dof = dm * silu
        delta = jnp.sum((dof * o).T, axis=0, keepdims=True)
        do = dof.astype(BF16)
        lse = lse_ref[0:1, :]
        masks = _band_masks(tq, tk, _mla_mask)
        dqa_ref[...] = jnp.zeros_like(dqa_ref)

        def tile(j, mask, off):
            rows = pl.ds(pl.multiple_of(j * tk, tk), tk)
            k = k_ref[rows, :]
            p = jnp.exp(_nt(k, q[off:, :]) * MLA_SCALE - lse[:, off:])
            if mask is not None:
                p = jnp.where(mask, p, 0.0)
            ds = (p * (_nt(v_ref[rows, :], do[off:, :]) - delta[:, off:]) * MLA_SCALE).astype(BF16)
            dva_ref[rows, :] += _nn(p.astype(BF16), do[off:, :])
            dk_ref[rows, :] += _nn(ds, q[off:, :])
            dqa_ref[:, off:] += _tn(k, ds)

        def step(jj, c):
            for u in range(unroll):
                tile(jj * unroll + u, None, 0)
            return c

        lax.fori_loop(0, i * nb // unroll, step, 0)
        for b in range(nb):
            tile(i * nb + b, masks[b], b * tk)
        dq_ref[...] = dqa_ref[...].T

        @pl.when(i == nq - 1)
        def _():
            dv_ref[...] = dva_ref[...].astype(BF16)

    blk = lambda w, off: pl.BlockSpec((tq, w), lambda h, i: (i, off + h))
    whole = lambda w: pl.BlockSpec((s, w), lambda h, i: (0, h))
    return pl.pallas_call(
        body, name="mla_bwd", grid=(HEADS, nq),
        in_specs=[blk(MLA_QK_PAD, 0), whole(MLA_QK_PAD), whole(HEAD_DIM), blk(HEAD_DIM, HEADS),
                  blk(HEAD_DIM, HEADS), blk(HEAD_DIM, 0), _row_spec(tq)],
        out_specs=[blk(MLA_QK_PAD, 0), whole(MLA_QK_PAD), whole(HEAD_DIM), blk(HEAD_DIM, 0)],
        out_shape=[jax.ShapeDtypeStruct((s, HEADS * MLA_QK_PAD), F32),
                   jax.ShapeDtypeStruct((s, HEADS * MLA_QK_PAD), F32),
                   jax.ShapeDtypeStruct((s, SB_WIDTH), BF16), jax.ShapeDtypeStruct((s, SB_WIDTH), BF16)],
        scratch_shapes=[pltpu.VMEM((s, HEAD_DIM), F32), pltpu.VMEM((MLA_QK_PAD, tq), F32)],
        compiler_params=_cp(("parallel", "arbitrary")),
    )(qm, km, vm, dmixed, projf, ob, lse)


def _norm_bwd(x, w, dn):
    r = lax.rsqrt(jnp.mean(x * x, axis=1, keepdims=True) + EPS)
    xhat = x * r
    g = dn * w
    return r * (g - xhat * jnp.mean(g * xhat, axis=1, keepdims=True)), dn * xhat


def _mla_bwd_post(dqm, dkm, dvm, projf, qw, kvw, wq, wk, wv, rc, rs1, rs2, tm=256):
    s = dqm.shape[0]

    def body(dq_ref, dk_ref, dv_ref, cq_ref, ckv_ref, qw_ref, kvw_ref, wq_ref, wk_ref, wv_ref,
             c_ref, s1_ref, s2_ref, dqp_ref, dkn_ref, dlat_ref, dqw_ref, dkvw_ref):
        i = pl.program_id(0)
        c, s1, s2 = c_ref[...], s1_ref[...], s2_ref[...]
        drot = jnp.zeros((tm, 128), F32)
        for h in range(HEADS):
            lo = h * MLA_QK_PAD
            dqp_ref[:, lo:lo + 128] = dq_ref[:, lo:lo + 128].astype(BF16)
            dqp_ref[:, lo + 128:lo + 256] = _rope_t(dq_ref[:, lo + 128:lo + 256], c, s1, s2).astype(BF16)
            dkn_ref[:, h * 128:(h + 1) * 128] = dk_ref[:, lo:lo + 128].astype(BF16)
            drot = drot + dk_ref[:, lo + 128:lo + 256]
        dlat_ref[:, Q_RANK + KV_RANK:Q_RANK + KV_RANK + 128] = _rope_t(drot, c, s1, s2).astype(BF16)
        dlat_ref[:, Q_RANK + KV_RANK + 128:] = jnp.zeros((tm, 128), BF16)
        dcq, dqw = _norm_bwd(cq_ref[...], qw_ref[...], _nt(dqp_ref[...], wq_ref[...]))
        dlat_ref[:, :Q_RANK] = dcq.astype(BF16)
        dnkv = _nt(dkn_ref[...], wk_ref[...]) + _nt(dv_ref[...], wv_ref[...])
        dckv, dkvw = _norm_bwd(ckv_ref[...], kvw_ref[...], dnkv)
        dlat_ref[:, Q_RANK:Q_RANK + KV_RANK] = dckv.astype(BF16)

        @pl.when(i == 0)
        def _():
            dqw_ref[...] = jnp.zeros_like(dqw_ref)
            dkvw_ref[...] = jnp.zeros_like(dkvw_ref)

        dqw_ref[...] += jnp.sum(dqw, axis=0, keepdims=True)
        dkvw_ref[...] += jnp.sum(dkvw, axis=0, keepdims=True)

    row = lambda w, b: pl.BlockSpec((tm, w), lambda i: (i, b))
    full = lambda a: pl.BlockSpec(a.shape, lambda i: (0, 0))
    sd = jax.ShapeDtypeStruct
    return pl.pallas_call(
        body, name="mla_bwd_post", grid=(s // tm,),
        in_specs=[row(HEADS * MLA_QK_PAD, 0), row(HEADS * MLA_QK_PAD, 0), row(SB_WIDTH, 0),
                  row(Q_RANK, 2048 // Q_RANK), row(KV_RANK, 2560 // KV_RANK),
                  full(qw), full(kvw), full(wq), full(wk), full(wv), row(128, 0), row(128, 0), row(128, 0)],
        out_specs=[row(HEADS * MLA_QK_PAD, 0), row(SB_WIDTH, 0), row(1024, 0),
                   pl.BlockSpec((1, Q_RANK), lambda i: (0, 0)), pl.BlockSpec((1, KV_RANK), lambda i: (0, 0))],
        out_shape=[sd((s, HEADS * MLA_QK_PAD), BF16), sd((s, SB_WIDTH), BF16), sd((s, 1024), BF16),
                   sd((1, Q_RANK), F32), sd((1, KV_RANK), F32)],
        compiler_params=_cp(("arbitrary",)),
    )(dqm, dkm, dvm, projf, projf, qw, kvw, wq, wk, wv, rc, rs1, rs2)


def _pre_bwd(pieces, win, x, pw, dout, psums, tm=256):
    s, d = x.shape
    npc, pw_ = len(pieces), pieces[0].shape[1]
    nw, nsteps = len(psums), s // tm

    def body(*refs):
        dps, (w_ref, x_ref, pw_ref, do_ref), refs = refs[:npc], refs[npc:npc + 4], refs[npc + 4:]
        ps, (dx_ref, dpw_ref), gots = refs[:nw], refs[nw:nw + 2], refs[nw + 2:2 * nw + 2]
        send_sems, recv_sems = refs[2 * nw + 2:]
        i = pl.program_id(0)

        def copies():
            if not nw:
                return []
            px, py, pc = _place()
            chips = [(1 - px, py), (px, 1 - py), (1 - px, 1 - py)]
            return [pltpu.make_async_remote_copy(
                src_ref=ps[w].at[2 * chip[0] + chip[1]], dst_ref=gots[w].at[j], send_sem=send_sems.at[3 * w + j],
                recv_sem=recv_sems.at[3 * w + j], device_id=(*chip, pc), device_id_type=MESH)
                for w in range(nw) for j, chip in enumerate(chips)]

        @pl.when(i == 0)
        def _():
            dpw_ref[...] = jnp.zeros_like(dpw_ref)
            for cp in copies():
                cp.start()

        dh = _nt(dps[0][...], w_ref[:, :pw_])
        for p in range(1, npc):
            dh = dh + _nt(dps[p][...], w_ref[:, p * pw_:(p + 1) * pw_])
        dx, dw = _norm_bwd(x_ref[...], pw_ref[...], dh)
        dx_ref[...] = do_ref[...] + dx
        dpw_ref[...] += jnp.sum(dw, axis=0, keepdims=True)

        @pl.when(i == nsteps - 1)
        def _():
            for cp in copies():
                cp.wait()

    rowd = pl.BlockSpec((tm, d), lambda i: (i, 0))
    vec = pl.BlockSpec((1, d), lambda i: (0, 0))
    out = pl.pallas_call(
        body, name="pre_bwd", grid=(nsteps,),
        in_specs=[pl.BlockSpec((tm, pw_), lambda i: (i, 0))] * npc
        + [pl.BlockSpec(win.shape, lambda i: (0, 0), pipeline_mode=pl.Buffered(1)), rowd, vec, rowd] + [ANY] * nw,
        out_specs=[rowd, vec] + [ANY] * nw,
        out_shape=[jax.ShapeDtypeStruct((s, d), F32), jax.ShapeDtypeStruct((1, d), F32)]
        + [jax.ShapeDtypeStruct((3,) + p.shape[1:], p.dtype) for p in psums],
        scratch_shapes=[pltpu.SemaphoreType.DMA((max(3 * nw, 1),)), pltpu.SemaphoreType.DMA((max(3 * nw, 1),))],
        compiler_params=_cp(("arbitrary",), has_side_effects=True),
    )(*pieces, win, x, pw, dout, *psums)
    return out[0], out[1], list(out[2:])


def _adamw(w, g, m, v):
    m = ADAM_B1 * m + (1.0 - ADAM_B1) * g
    v = ADAM_B2 * v + (1.0 - ADAM_B2) * (g * g)
    delta = -ADAM_LR * ((m / ADAM_C1) / (jnp.sqrt(v / ADAM_C2) + ADAM_EPS) + ADAM_WD * w)
    return delta, m, v


def _row_block(rows):
    return math.gcd(rows, 128)


def _sum_parts(core, own, got, name):
    n, hr, cols = got.shape
    tr = _row_block(hr)
    nblk = hr // tr

    def body(c_ref, o_ref, p_ref, g_ref):
        g = o_ref[...]
        for k in range(n):
            g = g + p_ref[k].astype(F32)
        g_ref[...] = g

    return pl.pallas_call(
        body, name=name,
        grid_spec=pltpu.PrefetchScalarGridSpec(
            num_scalar_prefetch=1, grid=(nblk,),
            in_specs=[pl.BlockSpec((tr, cols), lambda i, c: (i, 0)),
                      pl.BlockSpec((n, tr, cols), lambda i, c: (0, i, 0))],
            out_specs=pl.BlockSpec((tr, cols), lambda i, c: (c[0] * nblk + i, 0))),
        out_shape=jax.ShapeDtypeStruct((2 * hr, cols), F32),
        compiler_params=_cp(("parallel",)),
    )(core, own, got)


def _adamw_shard(g, w, m, v, name):
    rows, cols = g.shape
    tr = _row_block(rows)

    def body(g_ref, w_ref, m_ref, v_ref, go_ref, d_ref, nm_ref, nv_ref):
        g_ = g_ref[...]
        go_ref[...] = g_
        d_ref[...], nm_ref[...], nv_ref[...] = _adamw(w_ref[...], g_, m_ref[...], v_ref[...])

    blk = pl.BlockSpec((tr, cols), lambda i: (i, 0))
    sd = jax.ShapeDtypeStruct((rows, cols), F32)
    return pl.pallas_call(
        body, name=name, grid=(rows // tr,), in_specs=[blk] * 4, out_specs=[blk] * 4, out_shape=[sd] * 4,
        compiler_params=_cp(("parallel",)),
    )(g, w, m, v)


def _pair_sum(place, grads, got, name):
    n, _, hr, cols = grads.shape
    tr = _row_block(hr)

    def body(p_ref, a_ref, b_ref, oa_ref, ob_ref, all_ref, own_ref):
        all_ref[...] = (a_ref[:, 0] + b_ref[...]).astype(BF16)
        own_ref[...] = oa_ref[0, 0] + ob_ref[0]

    return pl.pallas_call(
        body, name=name,
        grid_spec=pltpu.PrefetchScalarGridSpec(
            num_scalar_prefetch=1, grid=(hr // tr,),
            in_specs=[pl.BlockSpec((n, 1, tr, cols), lambda i, p: (0, p[0], i, 0)),
                      pl.BlockSpec((n, tr, cols), lambda i, p: (0, i, 0)),
                      pl.BlockSpec((1, 1, tr, cols), lambda i, p: (p[1], p[0], i, 0)),
                      pl.BlockSpec((1, tr, cols), lambda i, p: (p[1], i, 0))],
            out_specs=[pl.BlockSpec((n, tr, cols), lambda i, p: (0, i, 0)),
                       pl.BlockSpec((tr, cols), lambda i, p: (i, 0))]),
        out_shape=[jax.ShapeDtypeStruct((n, hr, cols), BF16), jax.ShapeDtypeStruct((hr, cols), F32)],
        compiler_params=_cp(("parallel",)),
    )(place, grads, got, grads, got)


def _place():
    return lax.axis_index("x"), lax.axis_index("y"), lax.axis_index("c")


W_NAMES = ("w_in", "w_q_up", "w_kv_up", "w_out")
NW = len(W_NAMES)


def _comm_call(body, name, ins, out_shape, n_copies, aliases=None):
    return pl.pallas_call(
        body, name=name, in_specs=[ANY] * len(ins), out_specs=[ANY] * len(out_shape), out_shape=out_shape,
        input_output_aliases=aliases or {},
        scratch_shapes=[pltpu.SemaphoreType.DMA((n_copies,)), pltpu.SemaphoreType.DMA((n_copies,))],
        compiler_params=pltpu.CompilerParams(has_side_effects=True),
    )(*ins)


def _gather_weights(shards):
    def body(*refs):
        ws, outs = refs[:NW], refs[NW:2 * NW]
        send_sems, recv_sems = refs[2 * NW:]
        x, y, c = _place()
        chips = [(1 - x, y), (x, 1 - y), (1 - x, 1 - y)]

        def rows(w, chip, core):
            half = ws[w].shape[0] // 2
            return outs[w].at[2 * chip[0] + chip[1], pl.ds(core * half, half), :]

        def copy(w, k, chip, core, to, own=False):
            half = ws[w].shape[0] // 2
            return pltpu.make_async_remote_copy(
                src_ref=ws[w].at[pl.ds(core * half, half), :] if own else rows(w, chip, core),
                dst_ref=rows(w, chip, core), send_sem=send_sems.at[6 * w + k], recv_sem=recv_sems.at[6 * w + k],
                device_id=to, device_id_type=MESH)

        first = [copy(w, j, (x, y), c, (*chip, c), own=True) for w in range(NW) for j, chip in enumerate(chips)]
        for cp in first:
            cp.start()
        passed = []
        for w in range(NW):
            for j, chip in enumerate(chips):
                copy(w, j, chip, c, (x, y, c)).wait_recv()
                passed.append(copy(w, 3 + j, chip, c, (x, y, 1 - c)))
                passed[-1].start()
        for w in range(NW):
            for j, chip in enumerate(chips):
                copy(w, 3 + j, chip, 1 - c, (x, y, c)).wait_recv()
        for cp in first + passed:
            cp.wait_send()

    return _comm_call(body, "gather_weights", shards,
                      [jax.ShapeDtypeStruct((4,) + w.shape, w.dtype) for w in shards], 6 * NW)


def _swap_halves(grads):
    def body(*refs):
        gs, gots = refs[:NW], refs[NW:2 * NW]
        send_sems, recv_sems = refs[2 * NW:]
        x, y, c = _place()
        copies = [pltpu.make_async_remote_copy(
            src_ref=gs[w].at[k, 1 - c], dst_ref=gots[w].at[k], send_sem=send_sems.at[4 * w + k],
            recv_sem=recv_sems.at[4 * w + k], device_id=(x, y, 1 - c), device_id_type=MESH)
            for w in range(NW) for k in range(4)]
        for cp in copies:
            cp.start()
        for cp in copies:
            cp.wait()

    return _comm_call(body, "swap_halves", grads,
                      [jax.ShapeDtypeStruct((4,) + g.shape[2:], g.dtype) for g in grads], 4 * NW)


def _share_with_sibling(arrs):
    na = len(arrs)

    def body(*refs):
        outs = refs[na:2 * na]
        send_sems, recv_sems = refs[2 * na:]
        x, y, c = _place()

        def half(ref):
            hr = ref.shape[0] // 2
            return ref.at[pl.ds(c * hr, hr), :]

        copies = [pltpu.make_async_remote_copy(
            src_ref=half(outs[k]), dst_ref=half(outs[k]), send_sem=send_sems.at[k], recv_sem=recv_sems.at[k],
            device_id=(x, y, 1 - c), device_id_type=MESH) for k in range(na)]
        for cp in copies:
            cp.start()
        for cp in copies:
            cp.wait()

    return _comm_call(body, "share_with_sibling", arrs, [jax.ShapeDtypeStruct(a.shape, a.dtype) for a in arrs],
                      na, aliases={k: k for k in range(na)})


def _norm_allreduce_adamw(part, w, m, v):
    r, lanes = part.shape

    def body(p_ref, w_ref, m_ref, v_ref, g_ref, d_ref, nm_ref, nv_ref, all_ref, send_sems, recv_sems):
        x, y, c = _place()
        me = 4 * x + 2 * y + c
        all_ref[me] = p_ref[...]
        copies = []
        for k in range(1, 8):
            peer = (x ^ (k >> 2), y ^ ((k >> 1) & 1), c ^ (k & 1))
            copies.append(pltpu.make_async_remote_copy(
                src_ref=p_ref, dst_ref=all_ref.at[me], send_sem=send_sems.at[k - 1], recv_sem=recv_sems.at[k - 1],
                device_id=peer, device_id_type=MESH))
        for cp in copies:
            cp.start()
        for cp in copies:
            cp.wait()
        g = all_ref[0]
        for k in range(1, 8):
            g = g + all_ref[k]
        g_ref[...] = g
        d_ref[...], nm_ref[...], nv_ref[...] = _adamw(w_ref[...], g, m_ref[...], v_ref[...])

    vm = pl.BlockSpec(memory_space=pltpu.VMEM)
    sd = jax.ShapeDtypeStruct((r, lanes), F32)
    return pl.pallas_call(
        body, name="norm_allreduce_adamw", in_specs=[vm] * 4, out_specs=[vm] * 4, out_shape=[sd] * 4,
        scratch_shapes=[pltpu.VMEM((8, r, lanes), F32), pltpu.SemaphoreType.DMA((7,)),
                        pltpu.SemaphoreType.DMA((7,))],
        compiler_params=pltpu.CompilerParams(has_side_effects=True),
    )(part, w, m, v)


LANES = 128


def _perm_in(shards):
    s0, s1, s2, s3 = shards
    w = D_IN // 4
    return jnp.concatenate([s0, s1, s2[:, :4096 - 2 * w], s3[:, 4928 - 3 * w:], s2[:, 4096 - 2 * w:],
                            s3[:, :4928 - 3 * w], jnp.zeros((s0.shape[0], D_INP - D_IN), s0.dtype)], axis=1)


def _by_chip_and_half(g, axis):
    rows, cols = g.shape
    if axis == 0:
        return g.reshape(4, 2, rows // 8, cols)
    return g.reshape(rows, 4, cols // 4).transpose(1, 0, 2).reshape(4, 2, rows // 2, cols // 4)


NORM_NAMES = ("pre_norm_w", "q_norm_w", "kv_norm_w", "post_norm_w")
NORM_SIZES = (D_MODEL, Q_RANK, KV_RANK, D_MODEL)
NORM_ROWS = 40


def _pack_norm(vs):
    flat = jnp.concatenate([v.reshape(-1) for v in vs])
    return jnp.pad(flat, (0, NORM_ROWS * LANES - flat.shape[0])).reshape(NORM_ROWS, LANES)


def _unpack_norm(p):
    flat, out, at = p.reshape(-1), [], 0
    for n in NORM_SIZES:
        out.append(flat[at:at + n].reshape(1, n))
        at += n
    return out


def _rope_tables(positions):
    inv_freq = ROPE_THETA ** (-jnp.arange(0, MLA_ROPE, 2, dtype=F32) / MLA_ROPE)
    ang = positions.astype(F32)[:, None] * inv_freq
    cos, sin, z = jnp.cos(ang), jnp.sin(ang), jnp.zeros_like(ang)
    return (jnp.concatenate([cos, cos, z, z], axis=1), jnp.concatenate([z, sin, z, z], axis=1),
            jnp.concatenate([-sin, z, z, z], axis=1))


def _local_step(x, positions, pre_w, win, q_w, wq, kv_w, wk, wv, wout, post_w, target):
    s = x.shape[0]
    rc, rs1, rs2 = _rope_tables(positions)
    h = _prenorm(x, pre_w)
    projb = _matmul(h, win, mode="nn", out_dtype=BF16, tm=512, tn=1024, tk=D_MODEL, name="proj_b", n=PB_W)
    projf = _matmul(h, win, mode="nn", out_dtype=F32, tm=512, tn=1024, tk=D_MODEL, name="proj_f", n=PF_W,
                    b_off=PB_W // 1024)
    oa, a_st, b_st = _sb_fwd(projb)
    nq, nkv, qm, km, vm = _mla_prep(projf, q_w, kv_w, wq, wk, wv, rc, rs1, rs2)
    ob, lse = _mla_fwd(qm, km, vm)
    mixed, dy, dout, err2, dpost = _out_post(oa, ob, projf, wout, x, target, post_w)

    dwout = _matmul(mixed, dy, mode="tn", out_dtype=F32, tm=1024, tn=1024, tk=min(4096, s), name="dw_out")
    dmixed = _matmul(dy, wout, mode="nt", out_dtype=F32, tm=512, tn=1024, tk=D_MODEL, name="d_mixed")
    dqa, dka, dva, dga = _sb_bwd(projb, projf, dmixed, oa, a_st, b_st)
    dqm, dkm, dvm, dgb = _mla_bwd(qm, km, vm, projf, dmixed, ob, lse)
    dqp, dkn, dlat, dqw, dkvw = _mla_bwd_post(dqm, dkm, dvm, projf, q_w, kv_w, wq, wk, wv, rc, rs1, rs2)
    tks = min(4096, s)
    dwq = _matmul(nq, dqp, mode="tn", out_dtype=F32, tm=Q_RANK, tn=1024, tk=tks, name="dw_q")
    dwk = _matmul(nkv, dkn, mode="tn", out_dtype=F32, tm=KV_RANK, tn=1024, tk=tks, name="dw_k")
    dwv = _matmul(nkv, dvm, mode="tn", out_dtype=F32, tm=KV_RANK, tn=1024, tk=tks, name="dw_v")
    w = D_IN // 4
    pad = jnp.zeros((s, -w % LANES), BF16)
    lat = Q_RANK + KV_RANK + MLA_ROPE
    dpad = jnp.concatenate([dqa, dka[:, :w - 1024], pad, dka[:, w - 1024:], dva[:, :2 * w - 2048], pad,
                            dva[:, 2 * w - 2048:], dga, dlat[:, :3 * w - 4096], pad,
                            dlat[:, 3 * w - 4096:lat], dgb, pad], axis=1)
    dwin = _dw_by_shard(h, dpad, w)
    return err2, (dqa, dka, dva, dga, dgb, dlat), dout, dwin, dqw, dwq, dkvw, dwk, dwv, dwout, dpost


def _kernel_layouts(in_shards, w_q_up, w_kv_up, w_out):
    win = _perm_in(in_shards)
    wq = jnp.pad(w_q_up.reshape(Q_RANK, HEADS, 192), ((0, 0), (0, 0), (0, 64))).reshape(Q_RANK, HEADS * MLA_QK_PAD)
    kv = w_kv_up.reshape(KV_RANK, HEADS, 256)
    wk = kv[:, :, :128].reshape(KV_RANK, SB_WIDTH)
    wv = kv[:, :, 128:].reshape(KV_RANK, SB_WIDTH)
    return win, wq, wk, wv, w_out


def _original_layouts(dwq, dwk, dwv):
    dq = dwq.reshape(Q_RANK, HEADS, MLA_QK_PAD)[:, :, :192].reshape(Q_RANK, HEADS * 192)
    dkv = jnp.concatenate([dwk.reshape(KV_RANK, HEADS, 128), dwv.reshape(KV_RANK, HEADS, 128)], axis=2)
    return dq, dkv.reshape(KV_RANK, 2 * SB_WIDTH)


def kernel(x, positions, pre_norm_w, w_in, q_norm_w, w_q_up, kv_norm_w, w_kv_up, w_out, post_norm_w, loss_target, m_pre_norm_w, m_w_in, m_q_norm_w, m_w_q_up, m_kv_norm_w, m_w_kv_up, m_w_out, m_post_norm_w, v_pre_norm_w, v_w_in, v_q_norm_w, v_w_q_up, v_kv_norm_w, v_w_kv_up, v_w_out, v_post_norm_w):
    c = lax.axis_index("c")
    chip = 2 * lax.axis_index("x") + lax.axis_index("y")
    shards = (w_in[0], w_q_up[0], w_kv_up[0], w_out[0])
    mine16 = [w.astype(BF16) for w in shards]
    others = _gather_weights(mine16)
    slot = lambda w, k: jnp.where(chip == k, mine16[w], others[w][k])
    cat = lambda w, ax: jnp.concatenate([slot(w, k) for k in range(4)], axis=ax)
    win, wq, wk, wv, wout = _kernel_layouts([slot(0, k) for k in range(4)], cat(1, 1), cat(2, 1), cat(3, 0))

    err2, pieces, dout, dwin, dqw, dwq, dkvw, dwk, dwv, dwout, dpost = _local_step(
        x[0], positions[0], pre_norm_w, win, q_norm_w, wq, kv_norm_w, wk, wv, wout, post_norm_w, loss_target[0])
    loss = lax.psum(0.5 * jnp.sum(err2) / D_MODEL, ("x", "y", "c"))

    dq, dkv = _original_layouts(dwq, dwk, dwv)
    grads = [dwin.reshape(4, 2, D_MODEL // 2, D_IN // 4)] + [
        _by_chip_and_half(g, ax) for g, ax in ((dq, 1), (dkv, 1), (dwout, 0))]
    place = jnp.stack([c, chip])
    halves = _swap_halves(grads)
    pairs = [_pair_sum(place, grads[w], halves[w], "pair_sum_" + W_NAMES[w]) for w in range(NW)]
    gx, dpre, gots = _pre_bwd(pieces, win, x[0], pre_norm_w, dout, [p[0] for p in pairs])
    sums = _share_with_sibling([_sum_parts(c.reshape(1), pairs[w][1], gots[w], "sum_parts_" + W_NAMES[w])
                                for w in range(NW)])
    ms = (m_w_in[0], m_w_q_up[0], m_w_kv_up[0], m_w_out[0])
    vs = (v_w_in[0], v_w_q_up[0], v_w_kv_up[0], v_w_out[0])
    done = [_adamw_shard(sums[w], shards[w], ms[w], vs[w], "adamw_" + W_NAMES[w]) for w in range(NW)]
    big = [[done[w][k] for w in range(NW)] for k in range(4)]

    small = _norm_allreduce_adamw(
        _pack_norm([dpre, dqw, dkvw, dpost]), _pack_norm([pre_norm_w, q_norm_w, kv_norm_w, post_norm_w]),
        _pack_norm([m_pre_norm_w, m_q_norm_w, m_kv_norm_w, m_post_norm_w]),
        _pack_norm([v_pre_norm_w, v_q_norm_w, v_kv_norm_w, v_post_norm_w]))
    small = [_unpack_norm(p) for p in small]

    def group(k):
        n, b = small[k], big[k]
        return (n[0], b[0][None], n[1], b[1][None], n[2], b[2][None], b[3][None], n[3])

    return (loss, gx[None], *group(0), *group(1), *group(2), *group(3))
```

```python
import functools
import math

import numpy as np
import jax
import jax.numpy as jnp
from jax import lax
from jax.experimental import pallas as pl
from jax.experimental.pallas import tpu as pltpu

F32 = jnp.float32
BF16 = jnp.bfloat16
MESH = pl.DeviceIdType.MESH

D_MODEL = 2048
HEADS = 8
HEAD_DIM = 128
SB_WIDTH = HEADS * HEAD_DIM
MLA_ROPE = 64
MLA_QK_PAD = 256
Q_RANK = 512
KV_RANK = 256
CHUNK = 64
EPS = 1e-6
ROPE_THETA = 10000.0
D_IN = 5952
D_INP = 6144
PB_W = 3072
PF_W = D_INP - PB_W
SB_SCALE = 1.0 / math.sqrt(HEAD_DIM)
MLA_SCALE = 1.0 / math.sqrt(HEAD_DIM + MLA_ROPE)
NEG = -1e30

ADAM_LR, ADAM_B1, ADAM_B2, ADAM_EPS, ADAM_WD, ADAM_STEP = 0.001, 0.9, 0.999, 1e-08, 0.01, 10
ADAM_C1 = 1.0 - ADAM_B1 ** ADAM_STEP
ADAM_C2 = 1.0 - ADAM_B2 ** ADAM_STEP

ANY = pl.BlockSpec(memory_space=pl.ANY)
VMEM_LIMIT = 56 * 1024 * 1024
TQ = 1024
TK = 256
MLA_TK = 512
MLA_FWD_TK = 1024
UNROLL = 2


def _cp(sem=None, **kw):
    return pltpu.CompilerParams(dimension_semantics=sem, vmem_limit_bytes=VMEM_LIMIT, **kw)


def _dot(a, b, dims):
    return lax.dot_general(a, b, (dims, ((), ())), preferred_element_type=F32)


def _nn(a, b):
    return _dot(a, b, ((1,), (0,)))


def _nt(a, b):
    return _dot(a, b, ((1,), (1,)))


def _tn(a, b):
    return _dot(a, b, ((0,), (0,)))


def _rope(x, c, s1, s2):
    return x * c + pltpu.roll(x, 32, 1) * s1 + pltpu.roll(x, 96, 1) * s2


def _rope_t(x, c, s1, s2):
    return x * c - pltpu.roll(x, 32, 1) * s1 - pltpu.roll(x, 96, 1) * s2


def _silu_parts(g):
    sg = jax.nn.sigmoid(g)
    return g * sg, sg * (1.0 + g * (1.0 - sg))


def _matmul(a, b, *, mode, out_dtype, tm, tn, tk, name, n=None, b_off=0):
    if mode == "tn":
        kk, m = a.shape
        n = b.shape[1] if n is None else n
    else:
        m, kk = a.shape
        n = (b.shape[1] if mode == "nn" else b.shape[0]) if n is None else n
    nk = kk // tk
    a_spec = {"nn": pl.BlockSpec((tm, tk), lambda j, i, k: (i, k)),
              "nt": pl.BlockSpec((tm, tk), lambda j, i, k: (i, k)),
              "tn": pl.BlockSpec((tk, tm), lambda j, i, k: (k, i))}[mode]
    b_spec = {"nn": pl.BlockSpec((tk, tn), lambda j, i, k: (k, j + b_off)),
              "nt": pl.BlockSpec((tn, tk), lambda j, i, k: (j, k)),
              "tn": pl.BlockSpec((tk, tn), lambda j, i, k: (k, j))}[mode]
    dims = {"nn": ((1,), (0,)), "nt": ((1,), (1,)), "tn": ((0,), (0,))}[mode]

    def body(a_ref, b_ref, o_ref, acc_ref):
        k = pl.program_id(2)
        part = _dot(a_ref[...], b_ref[...], dims)
        if nk == 1:
            o_ref[...] = part.astype(out_dtype)
        else:
            @pl.when(k == 0)
            def _():
                acc_ref[...] = part

            @pl.when(k > 0)
            def _():
                acc_ref[...] += part

            @pl.when(k == nk - 1)
            def _():
                o_ref[...] = acc_ref[...].astype(out_dtype)

    return pl.pallas_call(
        body, name=name, grid=(n // tn, m // tm, nk),
        in_specs=[a_spec, b_spec], out_specs=pl.BlockSpec((tm, tn), lambda j, i, k: (i, j)),
        out_shape=jax.ShapeDtypeStruct((m, n), out_dtype),
        scratch_shapes=[pltpu.VMEM((tm, tn) if nk > 1 else (8, 128), F32)],
        compiler_params=_cp(("parallel", "parallel", "arbitrary")),
    )(a, b)


def _dw_by_shard(h, dp4, tm=1024, tk=2048):
    s, d = h.shape
    n, _, cols = dp4.shape
    tk = min(tk, s)
    nk = s // tk

    def body(a_ref, b_ref, o_ref, acc_ref):
        k = pl.program_id(2)
        part = _tn(a_ref[...], b_ref[...])

        @pl.when(k == 0)
        def _():
            acc_ref[...] = part

        @pl.when(k > 0)
        def _():
            acc_ref[...] += part

        @pl.when(k == nk - 1)
        def _():
            o_ref[...] = acc_ref[...]

    return pl.pallas_call(
        body, name="dw_in", grid=(n, d // tm, nk),
        in_specs=[pl.BlockSpec((tk, tm), lambda j, i, k: (k, i)),
                  pl.BlockSpec((None, tk, cols), lambda j, i, k: (j, k, 0))],
        out_specs=pl.BlockSpec((None, tm, cols), lambda j, i, k: (j, i, 0)),
        out_shape=jax.ShapeDtypeStruct((n, d, cols), F32),
        scratch_shapes=[pltpu.VMEM((tm, cols), F32)],
        compiler_params=_cp(("parallel", "parallel", "arbitrary")),
    )(h, dp4)


def _prenorm(x, w, tm=512):
    s, d = x.shape

    def body(x_ref, w_ref, h_ref):
        xv = x_ref[...]
        r = lax.rsqrt(jnp.mean(xv * xv, axis=1, keepdims=True) + EPS)
        h_ref[...] = ((xv * r) * w_ref[...]).astype(BF16)

    return pl.pallas_call(
        body, name="prenorm", grid=(s // tm,),
        in_specs=[pl.BlockSpec((tm, d), lambda i: (i, 0)), pl.BlockSpec((1, d), lambda i: (0, 0))],
        out_specs=pl.BlockSpec((tm, d), lambda i: (i, 0)),
        out_shape=jax.ShapeDtypeStruct((s, d), BF16),
        compiler_params=_cp(("parallel",)),
    )(x, w)


def _tri(n, cmp, value):
    row = lax.broadcasted_iota(jnp.int32, (n, n), 0)
    col = lax.broadcasted_iota(jnp.int32, (n, n), 1)
    return jnp.where(cmp(row, col), value, 0.0).astype(BF16)


def _softplus(z, mask):
    sp = jnp.maximum(z, 0.0) + jnp.log(1.0 + jnp.exp(-jnp.abs(z)))
    return sp if mask is None else jnp.where(mask, sp, 0.0)


def _tiles(s, tk=TK):
    tq = min(TQ, s)
    return tq, tk, math.gcd(tq // tk, UNROLL)


def _band_masks(tq, tk, fn):
    out = []
    for b in range(tq // tk):
        key = lax.broadcasted_iota(jnp.int32, (tk, tq - b * tk), 0) + b * tk
        qry = lax.broadcasted_iota(jnp.int32, (tk, tq - b * tk), 1) + b * tk
        out.append(fn(qry, key))
    return out


def _row_spec(tq):
    return pl.BlockSpec((8, tq), lambda h, i: (h, i))


def _sb_store_shape(s, tq, tk):
    nb, nq = tq // tk, s // tq
    return (HEADS, nb * nq * (nq + 1) // 2, tk, tq)


def _sb_fwd(projb):
    s = projb.shape[0]
    tq, tk, _ = _tiles(s)
    nb = tq // tk

    def body(q_ref, k_ref, v_ref, o_ref, a_st, b_st, acc_ref, car_ref, z_ref, a_buf, b_buf, sems):
        h, i = pl.program_id(0), pl.program_id(1)
        q = q_ref[...]
        from_here = _tri(tk, lambda s_, j: j >= s_, -1.0)
        masks = _band_masks(tq, tk, lambda t, s_: s_ < t)
        acc_ref[...] = jnp.zeros_like(acc_ref)
        car_ref[...] = jnp.zeros_like(car_ref)
        first = nb * (i * (i + 1) // 2)

        def stores(st, g):
            at = pl.ds(first + g * nb, nb)
            return (pltpu.make_async_copy(a_buf.at[st], a_st.at[h, at], sems.at[st, 0]),
                    pltpu.make_async_copy(b_buf.at[st], b_st.at[h, at], sems.at[st, 1]))

        def tile(j, mask, off, st, u):
            rows = pl.ds(pl.multiple_of(j * tk, tk), tk)
            z_ref[:, off:] = _nt(k_ref[rows, :], q[off:, :]) * SB_SCALE
            z = z_ref[:, off:]
            sp = _softplus(z, mask)
            a = jnp.exp(z + _nn(from_here, sp.astype(BF16)) + car_ref[:, off:])
            beta = jnp.exp(z - sp)
            if mask is not None:
                a = jnp.where(mask, a, 0.0)
                beta = jnp.where(mask, beta, 0.0)
            a = a.astype(BF16)
            if off:
                a_buf[st, u, :, :off] = jnp.zeros((tk, off), BF16)
                b_buf[st, u, :, :off] = jnp.zeros((tk, off), BF16)
            a_buf[st, u, :, off:] = a
            b_buf[st, u, :, off:] = beta.astype(BF16)
            acc_ref[:, off:] += _tn(v_ref[rows, :], a)
            car_ref[:, off:] -= jnp.sum(sp, axis=0, keepdims=True)

        for b in reversed(range(nb)):
            tile(i * nb + b, masks[b], b * tk, 0, b)
        for cp in stores(0, i):
            cp.start()

        def step(jj, c):
            st, g = (jj + 1) % 2, i - 1 - jj

            @pl.when(jj >= 1)
            def _():
                for cp in stores(st, g):
                    cp.wait()

            for u in reversed(range(nb)):
                tile(g * nb + u, None, 0, st, u)
            for cp in stores(st, g):
                cp.start()
            return c

        lax.fori_loop(0, i, step, 0)
        for cp in stores(0, 0):
            cp.wait()

        @pl.when(i >= 1)
        def _():
            for cp in stores(1, 0):
                cp.wait()

        o_ref[...] = acc_ref[...].T

    blk = pl.BlockSpec((tq, HEAD_DIM), lambda h, i: (i, h))
    st = jax.ShapeDtypeStruct(_sb_store_shape(s, tq, tk), BF16)
    return pl.pallas_call(
        body, name="sb_fwd", grid=(HEADS, s // tq),
        in_specs=[blk, pl.BlockSpec((s, HEAD_DIM), lambda h, i: (0, HEADS + h)),
                  pl.BlockSpec((s, HEAD_DIM), lambda h, i: (0, 2 * HEADS + h))],
        out_specs=[blk, ANY, ANY],
        out_shape=[jax.ShapeDtypeStruct((s, SB_WIDTH), F32), st, st],
        scratch_shapes=[pltpu.VMEM((HEAD_DIM, tq), F32), pltpu.VMEM((1, tq), F32), pltpu.VMEM((tk, tq), F32),
                        pltpu.VMEM((2, nb, tk, tq), BF16), pltpu.VMEM((2, nb, tk, tq), BF16),
                        pltpu.SemaphoreType.DMA((2, 2))],
        compiler_params=_cp(("arbitrary", "arbitrary")),
    )(projb, projb, projb)


def _mla_prep(projf, qw, kvw, wq, wk, wv, rc, rs1, rs2, tm=512):
    s = projf.shape[0]

    def body(cq_ref, ckv_ref, kr_ref, qw_ref, kvw_ref, wq_ref, wk_ref, wv_ref, c_ref, s1_ref, s2_ref,
             nq_ref, nkv_ref, q_ref, k_ref, v_ref):
        c, s1, s2 = c_ref[...], s1_ref[...], s2_ref[...]
        cq = cq_ref[...]
        nq = ((cq * lax.rsqrt(jnp.mean(cq * cq, axis=1, keepdims=True) + EPS)) * qw_ref[...]).astype(BF16)
        nq_ref[...] = nq
        qf = _nn(nq, wq_ref[...])
        ckv = ckv_ref[...]
        nkv = ((ckv * lax.rsqrt(jnp.mean(ckv * ckv, axis=1, keepdims=True) + EPS)) * kvw_ref[...]).astype(BF16)
        nkv_ref[...] = nkv
        kn = _nn(nkv, wk_ref[...])
        v_ref[...] = _nn(nkv, wv_ref[...]).astype(BF16)
        krot = _rope(kr_ref[...], c, s1, s2).astype(BF16)
        for h in range(HEADS):
            lo = h * MLA_QK_PAD
            q_ref[:, lo:lo + 128] = qf[:, lo:lo + 128].astype(BF16)
            q_ref[:, lo + 128:lo + 256] = _rope(qf[:, lo + 128:lo + 256], c, s1, s2).astype(BF16)
            k_ref[:, lo:lo + 128] = kn[:, h * 128:(h + 1) * 128].astype(BF16)
            k_ref[:, lo + 128:lo + 256] = krot

    row = lambda w, b: pl.BlockSpec((tm, w), lambda i: (i, b))
    full = lambda a: pl.BlockSpec(a.shape, lambda i: (0, 0))
    return pl.pallas_call(
        body, name="mla_prep", grid=(s // tm,),
        in_specs=[row(Q_RANK, 2048 // Q_RANK), row(KV_RANK, 2560 // KV_RANK), row(128, 2816 // 128),
                  full(qw), full(kvw), full(wq), full(wk), full(wv), row(128, 0), row(128, 0), row(128, 0)],
        out_specs=[row(Q_RANK, 0), row(KV_RANK, 0), row(HEADS * MLA_QK_PAD, 0), row(HEADS * MLA_QK_PAD, 0),
                   row(SB_WIDTH, 0)],
        out_shape=[jax.ShapeDtypeStruct((s, Q_RANK), BF16), jax.ShapeDtypeStruct((s, KV_RANK), BF16),
                   jax.ShapeDtypeStruct((s, HEADS * MLA_QK_PAD), BF16),
                   jax.ShapeDtypeStruct((s, HEADS * MLA_QK_PAD), BF16),
                   jax.ShapeDtypeStruct((s, SB_WIDTH), BF16)],
        compiler_params=_cp(("parallel",)),
    )(projf, projf, projf, qw, kvw, wq, wk, wv, rc, rs1, rs2)


def _mla_mask(qry, key):
    return (key // CHUNK) <= (qry // CHUNK)


def _mla_fwd(qm, km, vm):
    s = qm.shape[0]
    tq, tk, unroll = _tiles(s, min(MLA_FWD_TK, s))
    nb = tq // tk

    def body(q_ref, k_ref, v_ref, o_ref, lse_ref, acc_ref, m_ref, l_ref):
        i = pl.program_id(1)
        q = q_ref[...]
        masks = _band_masks(tq, tk, _mla_mask)
        acc_ref[...] = jnp.zeros_like(acc_ref)
        m_ref[...] = jnp.full_like(m_ref, NEG)
        l_ref[...] = jnp.zeros_like(l_ref)

        def tile(j, mask, off):
            rows = pl.ds(pl.multiple_of(j * tk, tk), tk)
            sc = _nt(k_ref[rows, :], q[off:, :]) * MLA_SCALE
            if mask is not None:
                sc = jnp.where(mask, sc, NEG)
            m_old = m_ref[:, off:]
            m_new = jnp.maximum(m_old, jnp.max(sc, axis=0, keepdims=True))
            p = jnp.exp(sc - m_new)
            alpha = jnp.exp(m_old - m_new)
            l_ref[:, off:] = alpha * l_ref[:, off:] + jnp.sum(p, axis=0, keepdims=True)
            acc_ref[:, off:] = alpha * acc_ref[:, off:] + _tn(v_ref[rows, :], p.astype(BF16))
            m_ref[:, off:] = m_new

        for b in range(nb):
            tile(i * nb + b, masks[b], b * tk)

        def step(jj, c):
            for u in range(unroll):
                tile(jj * unroll + u, None, 0)
            return c

        lax.fori_loop(0, i * nb // unroll, step, 0)
        o_ref[...] = (acc_ref[...] / l_ref[...]).T
        lse_ref[...] = jnp.broadcast_to(m_ref[...] + jnp.log(l_ref[...]), (8, tq))

    return pl.pallas_call(
        body, name="mla_fwd", grid=(HEADS, s // tq),
        in_specs=[pl.BlockSpec((tq, MLA_QK_PAD), lambda h, i: (i, h)),
                  pl.BlockSpec((s, MLA_QK_PAD), lambda h, i: (0, h)),
                  pl.BlockSpec((s, HEAD_DIM), lambda h, i: (0, h))],
        out_specs=[pl.BlockSpec((tq, HEAD_DIM), lambda h, i: (i, h)), _row_spec(tq)],
        out_shape=[jax.ShapeDtypeStruct((s, SB_WIDTH), F32), jax.ShapeDtypeStruct((8 * HEADS, s), F32)],
        scratch_shapes=[pltpu.VMEM((HEAD_DIM, tq), F32), pltpu.VMEM((1, tq), F32), pltpu.VMEM((1, tq), F32)],
        compiler_params=_cp(("parallel", "arbitrary")),
    )(qm, km, vm)


def _out_post(oa, ob, projf, wout, x, target, pw, tm=256):
    s, d = x.shape

    def body(oa_ref, ob_ref, ga_ref, gb_ref, w_ref, x_ref, t_ref, pw_ref,
             mix_ref, dy_ref, dout_ref, loss_ref, dpw_ref):
        i = pl.program_id(0)
        sa, _ = _silu_parts(ga_ref[...])
        sb, _ = _silu_parts(gb_ref[...])
        mix_ref[:, :SB_WIDTH] = (oa_ref[...] * sa).astype(BF16)
        mix_ref[:, SB_WIDTH:] = (ob_ref[...] * sb).astype(BF16)
        y = _nn(mix_ref[...], w_ref[...])
        r = lax.rsqrt(jnp.mean(y * y, axis=1, keepdims=True) + EPS)
        yhat = y * r
        pwv = pw_ref[...]
        err = (x_ref[...] + yhat * pwv) - t_ref[...]
        dout = err * (1.0 / d)
        dout_ref[...] = dout
        g = dout * pwv
        dy_ref[...] = (r * (g - yhat * jnp.mean(g * yhat, axis=1, keepdims=True))).astype(BF16)

        @pl.when(i == 0)
        def _():
            loss_ref[...] = jnp.zeros_like(loss_ref)
            dpw_ref[...] = jnp.zeros_like(dpw_ref)

        loss_ref[...] += jnp.sum(err * err, axis=0, keepdims=True)
        dpw_ref[...] += jnp.sum(dout * yhat, axis=0, keepdims=True)

    row = lambda w, b: pl.BlockSpec((tm, w), lambda i: (i, b))
    vec = pl.BlockSpec((1, d), lambda i: (0, 0))
    return pl.pallas_call(
        body, name="out_post", grid=(s // tm,),
        in_specs=[row(SB_WIDTH, 0), row(SB_WIDTH, 0), row(SB_WIDTH, 0), row(SB_WIDTH, 1),
                  pl.BlockSpec(wout.shape, lambda i: (0, 0)), row(d, 0), row(d, 0), vec],
        out_specs=[row(d, 0), row(d, 0), row(d, 0), vec, vec],
        out_shape=[jax.ShapeDtypeStruct((s, d), BF16), jax.ShapeDtypeStruct((s, d), BF16),
                   jax.ShapeDtypeStruct((s, d), F32), jax.ShapeDtypeStruct((1, d), F32),
                   jax.ShapeDtypeStruct((1, d), F32)],
        compiler_params=_cp(("arbitrary",)),
    )(oa, ob, projf, projf, wout, x, target, pw)


def _sb_bwd(projb, projf, dmixed, oa, a_st, b_st):
    s = projb.shape[0]
    tq, tk, _ = _tiles(s)
    nb, nq = tq // tk, s // tq

    def body(q_ref, k_ref, v_ref, dm_ref, g_ref, o_ref, a_st, b_st, dq_ref, dk_ref, dv_ref, dg_ref,
             dka_ref, dva_ref, dqa_ref, cg_ref, a_buf, b_buf, sems):
        h, i = pl.program_id(0), pl.program_id(1)

        @pl.when(i == 0)
        def _():
            dka_ref[...] = jnp.zeros_like(dka_ref)
            dva_ref[...] = jnp.zeros_like(dva_ref)

        q = q_ref[...]
        silu, dsilu = _silu_parts(g_ref[...])
        dm = dm_ref[...]
        dg_ref[...] = (dm * o_ref[...] * dsilu).astype(BF16)
        do = (dm * silu).astype(BF16)
        up_to_here = _tri(tk, lambda s_, j: j <= s_, 1.0)
        dqa_ref[...] = jnp.zeros_like(dqa_ref)
        cg_ref[...] = jnp.zeros_like(cg_ref)
        first = nb * (i * (i + 1) // 2)

        def loads(st, g):
            at = pl.ds(first + g * nb, nb)
            return (pltpu.make_async_copy(a_st.at[h, at], a_buf.at[st], sems.at[st, 0]),
                    pltpu.make_async_copy(b_st.at[h, at], b_buf.at[st], sems.at[st, 1]))

        def tile(j, off, st, u):
            rows = pl.ds(pl.multiple_of(j * tk, tk), tk)
            k = k_ref[rows, :]
            a = a_buf[st, u, :, off:]
            g = a.astype(F32) * _nt(v_ref[rows, :], do[off:, :])
            dva_ref[rows, :] += _nn(a, do[off:, :])
            cum = _nn(up_to_here, g.astype(BF16)) + cg_ref[:, off:]
            dz = ((g - b_buf[st, u, :, off:].astype(F32) * cum) * SB_SCALE).astype(BF16)
            dqa_ref[:, off:] += _tn(k, dz)
            dka_ref[rows, :] += _nn(dz, q[off:, :])
            cg_ref[:, off:] += jnp.sum(g, axis=0, keepdims=True)

        for cp in loads(0, 0):
            cp.start()

        def step(g, c):
            st = g % 2
            for cp in loads(1 - st, g + 1):
                cp.start()
            for cp in loads(st, g):
                cp.wait()
            for u in range(nb):
                tile(g * nb + u, 0, st, u)
            return c

        lax.fori_loop(0, i, step, 0)
        for cp in loads(i % 2, i):
            cp.wait()
        for b in range(nb):
            tile(i * nb + b, b * tk, i % 2, b)
        dq_ref[...] = dqa_ref[...].T.astype(BF16)

        @pl.when(i == nq - 1)
        def _():
            dk_ref[...] = dka_ref[...].astype(BF16)
            dv_ref[...] = dva_ref[...].astype(BF16)

    blk = lambda off: pl.BlockSpec((tq, HEAD_DIM), lambda h, i: (i, off + h))
    whole = lambda off: pl.BlockSpec((s, HEAD_DIM), lambda h, i: (0, off + h))
    o_sd = jax.ShapeDtypeStruct((s, SB_WIDTH), BF16)
    return pl.pallas_call(
        body, name="sb_bwd", grid=(HEADS, nq),
        in_specs=[blk(0), whole(HEADS), whole(2 * HEADS), blk(0), blk(0), blk(0), ANY, ANY],
        out_specs=[blk(0), whole(0), whole(0), blk(0)],
        out_shape=[o_sd, o_sd, o_sd, o_sd],
        scratch_shapes=[pltpu.VMEM((s, HEAD_DIM), F32), pltpu.VMEM((s, HEAD_DIM), F32),
                        pltpu.VMEM((HEAD_DIM, tq), F32), pltpu.VMEM((1, tq), F32),
                        pltpu.VMEM((2, nb, tk, tq), BF16), pltpu.VMEM((2, nb, tk, tq), BF16),
                        pltpu.SemaphoreType.DMA((2, 2))],
        compiler_params=_cp(("arbitrary", "arbitrary")),
    )(projb, projb, projb, dmixed, projf, oa, a_st, b_st)


def _mla_bwd(qm, km, vm, projf, dmixed, ob, lse):
    s = qm.shape[0]
    tq, tk, unroll = _tiles(s, MLA_TK)
    nb, nq = tq // tk, s // tq

    def body(q_ref, k_ref, v_ref, dm_ref, g_ref, o_ref, lse_ref, dq_ref, dk_ref, dv_ref, dg_ref,
             dva_ref, dqa_ref):
        i = pl.program_id(1)

        @pl.when(i == 0)
        def _():
            dk_ref[...] = jnp.zeros_like(dk_ref)
            dva_ref[...] = jnp.zeros_like(dva_ref)

        q = q_ref[...]
        silu, dsilu = _silu_parts(g_ref[...])
        dm = dm_ref[...]
        o = o_ref[...]
        dg_ref[...] = (dm * o * dsilu).astype(BF16)
        dof = dm * silu
        delta = jnp.sum((dof * o).T, axis=0, keepdims=True)
        do = dof.astype(BF16)
        lse = lse_ref[0:1, :]
        masks = _band_masks(tq, tk, _mla_mask)
        dqa_ref[...] = jnp.zeros_like(dqa_ref)

        def tile(j, mask, off):
            rows = pl.ds(pl.multiple_of(j * tk, tk), tk)
            k = k_ref[rows, :]
            p = jnp.exp(_nt(k, q[off:, :]) * MLA_SCALE - lse[:, off:])
            if mask is not None:
                p = jnp.where(mask, p, 0.0)
            ds = (p * (_nt(v_ref[rows, :], do[off:, :]) - delta[:, off:]) * MLA_SCALE).astype(BF16)
            dva_ref[rows, :] += _nn(p.astype(BF16), do[off:, :])
            dk_ref[rows, :] += _nn(ds, q[off:, :])
            dqa_ref[:, off:] += _tn(k, ds)

        def step(jj, c):
            for u in range(unroll):
                tile(jj * unroll + u, None, 0)
            return c

        lax.fori_loop(0, i * nb // unroll, step, 0)
        for b in range(nb):
            tile(i * nb + b, masks[b], b * tk)
        dq_ref[...] = dqa_ref[...].T

        @pl.when(i == nq - 1)
        def _():
            dv_ref[...] = dva_ref[...].astype(BF16)

    blk = lambda w, off: pl.BlockSpec((tq, w), lambda h, i: (i, off + h))
    whole = lambda w: pl.BlockSpec((s, w), lambda h, i: (0, h))
    return pl.pallas_call(
        body, name="mla_bwd", grid=(HEADS, nq),
        in_specs=[blk(MLA_QK_PAD, 0), whole(MLA_QK_PAD), whole(HEAD_DIM), blk(HEAD_DIM, HEADS),
                  blk(HEAD_DIM, HEADS), blk(HEAD_DIM, 0), _row_spec(tq)],
        out_specs=[blk(MLA_QK_PAD, 0), whole(MLA_QK_PAD), whole(HEAD_DIM), blk(HEAD_DIM, 0)],
        out_shape=[jax.ShapeDtypeStruct((s, HEADS * MLA_QK_PAD), F32),
                   jax.ShapeDtypeStruct((s, HEADS * MLA_QK_PAD), F32),
                   jax.ShapeDtypeStruct((s, SB_WIDTH), BF16), jax.ShapeDtypeStruct((s, SB_WIDTH), BF16)],
        scratch_shapes=[pltpu.VMEM((s, HEAD_DIM), F32), pltpu.VMEM((MLA_QK_PAD, tq), F32)],
        compiler_params=_cp(("parallel", "arbitrary")),
    )(qm, km, vm, dmixed, projf, ob, lse)


def _norm_bwd(x, w, dn):
    r = lax.rsqrt(jnp.mean(x * x, axis=1, keepdims=True) + EPS)
    xhat = x * r
    g = dn * w
    return r * (g - xhat * jnp.mean(g * xhat, axis=1, keepdims=True)), dn * xhat


def _mla_bwd_post(dqm, dkm, dvm, projf, qw, kvw, wq, wk, wv, rc, rs1, rs2, tm=256):
    s = dqm.shape[0]

    def body(dq_ref, dk_ref, dv_ref, cq_ref, ckv_ref, qw_ref, kvw_ref, wq_ref, wk_ref, wv_ref,
             c_ref, s1_ref, s2_ref, dqp_ref, dkn_ref, dlat_ref, dqw_ref, dkvw_ref):
        i = pl.program_id(0)
        c, s1, s2 = c_ref[...], s1_ref[...], s2_ref[...]
        drot = jnp.zeros((tm, 128), F32)
        for h in range(HEADS):
            lo = h * MLA_QK_PAD
            dqp_ref[:, lo:lo + 128] = dq_ref[:, lo:lo + 128].astype(BF16)
            dqp_ref[:, lo + 128:lo + 256] = _rope_t(dq_ref[:, lo + 128:lo + 256], c, s1, s2).astype(BF16)
            dkn_ref[:, h * 128:(h + 1) * 128] = dk_ref[:, lo:lo + 128].astype(BF16)
            drot = drot + dk_ref[:, lo + 128:lo + 256]
        dlat_ref[:, Q_RANK + KV_RANK:Q_RANK + KV_RANK + 128] = _rope_t(drot, c, s1, s2).astype(BF16)
        dlat_ref[:, Q_RANK + KV_RANK + 128:] = jnp.zeros((tm, 128), BF16)
        dcq, dqw = _norm_bwd(cq_ref[...], qw_ref[...], _nt(dqp_ref[...], wq_ref[...]))
        dlat_ref[:, :Q_RANK] = dcq.astype(BF16)
        dnkv = _nt(dkn_ref[...], wk_ref[...]) + _nt(dv_ref[...], wv_ref[...])
        dckv, dkvw = _norm_bwd(ckv_ref[...], kvw_ref[...], dnkv)
        dlat_ref[:, Q_RANK:Q_RANK + KV_RANK] = dckv.astype(BF16)

        @pl.when(i == 0)
        def _():
            dqw_ref[...] = jnp.zeros_like(dqw_ref)
            dkvw_ref[...] = jnp.zeros_like(dkvw_ref)

        dqw_ref[...] += jnp.sum(dqw, axis=0, keepdims=True)
        dkvw_ref[...] += jnp.sum(dkvw, axis=0, keepdims=True)

    row = lambda w, b: pl.BlockSpec((tm, w), lambda i: (i, b))
    full = lambda a: pl.BlockSpec(a.shape, lambda i: (0, 0))
    sd = jax.ShapeDtypeStruct
    return pl.pallas_call(
        body, name="mla_bwd_post", grid=(s // tm,),
        in_specs=[row(HEADS * MLA_QK_PAD, 0), row(HEADS * MLA_QK_PAD, 0), row(SB_WIDTH, 0),
                  row(Q_RANK, 2048 // Q_RANK), row(KV_RANK, 2560 // KV_RANK),
                  full(qw), full(kvw), full(wq), full(wk), full(wv), row(128, 0), row(128, 0), row(128, 0)],
        out_specs=[row(HEADS * MLA_QK_PAD, 0), row(SB_WIDTH, 0), row(1024, 0),
                   pl.BlockSpec((1, Q_RANK), lambda i: (0, 0)), pl.BlockSpec((1, KV_RANK), lambda i: (0, 0))],
        out_shape=[sd((s, HEADS * MLA_QK_PAD), BF16), sd((s, SB_WIDTH), BF16), sd((s, 1024), BF16),
                   sd((1, Q_RANK), F32), sd((1, KV_RANK), F32)],
        compiler_params=_cp(("arbitrary",)),
    )(dqm, dkm, dvm, projf, projf, qw, kvw, wq, wk, wv, rc, rs1, rs2)


def _pre_bwd(pieces, win, x, pw, dout, psums, blocks, gx=None, tm=256):
    s, d = x.shape
    npc, pw_ = len(pieces), pieces[0].shape[1]
    lo, nsteps = blocks
    nw, ng = len(psums), 0 if gx is None else 1

    def body(*refs):
        dps, (w_ref, x_ref, pw_ref, do_ref), refs = refs[:npc], refs[npc:npc + 4], refs[npc + 4 + ng:]
        ps, (dx_ref, dpw_ref), gots = refs[:nw], refs[nw:nw + 2], refs[nw + 2:2 * nw + 2]
        send_sems, recv_sems = refs[2 * nw + 2:]
        i = pl.program_id(0)

        def copies():
            if not nw:
                return []
            px, py, pc = _place()
            chips = [(1 - px, py), (px, 1 - py), (1 - px, 1 - py)]
            return [pltpu.make_async_remote_copy(
                src_ref=ps[w].at[2 * chip[0] + chip[1]], dst_ref=gots[w].at[j], send_sem=send_sems.at[3 * w + j],
                recv_sem=recv_sems.at[3 * w + j], device_id=(*chip, pc), device_id_type=MESH)
                for w in range(nw) for j, chip in enumerate(chips)]

        @pl.when(i == 0)
        def _():
            dpw_ref[...] = jnp.zeros_like(dpw_ref)
            for cp in copies():
                cp.start()

        dh = _nt(dps[0][...], w_ref[:, :pw_])
        for p in range(1, npc):
            dh = dh + _nt(dps[p][...], w_ref[:, p * pw_:(p + 1) * pw_])
        dx, dw = _norm_bwd(x_ref[...], pw_ref[...], dh)
        dx_ref[...] = do_ref[...] + dx
        dpw_ref[...] += jnp.sum(dw, axis=0, keepdims=True)

        @pl.when(i == nsteps - 1)
        def _():
            for cp in copies():
                cp.wait()

    rowd = pl.BlockSpec((tm, d), lambda i: (i + lo, 0))
    vec = pl.BlockSpec((1, d), lambda i: (0, 0))
    out = pl.pallas_call(
        body, name="pre_bwd" if nw else "pre_bwd_rest", grid=(nsteps,),
        in_specs=[pl.BlockSpec((tm, pw_), lambda i: (i + lo, 0))] * npc
        + [pl.BlockSpec(win.shape, lambda i: (0, 0), pipeline_mode=pl.Buffered(1)), rowd, vec, rowd]
        + [ANY] * (ng + nw),
        out_specs=[rowd, vec] + [ANY] * nw, input_output_aliases={npc + 4: 0} if ng else {},
        out_shape=[jax.ShapeDtypeStruct((s, d), F32), jax.ShapeDtypeStruct((1, d), F32)]
        + [jax.ShapeDtypeStruct((3,) + p.shape[1:], p.dtype) for p in psums],
        scratch_shapes=[pltpu.SemaphoreType.DMA((max(3 * nw, 1),)), pltpu.SemaphoreType.DMA((max(3 * nw, 1),))],
        compiler_params=_cp(("arbitrary",), has_side_effects=True),
    )(*pieces, win, x, pw, dout, *([] if gx is None else [gx]), *psums)
    return out[0], out[1], list(out[2:])


def _adamw(w, g, m, v):
    m = ADAM_B1 * m + (1.0 - ADAM_B1) * g
    v = ADAM_B2 * v + (1.0 - ADAM_B2) * (g * g)
    delta = -ADAM_LR * ((m / ADAM_C1) / (jnp.sqrt(v / ADAM_C2) + ADAM_EPS) + ADAM_WD * w)
    return delta, m, v


def _row_block(rows):
    return math.gcd(rows, 128)


def _sum_parts(core, own, got, name):
    n, hr, cols = got.shape
    tr = _row_block(hr)
    nblk = hr // tr

    def body(c_ref, o_ref, p_ref, g_ref):
        g = o_ref[...]
        for k in range(n):
            g = g + p_ref[k].astype(F32)
        g_ref[...] = g

    return pl.pallas_call(
        body, name=name,
        grid_spec=pltpu.PrefetchScalarGridSpec(
            num_scalar_prefetch=1, grid=(nblk,),
            in_specs=[pl.BlockSpec((tr, cols), lambda i, c: (i, 0)),
                      pl.BlockSpec((n, tr, cols), lambda i, c: (0, i, 0))],
            out_specs=pl.BlockSpec((tr, cols), lambda i, c: (c[0] * nblk + i, 0))),
        out_shape=jax.ShapeDtypeStruct((2 * hr, cols), F32),
        compiler_params=_cp(("parallel",)),
    )(core, own, got)


def _adamw_shard(g, w, m, v, name):
    rows, cols = g.shape
    tr = _row_block(rows)

    def body(g_ref, w_ref, m_ref, v_ref, go_ref, d_ref, nm_ref, nv_ref):
        g_ = g_ref[...]
        go_ref[...] = g_
        d_ref[...], nm_ref[...], nv_ref[...] = _adamw(w_ref[...], g_, m_ref[...], v_ref[...])

    blk = pl.BlockSpec((tr, cols), lambda i: (i, 0))
    sd = jax.ShapeDtypeStruct((rows, cols), F32)
    return pl.pallas_call(
        body, name=name, grid=(rows // tr,), in_specs=[blk] * 4, out_specs=[blk] * 4, out_shape=[sd] * 4,
        compiler_params=_cp(("parallel",)),
    )(g, w, m, v)


def _pair_sum(place, grads, got, name):
    n, _, hr, cols = grads.shape
    tr = _row_block(hr)

    def body(p_ref, a_ref, b_ref, oa_ref, ob_ref, all_ref, own_ref):
        all_ref[...] = (a_ref[:, 0] + b_ref[...]).astype(BF16)
        own_ref[...] = oa_ref[0, 0] + ob_ref[0]

    return pl.pallas_call(
        body, name=name,
        grid_spec=pltpu.PrefetchScalarGridSpec(
            num_scalar_prefetch=1, grid=(hr // tr,),
            in_specs=[pl.BlockSpec((n, 1, tr, cols), lambda i, p: (0, p[0], i, 0)),
                      pl.BlockSpec((n, tr, cols), lambda i, p: (0, i, 0)),
                      pl.BlockSpec((1, 1, tr, cols), lambda i, p: (p[1], p[0], i, 0)),
                      pl.BlockSpec((1, tr, cols), lambda i, p: (p[1], i, 0))],
            out_specs=[pl.BlockSpec((n, tr, cols), lambda i, p: (0, i, 0)),
                       pl.BlockSpec((tr, cols), lambda i, p: (i, 0))]),
        out_shape=[jax.ShapeDtypeStruct((n, hr, cols), BF16), jax.ShapeDtypeStruct((hr, cols), F32)],
        compiler_params=_cp(("parallel",)),
    )(place, grads, got, grads, got)


def _place():
    return lax.axis_index("x"), lax.axis_index("y"), lax.axis_index("c")


W_NAMES = ("w_in", "w_q_up", "w_kv_up", "w_out")
NW = len(W_NAMES)


def _comm_call(body, name, ins, out_shape, n_copies, aliases=None):
    return pl.pallas_call(
        body, name=name, in_specs=[ANY] * len(ins), out_specs=[ANY] * len(out_shape), out_shape=out_shape,
        input_output_aliases=aliases or {},
        scratch_shapes=[pltpu.SemaphoreType.DMA((n_copies,)), pltpu.SemaphoreType.DMA((n_copies,))],
        compiler_params=pltpu.CompilerParams(has_side_effects=True),
    )(*ins)


def _gather_weights(shards):
    def body(*refs):
        ws, outs = refs[:NW], refs[NW:2 * NW]
        send_sems, recv_sems = refs[2 * NW:]
        x, y, c = _place()
        chips = [(1 - x, y), (x, 1 - y), (1 - x, 1 - y)]

        def rows(w, chip, core):
            half = ws[w].shape[0] // 2
            return outs[w].at[2 * chip[0] + chip[1], pl.ds(core * half, half), :]

        def copy(w, k, chip, core, to, own=False):
            half = ws[w].shape[0] // 2
            return pltpu.make_async_remote_copy(
                src_ref=ws[w].at[pl.ds(core * half, half), :] if own else rows(w, chip, core),
                dst_ref=rows(w, chip, core), send_sem=send_sems.at[6 * w + k], recv_sem=recv_sems.at[6 * w + k],
                device_id=to, device_id_type=MESH)

        first = [copy(w, j, (x, y), c, (*chip, c), own=True) for w in range(NW) for j, chip in enumerate(chips)]
        for cp in first:
            cp.start()
        passed = []
        for w in range(NW):
            for j, chip in enumerate(chips):
                copy(w, j, chip, c, (x, y, c)).wait_recv()
                passed.append(copy(w, 3 + j, chip, c, (x, y, 1 - c)))
                passed[-1].start()
        for w in range(NW):
            for j, chip in enumerate(chips):
                copy(w, 3 + j, chip, 1 - c, (x, y, c)).wait_recv()
        for cp in first + passed:
            cp.wait_send()

    return _comm_call(body, "gather_weights", shards,
                      [jax.ShapeDtypeStruct((4,) + w.shape, w.dtype) for w in shards], 6 * NW)


def _swap_halves(grads):
    def body(*refs):
        gs, gots = refs[:NW], refs[NW:2 * NW]
        send_sems, recv_sems = refs[2 * NW:]
        x, y, c = _place()
        copies = [pltpu.make_async_remote_copy(
            src_ref=gs[w].at[k, 1 - c], dst_ref=gots[w].at[k], send_sem=send_sems.at[4 * w + k],
            recv_sem=recv_sems.at[4 * w + k], device_id=(x, y, 1 - c), device_id_type=MESH)
            for w in range(NW) for k in range(4)]
        for cp in copies:
            cp.start()
        for cp in copies:
            cp.wait()

    return _comm_call(body, "swap_halves", grads,
                      [jax.ShapeDtypeStruct((4,) + g.shape[2:], g.dtype) for g in grads], 4 * NW)


def _share_with_sibling(arrs):
    na = len(arrs)

    def body(*refs):
        outs = refs[na:2 * na]
        send_sems, recv_sems = refs[2 * na:]
        x, y, c = _place()

        def half(ref):
            hr = ref.shape[0] // 2
            return ref.at[pl.ds(c * hr, hr), :]

        copies = [pltpu.make_async_remote_copy(
            src_ref=half(outs[k]), dst_ref=half(outs[k]), send_sem=send_sems.at[k], recv_sem=recv_sems.at[k],
            device_id=(x, y, 1 - c), device_id_type=MESH) for k in range(na)]
        for cp in copies:
            cp.start()
        for cp in copies:
            cp.wait()

    return _comm_call(body, "share_with_sibling", arrs, [jax.ShapeDtypeStruct(a.shape, a.dtype) for a in arrs],
                      na, aliases={k: k for k in range(na)})


def _norm_allreduce_adamw(part, w, m, v):
    r, lanes = part.shape

    def body(p_ref, w_ref, m_ref, v_ref, g_ref, d_ref, nm_ref, nv_ref, all_ref, send_sems, recv_sems):
        x, y, c = _place()
        me = 4 * x + 2 * y + c
        all_ref[me] = p_ref[...]
        copies = []
        for k in range(1, 8):
            peer = (x ^ (k >> 2), y ^ ((k >> 1) & 1), c ^ (k & 1))
            copies.append(pltpu.make_async_remote_copy(
                src_ref=p_ref, dst_ref=all_ref.at[me], send_sem=send_sems.at[k - 1], recv_sem=recv_sems.at[k - 1],
                device_id=peer, device_id_type=MESH))
        for cp in copies:
            cp.start()
        for cp in copies:
            cp.wait()
        g = all_ref[0]
        for k in range(1, 8):
            g = g + all_ref[k]
        g_ref[...] = g
        d_ref[...], nm_ref[...], nv_ref[...] = _adamw(w_ref[...], g, m_ref[...], v_ref[...])

    vm = pl.BlockSpec(memory_space=pltpu.VMEM)
    sd = jax.ShapeDtypeStruct((r, lanes), F32)
    return pl.pallas_call(
        body, name="norm_allreduce_adamw", in_specs=[vm] * 4, out_specs=[vm] * 4, out_shape=[sd] * 4,
        scratch_shapes=[pltpu.VMEM((8, r, lanes), F32), pltpu.SemaphoreType.DMA((7,)),
                        pltpu.SemaphoreType.DMA((7,))],
        compiler_params=pltpu.CompilerParams(has_side_effects=True),
    )(part, w, m, v)


LANES = 128


def _perm_in(shards):
    s0, s1, s2, s3 = shards
    w = D_IN // 4
    return jnp.concatenate([s0, s1, s2[:, :4096 - 2 * w], s3[:, 4928 - 3 * w:], s2[:, 4096 - 2 * w:],
                            s3[:, :4928 - 3 * w], jnp.zeros((s0.shape[0], D_INP - D_IN), s0.dtype)], axis=1)


def _by_chip_and_half(g, axis):
    rows, cols = g.shape
    if axis == 0:
        return g.reshape(4, 2, rows // 8, cols)
    return g.reshape(rows, 4, cols // 4).transpose(1, 0, 2).reshape(4, 2, rows // 2, cols // 4)


NORM_NAMES = ("pre_norm_w", "q_norm_w", "kv_norm_w", "post_norm_w")
NORM_SIZES = (D_MODEL, Q_RANK, KV_RANK, D_MODEL)
NORM_ROWS = 40


def _pack_norm(vs):
    flat = jnp.concatenate([v.reshape(-1) for v in vs])
    return jnp.pad(flat, (0, NORM_ROWS * LANES - flat.shape[0])).reshape(NORM_ROWS, LANES)


def _unpack_norm(p):
    flat, out, at = p.reshape(-1), [], 0
    for n in NORM_SIZES:
        out.append(flat[at:at + n].reshape(1, n))
        at += n
    return out


def _rope_tables(positions):
    inv_freq = ROPE_THETA ** (-jnp.arange(0, MLA_ROPE, 2, dtype=F32) / MLA_ROPE)
    ang = positions.astype(F32)[:, None] * inv_freq
    cos, sin, z = jnp.cos(ang), jnp.sin(ang), jnp.zeros_like(ang)
    return (jnp.concatenate([cos, cos, z, z], axis=1), jnp.concatenate([z, sin, z, z], axis=1),
            jnp.concatenate([-sin, z, z, z], axis=1))


def _local_step(x, positions, pre_w, win, q_w, wq, kv_w, wk, wv, wout, post_w, target):
    s = x.shape[0]
    rc, rs1, rs2 = _rope_tables(positions)
    h = _prenorm(x, pre_w)
    projb = _matmul(h, win, mode="nn", out_dtype=BF16, tm=512, tn=1024, tk=D_MODEL, name="proj_b", n=PB_W)
    projf = _matmul(h, win, mode="nn", out_dtype=F32, tm=512, tn=1024, tk=D_MODEL, name="proj_f", n=PF_W,
                    b_off=PB_W // 1024)
    oa, a_st, b_st = _sb_fwd(projb)
    nq, nkv, qm, km, vm = _mla_prep(projf, q_w, kv_w, wq, wk, wv, rc, rs1, rs2)
    ob, lse = _mla_fwd(qm, km, vm)
    mixed, dy, dout, err2, dpost = _out_post(oa, ob, projf, wout, x, target, post_w)

    dwout = _matmul(mixed, dy, mode="tn", out_dtype=F32, tm=1024, tn=1024, tk=min(4096, s), name="dw_out")
    dmixed = _matmul(dy, wout, mode="nt", out_dtype=F32, tm=512, tn=1024, tk=D_MODEL, name="d_mixed")
    dqa, dka, dva, dga = _sb_bwd(projb, projf, dmixed, oa, a_st, b_st)
    dqm, dkm, dvm, dgb = _mla_bwd(qm, km, vm, projf, dmixed, ob, lse)
    dqp, dkn, dlat, dqw, dkvw = _mla_bwd_post(dqm, dkm, dvm, projf, q_w, kv_w, wq, wk, wv, rc, rs1, rs2)
    tks = min(4096, s)
    dwq = _matmul(nq, dqp, mode="tn", out_dtype=F32, tm=Q_RANK, tn=1024, tk=tks, name="dw_q")
    dwk = _matmul(nkv, dkn, mode="tn", out_dtype=F32, tm=KV_RANK, tn=1024, tk=tks, name="dw_k")
    dwv = _matmul(nkv, dvm, mode="tn", out_dtype=F32, tm=KV_RANK, tn=1024, tk=tks, name="dw_v")
    lat = Q_RANK + KV_RANK + MLA_ROPE
    orig = jnp.concatenate([dqa, dka, dva, dga, dlat[:, :lat], dgb], axis=1)
    dwin = _dw_by_shard(h, jnp.stack(jnp.split(orig, 4, axis=1)))
    return err2, (dqa, dka, dva, dga, dgb, dlat), dout, dwin, dqw, dwq, dkvw, dwk, dwv, dwout, dpost


def _kernel_layouts(in_shards, w_q_up, w_kv_up, w_out):
    win = _perm_in(in_shards)
    wq = jnp.pad(w_q_up.reshape(Q_RANK, HEADS, 192), ((0, 0), (0, 0), (0, 64))).reshape(Q_RANK, HEADS * MLA_QK_PAD)
    kv = w_kv_up.reshape(KV_RANK, HEADS, 256)
    wk = kv[:, :, :128].reshape(KV_RANK, SB_WIDTH)
    wv = kv[:, :, 128:].reshape(KV_RANK, SB_WIDTH)
    return win, wq, wk, wv, w_out


def _original_layouts(dwq, dwk, dwv):
    dq = dwq.reshape(Q_RANK, HEADS, MLA_QK_PAD)[:, :, :192].reshape(Q_RANK, HEADS * 192)
    dkv = jnp.concatenate([dwk.reshape(KV_RANK, HEADS, 128), dwv.reshape(KV_RANK, HEADS, 128)], axis=2)
    return dq, dkv.reshape(KV_RANK, 2 * SB_WIDTH)


def kernel(x, positions, pre_norm_w, w_in, q_norm_w, w_q_up, kv_norm_w, w_kv_up, w_out, post_norm_w, loss_target, m_pre_norm_w, m_w_in, m_q_norm_w, m_w_q_up, m_kv_norm_w, m_w_kv_up, m_w_out, m_post_norm_w, v_pre_norm_w, v_w_in, v_q_norm_w, v_w_q_up, v_kv_norm_w, v_w_kv_up, v_w_out, v_post_norm_w):
    c = lax.axis_index("c")
    chip = 2 * lax.axis_index("x") + lax.axis_index("y")
    shards = (w_in[0], w_q_up[0], w_kv_up[0], w_out[0])
    mine16 = [w.astype(BF16) for w in shards]
    others = _gather_weights(mine16)
    slot = lambda w, k: jnp.where(chip == k, mine16[w], others[w][k])
    cat = lambda w, ax: jnp.concatenate([slot(w, k) for k in range(4)], axis=ax)
    win, wq, wk, wv, wout = _kernel_layouts([slot(0, k) for k in range(4)], cat(1, 1), cat(2, 1), cat(3, 0))

    err2, pieces, dout, dwin, dqw, dwq, dkvw, dwk, dwv, dwout, dpost = _local_step(
        x[0], positions[0], pre_norm_w, win, q_norm_w, wq, kv_norm_w, wk, wv, wout, post_norm_w, loss_target[0])
    loss = lax.psum(0.5 * jnp.sum(err2) / D_MODEL, ("x", "y", "c"))

    dq, dkv = _original_layouts(dwq, dwk, dwv)
    grads = [dwin.reshape(4, 2, D_MODEL // 2, D_IN // 4)] + [
        _by_chip_and_half(g, ax) for g, ax in ((dq, 1), (dkv, 1), (dwout, 0))]
    place = jnp.stack([c, chip])
    halves = _swap_halves(grads)
    pairs = [_pair_sum(place, grads[w], halves[w], "pair_sum_" + W_NAMES[w]) for w in range(NW)]
    nblk = x.shape[1] // 256
    rest = max(nblk // 8, 1)
    gx, dpre, gots = _pre_bwd(pieces, win, x[0], pre_norm_w, dout, [p[0] for p in pairs], (0, nblk - rest))
    gx, dpre2, _ = _pre_bwd(pieces, win, x[0], pre_norm_w, dout, [], (nblk - rest, rest), gx=gx)
    dpre = dpre + dpre2
    sums = _share_with_sibling([_sum_parts(c.reshape(1), pairs[w][1], gots[w], "sum_parts_" + W_NAMES[w])
                                for w in range(NW)])
    ms = (m_w_in[0], m_w_q_up[0], m_w_kv_up[0], m_w_out[0])
    vs = (v_w_in[0], v_w_q_up[0], v_w_kv_up[0], v_w_out[0])
    done = [_adamw_shard(sums[w], shards[w], ms[w], vs[w], "adamw_" + W_NAMES[w]) for w in range(NW)]
    big = [[done[w][k] for w in range(NW)] for k in range(4)]

    small = _norm_allreduce_adamw(
        _pack_norm([dpre, dqw, dkvw, dpost]), _pack_norm([pre_norm_w, q_norm_w, kv_norm_w, post_norm_w]),
        _pack_norm([m_pre_norm_w, m_q_norm_w, m_kv_norm_w, m_post_norm_w]),
        _pack_norm([v_pre_norm_w, v_q_norm_w, v_kv_norm_w, v_post_norm_w]))
    small = [_unpack_norm(p) for p in small]

    def group(k):
        n, b = small[k], big[k]
        return (n[0], b[0][None], n[1], b[1][None], n[2], b[2][None], b[3][None], n[3])

    return (loss, gx[None], *group(0), *group(1), *group(2), *group(3))
```

```python
import functools
import math

import numpy as np
import jax
import jax.numpy as jnp
from jax import lax
from jax.experimental import pallas as pl
from jax.experimental.pallas import tpu as pltpu

F32 = jnp.float32
BF16 = jnp.bfloat16
MESH = pl.DeviceIdType.MESH

D_MODEL = 2048
HEADS = 8
HEAD_DIM = 128
SB_WIDTH = HEADS * HEAD_DIM
MLA_ROPE = 64
MLA_QK_PAD = 256
Q_RANK = 512
KV_RANK = 256
CHUNK = 64
EPS = 1e-6
ROPE_THETA = 10000.0
D_IN = 5952
D_INP = 6144
PB_W = 3072
PF_W = D_INP - PB_W
SB_SCALE = 1.0 / math.sqrt(HEAD_DIM)
MLA_SCALE = 1.0 / math.sqrt(HEAD_DIM + MLA_ROPE)
NEG = -1e30

ADAM_LR, ADAM_B1, ADAM_B2, ADAM_EPS, ADAM_WD, ADAM_STEP = 0.001, 0.9, 0.999, 1e-08, 0.01, 10
ADAM_C1 = 1.0 - ADAM_B1 ** ADAM_STEP
ADAM_C2 = 1.0 - ADAM_B2 ** ADAM_STEP

ANY = pl.BlockSpec(memory_space=pl.ANY)
VMEM_LIMIT = 56 * 1024 * 1024
TQ = 1024
TK = 256
MLA_TK = 512
MLA_FWD_TK = 1024
UNROLL = 2


def _cp(sem=None, **kw):
    return pltpu.CompilerParams(dimension_semantics=sem, vmem_limit_bytes=VMEM_LIMIT, **kw)


def _dot(a, b, dims):
    return lax.dot_general(a, b, (dims, ((), ())), preferred_element_type=F32)


def _nn(a, b):
    return _dot(a, b, ((1,), (0,)))


def _nt(a, b):
    return _dot(a, b, ((1,), (1,)))


def _tn(a, b):
    return _dot(a, b, ((0,), (0,)))


def _rope(x, c, s1, s2):
    return x * c + pltpu.roll(x, 32, 1) * s1 + pltpu.roll(x, 96, 1) * s2


def _rope_t(x, c, s1, s2):
    return x * c - pltpu.roll(x, 32, 1) * s1 - pltpu.roll(x, 96, 1) * s2


def _silu_parts(g):
    sg = jax.nn.sigmoid(g)
    return g * sg, sg * (1.0 + g * (1.0 - sg))


def _matmul(a, b, *, mode, out_dtype, tm, tn, tk, name, n=None, b_off=0):
    if mode == "tn":
        kk, m = a.shape
        n = b.shape[1] if n is None else n
    else:
        m, kk = a.shape
        n = (b.shape[1] if mode == "nn" else b.shape[0]) if n is None else n
    nk = kk // tk
    a_spec = {"nn": pl.BlockSpec((tm, tk), lambda j, i, k: (i, k)),
              "nt": pl.BlockSpec((tm, tk), lambda j, i, k: (i, k)),
              "tn": pl.BlockSpec((tk, tm), lambda j, i, k: (k, i))}[mode]
    b_spec = {"nn": pl.BlockSpec((tk, tn), lambda j, i, k: (k, j + b_off)),
              "nt": pl.BlockSpec((tn, tk), lambda j, i, k: (j, k)),
              "tn": pl.BlockSpec((tk, tn), lambda j, i, k: (k, j))}[mode]
    dims = {"nn": ((1,), (0,)), "nt": ((1,), (1,)), "tn": ((0,), (0,))}[mode]

    def body(a_ref, b_ref, o_ref, acc_ref):
        k = pl.program_id(2)
        part = _dot(a_ref[...], b_ref[...], dims)
        if nk == 1:
            o_ref[...] = part.astype(out_dtype)
        else:
            @pl.when(k == 0)
            def _():
                acc_ref[...] = part

            @pl.when(k > 0)
            def _():
                acc_ref[...] += part

            @pl.when(k == nk - 1)
            def _():
                o_ref[...] = acc_ref[...].astype(out_dtype)

    return pl.pallas_call(
        body, name=name, grid=(n // tn, m // tm, nk),
        in_specs=[a_spec, b_spec], out_specs=pl.BlockSpec((tm, tn), lambda j, i, k: (i, j)),
        out_shape=jax.ShapeDtypeStruct((m, n), out_dtype),
        scratch_shapes=[pltpu.VMEM((tm, tn) if nk > 1 else (8, 128), F32)],
        compiler_params=_cp(("parallel", "parallel", "arbitrary")),
    )(a, b)


def _dw_by_shard(h, dp4, tm=1024, tk=2048):
    s, d = h.shape
    n, _, cols = dp4.shape
    tk = min(tk, s)
    nk = s // tk

    def body(a_ref, b_ref, o_ref, acc_ref):
        k = pl.program_id(2)
        part = _tn(a_ref[...], b_ref[...])

        @pl.when(k == 0)
        def _():
            acc_ref[...] = part

        @pl.when(k > 0)
        def _():
            acc_ref[...] += part

        @pl.when(k == nk - 1)
        def _():
            o_ref[...] = acc_ref[...]

    return pl.pallas_call(
        body, name="dw_in", grid=(n, d // tm, nk),
        in_specs=[pl.BlockSpec((tk, tm), lambda j, i, k: (k, i)),
                  pl.BlockSpec((None, tk, cols), lambda j, i, k: (j, k, 0))],
        out_specs=pl.BlockSpec((None, tm, cols), lambda j, i, k: (j, i, 0)),
        out_shape=jax.ShapeDtypeStruct((n, d, cols), F32),
        scratch_shapes=[pltpu.VMEM((tm, cols), F32)],
        compiler_params=_cp(("parallel", "parallel", "arbitrary")),
    )(h, dp4)


def _prenorm(x, w, tm=512):
    s, d = x.shape

    def body(x_ref, w_ref, h_ref):
        xv = x_ref[...]
        r = lax.rsqrt(jnp.mean(xv * xv, axis=1, keepdims=True) + EPS)
        h_ref[...] = ((xv * r) * w_ref[...]).astype(BF16)

    return pl.pallas_call(
        body, name="prenorm", grid=(s // tm,),
        in_specs=[pl.BlockSpec((tm, d), lambda i: (i, 0)), pl.BlockSpec((1, d), lambda i: (0, 0))],
        out_specs=pl.BlockSpec((tm, d), lambda i: (i, 0)),
        out_shape=jax.ShapeDtypeStruct((s, d), BF16),
        compiler_params=_cp(("parallel",)),
    )(x, w)


def _tri(n, cmp, value):
    row = lax.broadcasted_iota(jnp.int32, (n, n), 0)
    col = lax.broadcasted_iota(jnp.int32, (n, n), 1)
    return jnp.where(cmp(row, col), value, 0.0).astype(BF16)


def _softplus(z, mask):
    sp = jnp.maximum(z, 0.0) + jnp.log(1.0 + jnp.exp(-jnp.abs(z)))
    return sp if mask is None else jnp.where(mask, sp, 0.0)


def _tiles(s, tk=TK):
    tq = min(TQ, s)
    return tq, tk, math.gcd(tq // tk, UNROLL)


def _band_masks(tq, tk, fn):
    out = []
    for b in range(tq // tk):
        key = lax.broadcasted_iota(jnp.int32, (tk, tq - b * tk), 0) + b * tk
        qry = lax.broadcasted_iota(jnp.int32, (tk, tq - b * tk), 1) + b * tk
        out.append(fn(qry, key))
    return out


def _row_spec(tq):
    return pl.BlockSpec((8, tq), lambda h, i: (h, i))


def _sb_store_shape(s, tq, tk):
    nb, nq = tq // tk, s // tq
    return (HEADS, nb * nq * (nq + 1) // 2, tk, tq)


def _sb_fwd(projb):
    s = projb.shape[0]
    tq, tk, _ = _tiles(s)
    nb = tq // tk

    def body(q_ref, k_ref, v_ref, o_ref, a_st, b_st, acc_ref, car_ref, z_ref, a_buf, b_buf, sems):
        h, i = pl.program_id(0), pl.program_id(1)
        q = q_ref[...]
        from_here = _tri(tk, lambda s_, j: j >= s_, -1.0)
        masks = _band_masks(tq, tk, lambda t, s_: s_ < t)
        acc_ref[...] = jnp.zeros_like(acc_ref)
        car_ref[...] = jnp.zeros_like(car_ref)
        first = nb * (i * (i + 1) // 2)

        def stores(st, g):
            at = pl.ds(first + g * nb, nb)
            return (pltpu.make_async_copy(a_buf.at[st], a_st.at[h, at], sems.at[st, 0]),
                    pltpu.make_async_copy(b_buf.at[st], b_st.at[h, at], sems.at[st, 1]))

        def tile(j, mask, off, st, u):
            rows = pl.ds(pl.multiple_of(j * tk, tk), tk)
            z_ref[:, off:] = _nt(k_ref[rows, :], q[off:, :]) * SB_SCALE
            z = z_ref[:, off:]
            sp = _softplus(z, mask)
            a = jnp.exp(z + _nn(from_here, sp.astype(BF16)) + car_ref[:, off:])
            beta = jnp.exp(z - sp)
            if mask is not None:
                a = jnp.where(mask, a, 0.0)
                beta = jnp.where(mask, beta, 0.0)
            a = a.astype(BF16)
            if off:
                a_buf[st, u, :, :off] = jnp.zeros((tk, off), BF16)
                b_buf[st, u, :, :off] = jnp.zeros((tk, off), BF16)
            a_buf[st, u, :, off:] = a
            b_buf[st, u, :, off:] = beta.astype(BF16)
            acc_ref[:, off:] += _tn(v_ref[rows, :], a)
            car_ref[:, off:] -= jnp.sum(sp, axis=0, keepdims=True)

        for b in reversed(range(nb)):
            tile(i * nb + b, masks[b], b * tk, 0, b)
        for cp in stores(0, i):
            cp.start()

        def step(jj, c):
            st, g = (jj + 1) % 2, i - 1 - jj

            @pl.when(jj >= 1)
            def _():
                for cp in stores(st, g):
                    cp.wait()

            for u in reversed(range(nb)):
                tile(g * nb + u, None, 0, st, u)
            for cp in stores(st, g):
                cp.start()
            return c

        lax.fori_loop(0, i, step, 0)
        for cp in stores(0, 0):
            cp.wait()

        @pl.when(i >= 1)
        def _():
            for cp in stores(1, 0):
                cp.wait()

        o_ref[...] = acc_ref[...].T

    blk = pl.BlockSpec((tq, HEAD_DIM), lambda h, i: (i, h))
    st = jax.ShapeDtypeStruct(_sb_store_shape(s, tq, tk), BF16)
    return pl.pallas_call(
        body, name="sb_fwd", grid=(HEADS, s // tq),
        in_specs=[blk, pl.BlockSpec((s, HEAD_DIM), lambda h, i: (0, HEADS + h)),
                  pl.BlockSpec((s, HEAD_DIM), lambda h, i: (0, 2 * HEADS + h))],
        out_specs=[blk, ANY, ANY],
        out_shape=[jax.ShapeDtypeStruct((s, SB_WIDTH), F32), st, st],
        scratch_shapes=[pltpu.VMEM((HEAD_DIM, tq), F32), pltpu.VMEM((1, tq), F32), pltpu.VMEM((tk, tq), F32),
                        pltpu.VMEM((2, nb, tk, tq), BF16), pltpu.VMEM((2, nb, tk, tq), BF16),
                        pltpu.SemaphoreType.DMA((2, 2))],
        compiler_params=_cp(("arbitrary", "arbitrary")),
    )(projb, projb, projb)


def _mla_prep(projf, qw, kvw, wq, wk, wv, rc, rs1, rs2, tm=512):
    s = projf.shape[0]

    def body(cq_ref, ckv_ref, kr_ref, qw_ref, kvw_ref, wq_ref, wk_ref, wv_ref, c_ref, s1_ref, s2_ref,
             nq_ref, nkv_ref, q_ref, k_ref, v_ref):
        c, s1, s2 = c_ref[...], s1_ref[...], s2_ref[...]
        cq = cq_ref[...]
        nq = ((cq * lax.rsqrt(jnp.mean(cq * cq, axis=1, keepdims=True) + EPS)) * qw_ref[...]).astype(BF16)
        nq_ref[...] = nq
        qf = _nn(nq, wq_ref[...])
        ckv = ckv_ref[...]
        nkv = ((ckv * lax.rsqrt(jnp.mean(ckv * ckv, axis=1, keepdims=True) + EPS)) * kvw_ref[...]).astype(BF16)
        nkv_ref[...] = nkv
        kn = _nn(nkv, wk_ref[...])
        v_ref[...] = _nn(nkv, wv_ref[...]).astype(BF16)
        krot = _rope(kr_ref[...], c, s1, s2).astype(BF16)
        for h in range(HEADS):
            lo = h * MLA_QK_PAD
            q_ref[:, lo:lo + 128] = qf[:, lo:lo + 128].astype(BF16)
            q_ref[:, lo + 128:lo + 256] = _rope(qf[:, lo + 128:lo + 256], c, s1, s2).astype(BF16)
            k_ref[:, lo:lo + 128] = kn[:, h * 128:(h + 1) * 128].astype(BF16)
            k_ref[:, lo + 128:lo + 256] = krot

    row = lambda w, b: pl.BlockSpec((tm, w), lambda i: (i, b))
    full = lambda a: pl.BlockSpec(a.shape, lambda i: (0, 0))
    return pl.pallas_call(
        body, name="mla_prep", grid=(s // tm,),
        in_specs=[row(Q_RANK, 2048 // Q_RANK), row(KV_RANK, 2560 // KV_RANK), row(128, 2816 // 128),
                  full(qw), full(kvw), full(wq), full(wk), full(wv), row(128, 0), row(128, 0), row(128, 0)],
        out_specs=[row(Q_RANK, 0), row(KV_RANK, 0), row(HEADS * MLA_QK_PAD, 0), row(HEADS * MLA_QK_PAD, 0),
                   row(SB_WIDTH, 0)],
        out_shape=[jax.ShapeDtypeStruct((s, Q_RANK), BF16), jax.ShapeDtypeStruct((s, KV_RANK), BF16),
                   jax.ShapeDtypeStruct((s, HEADS * MLA_QK_PAD), BF16),
                   jax.ShapeDtypeStruct((s, HEADS * MLA_QK_PAD), BF16),
                   jax.ShapeDtypeStruct((s, SB_WIDTH), BF16)],
        compiler_params=_cp(("parallel",)),
    )(projf, projf, projf, qw, kvw, wq, wk, wv, rc, rs1, rs2)


def _mla_mask(qry, key):
    return (key // CHUNK) <= (qry // CHUNK)


def _mla_fwd(qm, km, vm):
    s = qm.shape[0]
    tq, tk, unroll = _tiles(s, min(MLA_FWD_TK, s))
    nb = tq // tk

    def body(q_ref, k_ref, v_ref, o_ref, lse_ref, acc_ref, m_ref, l_ref):
        i = pl.program_id(1)
        q = q_ref[...]
        masks = _band_masks(tq, tk, _mla_mask)
        acc_ref[...] = jnp.zeros_like(acc_ref)
        m_ref[...] = jnp.full_like(m_ref, NEG)
        l_ref[...] = jnp.zeros_like(l_ref)

        def tile(j, mask, off):
            rows = pl.ds(pl.multiple_of(j * tk, tk), tk)
            sc = _nt(k_ref[rows, :], q[off:, :]) * MLA_SCALE
            if mask is not None:
                sc = jnp.where(mask, sc, NEG)
            m_old = m_ref[:, off:]
            m_new = jnp.maximum(m_old, jnp.max(sc, axis=0, keepdims=True))
            p = jnp.exp(sc - m_new)
            alpha = jnp.exp(m_old - m_new)
            l_ref[:, off:] = alpha * l_ref[:, off:] + jnp.sum(p, axis=0, keepdims=True)
            acc_ref[:, off:] = alpha * acc_ref[:, off:] + _tn(v_ref[rows, :], p.astype(BF16))
            m_ref[:, off:] = m_new

        for b in range(nb):
            tile(i * nb + b, masks[b], b * tk)

        def step(jj, c):
            for u in range(unroll):
                tile(jj * unroll + u, None, 0)
            return c

        lax.fori_loop(0, i * nb // unroll, step, 0)
        o_ref[...] = (acc_ref[...] / l_ref[...]).T
        lse_ref[...] = jnp.broadcast_to(m_ref[...] + jnp.log(l_ref[...]), (8, tq))

    return pl.pallas_call(
        body, name="mla_fwd", grid=(HEADS, s // tq),
        in_specs=[pl.BlockSpec((tq, MLA_QK_PAD), lambda h, i: (i, h)),
                  pl.BlockSpec((s, MLA_QK_PAD), lambda h, i: (0, h)),
                  pl.BlockSpec((s, HEAD_DIM), lambda h, i: (0, h))],
        out_specs=[pl.BlockSpec((tq, HEAD_DIM), lambda h, i: (i, h)), _row_spec(tq)],
        out_shape=[jax.ShapeDtypeStruct((s, SB_WIDTH), F32), jax.ShapeDtypeStruct((8 * HEADS, s), F32)],
        scratch_shapes=[pltpu.VMEM((HEAD_DIM, tq), F32), pltpu.VMEM((1, tq), F32), pltpu.VMEM((1, tq), F32)],
        compiler_params=_cp(("parallel", "arbitrary")),
    )(qm, km, vm)


def _out_post(oa, ob, projf, wout, x, target, pw, tm=256):
    s, d = x.shape

    def body(oa_ref, ob_ref, ga_ref, gb_ref, w_ref, x_ref, t_ref, pw_ref,
             mix_ref, dy_ref, dout_ref, loss_ref, dpw_ref):
        i = pl.program_id(0)
        sa, _ = _silu_parts(ga_ref[...])
        sb, _ = _silu_parts(gb_ref[...])
        mix_ref[:, :SB_WIDTH] = (oa_ref[...] * sa).astype(BF16)
        mix_ref[:, SB_WIDTH:] = (ob_ref[...] * sb).astype(BF16)
        y = _nn(mix_ref[...], w_ref[...])
        r = lax.rsqrt(jnp.mean(y * y, axis=1, keepdims=True) + EPS)
        yhat = y * r
        pwv = pw_ref[...]
        err = (x_ref[...] + yhat * pwv) - t_ref[...]
        dout = err * (1.0 / d)
        dout_ref[...] = dout
        g = dout * pwv
        dy_ref[...] = (r * (g - yhat * jnp.mean(g * yhat, axis=1, keepdims=True))).astype(BF16)

        @pl.when(i == 0)
        def _():
            loss_ref[...] = jnp.zeros_like(loss_ref)
            dpw_ref[...] = jnp.zeros_like(dpw_ref)

        loss_ref[...] += jnp.sum(err * err, axis=0, keepdims=True)
        dpw_ref[...] += jnp.sum(dout * yhat, axis=0, keepdims=True)

    row = lambda w, b: pl.BlockSpec((tm, w), lambda i: (i, b))
    vec = pl.BlockSpec((1, d), lambda i: (0, 0))
    return pl.pallas_call(
        body, name="out_post", grid=(s // tm,),
        in_specs=[row(SB_WIDTH, 0), row(SB_WIDTH, 0), row(SB_WIDTH, 0), row(SB_WIDTH, 1),
                  pl.BlockSpec(wout.shape, lambda i: (0, 0)), row(d, 0), row(d, 0), vec],
        out_specs=[row(d, 0), row(d, 0), row(d, 0), vec, vec],
        out_shape=[jax.ShapeDtypeStruct((s, d), BF16), jax.ShapeDtypeStruct((s, d), BF16),
                   jax.ShapeDtypeStruct((s, d), F32), jax.ShapeDtypeStruct((1, d), F32),
                   jax.ShapeDtypeStruct((1, d), F32)],
        compiler_params=_cp(("arbitrary",)),
    )(oa, ob, projf, projf, wout, x, target, pw)


def _sb_bwd(projb, projf, dmixed, oa, a_st, b_st):
    s = projb.shape[0]
    tq, tk, _ = _tiles(s)
    nb, nq = tq // tk, s // tq

    def body(q_ref, k_ref, v_ref, dm_ref, g_ref, o_ref, a_st, b_st, dq_ref, dk_ref, dv_ref, dg_ref,
             dka_ref, dva_ref, dqa_ref, cg_ref, a_buf, b_buf, sems):
        h, i = pl.program_id(0), pl.program_id(1)

        @pl.when(i == 0)
        def _():
            dka_ref[...] = jnp.zeros_like(dka_ref)
            dva_ref[...] = jnp.zeros_like(dva_ref)

        q = q_ref[...]
        silu, dsilu = _silu_parts(g_ref[...])
        dm = dm_ref[...]
        dg_ref[...] = (dm * o_ref[...] * dsilu).astype(BF16)
        do = (dm * silu).astype(BF16)
        up_to_here = _tri(tk, lambda s_, j: j <= s_, 1.0)
        dqa_ref[...] = jnp.zeros_like(dqa_ref)
        cg_ref[...] = jnp.zeros_like(cg_ref)
        first = nb * (i * (i + 1) // 2)

        def loads(st, g):
            at = pl.ds(first + g * nb, nb)
            return (pltpu.make_async_copy(a_st.at[h, at], a_buf.at[st], sems.at[st, 0]),
                    pltpu.make_async_copy(b_st.at[h, at], b_buf.at[st], sems.at[st, 1]))

        def tile(j, off, st, u):
            rows = pl.ds(pl.multiple_of(j * tk, tk), tk)
            k = k_ref[rows, :]
            a = a_buf[st, u, :, off:]
            g = a.astype(F32) * _nt(v_ref[rows, :], do[off:, :])
            dva_ref[rows, :] += _nn(a, do[off:, :])
            cum = _nn(up_to_here, g.astype(BF16)) + cg_ref[:, off:]
            dz = ((g - b_buf[st, u, :, off:].astype(F32) * cum) * SB_SCALE).astype(BF16)
            dqa_ref[:, off:] += _tn(k, dz)
            dka_ref[rows, :] += _nn(dz, q[off:, :])
            cg_ref[:, off:] += jnp.sum(g, axis=0, keepdims=True)

        for cp in loads(0, 0):
            cp.start()

        def step(g, c):
            st = g % 2
            for cp in loads(1 - st, g + 1):
                cp.start()
            for cp in loads(st, g):
                cp.wait()
            for u in range(nb):
                tile(g * nb + u, 0, st, u)
            return c

        lax.fori_loop(0, i, step, 0)
        for cp in loads(i % 2, i):
            cp.wait()
        for b in range(nb):
            tile(i * nb + b, b * tk, i % 2, b)
        dq_ref[...] = dqa_ref[...].T.astype(BF16)

        @pl.when(i == nq - 1)
        def _():
            dk_ref[...] = dka_ref[...].astype(BF16)
            dv_ref[...] = dva_ref[...].astype(BF16)

    blk = lambda off: pl.BlockSpec((tq, HEAD_DIM), lambda h, i: (i, off + h))
    whole = lambda off: pl.BlockSpec((s, HEAD_DIM), lambda h, i: (0, off + h))
    o_sd = jax.ShapeDtypeStruct((s, SB_WIDTH), BF16)
    return pl.pallas_call(
        body, name="sb_bwd", grid=(HEADS, nq),
        in_specs=[blk(0), whole(HEADS), whole(2 * HEADS), blk(0), blk(0), blk(0), ANY, ANY],
        out_specs=[blk(0), whole(0), whole(0), blk(0)],
        out_shape=[o_sd, o_sd, o_sd, o_sd],
        scratch_shapes=[pltpu.VMEM((s, HEAD_DIM), F32), pltpu.VMEM((s, HEAD_DIM), F32),
                        pltpu.VMEM((HEAD_DIM, tq), F32), pltpu.VMEM((1, tq), F32),
                        pltpu.VMEM((2, nb, tk, tq), BF16), pltpu.VMEM((2, nb, tk, tq), BF16),
                        pltpu.SemaphoreType.DMA((2, 2))],
        compiler_params=_cp(("arbitrary", "arbitrary")),
    )(projb, projb, projb, dmixed, projf, oa, a_st, b_st)


def _mla_bwd(qm, km, vm, projf, dmixed, ob, lse):
    s = qm.shape[0]
    tq, tk, unroll = _tiles(s, MLA_TK)
    nb, nq = tq // tk, s // tq

    def body(q_ref, k_ref, v_ref, dm_ref, g_ref, o_ref, lse_ref, dq_ref, dk_ref, dv_ref, dg_ref,
             dva_ref, dqa_ref):
        i = pl.program_id(1)

        @pl.when(i == 0)
        def _():
            dk_ref[...] = jnp.zeros_like(dk_ref)
            dva_ref[...] = jnp.zeros_like(dva_ref)

        q = q_ref[...]
        silu, dsilu = _silu_parts(g_ref[...])
        dm = dm_ref[...]
        o = o_ref[...]
        dg_ref[...] = (dm * o * dsilu).astype(BF16)
        dof = dm * silu
        delta = jnp.sum((dof * o).T, axis=0, keepdims=True)
        do = dof.astype(BF16)
        lse = lse_ref[0:1, :]
        masks = _band_masks(tq, tk, _mla_mask)
        dqa_ref[...] = jnp.zeros_like(dqa_ref)

        def tile(j, mask, off):
            rows = pl.ds(pl.multiple_of(j * tk, tk), tk)
            k = k_ref[rows, :]
            p = jnp.exp(_nt(k, q[off:, :]) * MLA_SCALE - lse[:, off:])
            if mask is not None:
                p = jnp.where(mask, p, 0.0)
            ds = (p * (_nt(v_ref[rows, :], do[off:, :]) - delta[:, off:]) * MLA_SCALE).astype(BF16)
            dva_ref[rows, :] += _nn(p.astype(BF16), do[off:, :])
            dk_ref[rows, :] += _nn(ds, q[off:, :])
            dqa_ref[:, off:] += _tn(k, ds)

        def step(jj, c):
            for u in range(unroll):
                tile(jj * unroll + u, None, 0)
            return c

        lax.fori_loop(0, i * nb // unroll, step, 0)
        for b in range(nb):
            tile(i * nb + b, masks[b], b * tk)
        dq_ref[...] = dqa_ref[...].T

        @pl.when(i == nq - 1)
        def _():
            dv_ref[...] = dva_ref[...].astype(BF16)

    blk = lambda w, off: pl.BlockSpec((tq, w), lambda h, i: (i, off + h))
    whole = lambda w: pl.BlockSpec((s, w), lambda h, i: (0, h))
    return pl.pallas_call(
        body, name="mla_bwd", grid=(HEADS, nq),
        in_specs=[blk(MLA_QK_PAD, 0), whole(MLA_QK_PAD), whole(HEAD_DIM), blk(HEAD_DIM, HEADS),
                  blk(HEAD_DIM, HEADS), blk(HEAD_DIM, 0), _row_spec(tq)],
        out_specs=[blk(MLA_QK_PAD, 0), whole(MLA_QK_PAD), whole(HEAD_DIM), blk(HEAD_DIM, 0)],
        out_shape=[jax.ShapeDtypeStruct((s, HEADS * MLA_QK_PAD), F32),
                   jax.ShapeDtypeStruct((s, HEADS * MLA_QK_PAD), F32),
                   jax.ShapeDtypeStruct((s, SB_WIDTH), BF16), jax.ShapeDtypeStruct((s, SB_WIDTH), BF16)],
        scratch_shapes=[pltpu.VMEM((s, HEAD_DIM), F32), pltpu.VMEM((MLA_QK_PAD, tq), F32)],
        compiler_params=_cp(("parallel", "arbitrary")),
    )(qm, km, vm, dmixed, projf, ob, lse)


def _norm_bwd(x, w, dn):
    r = lax.rsqrt(jnp.mean(x * x, axis=1, keepdims=True) + EPS)
    xhat = x * r
    g = dn * w
    return r * (g - xhat * jnp.mean(g * xhat, axis=1, keepdims=True)), dn * xhat


def _mla_bwd_post(dqm, dkm, dvm, projf, qw, kvw, wq, wk, wv, rc, rs1, rs2, tm=256):
    s = dqm.shape[0]

    def body(dq_ref, dk_ref, dv_ref, cq_ref, ckv_ref, qw_ref, kvw_ref, wq_ref, wk_ref, wv_ref,
             c_ref, s1_ref, s2_ref, dqp_ref, dkn_ref, dlat_ref, dqw_ref, dkvw_ref):
        i = pl.program_id(0)
        c, s1, s2 = c_ref[...], s1_ref[...], s2_ref[...]
        drot = jnp.zeros((tm, 128), F32)
        for h in range(HEADS):
            lo = h * MLA_QK_PAD
            dqp_ref[:, lo:lo + 128] = dq_ref[:, lo:lo + 128].astype(BF16)
            dqp_ref[:, lo + 128:lo + 256] = _rope_t(dq_ref[:, lo + 128:lo + 256], c, s1, s2).astype(BF16)
            dkn_ref[:, h * 128:(h + 1) * 128] = dk_ref[:, lo:lo + 128].astype(BF16)
            drot = drot + dk_ref[:, lo + 128:lo + 256]
        dlat_ref[:, Q_RANK + KV_RANK:Q_RANK + KV_RANK + 128] = _rope_t(drot, c, s1, s2).astype(BF16)
        dlat_ref[:, Q_RANK + KV_RANK + 128:] = jnp.zeros((tm, 128), BF16)
        dcq, dqw = _norm_bwd(cq_ref[...], qw_ref[...], _nt(dqp_ref[...], wq_ref[...]))
        dlat_ref[:, :Q_RANK] = dcq.astype(BF16)
        dnkv = _nt(dkn_ref[...], wk_ref[...]) + _nt(dv_ref[...], wv_ref[...])
        dckv, dkvw = _norm_bwd(ckv_ref[...], kvw_ref[...], dnkv)
        dlat_ref[:, Q_RANK:Q_RANK + KV_RANK] = dckv.astype(BF16)

        @pl.when(i == 0)
        def _():
            dqw_ref[...] = jnp.zeros_like(dqw_ref)
            dkvw_ref[...] = jnp.zeros_like(dkvw_ref)

        dqw_ref[...] += jnp.sum(dqw, axis=0, keepdims=True)
        dkvw_ref[...] += jnp.sum(dkvw, axis=0, keepdims=True)

    row = lambda w, b: pl.BlockSpec((tm, w), lambda i: (i, b))
    full = lambda a: pl.BlockSpec(a.shape, lambda i: (0, 0))
    sd = jax.ShapeDtypeStruct
    return pl.pallas_call(
        body, name="mla_bwd_post", grid=(s // tm,),
        in_specs=[row(HEADS * MLA_QK_PAD, 0), row(HEADS * MLA_QK_PAD, 0), row(SB_WIDTH, 0),
                  row(Q_RANK, 2048 // Q_RANK), row(KV_RANK, 2560 // KV_RANK),
                  full(qw), full(kvw), full(wq), full(wk), full(wv), row(128, 0), row(128, 0), row(128, 0)],
        out_specs=[row(HEADS * MLA_QK_PAD, 0), row(SB_WIDTH, 0), row(1024, 0),
                   pl.BlockSpec((1, Q_RANK), lambda i: (0, 0)), pl.BlockSpec((1, KV_RANK), lambda i: (0, 0))],
        out_shape=[sd((s, HEADS * MLA_QK_PAD), BF16), sd((s, SB_WIDTH), BF16), sd((s, 1024), BF16),
                   sd((1, Q_RANK), F32), sd((1, KV_RANK), F32)],
        compiler_params=_cp(("arbitrary",)),
    )(dqm, dkm, dvm, projf, projf, qw, kvw, wq, wk, wv, rc, rs1, rs2)


def _pre_bwd(pieces, win, x, pw, dout, psums, tm=256):
    s, d = x.shape
    npc, pw_ = len(pieces), pieces[0].shape[1]
    nw, nsteps = len(psums), s // tm

    def body(*refs):
        dps, (w_ref, x_ref, pw_ref, do_ref), refs = refs[:npc], refs[npc:npc + 4], refs[npc + 4:]
        ps, (dx_ref, dpw_ref), gots = refs[:nw], refs[nw:nw + 2], refs[nw + 2:2 * nw + 2]
        send_sems, recv_sems = refs[2 * nw + 2:]
        i = pl.program_id(0)

        def copies():
            if not nw:
                return []
            px, py, pc = _place()
            chips = [(1 - px, py), (px, 1 - py), (1 - px, 1 - py)]
            return [pltpu.make_async_remote_copy(
                src_ref=ps[w].at[2 * chip[0] + chip[1]], dst_ref=gots[w].at[j], send_sem=send_sems.at[3 * w + j],
                recv_sem=recv_sems.at[3 * w + j], device_id=(*chip, pc), device_id_type=MESH)
                for w in range(nw) for j, chip in enumerate(chips)]

        @pl.when(i == 0)
        def _():
            dpw_ref[...] = jnp.zeros_like(dpw_ref)
            for cp in copies():
                cp.start()

        dh = _nt(dps[0][...], w_ref[:, :pw_])
        for p in range(1, npc):
            dh = dh + _nt(dps[p][...], w_ref[:, p * pw_:(p + 1) * pw_])
        dx, dw = _norm_bwd(x_ref[...], pw_ref[...], dh)
        dx_ref[...] = do_ref[...] + dx
        dpw_ref[...] += jnp.sum(dw, axis=0, keepdims=True)

        @pl.when(i == nsteps - 1)
        def _():
            for cp in copies():
                cp.wait()

    rowd = pl.BlockSpec((tm, d), lambda i: (i, 0))
    vec = pl.BlockSpec((1, d), lambda i: (0, 0))
    out = pl.pallas_call(
        body, name="pre_bwd", grid=(nsteps,),
        in_specs=[pl.BlockSpec((tm, pw_), lambda i: (i, 0))] * npc
        + [pl.BlockSpec(win.shape, lambda i: (0, 0), pipeline_mode=pl.Buffered(1)), rowd, vec, rowd] + [ANY] * nw,
        out_specs=[rowd, vec] + [ANY] * nw,
        out_shape=[jax.ShapeDtypeStruct((s, d), F32), jax.ShapeDtypeStruct((1, d), F32)]
        + [jax.ShapeDtypeStruct((3,) + p.shape[1:], p.dtype) for p in psums],
        scratch_shapes=[pltpu.SemaphoreType.DMA((max(3 * nw, 1),)), pltpu.SemaphoreType.DMA((max(3 * nw, 1),))],
        compiler_params=_cp(("arbitrary",), has_side_effects=True),
    )(*pieces, win, x, pw, dout, *psums)
    return out[0], out[1], list(out[2:])


def _adamw(w, g, m, v):
    m = ADAM_B1 * m + (1.0 - ADAM_B1) * g
    v = ADAM_B2 * v + (1.0 - ADAM_B2) * (g * g)
    delta = -ADAM_LR * ((m / ADAM_C1) / (jnp.sqrt(v / ADAM_C2) + ADAM_EPS) + ADAM_WD * w)
    return delta, m, v


def _row_block(rows):
    return math.gcd(rows, 128)


def _sum_parts(core, own, got, name):
    n, hr, cols = got.shape
    tr = _row_block(hr)
    nblk = hr // tr

    def body(c_ref, o_ref, p_ref, g_ref):
        g = o_ref[...]
        for k in range(n):
            g = g + p_ref[k].astype(F32)
        g_ref[...] = g

    return pl.pallas_call(
        body, name=name,
        grid_spec=pltpu.PrefetchScalarGridSpec(
            num_scalar_prefetch=1, grid=(nblk,),
            in_specs=[pl.BlockSpec((tr, cols), lambda i, c: (i, 0)),
                      pl.BlockSpec((n, tr, cols), lambda i, c: (0, i, 0))],
            out_specs=pl.BlockSpec((tr, cols), lambda i, c: (c[0] * nblk + i, 0))),
        out_shape=jax.ShapeDtypeStruct((2 * hr, cols), F32),
        compiler_params=_cp(("parallel",)),
    )(core, own, got)


def _adamw_shard(g, w, m, v, name):
    rows, cols = g.shape
    tr = _row_block(rows)

    def body(g_ref, w_ref, m_ref, v_ref, go_ref, d_ref, nm_ref, nv_ref):
        g_ = g_ref[...]
        go_ref[...] = g_
        d_ref[...], nm_ref[...], nv_ref[...] = _adamw(w_ref[...], g_, m_ref[...], v_ref[...])

    blk = pl.BlockSpec((tr, cols), lambda i: (i, 0))
    sd = jax.ShapeDtypeStruct((rows, cols), F32)
    return pl.pallas_call(
        body, name=name, grid=(rows // tr,), in_specs=[blk] * 4, out_specs=[blk] * 4, out_shape=[sd] * 4,
        compiler_params=_cp(("parallel",)),
    )(g, w, m, v)


def _pair_sum(place, grads, got, name):
    n, _, hr, cols = grads.shape
    tr = _row_block(hr)

    def body(p_ref, a_ref, b_ref, oa_ref, ob_ref, all_ref, own_ref):
        all_ref[...] = (a_ref[:, 0] + b_ref[...]).astype(BF16)
        own_ref[...] = oa_ref[0, 0] + ob_ref[0]

    return pl.pallas_call(
        body, name=name,
        grid_spec=pltpu.PrefetchScalarGridSpec(
            num_scalar_prefetch=1, grid=(hr // tr,),
            in_specs=[pl.BlockSpec((n, 1, tr, cols), lambda i, p: (0, p[0], i, 0)),
                      pl.BlockSpec((n, tr, cols), lambda i, p: (0, i, 0)),
                      pl.BlockSpec((1, 1, tr, cols), lambda i, p: (p[1], p[0], i, 0)),
                      pl.BlockSpec((1, tr, cols), lambda i, p: (p[1], i, 0))],
            out_specs=[pl.BlockSpec((n, tr, cols), lambda i, p: (0, i, 0)),
                       pl.BlockSpec((tr, cols), lambda i, p: (i, 0))]),
        out_shape=[jax.ShapeDtypeStruct((n, hr, cols), BF16), jax.ShapeDtypeStruct((hr, cols), F32)],
        compiler_params=_cp(("parallel",)),
    )(place, grads, got, grads, got)


def _place():
    return lax.axis_index("x"), lax.axis_index("y"), lax.axis_index("c")


W_NAMES = ("w_in", "w_q_up", "w_kv_up", "w_out")
NW = len(W_NAMES)


def _comm_call(body, name, ins, out_shape, n_copies, aliases=None):
    return pl.pallas_call(
        body, name=name, in_specs=[ANY] * len(ins), out_specs=[ANY] * len(out_shape), out_shape=out_shape,
        input_output_aliases=aliases or {},
        scratch_shapes=[pltpu.SemaphoreType.DMA((n_copies,)), pltpu.SemaphoreType.DMA((n_copies,))],
        compiler_params=pltpu.CompilerParams(has_side_effects=True),
    )(*ins)


def _gather_weights(shards):
    def body(*refs):
        ws, outs = refs[:NW], refs[NW:2 * NW]
        send_sems, recv_sems = refs[2 * NW:]
        x, y, c = _place()
        chips = [(1 - x, y), (x, 1 - y), (1 - x, 1 - y)]

        def rows(w, chip, core):
            half = ws[w].shape[0] // 2
            return outs[w].at[2 * chip[0] + chip[1], pl.ds(core * half, half), :]

        def copy(w, k, chip, core, to, own=False):
            half = ws[w].shape[0] // 2
            return pltpu.make_async_remote_copy(
                src_ref=ws[w].at[pl.ds(core * half, half), :] if own else rows(w, chip, core),
                dst_ref=rows(w, chip, core), send_sem=send_sems.at[6 * w + k], recv_sem=recv_sems.at[6 * w + k],
                device_id=to, device_id_type=MESH)

        first = [copy(w, j, (x, y), c, (*chip, c), own=True) for w in range(NW) for j, chip in enumerate(chips)]
        for cp in first:
            cp.start()
        passed = []
        for w in range(NW):
            for j, chip in enumerate(chips):
                copy(w, j, chip, c, (x, y, c)).wait_recv()
                passed.append(copy(w, 3 + j, chip, c, (x, y, 1 - c)))
                passed[-1].start()
        for w in range(NW):
            for j, chip in enumerate(chips):
                copy(w, 3 + j, chip, 1 - c, (x, y, c)).wait_recv()
        for cp in first + passed:
            cp.wait_send()

    return _comm_call(body, "gather_weights", shards,
                      [jax.ShapeDtypeStruct((4,) + w.shape, w.dtype) for w in shards], 6 * NW)


def _swap_halves(grads):
    def body(*refs):
        gs, gots = refs[:NW], refs[NW:2 * NW]
        send_sems, recv_sems = refs[2 * NW:]
        x, y, c = _place()
        copies = [pltpu.make_async_remote_copy(
            src_ref=gs[w].at[k, 1 - c], dst_ref=gots[w].at[k], send_sem=send_sems.at[4 * w + k],
            recv_sem=recv_sems.at[4 * w + k], device_id=(x, y, 1 - c), device_id_type=MESH)
            for w in range(NW) for k in range(4)]
        for cp in copies:
            cp.start()
        for cp in copies:
            cp.wait()

    return _comm_call(body, "swap_halves", grads,
                      [jax.ShapeDtypeStruct((4,) + g.shape[2:], g.dtype) for g in grads], 4 * NW)


def _share_with_sibling(arrs):
    na = len(arrs)

    def body(*refs):
        outs = refs[na:2 * na]
        send_sems, recv_sems = refs[2 * na:]
        x, y, c = _place()

        def half(ref):
            hr = ref.shape[0] // 2
            return ref.at[pl.ds(c * hr, hr), :]

        copies = [pltpu.make_async_remote_copy(
            src_ref=half(outs[k]), dst_ref=half(outs[k]), send_sem=send_sems.at[k], recv_sem=recv_sems.at[k],
            device_id=(x, y, 1 - c), device_id_type=MESH) for k in range(na)]
        for cp in copies:
            cp.start()
        for cp in copies:
            cp.wait()

    return _comm_call(body, "share_with_sibling", arrs, [jax.ShapeDtypeStruct(a.shape, a.dtype) for a in arrs],
                      na, aliases={k: k for k in range(na)})


def _norm_allreduce_adamw(part, w, m, v):
    r, lanes = part.shape

    def body(p_ref, w_ref, m_ref, v_ref, g_ref, d_ref, nm_ref, nv_ref, all_ref, send_sems, recv_sems):
        x, y, c = _place()
        me = 4 * x + 2 * y + c
        all_ref[me] = p_ref[...]
        copies = []
        for k in range(1, 8):
            peer = (x ^ (k >> 2), y ^ ((k >> 1) & 1), c ^ (k & 1))
            copies.append(pltpu.make_async_remote_copy(
                src_ref=p_ref, dst_ref=all_ref.at[me], send_sem=send_sems.at[k - 1], recv_sem=recv_sems.at[k - 1],
                device_id=peer, device_id_type=MESH))
        for cp in copies:
            cp.start()
        for cp in copies:
            cp.wait()
        g = all_ref[0]
        for k in range(1, 8):
            g = g + all_ref[k]
        g_ref[...] = g
        d_ref[...], nm_ref[...], nv_ref[...] = _adamw(w_ref[...], g, m_ref[...], v_ref[...])

    vm = pl.BlockSpec(memory_space=pltpu.VMEM)
    sd = jax.ShapeDtypeStruct((r, lanes), F32)
    return pl.pallas_call(
        body, name="norm_allreduce_adamw", in_specs=[vm] * 4, out_specs=[vm] * 4, out_shape=[sd] * 4,
        scratch_shapes=[pltpu.VMEM((8, r, lanes), F32), pltpu.SemaphoreType.DMA((7,)),
                        pltpu.SemaphoreType.DMA((7,))],
        compiler_params=pltpu.CompilerParams(has_side_effects=True),
    )(part, w, m, v)


LANES = 128


def _perm_in(shards):
    s0, s1, s2, s3 = shards
    w = D_IN // 4
    return jnp.concatenate([s0, s1, s2[:, :4096 - 2 * w], s3[:, 4928 - 3 * w:], s2[:, 4096 - 2 * w:],
                            s3[:, :4928 - 3 * w], jnp.zeros((s0.shape[0], D_INP - D_IN), s0.dtype)], axis=1)


def _by_chip_and_half(g, axis):
    rows, cols = g.shape
    if axis == 0:
        return g.reshape(4, 2, rows // 8, cols)
    return g.reshape(rows, 4, cols // 4).transpose(1, 0, 2).reshape(4, 2, rows // 2, cols // 4)


NORM_NAMES = ("pre_norm_w", "q_norm_w", "kv_norm_w", "post_norm_w")
NORM_SIZES = (D_MODEL, Q_RANK, KV_RANK, D_MODEL)
NORM_ROWS = 40


def _pack_norm(vs):
    flat = jnp.concatenate([v.reshape(-1) for v in vs])
    return jnp.pad(flat, (0, NORM_ROWS * LANES - flat.shape[0])).reshape(NORM_ROWS, LANES)


def _unpack_norm(p):
    flat, out, at = p.reshape(-1), [], 0
    for n in NORM_SIZES:
        out.append(flat[at:at + n].reshape(1, n))
        at += n
    return out


def _rope_tables(positions):
    inv_freq = ROPE_THETA ** (-jnp.arange(0, MLA_ROPE, 2, dtype=F32) / MLA_ROPE)
    ang = positions.astype(F32)[:, None] * inv_freq
    cos, sin, z = jnp.cos(ang), jnp.sin(ang), jnp.zeros_like(ang)
    return (jnp.concatenate([cos, cos, z, z], axis=1), jnp.concatenate([z, sin, z, z], axis=1),
            jnp.concatenate([-sin, z, z, z], axis=1))


def _local_step(x, positions, pre_w, win, q_w, wq, kv_w, wk, wv, wout, post_w, target):
    s = x.shape[0]
    rc, rs1, rs2 = _rope_tables(positions)
    h = _prenorm(x, pre_w)
    projb = _matmul(h, win, mode="nn", out_dtype=BF16, tm=512, tn=1024, tk=D_MODEL, name="proj_b", n=PB_W)
    projf = _matmul(h, win, mode="nn", out_dtype=F32, tm=512, tn=1024, tk=D_MODEL, name="proj_f", n=PF_W,
                    b_off=PB_W // 1024)
    oa, a_st, b_st = _sb_fwd(projb)
    nq, nkv, qm, km, vm = _mla_prep(projf, q_w, kv_w, wq, wk, wv, rc, rs1, rs2)
    ob, lse = _mla_fwd(qm, km, vm)
    mixed, dy, dout, err2, dpost = _out_post(oa, ob, projf, wout, x, target, post_w)

    dwout = _matmul(mixed, dy, mode="tn", out_dtype=F32, tm=1024, tn=1024, tk=min(4096, s), name="dw_out")
    dmixed = _matmul(dy, wout, mode="nt", out_dtype=F32, tm=512, tn=1024, tk=D_MODEL, name="d_mixed")
    dqa, dka, dva, dga = _sb_bwd(projb, projf, dmixed, oa, a_st, b_st)
    dqm, dkm, dvm, dgb = _mla_bwd(qm, km, vm, projf, dmixed, ob, lse)
    dqp, dkn, dlat, dqw, dkvw = _mla_bwd_post(dqm, dkm, dvm, projf, q_w, kv_w, wq, wk, wv, rc, rs1, rs2)
    tks = min(4096, s)
    dwq = _matmul(nq, dqp, mode="tn", out_dtype=F32, tm=Q_RANK, tn=1024, tk=tks, name="dw_q")
    dwk = _matmul(nkv, dkn, mode="tn", out_dtype=F32, tm=KV_RANK, tn=1024, tk=tks, name="dw_k")
    dwv = _matmul(nkv, dvm, mode="tn", out_dtype=F32, tm=KV_RANK, tn=1024, tk=tks, name="dw_v")
    lat = Q_RANK + KV_RANK + MLA_ROPE
    orig = jnp.concatenate([dqa, dka, dva, dga, dlat[:, :lat], dgb], axis=1)
    dwin = _dw_by_shard(h, jnp.stack(jnp.split(orig, 4, axis=1)))
    return err2, (dqa, dka, dva, dga, dgb, dlat), dout, dwin, dqw, dwq, dkvw, dwk, dwv, dwout, dpost


def _kernel_layouts(in_shards, w_q_up, w_kv_up, w_out):
    win = _perm_in(in_shards)
    wq = jnp.pad(w_q_up.reshape(Q_RANK, HEADS, 192), ((0, 0), (0, 0), (0, 64))).reshape(Q_RANK, HEADS * MLA_QK_PAD)
    kv = w_kv_up.reshape(KV_RANK, HEADS, 256)
    wk = kv[:, :, :128].reshape(KV_RANK, SB_WIDTH)
    wv = kv[:, :, 128:].reshape(KV_RANK, SB_WIDTH)
    return win, wq, wk, wv, w_out


def _original_layouts(dwq, dwk, dwv):
    dq = dwq.reshape(Q_RANK, HEADS, MLA_QK_PAD)[:, :, :192].reshape(Q_RANK, HEADS * 192)
    dkv = jnp.concatenate([dwk.reshape(KV_RANK, HEADS, 128), dwv.reshape(KV_RANK, HEADS, 128)], axis=2)
    return dq, dkv.reshape(KV_RANK, 2 * SB_WIDTH)


def kernel(x, positions, pre_norm_w, w_in, q_norm_w, w_q_up, kv_norm_w, w_kv_up, w_out, post_norm_w, loss_target, m_pre_norm_w, m_w_in, m_q_norm_w, m_w_q_up, m_kv_norm_w, m_w_kv_up, m_w_out, m_post_norm_w, v_pre_norm_w, v_w_in, v_q_norm_w, v_w_q_up, v_kv_norm_w, v_w_kv_up, v_w_out, v_post_norm_w):
    c = lax.axis_index("c")
    chip = 2 * lax.axis_index("x") + lax.axis_index("y")
    shards = (w_in[0], w_q_up[0], w_kv_up[0], w_out[0])
    mine16 = [w.astype(BF16) for w in shards]
    others = _gather_weights(mine16)
    slot = lambda w, k: jnp.where(chip == k, mine16[w], others[w][k])
    cat = lambda w, ax: jnp.concatenate([slot(w, k) for k in range(4)], axis=ax)
    win, wq, wk, wv, wout = _kernel_layouts([slot(0, k) for k in range(4)], cat(1, 1), cat(2, 1), cat(3, 0))

    err2, pieces, dout, dwin, dqw, dwq, dkvw, dwk, dwv, dwout, dpost = _local_step(
        x[0], positions[0], pre_norm_w, win, q_norm_w, wq, kv_norm_w, wk, wv, wout, post_norm_w, loss_target[0])
    loss = lax.psum(0.5 * jnp.sum(err2) / D_MODEL, ("x", "y", "c"))

    dq, dkv = _original_layouts(dwq, dwk, dwv)
    grads = [dwin.reshape(4, 2, D_MODEL // 2, D_IN // 4)] + [
        _by_chip_and_half(g, ax) for g, ax in ((dq, 1), (dkv, 1), (dwout, 0))]
    place = jnp.stack([c, chip])
    halves = _swap_halves(grads)
    pairs = [_pair_sum(place, grads[w], halves[w], "pair_sum_" + W_NAMES[w]) for w in range(NW)]
    gx, dpre, gots = _pre_bwd(pieces, win, x[0], pre_norm_w, dout, [p[0] for p in pairs])
    sums = _share_with_sibling([_sum_parts(c.reshape(1), pairs[w][1], gots[w], "sum_parts_" + W_NAMES[w])
                                for w in range(NW)])
    ms = (m_w_in[0], m_w_q_up[0], m_w_kv_up[0], m_w_out[0])
    vs = (v_w_in[0], v_w_q_up[0], v_w_kv_up[0], v_w_out[0])
    done = [_adamw_shard(sums[w], shards[w], ms[w], vs[w], "adamw_" + W_NAMES[w]) for w in range(NW)]
    big = [[done[w][k] for w in range(NW)] for k in range(4)]

    small = _norm_allreduce_adamw(
        _pack_norm([dpre, dqw, dkvw, dpost]), _pack_norm([pre_norm_w, q_norm_w, kv_norm_w, post_norm_w]),
        _pack_norm([m_pre_norm_w, m_q_norm_w, m_kv_norm_w, m_post_norm_w]),
        _pack_norm([v_pre_norm_w, v_q_norm_w, v_kv_norm_w, v_post_norm_w]))
    small = [_unpack_norm(p) for p in small]

    def group(k):
        n, b = small[k], big[k]
        return (n[0], b[0][None], n[1], b[1][None], n[2], b[2][None], b[3][None], n[3])

    return (loss, gx[None], *group(0), *group(1), *group(2), *group(3))
```

```python
import functools
import math

import numpy as np
import jax
import jax.numpy as jnp
from jax import lax
from jax.experimental import pallas as pl
from jax.experimental.pallas import tpu as pltpu

F32 = jnp.float32
BF16 = jnp.bfloat16
MESH = pl.DeviceIdType.MESH

D_MODEL = 2048
HEADS = 8
HEAD_DIM = 128
SB_WIDTH = HEADS * HEAD_DIM
MLA_ROPE = 64
MLA_QK_PAD = 256
Q_RANK = 512
KV_RANK = 256
CHUNK = 64
EPS = 1e-6
ROPE_THETA = 10000.0
D_IN = 5952
D_INP = 6144
PB_W = 3072
PF_W = D_INP - PB_W
SB_SCALE = 1.0 / math.sqrt(HEAD_DIM)
MLA_SCALE = 1.0 / math.sqrt(HEAD_DIM + MLA_ROPE)
NEG = -1e30

ADAM_LR, ADAM_B1, ADAM_B2, ADAM_EPS, ADAM_WD, ADAM_STEP = 0.001, 0.9, 0.999, 1e-08, 0.01, 10
ADAM_C1 = 1.0 - ADAM_B1 ** ADAM_STEP
ADAM_C2 = 1.0 - ADAM_B2 ** ADAM_STEP

ANY = pl.BlockSpec(memory_space=pl.ANY)
VMEM_LIMIT = 56 * 1024 * 1024
TQ = 1024
TK = 256
MLA_TK = 256
MLA_FWD_TK = 1024
UNROLL = 4


def _cp(sem=None, **kw):
    return pltpu.CompilerParams(dimension_semantics=sem, vmem_limit_bytes=VMEM_LIMIT, **kw)


def _dot(a, b, dims):
    return lax.dot_general(a, b, (dims, ((), ())), preferred_element_type=F32)


def _nn(a, b):
    return _dot(a, b, ((1,), (0,)))


def _nt(a, b):
    return _dot(a, b, ((1,), (1,)))


def _tn(a, b):
    return _dot(a, b, ((0,), (0,)))


def _rope(x, c, s1, s2):
    return x * c + pltpu.roll(x, 32, 1) * s1 + pltpu.roll(x, 96, 1) * s2


def _rope_t(x, c, s1, s2):
    return x * c - pltpu.roll(x, 32, 1) * s1 - pltpu.roll(x, 96, 1) * s2


def _silu_parts(g):
    sg = jax.nn.sigmoid(g)
    return g * sg, sg * (1.0 + g * (1.0 - sg))


def _matmul(a, b, *, mode, out_dtype, tm, tn, tk, name, n=None, b_off=0):
    if mode == "tn":
        kk, m = a.shape
        n = b.shape[1] if n is None else n
    else:
        m, kk = a.shape
        n = (b.shape[1] if mode == "nn" else b.shape[0]) if n is None else n
    nk = kk // tk
    a_spec = {"nn": pl.BlockSpec((tm, tk), lambda j, i, k: (i, k)),
              "nt": pl.BlockSpec((tm, tk), lambda j, i, k: (i, k)),
              "tn": pl.BlockSpec((tk, tm), lambda j, i, k: (k, i))}[mode]
    b_spec = {"nn": pl.BlockSpec((tk, tn), lambda j, i, k: (k, j + b_off)),
              "nt": pl.BlockSpec((tn, tk), lambda j, i, k: (j, k)),
              "tn": pl.BlockSpec((tk, tn), lambda j, i, k: (k, j))}[mode]
    dims = {"nn": ((1,), (0,)), "nt": ((1,), (1,)), "tn": ((0,), (0,))}[mode]

    def body(a_ref, b_ref, o_ref, acc_ref):
        k = pl.program_id(2)
        part = _dot(a_ref[...], b_ref[...], dims)
        if nk == 1:
            o_ref[...] = part.astype(out_dtype)
        else:
            @pl.when(k == 0)
            def _():
                acc_ref[...] = part

            @pl.when(k > 0)
            def _():
                acc_ref[...] += part

            @pl.when(k == nk - 1)
            def _():
                o_ref[...] = acc_ref[...].astype(out_dtype)

    return pl.pallas_call(
        body, name=name, grid=(n // tn, m // tm, nk),
        in_specs=[a_spec, b_spec], out_specs=pl.BlockSpec((tm, tn), lambda j, i, k: (i, j)),
        out_shape=jax.ShapeDtypeStruct((m, n), out_dtype),
        scratch_shapes=[pltpu.VMEM((tm, tn) if nk > 1 else (8, 128), F32)],
        compiler_params=_cp(("parallel", "parallel", "arbitrary")),
    )(a, b)


def _dw_by_shard(h, dp4, tm=1024, tk=2048):
    s, d = h.shape
    n, _, cols = dp4.shape
    tk = min(tk, s)
    nk = s // tk

    def body(a_ref, b_ref, o_ref, acc_ref):
        k = pl.program_id(2)
        part = _tn(a_ref[...], b_ref[...])

        @pl.when(k == 0)
        def _():
            acc_ref[...] = part

        @pl.when(k > 0)
        def _():
            acc_ref[...] += part

        @pl.when(k == nk - 1)
        def _():
            o_ref[...] = acc_ref[...]

    return pl.pallas_call(
        body, name="dw_in", grid=(n, d // tm, nk),
        in_specs=[pl.BlockSpec((tk, tm), lambda j, i, k: (k, i)),
                  pl.BlockSpec((None, tk, cols), lambda j, i, k: (j, k, 0))],
        out_specs=pl.BlockSpec((None, tm, cols), lambda j, i, k: (j, i, 0)),
        out_shape=jax.ShapeDtypeStruct((n, d, cols), F32),
        scratch_shapes=[pltpu.VMEM((tm, cols), F32)],
        compiler_params=_cp(("parallel", "parallel", "arbitrary")),
    )(h, dp4)


def _prenorm(x, w, tm=512):
    s, d = x.shape

    def body(x_ref, w_ref, h_ref):
        xv = x_ref[...]
        r = lax.rsqrt(jnp.mean(xv * xv, axis=1, keepdims=True) + EPS)
        h_ref[...] = ((xv * r) * w_ref[...]).astype(BF16)

    return pl.pallas_call(
        body, name="prenorm", grid=(s // tm,),
        in_specs=[pl.BlockSpec((tm, d), lambda i: (i, 0)), pl.BlockSpec((1, d), lambda i: (0, 0))],
        out_specs=pl.BlockSpec((tm, d), lambda i: (i, 0)),
        out_shape=jax.ShapeDtypeStruct((s, d), BF16),
        compiler_params=_cp(("parallel",)),
    )(x, w)


def _tri(n, cmp, value):
    row = lax.broadcasted_iota(jnp.int32, (n, n), 0)
    col = lax.broadcasted_iota(jnp.int32, (n, n), 1)
    return jnp.where(cmp(row, col), value, 0.0).astype(BF16)


def _softplus(z, mask):
    sp = jnp.maximum(z, 0.0) + jnp.log(1.0 + jnp.exp(-jnp.abs(z)))
    return sp if mask is None else jnp.where(mask, sp, 0.0)


def _tiles(s, tk=TK):
    tq = min(TQ, s)
    return tq, tk, math.gcd(tq // tk, UNROLL)


def _band_masks(tq, tk, fn):
    out = []
    for b in range(tq // tk):
        key = lax.broadcasted_iota(jnp.int32, (tk, tq - b * tk), 0) + b * tk
        qry = lax.broadcasted_iota(jnp.int32, (tk, tq - b * tk), 1) + b * tk
        out.append(fn(qry, key))
    return out


def _row_spec(tq):
    return pl.BlockSpec((8, tq), lambda h, i: (h, i))


def _sb_store_shape(s, tq, tk):
    nb, nq = tq // tk, s // tq
    return (HEADS, nb * nq * (nq + 1) // 2, tk, tq)


def _sb_fwd(projb):
    s = projb.shape[0]
    tq, tk, _ = _tiles(s)
    nb = tq // tk

    def body(q_ref, k_ref, v_ref, o_ref, a_st, b_st, acc_ref, car_ref, z_ref, a_buf, b_buf, sems):
        h, i = pl.program_id(0), pl.program_id(1)
        q = q_ref[...]
        from_here = _tri(tk, lambda s_, j: j >= s_, -1.0)
        masks = _band_masks(tq, tk, lambda t, s_: s_ < t)
        acc_ref[...] = jnp.zeros_like(acc_ref)
        car_ref[...] = jnp.zeros_like(car_ref)
        first = nb * (i * (i + 1) // 2)

        def stores(st, g):
            at = pl.ds(first + g * nb, nb)
            return (pltpu.make_async_copy(a_buf.at[st], a_st.at[h, at], sems.at[st, 0]),
                    pltpu.make_async_copy(b_buf.at[st], b_st.at[h, at], sems.at[st, 1]))

        def tile(j, mask, off, st, u):
            rows = pl.ds(pl.multiple_of(j * tk, tk), tk)
            z_ref[:, off:] = _nt(k_ref[rows, :], q[off:, :]) * SB_SCALE
            z = z_ref[:, off:]
            sp = _softplus(z, mask)
            a = jnp.exp(z + _nn(from_here, sp.astype(BF16)) + car_ref[:, off:])
            beta = jnp.exp(z - sp)
            if mask is not None:
                a = jnp.where(mask, a, 0.0)
                beta = jnp.where(mask, beta, 0.0)
            a = a.astype(BF16)
            if off:
                a_buf[st, u, :, :off] = jnp.zeros((tk, off), BF16)
                b_buf[st, u, :, :off] = jnp.zeros((tk, off), BF16)
            a_buf[st, u, :, off:] = a
            b_buf[st, u, :, off:] = beta.astype(BF16)
            acc_ref[:, off:] += _tn(v_ref[rows, :], a)
            car_ref[:, off:] -= jnp.sum(sp, axis=0, keepdims=True)

        for b in reversed(range(nb)):
            tile(i * nb + b, masks[b], b * tk, 0, b)
        for cp in stores(0, i):
            cp.start()

        def step(jj, c):
            st, g = (jj + 1) % 2, i - 1 - jj

            @pl.when(jj >= 1)
            def _():
                for cp in stores(st, g):
                    cp.wait()

            for u in reversed(range(nb)):
                tile(g * nb + u, None, 0, st, u)
            for cp in stores(st, g):
                cp.start()
            return c

        lax.fori_loop(0, i, step, 0)
        for cp in stores(0, 0):
            cp.wait()

        @pl.when(i >= 1)
        def _():
            for cp in stores(1, 0):
                cp.wait()

        o_ref[...] = acc_ref[...].T

    blk = pl.BlockSpec((tq, HEAD_DIM), lambda h, i: (i, h))
    st = jax.ShapeDtypeStruct(_sb_store_shape(s, tq, tk), BF16)
    return pl.pallas_call(
        body, name="sb_fwd", grid=(HEADS, s // tq),
        in_specs=[blk, pl.BlockSpec((s, HEAD_DIM), lambda h, i: (0, HEADS + h)),
                  pl.BlockSpec((s, HEAD_DIM), lambda h, i: (0, 2 * HEADS + h))],
        out_specs=[blk, ANY, ANY],
        out_shape=[jax.ShapeDtypeStruct((s, SB_WIDTH), F32), st, st],
        scratch_shapes=[pltpu.VMEM((HEAD_DIM, tq), F32), pltpu.VMEM((1, tq), F32), pltpu.VMEM((tk, tq), F32),
                        pltpu.VMEM((2, nb, tk, tq), BF16), pltpu.VMEM((2, nb, tk, tq), BF16),
                        pltpu.SemaphoreType.DMA((2, 2))],
        compiler_params=_cp(("arbitrary", "arbitrary")),
    )(projb, projb, projb)


def _mla_prep(projf, qw, kvw, wq, wk, wv, rc, rs1, rs2, tm=512):
    s = projf.shape[0]

    def body(cq_ref, ckv_ref, kr_ref, qw_ref, kvw_ref, wq_ref, wk_ref, wv_ref, c_ref, s1_ref, s2_ref,
             nq_ref, nkv_ref, q_ref, k_ref, v_ref):
        c, s1, s2 = c_ref[...], s1_ref[...], s2_ref[...]
        cq = cq_ref[...]
        nq = ((cq * lax.rsqrt(jnp.mean(cq * cq, axis=1, keepdims=True) + EPS)) * qw_ref[...]).astype(BF16)
        nq_ref[...] = nq
        qf = _nn(nq, wq_ref[...])
        ckv = ckv_ref[...]
        nkv = ((ckv * lax.rsqrt(jnp.mean(ckv * ckv, axis=1, keepdims=True) + EPS)) * kvw_ref[...]).astype(BF16)
        nkv_ref[...] = nkv
        kn = _nn(nkv, wk_ref[...])
        v_ref[...] = _nn(nkv, wv_ref[...]).astype(BF16)
        krot = _rope(kr_ref[...], c, s1, s2).astype(BF16)
        for h in range(HEADS):
            lo = h * MLA_QK_PAD
            q_ref[:, lo:lo + 128] = qf[:, lo:lo + 128].astype(BF16)
            q_ref[:, lo + 128:lo + 256] = _rope(qf[:, lo + 128:lo + 256], c, s1, s2).astype(BF16)
            k_ref[:, lo:lo + 128] = kn[:, h * 128:(h + 1) * 128].astype(BF16)
            k_ref[:, lo + 128:lo + 256] = krot

    row = lambda w, b: pl.BlockSpec((tm, w), lambda i: (i, b))
    full = lambda a: pl.BlockSpec(a.shape, lambda i: (0, 0))
    return pl.pallas_call(
        body, name="mla_prep", grid=(s // tm,),
        in_specs=[row(Q_RANK, 2048 // Q_RANK), row(KV_RANK, 2560 // KV_RANK), row(128, 2816 // 128),
                  full(qw), full(kvw), full(wq), full(wk), full(wv), row(128, 0), row(128, 0), row(128, 0)],
        out_specs=[row(Q_RANK, 0), row(KV_RANK, 0), row(HEADS * MLA_QK_PAD, 0), row(HEADS * MLA_QK_PAD, 0),
                   row(SB_WIDTH, 0)],
        out_shape=[jax.ShapeDtypeStruct((s, Q_RANK), BF16), jax.ShapeDtypeStruct((s, KV_RANK), BF16),
                   jax.ShapeDtypeStruct((s, HEADS * MLA_QK_PAD), BF16),
                   jax.ShapeDtypeStruct((s, HEADS * MLA_QK_PAD), BF16),
                   jax.ShapeDtypeStruct((s, SB_WIDTH), BF16)],
        compiler_params=_cp(("parallel",)),
    )(projf, projf, projf, qw, kvw, wq, wk, wv, rc, rs1, rs2)


def _mla_mask(qry, key):
    return (key // CHUNK) <= (qry // CHUNK)


def _mla_fwd(qm, km, vm):
    s = qm.shape[0]
    tq, tk, unroll = _tiles(s, min(MLA_FWD_TK, s))
    nb = tq // tk

    def body(q_ref, k_ref, v_ref, o_ref, lse_ref, acc_ref, m_ref, l_ref):
        i = pl.program_id(1)
        q = q_ref[...]
        masks = _band_masks(tq, tk, _mla_mask)
        acc_ref[...] = jnp.zeros_like(acc_ref)
        m_ref[...] = jnp.full_like(m_ref, NEG)
        l_ref[...] = jnp.zeros_like(l_ref)

        def tile(j, mask, off):
            rows = pl.ds(pl.multiple_of(j * tk, tk), tk)
            sc = _nt(k_ref[rows, :], q[off:, :]) * MLA_SCALE
            if mask is not None:
                sc = jnp.where(mask, sc, NEG)
            m_old = m_ref[:, off:]
            m_new = jnp.maximum(m_old, jnp.max(sc, axis=0, keepdims=True))
            p = jnp.exp(sc - m_new)
            alpha = jnp.exp(m_old - m_new)
            l_ref[:, off:] = alpha * l_ref[:, off:] + jnp.sum(p, axis=0, keepdims=True)
            acc_ref[:, off:] = alpha * acc_ref[:, off:] + _tn(v_ref[rows, :], p.astype(BF16))
            m_ref[:, off:] = m_new

        for b in range(nb):
            tile(i * nb + b, masks[b], b * tk)

        def step(jj, c):
            for u in range(unroll):
                tile(jj * unroll + u, None, 0)
            return c

        lax.fori_loop(0, i * nb // unroll, step, 0)
        o_ref[...] = (acc_ref[...] / l_ref[...]).T
        lse_ref[...] = jnp.broadcast_to(m_ref[...] + jnp.log(l_ref[...]), (8, tq))

    return pl.pallas_call(
        body, name="mla_fwd", grid=(HEADS, s // tq),
        in_specs=[pl.BlockSpec((tq, MLA_QK_PAD), lambda h, i: (i, h)),
                  pl.BlockSpec((s, MLA_QK_PAD), lambda h, i: (0, h)),
                  pl.BlockSpec((s, HEAD_DIM), lambda h, i: (0, h))],
        out_specs=[pl.BlockSpec((tq, HEAD_DIM), lambda h, i: (i, h)), _row_spec(tq)],
        out_shape=[jax.ShapeDtypeStruct((s, SB_WIDTH), F32), jax.ShapeDtypeStruct((8 * HEADS, s), F32)],
        scratch_shapes=[pltpu.VMEM((HEAD_DIM, tq), F32), pltpu.VMEM((1, tq), F32), pltpu.VMEM((1, tq), F32)],
        compiler_params=_cp(("parallel", "arbitrary")),
    )(qm, km, vm)


def _out_post(oa, ob, projf, wout, x, target, pw, tm=256):
    s, d = x.shape

    def body(oa_ref, ob_ref, ga_ref, gb_ref, w_ref, x_ref, t_ref, pw_ref,
             mix_ref, dy_ref, dout_ref, loss_ref, dpw_ref):
        i = pl.program_id(0)
        sa, _ = _silu_parts(ga_ref[...])
        sb, _ = _silu_parts(gb_ref[...])
        mix_ref[:, :SB_WIDTH] = (oa_ref[...] * sa).astype(BF16)
        mix_ref[:, SB_WIDTH:] = (ob_ref[...] * sb).astype(BF16)
        y = _nn(mix_ref[...], w_ref[...])
        r = lax.rsqrt(jnp.mean(y * y, axis=1, keepdims=True) + EPS)
        yhat = y * r
        pwv = pw_ref[...]
        err = (x_ref[...] + yhat * pwv) - t_ref[...]
        dout = err * (1.0 / d)
        dout_ref[...] = dout
        g = dout * pwv
        dy_ref[...] = (r * (g - yhat * jnp.mean(g * yhat, axis=1, keepdims=True))).astype(BF16)

        @pl.when(i == 0)
        def _():
            loss_ref[...] = jnp.zeros_like(loss_ref)
            dpw_ref[...] = jnp.zeros_like(dpw_ref)

        loss_ref[...] += jnp.sum(err * err, axis=0, keepdims=True)
        dpw_ref[...] += jnp.sum(dout * yhat, axis=0, keepdims=True)

    row = lambda w, b: pl.BlockSpec((tm, w), lambda i: (i, b))
    vec = pl.BlockSpec((1, d), lambda i: (0, 0))
    return pl.pallas_call(
        body, name="out_post", grid=(s // tm,),
        in_specs=[row(SB_WIDTH, 0), row(SB_WIDTH, 0), row(SB_WIDTH, 0), row(SB_WIDTH, 1),
                  pl.BlockSpec(wout.shape, lambda i: (0, 0)), row(d, 0), row(d, 0), vec],
        out_specs=[row(d, 0), row(d, 0), row(d, 0), vec, vec],
        out_shape=[jax.ShapeDtypeStruct((s, d), BF16), jax.ShapeDtypeStruct((s, d), BF16),
                   jax.ShapeDtypeStruct((s, d), F32), jax.ShapeDtypeStruct((1, d), F32),
                   jax.ShapeDtypeStruct((1, d), F32)],
        compiler_params=_cp(("arbitrary",)),
    )(oa, ob, projf, projf, wout, x, target, pw)


def _sb_bwd(projb, projf, dmixed, oa, a_st, b_st):
    s = projb.shape[0]
    tq, tk, _ = _tiles(s)
    nb, nq = tq // tk, s // tq

    def body(q_ref, k_ref, v_ref, dm_ref, g_ref, o_ref, a_st, b_st, dq_ref, dk_ref, dv_ref, dg_ref,
             dka_ref, dva_ref, dqa_ref, cg_ref, a_buf, b_buf, sems):
        h, i = pl.program_id(0), pl.program_id(1)

        @pl.when(i == 0)
        def _():
            dka_ref[...] = jnp.zeros_like(dka_ref)
            dva_ref[...] = jnp.zeros_like(dva_ref)

        q = q_ref[...]
        silu, dsilu = _silu_parts(g_ref[...])
        dm = dm_ref[...]
        dg_ref[...] = (dm * o_ref[...] * dsilu).astype(BF16)
        do = (dm * silu).astype(BF16)
        up_to_here = _tri(tk, lambda s_, j: j <= s_, 1.0)
        dqa_ref[...] = jnp.zeros_like(dqa_ref)
        cg_ref[...] = jnp.zeros_like(cg_ref)
        first = nb * (i * (i + 1) // 2)

        def loads(st, g):
            at = pl.ds(first + g * nb, nb)
            return (pltpu.make_async_copy(a_st.at[h, at], a_buf.at[st], sems.at[st, 0]),
                    pltpu.make_async_copy(b_st.at[h, at], b_buf.at[st], sems.at[st, 1]))

        def tile(j, off, st, u):
            rows = pl.ds(pl.multiple_of(j * tk, tk), tk)
            k = k_ref[rows, :]
            a = a_buf[st, u, :, off:]
            g = a.astype(F32) * _nt(v_ref[rows, :], do[off:, :])
            dva_ref[rows, :] += _nn(a, do[off:, :])
            cum = _nn(up_to_here, g.astype(BF16)) + cg_ref[:, off:]
            dz = ((g - b_buf[st, u, :, off:].astype(F32) * cum) * SB_SCALE).astype(BF16)
            dqa_ref[:, off:] += _tn(k, dz)
            dka_ref[rows, :] += _nn(dz, q[off:, :])
            cg_ref[:, off:] += jnp.sum(g, axis=0, keepdims=True)

        for cp in loads(0, 0):
            cp.start()

        def step(g, c):
            st = g % 2
            for cp in loads(1 - st, g + 1):
                cp.start()
            for cp in loads(st, g):
                cp.wait()
            for u in range(nb):
                tile(g * nb + u, 0, st, u)
            return c

        lax.fori_loop(0, i, step, 0)
        for cp in loads(i % 2, i):
            cp.wait()
        for b in range(nb):
            tile(i * nb + b, b * tk, i % 2, b)
        dq_ref[...] = dqa_ref[...].T.astype(BF16)

        @pl.when(i == nq - 1)
        def _():
            dk_ref[...] = dka_ref[...].astype(BF16)
            dv_ref[...] = dva_ref[...].astype(BF16)

    blk = lambda off: pl.BlockSpec((tq, HEAD_DIM), lambda h, i: (i, off + h))
    whole = lambda off: pl.BlockSpec((s, HEAD_DIM), lambda h, i: (0, off + h))
    o_sd = jax.ShapeDtypeStruct((s, SB_WIDTH), BF16)
    return pl.pallas_call(
        body, name="sb_bwd", grid=(HEADS, nq),
        in_specs=[blk(0), whole(HEADS), whole(2 * HEADS), blk(0), blk(0), blk(0), ANY, ANY],
        out_specs=[blk(0), whole(0), whole(0), blk(0)],
        out_shape=[o_sd, o_sd, o_sd, o_sd],
        scratch_shapes=[pltpu.VMEM((s, HEAD_DIM), F32), pltpu.VMEM((s, HEAD_DIM), F32),
                        pltpu.VMEM((HEAD_DIM, tq), F32), pltpu.VMEM((1, tq), F32),
                        pltpu.VMEM((2, nb, tk, tq), BF16), pltpu.VMEM((2, nb, tk, tq), BF16),
                        pltpu.SemaphoreType.DMA((2, 2))],
        compiler_params=_cp(("arbitrary", "arbitrary")),
    )(projb, projb, projb, dmixed, projf, oa, a_st, b_st)


def _mla_bwd(qm, km, vm, projf, dmixed, ob, lse):
    s = qm.shape[0]
    tq, tk, unroll = _tiles(s, MLA_TK)
    nb, nq = tq // tk, s // tq

    def body(q_ref, k_ref, v_ref, dm_ref, g_ref, o_ref, lse_ref, dq_ref, dk_ref, dv_ref, dg_ref,
             dva_ref, dqa_ref):
        i = pl.program_id(1)

        @pl.when(i == 0)
        def _():
            dk_ref[...] = jnp.zeros_like(dk_ref)
            dva_ref[...] = jnp.zeros_like(dva_ref)

        q = q_ref[...]
        silu, dsilu = _silu_parts(g_ref[...])
        dm = dm_ref[...]
        o = o_ref[...]
        dg_ref[...] = (dm * o * dsilu).astype(BF16)
        dof = dm * silu
        delta = jnp.sum((dof * o).T, axis=0, keepdims=True)
        do = dof.astype(BF16)
        lse = lse_ref[0:1, :]
        masks = _band_masks(tq, tk, _mla_mask)
        dqa_ref[...] = jnp.zeros_like(dqa_ref)

        def tile(j, mask, off):
            rows = pl.ds(pl.multiple_of(j * tk, tk), tk)
            k = k_ref[rows, :]
            p = jnp.exp(_nt(k, q[off:, :]) * MLA_SCALE - lse[:, off:])
            if mask is not None:
                p = jnp.where(mask, p, 0.0)
            ds = (p * (_nt(v_ref[rows, :], do[off:, :]) - delta[:, off:]) * MLA_SCALE).astype(BF16)
            dva_ref[rows, :] += _nn(p.astype(BF16), do[off:, :])
            dk_ref[rows, :] += _nn(ds, q[off:, :])
            dqa_ref[:, off:] += _tn(k, ds)

        def step(jj, c):
            for u in range(unroll):
                tile(jj * unroll + u, None, 0)
            return c

        lax.fori_loop(0, i * nb // unroll, step, 0)
        for b in range(nb):
            tile(i * nb + b, masks[b], b * tk)
        dq_ref[...] = dqa_ref[...].T

        @pl.when(i == nq - 1)
        def _():
            dv_ref[...] = dva_ref[...].astype(BF16)

    blk = lambda w, off: pl.BlockSpec((tq, w), lambda h, i: (i, off + h))
    whole = lambda w: pl.BlockSpec((s, w), lambda h, i: (0, h))
    return pl.pallas_call(
        body, name="mla_bwd", grid=(HEADS, nq),
        in_specs=[blk(MLA_QK_PAD, 0), whole(MLA_QK_PAD), whole(HEAD_DIM), blk(HEAD_DIM, HEADS),
                  blk(HEAD_DIM, HEADS), blk(HEAD_DIM, 0), _row_spec(tq)],
        out_specs=[blk(MLA_QK_PAD, 0), whole(MLA_QK_PAD), whole(HEAD_DIM), blk(HEAD_DIM, 0)],
        out_shape=[jax.ShapeDtypeStruct((s, HEADS * MLA_QK_PAD), F32),
                   jax.ShapeDtypeStruct((s, HEADS * MLA_QK_PAD), F32),
                   jax.ShapeDtypeStruct((s, SB_WIDTH), BF16), jax.ShapeDtypeStruct((s, SB_WIDTH), BF16)],
        scratch_shapes=[pltpu.VMEM((s, HEAD_DIM), F32), pltpu.VMEM((MLA_QK_PAD, tq), F32)],
        compiler_params=_cp(("parallel", "arbitrary")),
    )(qm, km, vm, dmixed, projf, ob, lse)


def _norm_bwd(x, w, dn):
    r = lax.rsqrt(jnp.mean(x * x, axis=1, keepdims=True) + EPS)
    xhat = x * r
    g = dn * w
    return r * (g - xhat * jnp.mean(g * xhat, axis=1, keepdims=True)), dn * xhat


def _mla_bwd_post(dqm, dkm, dvm, projf, qw, kvw, wq, wk, wv, rc, rs1, rs2, tm=256):
    s = dqm.shape[0]

    def body(dq_ref, dk_ref, dv_ref, cq_ref, ckv_ref, qw_ref, kvw_ref, wq_ref, wk_ref, wv_ref,
             c_ref, s1_ref, s2_ref, dqp_ref, dkn_ref, dlat_ref, dqw_ref, dkvw_ref):
        i = pl.program_id(0)
        c, s1, s2 = c_ref[...], s1_ref[...], s2_ref[...]
        drot = jnp.zeros((tm, 128), F32)
        for h in range(HEADS):
            lo = h * MLA_QK_PAD
            dqp_ref[:, lo:lo + 128] = dq_ref[:, lo:lo + 128].astype(BF16)
            dqp_ref[:, lo + 128:lo + 256] = _rope_t(dq_ref[:, lo + 128:lo + 256], c, s1, s2).astype(BF16)
            dkn_ref[:, h * 128:(h + 1) * 128] = dk_ref[:, lo:lo + 128].astype(BF16)
            drot = drot + dk_ref[:, lo + 128:lo + 256]
        dlat_ref[:, Q_RANK + KV_RANK:Q_RANK + KV_RANK + 128] = _rope_t(drot, c, s1, s2).astype(BF16)
        dlat_ref[:, Q_RANK + KV_RANK + 128:] = jnp.zeros((tm, 128), BF16)
        dcq, dqw = _norm_bwd(cq_ref[...], qw_ref[...], _nt(dqp_ref[...], wq_ref[...]))
        dlat_ref[:, :Q_RANK] = dcq.astype(BF16)
        dnkv = _nt(dkn_ref[...], wk_ref[...]) + _nt(dv_ref[...], wv_ref[...])
        dckv, dkvw = _norm_bwd(ckv_ref[...], kvw_ref[...], dnkv)
        dlat_ref[:, Q_RANK:Q_RANK + KV_RANK] = dckv.astype(BF16)

        @pl.when(i == 0)
        def _():
            dqw_ref[...] = jnp.zeros_like(dqw_ref)
            dkvw_ref[...] = jnp.zeros_like(dkvw_ref)

        dqw_ref[...] += jnp.sum(dqw, axis=0, keepdims=True)
        dkvw_ref[...] += jnp.sum(dkvw, axis=0, keepdims=True)

    row = lambda w, b: pl.BlockSpec((tm, w), lambda i: (i, b))
    full = lambda a: pl.BlockSpec(a.shape, lambda i: (0, 0))
    sd = jax.ShapeDtypeStruct
    return pl.pallas_call(
        body, name="mla_bwd_post", grid=(s // tm,),
        in_specs=[row(HEADS * MLA_QK_PAD, 0), row(HEADS * MLA_QK_PAD, 0), row(SB_WIDTH, 0),
                  row(Q_RANK, 2048 // Q_RANK), row(KV_RANK, 2560 // KV_RANK),
                  full(qw), full(kvw), full(wq), full(wk), full(wv), row(128, 0), row(128, 0), row(128, 0)],
        out_specs=[row(HEADS * MLA_QK_PAD, 0), row(SB_WIDTH, 0), row(1024, 0),
                   pl.BlockSpec((1, Q_RANK), lambda i: (0, 0)), pl.BlockSpec((1, KV_RANK), lambda i: (0, 0))],
        out_shape=[sd((s, HEADS * MLA_QK_PAD), BF16), sd((s, SB_WIDTH), BF16), sd((s, 1024), BF16),
                   sd((1, Q_RANK), F32), sd((1, KV_RANK), F32)],
        compiler_params=_cp(("arbitrary",)),
    )(dqm, dkm, dvm, projf, projf, qw, kvw, wq, wk, wv, rc, rs1, rs2)


def _pre_bwd(pieces, win, x, pw, dout, psums, tm=256):
    s, d = x.shape
    npc, pw_ = len(pieces), pieces[0].shape[1]
    nw, nsteps = len(psums), s // tm

    def body(*refs):
        dps, (w_ref, x_ref, pw_ref, do_ref), refs = refs[:npc], refs[npc:npc + 4], refs[npc + 4:]
        ps, (dx_ref, dpw_ref), gots = refs[:nw], refs[nw:nw + 2], refs[nw + 2:2 * nw + 2]
        send_sems, recv_sems = refs[2 * nw + 2:]
        i = pl.program_id(0)

        def copies():
            if not nw:
                return []
            px, py, pc = _place()
            chips = [(1 - px, py), (px, 1 - py), (1 - px, 1 - py)]
            return [pltpu.make_async_remote_copy(
                src_ref=ps[w].at[2 * chip[0] + chip[1]], dst_ref=gots[w].at[j], send_sem=send_sems.at[3 * w + j],
                recv_sem=recv_sems.at[3 * w + j], device_id=(*chip, pc), device_id_type=MESH)
                for w in range(nw) for j, chip in enumerate(chips)]

        @pl.when(i == 0)
        def _():
            dpw_ref[...] = jnp.zeros_like(dpw_ref)
            for cp in copies():
                cp.start()

        dh = _nt(dps[0][...], w_ref[:, :pw_])
        for p in range(1, npc):
            dh = dh + _nt(dps[p][...], w_ref[:, p * pw_:(p + 1) * pw_])
        dx, dw = _norm_bwd(x_ref[...], pw_ref[...], dh)
        dx_ref[...] = do_ref[...] + dx
        dpw_ref[...] += jnp.sum(dw, axis=0, keepdims=True)

        @pl.when(i == nsteps - 1)
        def _():
            for cp in copies():
                cp.wait()

    rowd = pl.BlockSpec((tm, d), lambda i: (i, 0))
    vec = pl.BlockSpec((1, d), lambda i: (0, 0))
    out = pl.pallas_call(
        body, name="pre_bwd", grid=(nsteps,),
        in_specs=[pl.BlockSpec((tm, pw_), lambda i: (i, 0))] * npc
        + [pl.BlockSpec(win.shape, lambda i: (0, 0), pipeline_mode=pl.Buffered(1)), rowd, vec, rowd] + [ANY] * nw,
        out_specs=[rowd, vec] + [ANY] * nw,
        out_shape=[jax.ShapeDtypeStruct((s, d), F32), jax.ShapeDtypeStruct((1, d), F32)]
        + [jax.ShapeDtypeStruct((3,) + p.shape[1:], p.dtype) for p in psums],
        scratch_shapes=[pltpu.SemaphoreType.DMA((max(3 * nw, 1),)), pltpu.SemaphoreType.DMA((max(3 * nw, 1),))],
        compiler_params=_cp(("arbitrary",), has_side_effects=True),
    )(*pieces, win, x, pw, dout, *psums)
    return out[0], out[1], list(out[2:])


def _adamw(w, g, m, v):
    m = ADAM_B1 * m + (1.0 - ADAM_B1) * g
    v = ADAM_B2 * v + (1.0 - ADAM_B2) * (g * g)
    delta = -ADAM_LR * ((m / ADAM_C1) / (jnp.sqrt(v / ADAM_C2) + ADAM_EPS) + ADAM_WD * w)
    return delta, m, v


def _row_block(rows):
    return math.gcd(rows, 128)


def _sum_parts(core, own, got, name):
    n, hr, cols = got.shape
    tr = _row_block(hr)
    nblk = hr // tr

    def body(c_ref, o_ref, p_ref, g_ref):
        g = o_ref[...]
        for k in range(n):
            g = g + p_ref[k].astype(F32)
        g_ref[...] = g

    return pl.pallas_call(
        body, name=name,
        grid_spec=pltpu.PrefetchScalarGridSpec(
            num_scalar_prefetch=1, grid=(nblk,),
            in_specs=[pl.BlockSpec((tr, cols), lambda i, c: (i, 0)),
                      pl.BlockSpec((n, tr, cols), lambda i, c: (0, i, 0))],
            out_specs=pl.BlockSpec((tr, cols), lambda i, c: (c[0] * nblk + i, 0))),
        out_shape=jax.ShapeDtypeStruct((2 * hr, cols), F32),
        compiler_params=_cp(("parallel",)),
    )(core, own, got)


def _adamw_shard(g, w, m, v, name):
    rows, cols = g.shape
    tr = _row_block(rows)

    def body(g_ref, w_ref, m_ref, v_ref, go_ref, d_ref, nm_ref, nv_ref):
        g_ = g_ref[...]
        go_ref[...] = g_
        d_ref[...], nm_ref[...], nv_ref[...] = _adamw(w_ref[...], g_, m_ref[...], v_ref[...])

    blk = pl.BlockSpec((tr, cols), lambda i: (i, 0))
    sd = jax.ShapeDtypeStruct((rows, cols), F32)
    return pl.pallas_call(
        body, name=name, grid=(rows // tr,), in_specs=[blk] * 4, out_specs=[blk] * 4, out_shape=[sd] * 4,
        compiler_params=_cp(("parallel",)),
    )(g, w, m, v)


def _pair_sum(place, grads, got, name):
    n, _, hr, cols = grads.shape
    tr = _row_block(hr)

    def body(p_ref, a_ref, b_ref, oa_ref, ob_ref, all_ref, own_ref):
        all_ref[...] = (a_ref[:, 0] + b_ref[...]).astype(BF16)
        own_ref[...] = oa_ref[0, 0] + ob_ref[0]

    return pl.pallas_call(
        body, name=name,
        grid_spec=pltpu.PrefetchScalarGridSpec(
            num_scalar_prefetch=1, grid=(hr // tr,),
            in_specs=[pl.BlockSpec((n, 1, tr, cols), lambda i, p: (0, p[0], i, 0)),
                      pl.BlockSpec((n, tr, cols), lambda i, p: (0, i, 0)),
                      pl.BlockSpec((1, 1, tr, cols), lambda i, p: (p[1], p[0], i, 0)),
                      pl.BlockSpec((1, tr, cols), lambda i, p: (p[1], i, 0))],
            out_specs=[pl.BlockSpec((n, tr, cols), lambda i, p: (0, i, 0)),
                       pl.BlockSpec((tr, cols), lambda i, p: (i, 0))]),
        out_shape=[jax.ShapeDtypeStruct((n, hr, cols), BF16), jax.ShapeDtypeStruct((hr, cols), F32)],
        compiler_params=_cp(("parallel",)),
    )(place, grads, got, grads, got)


def _place():
    return lax.axis_index("x"), lax.axis_index("y"), lax.axis_index("c")


W_NAMES = ("w_in", "w_q_up", "w_kv_up", "w_out")
NW = len(W_NAMES)


def _comm_call(body, name, ins, out_shape, n_copies, aliases=None):
    return pl.pallas_call(
        body, name=name, in_specs=[ANY] * len(ins), out_specs=[ANY] * len(out_shape), out_shape=out_shape,
        input_output_aliases=aliases or {},
        scratch_shapes=[pltpu.SemaphoreType.DMA((n_copies,)), pltpu.SemaphoreType.DMA((n_copies,))],
        compiler_params=pltpu.CompilerParams(has_side_effects=True),
    )(*ins)


def _gather_weights(shards):
    def body(*refs):
        ws, outs = refs[:NW], refs[NW:2 * NW]
        send_sems, recv_sems = refs[2 * NW:]
        x, y, c = _place()
        chips = [(1 - x, y), (x, 1 - y), (1 - x, 1 - y)]

        def rows(w, chip, core):
            half = ws[w].shape[0] // 2
            return outs[w].at[2 * chip[0] + chip[1], pl.ds(core * half, half), :]

        def copy(w, k, chip, core, to, own=False):
            half = ws[w].shape[0] // 2
            return pltpu.make_async_remote_copy(
                src_ref=ws[w].at[pl.ds(core * half, half), :] if own else rows(w, chip, core),
                dst_ref=rows(w, chip, core), send_sem=send_sems.at[6 * w + k], recv_sem=recv_sems.at[6 * w + k],
                device_id=to, device_id_type=MESH)

        first = [copy(w, j, (x, y), c, (*chip, c), own=True) for w in range(NW) for j, chip in enumerate(chips)]
        for cp in first:
            cp.start()
        passed = []
        for w in range(NW):
            for j, chip in enumerate(chips):
                copy(w, j, chip, c, (x, y, c)).wait_recv()
                passed.append(copy(w, 3 + j, chip, c, (x, y, 1 - c)))
                passed[-1].start()
        for w in range(NW):
            for j, chip in enumerate(chips):
                copy(w, 3 + j, chip, 1 - c, (x, y, c)).wait_recv()
        for cp in first + passed:
            cp.wait_send()

    return _comm_call(body, "gather_weights", shards,
                      [jax.ShapeDtypeStruct((4,) + w.shape, w.dtype) for w in shards], 6 * NW)


def _swap_halves(grads):
    def body(*refs):
        gs, gots = refs[:NW], refs[NW:2 * NW]
        send_sems, recv_sems = refs[2 * NW:]
        x, y, c = _place()
        copies = [pltpu.make_async_remote_copy(
            src_ref=gs[w].at[k, 1 - c], dst_ref=gots[w].at[k], send_sem=send_sems.at[4 * w + k],
            recv_sem=recv_sems.at[4 * w + k], device_id=(x, y, 1 - c), device_id_type=MESH)
            for w in range(NW) for k in range(4)]
        for cp in copies:
            cp.start()
        for cp in copies:
            cp.wait()

    return _comm_call(body, "swap_halves", grads,
                      [jax.ShapeDtypeStruct((4,) + g.shape[2:], g.dtype) for g in grads], 4 * NW)


def _share_with_sibling(arrs):
    na = len(arrs)

    def body(*refs):
        outs = refs[na:2 * na]
        send_sems, recv_sems = refs[2 * na:]
        x, y, c = _place()

        def half(ref):
            hr = ref.shape[0] // 2
            return ref.at[pl.ds(c * hr, hr), :]

        copies = [pltpu.make_async_remote_copy(
            src_ref=half(outs[k]), dst_ref=half(outs[k]), send_sem=send_sems.at[k], recv_sem=recv_sems.at[k],
            device_id=(x, y, 1 - c), device_id_type=MESH) for k in range(na)]
        for cp in copies:
            cp.start()
        for cp in copies:
            cp.wait()

    return _comm_call(body, "share_with_sibling", arrs, [jax.ShapeDtypeStruct(a.shape, a.dtype) for a in arrs],
                      na, aliases={k: k for k in range(na)})


def _norm_allreduce_adamw(part, w, m, v):
    r, lanes = part.shape

    def body(p_ref, w_ref, m_ref, v_ref, g_ref, d_ref, nm_ref, nv_ref, all_ref, send_sems, recv_sems):
        x, y, c = _place()
        me = 4 * x + 2 * y + c
        all_ref[me] = p_ref[...]
        copies = []
        for k in range(1, 8):
            peer = (x ^ (k >> 2), y ^ ((k >> 1) & 1), c ^ (k & 1))
            copies.append(pltpu.make_async_remote_copy(
                src_ref=p_ref, dst_ref=all_ref.at[me], send_sem=send_sems.at[k - 1], recv_sem=recv_sems.at[k - 1],
                device_id=peer, device_id_type=MESH))
        for cp in copies:
            cp.start()
        for cp in copies:
            cp.wait()
        g = all_ref[0]
        for k in range(1, 8):
            g = g + all_ref[k]
        g_ref[...] = g
        d_ref[...], nm_ref[...], nv_ref[...] = _adamw(w_ref[...], g, m_ref[...], v_ref[...])

    vm = pl.BlockSpec(memory_space=pltpu.VMEM)
    sd = jax.ShapeDtypeStruct((r, lanes), F32)
    return pl.pallas_call(
        body, name="norm_allreduce_adamw", in_specs=[vm] * 4, out_specs=[vm] * 4, out_shape=[sd] * 4,
        scratch_shapes=[pltpu.VMEM((8, r, lanes), F32), pltpu.SemaphoreType.DMA((7,)),
                        pltpu.SemaphoreType.DMA((7,))],
        compiler_params=pltpu.CompilerParams(has_side_effects=True),
    )(part, w, m, v)


LANES = 128


def _perm_in(shards):
    s0, s1, s2, s3 = shards
    w = D_IN // 4
    return jnp.concatenate([s0, s1, s2[:, :4096 - 2 * w], s3[:, 4928 - 3 * w:], s2[:, 4096 - 2 * w:],
                            s3[:, :4928 - 3 * w], jnp.zeros((s0.shape[0], D_INP - D_IN), s0.dtype)], axis=1)


def _by_chip_and_half(g, axis):
    rows, cols = g.shape
    if axis == 0:
        return g.reshape(4, 2, rows // 8, cols)
    return g.reshape(rows, 4, cols // 4).transpose(1, 0, 2).reshape(4, 2, rows // 2, cols // 4)


NORM_NAMES = ("pre_norm_w", "q_norm_w", "kv_norm_w", "post_norm_w")
NORM_SIZES = (D_MODEL, Q_RANK, KV_RANK, D_MODEL)
NORM_ROWS = 40


def _pack_norm(vs):
    flat = jnp.concatenate([v.reshape(-1) for v in vs])
    return jnp.pad(flat, (0, NORM_ROWS * LANES - flat.shape[0])).reshape(NORM_ROWS, LANES)


def _unpack_norm(p):
    flat, out, at = p.reshape(-1), [], 0
    for n in NORM_SIZES:
        out.append(flat[at:at + n].reshape(1, n))
        at += n
    return out


def _rope_tables(positions):
    inv_freq = ROPE_THETA ** (-jnp.arange(0, MLA_ROPE, 2, dtype=F32) / MLA_ROPE)
    ang = positions.astype(F32)[:, None] * inv_freq
    cos, sin, z = jnp.cos(ang), jnp.sin(ang), jnp.zeros_like(ang)
    return (jnp.concatenate([cos, cos, z, z], axis=1), jnp.concatenate([z, sin, z, z], axis=1),
            jnp.concatenate([-sin, z, z, z], axis=1))


def _local_step(x, positions, pre_w, win, q_w, wq, kv_w, wk, wv, wout, post_w, target):
    s = x.shape[0]
    rc, rs1, rs2 = _rope_tables(positions)
    h = _prenorm(x, pre_w)
    projb = _matmul(h, win, mode="nn", out_dtype=BF16, tm=512, tn=1024, tk=D_MODEL, name="proj_b", n=PB_W)
    projf = _matmul(h, win, mode="nn", out_dtype=F32, tm=512, tn=1024, tk=D_MODEL, name="proj_f", n=PF_W,
                    b_off=PB_W // 1024)
    oa, a_st, b_st = _sb_fwd(projb)
    nq, nkv, qm, km, vm = _mla_prep(projf, q_w, kv_w, wq, wk, wv, rc, rs1, rs2)
    ob, lse = _mla_fwd(qm, km, vm)
    mixed, dy, dout, err2, dpost = _out_post(oa, ob, projf, wout, x, target, post_w)

    dwout = _matmul(mixed, dy, mode="tn", out_dtype=F32, tm=1024, tn=1024, tk=min(4096, s), name="dw_out")
    dmixed = _matmul(dy, wout, mode="nt", out_dtype=F32, tm=512, tn=1024, tk=D_MODEL, name="d_mixed")
    dqa, dka, dva, dga = _sb_bwd(projb, projf, dmixed, oa, a_st, b_st)
    dqm, dkm, dvm, dgb = _mla_bwd(qm, km, vm, projf, dmixed, ob, lse)
    dqp, dkn, dlat, dqw, dkvw = _mla_bwd_post(dqm, dkm, dvm, projf, q_w, kv_w, wq, wk, wv, rc, rs1, rs2)
    tks = min(4096, s)
    dwq = _matmul(nq, dqp, mode="tn", out_dtype=F32, tm=Q_RANK, tn=1024, tk=tks, name="dw_q")
    dwk = _matmul(nkv, dkn, mode="tn", out_dtype=F32, tm=KV_RANK, tn=1024, tk=tks, name="dw_k")
    dwv = _matmul(nkv, dvm, mode="tn", out_dtype=F32, tm=KV_RANK, tn=1024, tk=tks, name="dw_v")
    lat = Q_RANK + KV_RANK + MLA_ROPE
    orig = jnp.concatenate([dqa, dka, dva, dga, dlat[:, :lat], dgb], axis=1)
    dwin = _dw_by_shard(h, jnp.stack(jnp.split(orig, 4, axis=1)))
    return err2, (dqa, dka, dva, dga, dgb, dlat), dout, dwin, dqw, dwq, dkvw, dwk, dwv, dwout, dpost


def _kernel_layouts(in_shards, w_q_up, w_kv_up, w_out):
    win = _perm_in(in_shards)
    wq = jnp.pad(w_q_up.reshape(Q_RANK, HEADS, 192), ((0, 0), (0, 0), (0, 64))).reshape(Q_RANK, HEADS * MLA_QK_PAD)
    kv = w_kv_up.reshape(KV_RANK, HEADS, 256)
    wk = kv[:, :, :128].reshape(KV_RANK, SB_WIDTH)
    wv = kv[:, :, 128:].reshape(KV_RANK, SB_WIDTH)
    return win, wq, wk, wv, w_out


def _original_layouts(dwq, dwk, dwv):
    dq = dwq.reshape(Q_RANK, HEADS, MLA_QK_PAD)[:, :, :192].reshape(Q_RANK, HEADS * 192)
    dkv = jnp.concatenate([dwk.reshape(KV_RANK, HEADS, 128), dwv.reshape(KV_RANK, HEADS, 128)], axis=2)
    return dq, dkv.reshape(KV_RANK, 2 * SB_WIDTH)


def kernel(x, positions, pre_norm_w, w_in, q_norm_w, w_q_up, kv_norm_w, w_kv_up, w_out, post_norm_w, loss_target, m_pre_norm_w, m_w_in, m_q_norm_w, m_w_q_up, m_kv_norm_w, m_w_kv_up, m_w_out, m_post_norm_w, v_pre_norm_w, v_w_in, v_q_norm_w, v_w_q_up, v_kv_norm_w, v_w_kv_up, v_w_out, v_post_norm_w):
    c = lax.axis_index("c")
    chip = 2 * lax.axis_index("x") + lax.axis_index("y")
    shards = (w_in[0], w_q_up[0], w_kv_up[0], w_out[0])
    mine16 = [w.astype(BF16) for w in shards]
    others = _gather_weights(mine16)
    slot = lambda w, k: jnp.where(chip == k, mine16[w], others[w][k])
    cat = lambda w, ax: jnp.concatenate([slot(w, k) for k in range(4)], axis=ax)
    win, wq, wk, wv, wout = _kernel_layouts([slot(0, k) for k in range(4)], cat(1, 1), cat(2, 1), cat(3, 0))

    err2, pieces, dout, dwin, dqw, dwq, dkvw, dwk, dwv, dwout, dpost = _local_step(
        x[0], positions[0], pre_norm_w, win, q_norm_w, wq, kv_norm_w, wk, wv, wout, post_norm_w, loss_target[0])
    loss = lax.psum(0.5 * jnp.sum(err2) / D_MODEL, ("x", "y", "c"))

    dq, dkv = _original_layouts(dwq, dwk, dwv)
    grads = [dwin.reshape(4, 2, D_MODEL // 2, D_IN // 4)] + [
        _by_chip_and_half(g, ax) for g, ax in ((dq, 1), (dkv, 1), (dwout, 0))]
    place = jnp.stack([c, chip])
    halves = _swap_halves(grads)
    pairs = [_pair_sum(place, grads[w], halves[w], "pair_sum_" + W_NAMES[w]) for w in range(NW)]
    gx, dpre, gots = _pre_bwd(pieces, win, x[0], pre_norm_w, dout, [p[0] for p in pairs])
    sums = _share_with_sibling([_sum_parts(c.reshape(1), pairs[w][1], gots[w], "sum_parts_" + W_NAMES[w])
                                for w in range(NW)])
    ms = (m_w_in[0], m_w_q_up[0], m_w_kv_up[0], m_w_out[0])
    vs = (v_w_in[0], v_w_q_up[0], v_w_kv_up[0], v_w_out[0])
    done = [_adamw_shard(sums[w], shards[w], ms[w], vs[w], "adamw_" + W_NAMES[w]) for w in range(NW)]
    big = [[done[w][k] for w in range(NW)] for k in range(4)]

    small = _norm_allreduce_adamw(
        _pack_norm([dpre, dqw, dkvw, dpost]), _pack_norm([pre_norm_w, q_norm_w, kv_norm_w, post_norm_w]),
        _pack_norm([m_pre_norm_w, m_q_norm_w, m_kv_norm_w, m_post_norm_w]),
        _pack_norm([v_pre_norm_w, v_q_norm_w, v_kv_norm_w, v_post_norm_w]))
    small = [_unpack_norm(p) for p in small]

    def group(k):
        n, b = small[k], big[k]
        return (n[0], b[0][None], n[1], b[1][None], n[2], b[2][None], b[3][None], n[3])

    return (loss, gx[None], *group(0), *group(1), *group(2), *group(3))
```

```python
import functools
import math

import numpy as np
import jax
import jax.numpy as jnp
from jax import lax
from jax.experimental import pallas as pl
from jax.experimental.pallas import tpu as pltpu

F32 = jnp.float32
BF16 = jnp.bfloat16
MESH = pl.DeviceIdType.MESH

D_MODEL = 2048
HEADS = 8
HEAD_DIM = 128
SB_WIDTH = HEADS * HEAD_DIM
MLA_ROPE = 64
MLA_QK_PAD = 256
Q_RANK = 512
KV_RANK = 256
CHUNK = 64
EPS = 1e-6
ROPE_THETA = 10000.0
D_IN = 5952
D_INP = 6144
PB_W = 3072
PF_W = D_INP - PB_W
SB_SCALE = 1.0 / math.sqrt(HEAD_DIM)
MLA_SCALE = 1.0 / math.sqrt(HEAD_DIM + MLA_ROPE)
NEG = -1e30

ADAM_LR, ADAM_B1, ADAM_B2, ADAM_EPS, ADAM_WD, ADAM_STEP = 0.001, 0.9, 0.999, 1e-08, 0.01, 10
ADAM_C1 = 1.0 - ADAM_B1 ** ADAM_STEP
ADAM_C2 = 1.0 - ADAM_B2 ** ADAM_STEP

ANY = pl.BlockSpec(memory_space=pl.ANY)
VMEM_LIMIT = 56 * 1024 * 1024
TQ = 1024
TK = 256
MLA_TK = 256
MLA_FWD_TK = 1024
UNROLL = 4


def _cp(sem=None, **kw):
    return pltpu.CompilerParams(dimension_semantics=sem, vmem_limit_bytes=VMEM_LIMIT, **kw)


def _dot(a, b, dims):
    return lax.dot_general(a, b, (dims, ((), ())), preferred_element_type=F32)


def _nn(a, b):
    return _dot(a, b, ((1,), (0,)))


def _nt(a, b):
    return _dot(a, b, ((1,), (1,)))


def _tn(a, b):
    return _dot(a, b, ((0,), (0,)))


def _rope(x, c, s1, s2):
    return x * c + pltpu.roll(x, 32, 1) * s1 + pltpu.roll(x, 96, 1) * s2


def _rope_t(x, c, s1, s2):
    return x * c - pltpu.roll(x, 32, 1) * s1 - pltpu.roll(x, 96, 1) * s2


def _silu_parts(g):
    sg = jax.nn.sigmoid(g)
    return g * sg, sg * (1.0 + g * (1.0 - sg))


def _matmul(a, b, *, mode, out_dtype, tm, tn, tk, name, n=None, b_off=0):
    if mode == "tn":
        kk, m = a.shape
        n = b.shape[1] if n is None else n
    else:
        m, kk = a.shape
        n = (b.shape[1] if mode == "nn" else b.shape[0]) if n is None else n
    nk = kk // tk
    a_spec = {"nn": pl.BlockSpec((tm, tk), lambda j, i, k: (i, k)),
              "nt": pl.BlockSpec((tm, tk), lambda j, i, k: (i, k)),
              "tn": pl.BlockSpec((tk, tm), lambda j, i, k: (k, i))}[mode]
    b_spec = {"nn": pl.BlockSpec((tk, tn), lambda j, i, k: (k, j + b_off)),
              "nt": pl.BlockSpec((tn, tk), lambda j, i, k: (j, k)),
              "tn": pl.BlockSpec((tk, tn), lambda j, i, k: (k, j))}[mode]
    dims = {"nn": ((1,), (0,)), "nt": ((1,), (1,)), "tn": ((0,), (0,))}[mode]

    def body(a_ref, b_ref, o_ref, acc_ref):
        k = pl.program_id(2)
        part = _dot(a_ref[...], b_ref[...], dims)
        if nk == 1:
            o_ref[...] = part.astype(out_dtype)
        else:
            @pl.when(k == 0)
            def _():
                acc_ref[...] = part

            @pl.when(k > 0)
            def _():
                acc_ref[...] += part

            @pl.when(k == nk - 1)
            def _():
                o_ref[...] = acc_ref[...].astype(out_dtype)

    return pl.pallas_call(
        body, name=name, grid=(n // tn, m // tm, nk),
        in_specs=[a_spec, b_spec], out_specs=pl.BlockSpec((tm, tn), lambda j, i, k: (i, j)),
        out_shape=jax.ShapeDtypeStruct((m, n), out_dtype),
        scratch_shapes=[pltpu.VMEM((tm, tn) if nk > 1 else (8, 128), F32)],
        compiler_params=_cp(("parallel", "parallel", "arbitrary")),
    )(a, b)


def _dw_by_shard(h, dp4, tm=1024, tk=2048):
    s, d = h.shape
    n, _, cols = dp4.shape
    tk = min(tk, s)
    nk = s // tk

    def body(a_ref, b_ref, o_ref, acc_ref):
        k = pl.program_id(2)
        part = _tn(a_ref[...], b_ref[...])

        @pl.when(k == 0)
        def _():
            acc_ref[...] = part

        @pl.when(k > 0)
        def _():
            acc_ref[...] += part

        @pl.when(k == nk - 1)
        def _():
            o_ref[...] = acc_ref[...]

    return pl.pallas_call(
        body, name="dw_in", grid=(n, d // tm, nk),
        in_specs=[pl.BlockSpec((tk, tm), lambda j, i, k: (k, i)),
                  pl.BlockSpec((None, tk, cols), lambda j, i, k: (j, k, 0))],
        out_specs=pl.BlockSpec((None, tm, cols), lambda j, i, k: (j, i, 0)),
        out_shape=jax.ShapeDtypeStruct((n, d, cols), F32),
        scratch_shapes=[pltpu.VMEM((tm, cols), F32)],
        compiler_params=_cp(("parallel", "parallel", "arbitrary")),
    )(h, dp4)


def _prenorm(x, w, tm=512):
    s, d = x.shape

    def body(x_ref, w_ref, h_ref):
        xv = x_ref[...]
        r = lax.rsqrt(jnp.mean(xv * xv, axis=1, keepdims=True) + EPS)
        h_ref[...] = ((xv * r) * w_ref[...]).astype(BF16)

    return pl.pallas_call(
        body, name="prenorm", grid=(s // tm,),
        in_specs=[pl.BlockSpec((tm, d), lambda i: (i, 0)), pl.BlockSpec((1, d), lambda i: (0, 0))],
        out_specs=pl.BlockSpec((tm, d), lambda i: (i, 0)),
        out_shape=jax.ShapeDtypeStruct((s, d), BF16),
        compiler_params=_cp(("parallel",)),
    )(x, w)


def _tri(n, cmp, value):
    row = lax.broadcasted_iota(jnp.int32, (n, n), 0)
    col = lax.broadcasted_iota(jnp.int32, (n, n), 1)
    return jnp.where(cmp(row, col), value, 0.0).astype(BF16)


def _softplus(z, mask):
    sp = jnp.maximum(z, 0.0) + jnp.log(1.0 + jnp.exp(-jnp.abs(z)))
    return sp if mask is None else jnp.where(mask, sp, 0.0)


def _tiles(s, tk=TK):
    tq = min(TQ, s)
    return tq, tk, math.gcd(tq // tk, UNROLL)


def _band_masks(tq, tk, fn):
    out = []
    for b in range(tq // tk):
        key = lax.broadcasted_iota(jnp.int32, (tk, tq - b * tk), 0) + b * tk
        qry = lax.broadcasted_iota(jnp.int32, (tk, tq - b * tk), 1) + b * tk
        out.append(fn(qry, key))
    return out


def _row_spec(tq):
    return pl.BlockSpec((8, tq), lambda h, i: (h, i))


def _sb_store_shape(s, tq, tk):
    nb, nq = tq // tk, s // tq
    return (HEADS, nb * nq * (nq + 1) // 2, tk, tq)


def _sb_fwd(projb):
    s = projb.shape[0]
    tq, tk, _ = _tiles(s)
    nb = tq // tk

    def body(q_ref, k_ref, v_ref, o_ref, a_st, b_st, acc_ref, car_ref, z_ref, a_buf, b_buf, sems):
        h, i = pl.program_id(0), pl.program_id(1)
        q = q_ref[...]
        from_here = _tri(tk, lambda s_, j: j >= s_, -1.0)
        masks = _band_masks(tq, tk, lambda t, s_: s_ < t)
        acc_ref[...] = jnp.zeros_like(acc_ref)
        car_ref[...] = jnp.zeros_like(car_ref)
        first = nb * (i * (i + 1) // 2)

        def stores(st, g):
            at = pl.ds(first + g * nb, nb)
            return (pltpu.make_async_copy(a_buf.at[st], a_st.at[h, at], sems.at[st, 0]),
                    pltpu.make_async_copy(b_buf.at[st], b_st.at[h, at], sems.at[st, 1]))

        def tile(j, mask, off, st, u):
            rows = pl.ds(pl.multiple_of(j * tk, tk), tk)
            z_ref[:, off:] = _nt(k_ref[rows, :], q[off:, :]) * SB_SCALE
            z = z_ref[:, off:]
            sp = _softplus(z, mask)
            a = jnp.exp(z + _nn(from_here, sp.astype(BF16)) + car_ref[:, off:])
            beta = jnp.exp(z - sp)
            if mask is not None:
                a = jnp.where(mask, a, 0.0)
                beta = jnp.where(mask, beta, 0.0)
            a = a.astype(BF16)
            if off:
                a_buf[st, u, :, :off] = jnp.zeros((tk, off), BF16)
                b_buf[st, u, :, :off] = jnp.zeros((tk, off), BF16)
            a_buf[st, u, :, off:] = a
            b_buf[st, u, :, off:] = beta.astype(BF16)
            acc_ref[:, off:] += _tn(v_ref[rows, :], a)
            car_ref[:, off:] -= jnp.sum(sp, axis=0, keepdims=True)

        for b in reversed(range(nb)):
            tile(i * nb + b, masks[b], b * tk, 0, b)
        for cp in stores(0, i):
            cp.start()

        def step(jj, c):
            st, g = (jj + 1) % 2, i - 1 - jj

            @pl.when(jj >= 1)
            def _():
                for cp in stores(st, g):
                    cp.wait()

            for u in reversed(range(nb)):
                tile(g * nb + u, None, 0, st, u)
            for cp in stores(st, g):
                cp.start()
            return c

        lax.fori_loop(0, i, step, 0)
        for cp in stores(0, 0):
            cp.wait()

        @pl.when(i >= 1)
        def _():
            for cp in stores(1, 0):
                cp.wait()

        o_ref[...] = acc_ref[...].T

    blk = pl.BlockSpec((tq, HEAD_DIM), lambda h, i: (i, h))
    st = jax.ShapeDtypeStruct(_sb_store_shape(s, tq, tk), BF16)
    return pl.pallas_call(
        body, name="sb_fwd", grid=(HEADS, s // tq),
        in_specs=[blk, pl.BlockSpec((s, HEAD_DIM), lambda h, i: (0, HEADS + h)),
                  pl.BlockSpec((s, HEAD_DIM), lambda h, i: (0, 2 * HEADS + h))],
        out_specs=[blk, ANY, ANY],
        out_shape=[jax.ShapeDtypeStruct((s, SB_WIDTH), F32), st, st],
        scratch_shapes=[pltpu.VMEM((HEAD_DIM, tq), F32), pltpu.VMEM((1, tq), F32), pltpu.VMEM((tk, tq), F32),
                        pltpu.VMEM((2, nb, tk, tq), BF16), pltpu.VMEM((2, nb, tk, tq), BF16),
                        pltpu.SemaphoreType.DMA((2, 2))],
        compiler_params=_cp(("arbitrary", "arbitrary")),
    )(projb, projb, projb)


def _mla_prep(projf, qw, kvw, wq, wk, wv, rc, rs1, rs2, tm=512):
    s = projf.shape[0]

    def body(cq_ref, ckv_ref, kr_ref, qw_ref, kvw_ref, wq_ref, wk_ref, wv_ref, c_ref, s1_ref, s2_ref,
             nq_ref, nkv_ref, q_ref, k_ref, v_ref):
        c, s1, s2 = c_ref[...], s1_ref[...], s2_ref[...]
        cq = cq_ref[...]
        nq = ((cq * lax.rsqrt(jnp.mean(cq * cq, axis=1, keepdims=True) + EPS)) * qw_ref[...]).astype(BF16)
        nq_ref[...] = nq
        qf = _nn(nq, wq_ref[...])
        ckv = ckv_ref[...]
        nkv = ((ckv * lax.rsqrt(jnp.mean(ckv * ckv, axis=1, keepdims=True) + EPS)) * kvw_ref[...]).astype(BF16)
        nkv_ref[...] = nkv
        kn = _nn(nkv, wk_ref[...])
        v_ref[...] = _nn(nkv, wv_ref[...]).astype(BF16)
        krot = _rope(kr_ref[...], c, s1, s2).astype(BF16)
        for h in range(HEADS):
            lo = h * MLA_QK_PAD
            q_ref[:, lo:lo + 128] = qf[:, lo:lo + 128].astype(BF16)
            q_ref[:, lo + 128:lo + 256] = _rope(qf[:, lo + 128:lo + 256], c, s1, s2).astype(BF16)
            k_ref[:, lo:lo + 128] = kn[:, h * 128:(h + 1) * 128].astype(BF16)
            k_ref[:, lo + 128:lo + 256] = krot

    row = lambda w, b: pl.BlockSpec((tm, w), lambda i: (i, b))
    full = lambda a: pl.BlockSpec(a.shape, lambda i: (0, 0))
    return pl.pallas_call(
        body, name="mla_prep", grid=(s // tm,),
        in_specs=[row(Q_RANK, 2048 // Q_RANK), row(KV_RANK, 2560 // KV_RANK), row(128, 2816 // 128),
                  full(qw), full(kvw), full(wq), full(wk), full(wv), row(128, 0), row(128, 0), row(128, 0)],
        out_specs=[row(Q_RANK, 0), row(KV_RANK, 0), row(HEADS * MLA_QK_PAD, 0), row(HEADS * MLA_QK_PAD, 0),
                   row(SB_WIDTH, 0)],
        out_shape=[jax.ShapeDtypeStruct((s, Q_RANK), BF16), jax.ShapeDtypeStruct((s, KV_RANK), BF16),
                   jax.ShapeDtypeStruct((s, HEADS * MLA_QK_PAD), BF16),
                   jax.ShapeDtypeStruct((s, HEADS * MLA_QK_PAD), BF16),
                   jax.ShapeDtypeStruct((s, SB_WIDTH), BF16)],
        compiler_params=_cp(("parallel",)),
    )(projf, projf, projf, qw, kvw, wq, wk, wv, rc, rs1, rs2)


def _mla_mask(qry, key):
    return (key // CHUNK) <= (qry // CHUNK)


def _mla_fwd(qm, km, vm):
    s = qm.shape[0]
    tq, tk, unroll = _tiles(s, min(MLA_FWD_TK, s))
    nb = tq // tk

    def body(q_ref, k_ref, v_ref, o_ref, lse_ref, acc_ref, m_ref, l_ref):
        i = pl.program_id(1)
        q = q_ref[...]
        masks = _band_masks(tq, tk, _mla_mask)
        acc_ref[...] = jnp.zeros_like(acc_ref)
        m_ref[...] = jnp.full_like(m_ref, NEG)
        l_ref[...] = jnp.zeros_like(l_ref)

        def tile(j, mask, off):
            rows = pl.ds(pl.multiple_of(j * tk, tk), tk)
            sc = _nt(k_ref[rows, :], q[off:, :]) * MLA_SCALE
            if mask is not None:
                sc = jnp.where(mask, sc, NEG)
            m_old = m_ref[:, off:]
            m_new = jnp.maximum(m_old, jnp.max(sc, axis=0, keepdims=True))
            p = jnp.exp(sc - m_new)
            alpha = jnp.exp(m_old - m_new)
            l_ref[:, off:] = alpha * l_ref[:, off:] + jnp.sum(p, axis=0, keepdims=True)
            acc_ref[:, off:] = alpha * acc_ref[:, off:] + _tn(v_ref[rows, :], p.astype(BF16))
            m_ref[:, off:] = m_new

        for b in range(nb):
            tile(i * nb + b, masks[b], b * tk)

        def step(jj, c):
            for u in range(unroll):
                tile(jj * unroll + u, None, 0)
            return c

        lax.fori_loop(0, i * nb // unroll, step, 0)
        o_ref[...] = (acc_ref[...] / l_ref[...]).T
        lse_ref[...] = jnp.broadcast_to(m_ref[...] + jnp.log(l_ref[...]), (8, tq))

    return pl.pallas_call(
        body, name="mla_fwd", grid=(HEADS, s // tq),
        in_specs=[pl.BlockSpec((tq, MLA_QK_PAD), lambda h, i: (i, h)),
                  pl.BlockSpec((s, MLA_QK_PAD), lambda h, i: (0, h)),
                  pl.BlockSpec((s, HEAD_DIM), lambda h, i: (0, h))],
        out_specs=[pl.BlockSpec((tq, HEAD_DIM), lambda h, i: (i, h)), _row_spec(tq)],
        out_shape=[jax.ShapeDtypeStruct((s, SB_WIDTH), F32), jax.ShapeDtypeStruct((8 * HEADS, s), F32)],
        scratch_shapes=[pltpu.VMEM((HEAD_DIM, tq), F32), pltpu.VMEM((1, tq), F32), pltpu.VMEM((1, tq), F32)],
        compiler_params=_cp(("parallel", "arbitrary")),
    )(qm, km, vm)


def _out_post(oa, ob, projf, wout, x, target, pw, tm=256):
    s, d = x.shape

    def body(oa_ref, ob_ref, ga_ref, gb_ref, w_ref, x_ref, t_ref, pw_ref,
             mix_ref, dy_ref, dout_ref, loss_ref, dpw_ref):
        i = pl.program_id(0)
        sa, _ = _silu_parts(ga_ref[...])
        sb, _ = _silu_parts(gb_ref[...])
        mix_ref[:, :SB_WIDTH] = (oa_ref[...] * sa).astype(BF16)
        mix_ref[:, SB_WIDTH:] = (ob_ref[...] * sb).astype(BF16)
        y = _nn(mix_ref[...], w_ref[...])
        r = lax.rsqrt(jnp.mean(y * y, axis=1, keepdims=True) + EPS)
        yhat = y * r
        pwv = pw_ref[...]
        err = (x_ref[...] + yhat * pwv) - t_ref[...]
        dout = err * (1.0 / d)
        dout_ref[...] = dout
        g = dout * pwv
        dy_ref[...] = (r * (g - yhat * jnp.mean(g * yhat, axis=1, keepdims=True))).astype(BF16)

        @pl.when(i == 0)
        def _():
            loss_ref[...] = jnp.zeros_like(loss_ref)
            dpw_ref[...] = jnp.zeros_like(dpw_ref)

        loss_ref[...] += jnp.sum(err * err, axis=0, keepdims=True)
        dpw_ref[...] += jnp.sum(dout * yhat, axis=0, keepdims=True)

    row = lambda w, b: pl.BlockSpec((tm, w), lambda i: (i, b))
    vec = pl.BlockSpec((1, d), lambda i: (0, 0))
    return pl.pallas_call(
        body, name="out_post", grid=(s // tm,),
        in_specs=[row(SB_WIDTH, 0), row(SB_WIDTH, 0), row(SB_WIDTH, 0), row(SB_WIDTH, 1),
                  pl.BlockSpec(wout.shape, lambda i: (0, 0)), row(d, 0), row(d, 0), vec],
        out_specs=[row(d, 0), row(d, 0), row(d, 0), vec, vec],
        out_shape=[jax.ShapeDtypeStruct((s, d), BF16), jax.ShapeDtypeStruct((s, d), BF16),
                   jax.ShapeDtypeStruct((s, d), F32), jax.ShapeDtypeStruct((1, d), F32),
                   jax.ShapeDtypeStruct((1, d), F32)],
        compiler_params=_cp(("arbitrary",)),
    )(oa, ob, projf, projf, wout, x, target, pw)


def _sb_bwd(projb, projf, dmixed, oa, a_st, b_st):
    s = projb.shape[0]
    tq, tk, _ = _tiles(s)
    nb, nq = tq // tk, s // tq

    def body(q_ref, k_ref, v_ref, dm_ref, g_ref, o_ref, a_st, b_st, dq_ref, dk_ref, dv_ref, dg_ref,
             dka_ref, dva_ref, dqa_ref, cg_ref, a_buf, b_buf, sems):
        h, i = pl.program_id(0), pl.program_id(1)

        @pl.when(i == 0)
        def _():
            dka_ref[...] = jnp.zeros_like(dka_ref)
            dva_ref[...] = jnp.zeros_like(dva_ref)

        q = q_ref[...]
        silu, dsilu = _silu_parts(g_ref[...])
        dm = dm_ref[...]
        dg_ref[...] = (dm * o_ref[...] * dsilu).astype(BF16)
        do = (dm * silu).astype(BF16)
        up_to_here = _tri(tk, lambda s_, j: j <= s_, 1.0)
        dqa_ref[...] = jnp.zeros_like(dqa_ref)
        cg_ref[...] = jnp.zeros_like(cg_ref)
        first = nb * (i * (i + 1) // 2)

        def loads(st, g):
            at = pl.ds(first + g * nb, nb)
            return (pltpu.make_async_copy(a_st.at[h, at], a_buf.at[st], sems.at[st, 0]),
                    pltpu.make_async_copy(b_st.at[h, at], b_buf.at[st], sems.at[st, 1]))

        def tile(j, off, st, u):
            rows = pl.ds(pl.multiple_of(j * tk, tk), tk)
            k = k_ref[rows, :]
            a = a_buf[st, u, :, off:]
            g = a.astype(F32) * _nt(v_ref[rows, :], do[off:, :])
            dva_ref[rows, :] += _nn(a, do[off:, :])
            cum = _nn(up_to_here, g.astype(BF16)) + cg_ref[:, off:]
            dz = ((g - b_buf[st, u, :, off:].astype(F32) * cum) * SB_SCALE).astype(BF16)
            dqa_ref[:, off:] += _tn(k, dz)
            dka_ref[rows, :] += _nn(dz, q[off:, :])
            cg_ref[:, off:] += jnp.sum(g, axis=0, keepdims=True)

        for cp in loads(0, 0):
            cp.start()

        def step(g, c):
            st = g % 2
            for cp in loads(1 - st, g + 1):
                cp.start()
            for cp in loads(st, g):
                cp.wait()
            for u in range(nb):
                tile(g * nb + u, 0, st, u)
            return c

        lax.fori_loop(0, i, step, 0)
        for cp in loads(i % 2, i):
            cp.wait()
        for b in range(nb):
            tile(i * nb + b, b * tk, i % 2, b)
        dq_ref[...] = dqa_ref[...].T.astype(BF16)

        @pl.when(i == nq - 1)
        def _():
            dk_ref[...] = dka_ref[...].astype(BF16)
            dv_ref[...] = dva_ref[...].astype(BF16)

    blk = lambda off: pl.BlockSpec((tq, HEAD_DIM), lambda h, i: (i, off + h))
    whole = lambda off: pl.BlockSpec((s, HEAD_DIM), lambda h, i: (0, off + h))
    o_sd = jax.ShapeDtypeStruct((s, SB_WIDTH), BF16)
    return pl.pallas_call(
        body, name="sb_bwd", grid=(HEADS, nq),
        in_specs=[blk(0), whole(HEADS), whole(2 * HEADS), blk(0), blk(0), blk(0), ANY, ANY],
        out_specs=[blk(0), whole(0), whole(0), blk(0)],
        out_shape=[o_sd, o_sd, o_sd, o_sd],
        scratch_shapes=[pltpu.VMEM((s, HEAD_DIM), F32), pltpu.VMEM((s, HEAD_DIM), F32),
                        pltpu.VMEM((HEAD_DIM, tq), F32), pltpu.VMEM((1, tq), F32),
                        pltpu.VMEM((2, nb, tk, tq), BF16), pltpu.VMEM((2, nb, tk, tq), BF16),
                        pltpu.SemaphoreType.DMA((2, 2))],
        compiler_params=_cp(("arbitrary", "arbitrary")),
    )(projb, projb, projb, dmixed, projf, oa, a_st, b_st)


def _mla_bwd(qm, km, vm, projf, dmixed, ob, lse):
    s = qm.shape[0]
    tq, tk, unroll = _tiles(s, MLA_TK)
    nb, nq = tq // tk, s // tq

    def body(q_ref, k_ref, v_ref, dm_ref, g_ref, o_ref, lse_ref, dq_ref, dk_ref, dv_ref, dg_ref,
             dva_ref, dqa_ref):
        i = pl.program_id(1)

        @pl.when(i == 0)
        def _():
            dk_ref[...] = jnp.zeros_like(dk_ref)
            dva_ref[...] = jnp.zeros_like(dva_ref)

        q = q_ref[...]
        silu, dsilu = _silu_parts(g_ref[...])
        dm = dm_ref[...]
        o = o_ref[...]
        dg_ref[...] = (dm * o * dsilu).astype(BF16)
        dof = dm * silu
        delta = jnp.sum((dof * o).T, axis=0, keepdims=True)
        do = dof.astype(BF16)
        lse = lse_ref[0:1, :]
        masks = _band_masks(tq, tk, _mla_mask)
        dqa_ref[...] = jnp.zeros_like(dqa_ref)

        def tile(j, mask, off):
            rows = pl.ds(pl.multiple_of(j * tk, tk), tk)
            k = k_ref[rows, :]
            p = jnp.exp(_nt(k, q[off:, :]) * MLA_SCALE - lse[:, off:])
            if mask is not None:
                p = jnp.where(mask, p, 0.0)
            ds = (p * (_nt(v_ref[rows, :], do[off:, :]) - delta[:, off:]) * MLA_SCALE).astype(BF16)
            dva_ref[rows, :] += _nn(p.astype(BF16), do[off:, :])
            dk_ref[rows, :] += _nn(ds, q[off:, :])
            dqa_ref[:, off:] += _tn(k, ds)

        def step(jj, c):
            for u in range(unroll):
                tile(jj * unroll + u, None, 0)
            return c

        lax.fori_loop(0, i * nb // unroll, step, 0)
        for b in range(nb):
            tile(i * nb + b, masks[b], b * tk)
        dq_ref[...] = dqa_ref[...].T

        @pl.when(i == nq - 1)
        def _():
            dv_ref[...] = dva_ref[...].astype(BF16)

    blk = lambda w, off: pl.BlockSpec((tq, w), lambda h, i: (i, off + h))
    whole = lambda w: pl.BlockSpec((s, w), lambda h, i: (0, h))
    return pl.pallas_call(
        body, name="mla_bwd", grid=(HEADS, nq),
        in_specs=[blk(MLA_QK_PAD, 0), whole(MLA_QK_PAD), whole(HEAD_DIM), blk(HEAD_DIM, HEADS),
                  blk(HEAD_DIM, HEADS), blk(HEAD_DIM, 0), _row_spec(tq)],
        out_specs=[blk(MLA_QK_PAD, 0), whole(MLA_QK_PAD), whole(HEAD_DIM), blk(HEAD_DIM, 0)],
        out_shape=[jax.ShapeDtypeStruct((s, HEADS * MLA_QK_PAD), F32),
                   jax.ShapeDtypeStruct((s, HEADS * MLA_QK_PAD), F32),
                   jax.ShapeDtypeStruct((s, SB_WIDTH), BF16), jax.ShapeDtypeStruct((s, SB_WIDTH), BF16)],
        scratch_shapes=[pltpu.VMEM((s, HEAD_DIM), F32), pltpu.VMEM((MLA_QK_PAD, tq), F32)],
        compiler_params=_cp(("parallel", "arbitrary")),
    )(qm, km, vm, dmixed, projf, ob, lse)


def _norm_bwd(x, w, dn):
    r = lax.rsqrt(jnp.mean(x * x, axis=1, keepdims=True) + EPS)
    xhat = x * r
    g = dn * w
    return r * (g - xhat * jnp.mean(g * xhat, axis=1, keepdims=True)), dn * xhat


def _mla_bwd_post(dqm, dkm, dvm, projf, qw, kvw, wq, wk, wv, rc, rs1, rs2, tm=256):
    s = dqm.shape[0]

    def body(dq_ref, dk_ref, dv_ref, cq_ref, ckv_ref, qw_ref, kvw_ref, wq_ref, wk_ref, wv_ref,
             c_ref, s1_ref, s2_ref, dqp_ref, dkn_ref, dlat_ref, dqw_ref, dkvw_ref):
        i = pl.program_id(0)
        c, s1, s2 = c_ref[...], s1_ref[...], s2_ref[...]
        drot = jnp.zeros((tm, 128), F32)
        for h in range(HEADS):
            lo = h * MLA_QK_PAD
            dqp_ref[:, lo:lo + 128] = dq_ref[:, lo:lo + 128].astype(BF16)
            dqp_ref[:, lo + 128:lo + 256] = _rope_t(dq_ref[:, lo + 128:lo + 256], c, s1, s2).astype(BF16)
            dkn_ref[:, h * 128:(h + 1) * 128] = dk_ref[:, lo:lo + 128].astype(BF16)
            drot = drot + dk_ref[:, lo + 128:lo + 256]
        dlat_ref[:, Q_RANK + KV_RANK:Q_RANK + KV_RANK + 128] = _rope_t(drot, c, s1, s2).astype(BF16)
        dlat_ref[:, Q_RANK + KV_RANK + 128:] = jnp.zeros((tm, 128), BF16)
        dcq, dqw = _norm_bwd(cq_ref[...], qw_ref[...], _nt(dqp_ref[...], wq_ref[...]))
        dlat_ref[:, :Q_RANK] = dcq.astype(BF16)
        dnkv = _nt(dkn_ref[...], wk_ref[...]) + _nt(dv_ref[...], wv_ref[...])
        dckv, dkvw = _norm_bwd(ckv_ref[...], kvw_ref[...], dnkv)
        dlat_ref[:, Q_RANK:Q_RANK + KV_RANK] = dckv.astype(BF16)

        @pl.when(i == 0)
        def _():
            dqw_ref[...] = jnp.zeros_like(dqw_ref)
            dkvw_ref[...] = jnp.zeros_like(dkvw_ref)

        dqw_ref[...] += jnp.sum(dqw, axis=0, keepdims=True)
        dkvw_ref[...] += jnp.sum(dkvw, axis=0, keepdims=True)

    row = lambda w, b: pl.BlockSpec((tm, w), lambda i: (i, b))
    full = lambda a: pl.BlockSpec(a.shape, lambda i: (0, 0))
    sd = jax.ShapeDtypeStruct
    return pl.pallas_call(
        body, name="mla_bwd_post", grid=(s // tm,),
        in_specs=[row(HEADS * MLA_QK_PAD, 0), row(HEADS * MLA_QK_PAD, 0), row(SB_WIDTH, 0),
                  row(Q_RANK, 2048 // Q_RANK), row(KV_RANK, 2560 // KV_RANK),
                  full(qw), full(kvw), full(wq), full(wk), full(wv), row(128, 0), row(128, 0), row(128, 0)],
        out_specs=[row(HEADS * MLA_QK_PAD, 0), row(SB_WIDTH, 0), row(1024, 0),
                   pl.BlockSpec((1, Q_RANK), lambda i: (0, 0)), pl.BlockSpec((1, KV_RANK), lambda i: (0, 0))],
        out_shape=[sd((s, HEADS * MLA_QK_PAD), BF16), sd((s, SB_WIDTH), BF16), sd((s, 1024), BF16),
                   sd((1, Q_RANK), F32), sd((1, KV_RANK), F32)],
        compiler_params=_cp(("arbitrary",)),
    )(dqm, dkm, dvm, projf, projf, qw, kvw, wq, wk, wv, rc, rs1, rs2)


def _pre_bwd(pieces, win, x, pw, dout, psums, tm=256):
    s, d = x.shape
    npc, pw_ = len(pieces), pieces[0].shape[1]
    nw, nsteps = len(psums), s // tm

    def body(*refs):
        dps, (w_ref, x_ref, pw_ref, do_ref), refs = refs[:npc], refs[npc:npc + 4], refs[npc + 4:]
        ps, (dx_ref, dpw_ref), gots = refs[:nw], refs[nw:nw + 2], refs[nw + 2:2 * nw + 2]
        send_sems, recv_sems = refs[2 * nw + 2:]
        i = pl.program_id(0)

        def copies():
            if not nw:
                return []
            px, py, pc = _place()
            chips = [(1 - px, py), (px, 1 - py), (1 - px, 1 - py)]
            return [pltpu.make_async_remote_copy(
                src_ref=ps[w].at[2 * chip[0] + chip[1]], dst_ref=gots[w].at[j], send_sem=send_sems.at[3 * w + j],
                recv_sem=recv_sems.at[3 * w + j], device_id=(*chip, pc), device_id_type=MESH)
                for w in range(nw) for j, chip in enumerate(chips)]

        @pl.when(i == 0)
        def _():
            dpw_ref[...] = jnp.zeros_like(dpw_ref)
            for cp in copies():
                cp.start()

        dh = _nt(dps[0][...], w_ref[:, :pw_])
        for p in range(1, npc):
            dh = dh + _nt(dps[p][...], w_ref[:, p * pw_:(p + 1) * pw_])
        dx, dw = _norm_bwd(x_ref[...], pw_ref[...], dh)
        dx_ref[...] = do_ref[...] + dx
        dpw_ref[...] += jnp.sum(dw, axis=0, keepdims=True)

        @pl.when(i == nsteps - 1)
        def _():
            for cp in copies():
                cp.wait()

    rowd = pl.BlockSpec((tm, d), lambda i: (i, 0))
    vec = pl.BlockSpec((1, d), lambda i: (0, 0))
    out = pl.pallas_call(
        body, name="pre_bwd", grid=(nsteps,),
        in_specs=[pl.BlockSpec((tm, pw_), lambda i: (i, 0))] * npc
        + [pl.BlockSpec(win.shape, lambda i: (0, 0), pipeline_mode=pl.Buffered(1)), rowd, vec, rowd] + [ANY] * nw,
        out_specs=[rowd, vec] + [ANY] * nw,
        out_shape=[jax.ShapeDtypeStruct((s, d), F32), jax.ShapeDtypeStruct((1, d), F32)]
        + [jax.ShapeDtypeStruct((3,) + p.shape[1:], p.dtype) for p in psums],
        scratch_shapes=[pltpu.SemaphoreType.DMA((max(3 * nw, 1),)), pltpu.SemaphoreType.DMA((max(3 * nw, 1),))],
        compiler_params=_cp(("arbitrary",), has_side_effects=True),
    )(*pieces, win, x, pw, dout, *psums)
    return out[0], out[1], list(out[2:])


def _adamw(w, g, m, v):
    m = ADAM_B1 * m + (1.0 - ADAM_B1) * g
    v = ADAM_B2 * v + (1.0 - ADAM_B2) * (g * g)
    delta = -ADAM_LR * ((m / ADAM_C1) / (jnp.sqrt(v / ADAM_C2) + ADAM_EPS) + ADAM_WD * w)
    return delta, m, v


def _row_block(rows):
    return math.gcd(rows, 256)


def _sum_parts(core, own, got, name):
    n, hr, cols = got.shape
    tr = _row_block(hr)
    nblk = hr // tr

    def body(c_ref, o_ref, p_ref, g_ref):
        g = o_ref[...]
        for k in range(n):
            g = g + p_ref[k].astype(F32)
        g_ref[...] = g

    return pl.pallas_call(
        body, name=name,
        grid_spec=pltpu.PrefetchScalarGridSpec(
            num_scalar_prefetch=1, grid=(nblk,),
            in_specs=[pl.BlockSpec((tr, cols), lambda i, c: (i, 0)),
                      pl.BlockSpec((n, tr, cols), lambda i, c: (0, i, 0))],
            out_specs=pl.BlockSpec((tr, cols), lambda i, c: (c[0] * nblk + i, 0))),
        out_shape=jax.ShapeDtypeStruct((2 * hr, cols), F32),
        compiler_params=_cp(("parallel",)),
    )(core, own, got)


def _adamw_shard(g, w, m, v, name):
    rows, cols = g.shape
    tr = _row_block(rows)

    def body(g_ref, w_ref, m_ref, v_ref, go_ref, d_ref, nm_ref, nv_ref):
        g_ = g_ref[...]
        go_ref[...] = g_
        d_ref[...], nm_ref[...], nv_ref[...] = _adamw(w_ref[...], g_, m_ref[...], v_ref[...])

    blk = pl.BlockSpec((tr, cols), lambda i: (i, 0))
    sd = jax.ShapeDtypeStruct((rows, cols), F32)
    return pl.pallas_call(
        body, name=name, grid=(rows // tr,), in_specs=[blk] * 4, out_specs=[blk] * 4, out_shape=[sd] * 4,
        compiler_params=_cp(("parallel",)),
    )(g, w, m, v)


def _pair_sum(place, grads, got, name):
    n, _, hr, cols = grads.shape
    tr = _row_block(hr)

    def body(p_ref, a_ref, b_ref, oa_ref, ob_ref, all_ref, own_ref):
        all_ref[...] = (a_ref[:, 0] + b_ref[...]).astype(BF16)
        own_ref[...] = oa_ref[0, 0] + ob_ref[0]

    return pl.pallas_call(
        body, name=name,
        grid_spec=pltpu.PrefetchScalarGridSpec(
            num_scalar_prefetch=1, grid=(hr // tr,),
            in_specs=[pl.BlockSpec((n, 1, tr, cols), lambda i, p: (0, p[0], i, 0)),
                      pl.BlockSpec((n, tr, cols), lambda i, p: (0, i, 0)),
                      pl.BlockSpec((1, 1, tr, cols), lambda i, p: (p[1], p[0], i, 0)),
                      pl.BlockSpec((1, tr, cols), lambda i, p: (p[1], i, 0))],
            out_specs=[pl.BlockSpec((n, tr, cols), lambda i, p: (0, i, 0)),
                       pl.BlockSpec((tr, cols), lambda i, p: (i, 0))]),
        out_shape=[jax.ShapeDtypeStruct((n, hr, cols), BF16), jax.ShapeDtypeStruct((hr, cols), F32)],
        compiler_params=_cp(("parallel",)),
    )(place, grads, got, grads, got)


def _place():
    return lax.axis_index("x"), lax.axis_index("y"), lax.axis_index("c")


W_NAMES = ("w_in", "w_q_up", "w_kv_up", "w_out")
NW = len(W_NAMES)


def _comm_call(body, name, ins, out_shape, n_copies, aliases=None):
    return pl.pallas_call(
        body, name=name, in_specs=[ANY] * len(ins), out_specs=[ANY] * len(out_shape), out_shape=out_shape,
        input_output_aliases=aliases or {},
        scratch_shapes=[pltpu.SemaphoreType.DMA((n_copies,)), pltpu.SemaphoreType.DMA((n_copies,))],
        compiler_params=pltpu.CompilerParams(has_side_effects=True),
    )(*ins)


def _gather_weights(shards):
    def body(*refs):
        ws, outs = refs[:NW], refs[NW:2 * NW]
        send_sems, recv_sems = refs[2 * NW:]
        x, y, c = _place()
        chips = [(1 - x, y), (x, 1 - y), (1 - x, 1 - y)]

        def rows(w, chip, core):
            half = ws[w].shape[0] // 2
            return outs[w].at[2 * chip[0] + chip[1], pl.ds(core * half, half), :]

        def copy(w, k, chip, core, to, own=False):
            half = ws[w].shape[0] // 2
            return pltpu.make_async_remote_copy(
                src_ref=ws[w].at[pl.ds(core * half, half), :] if own else rows(w, chip, core),
                dst_ref=rows(w, chip, core), send_sem=send_sems.at[6 * w + k], recv_sem=recv_sems.at[6 * w + k],
                device_id=to, device_id_type=MESH)

        first = [copy(w, j, (x, y), c, (*chip, c), own=True) for w in range(NW) for j, chip in enumerate(chips)]
        for cp in first:
            cp.start()
        passed = []
        for w in range(NW):
            for j, chip in enumerate(chips):
                copy(w, j, chip, c, (x, y, c)).wait_recv()
                passed.append(copy(w, 3 + j, chip, c, (x, y, 1 - c)))
                passed[-1].start()
        for w in range(NW):
            for j, chip in enumerate(chips):
                copy(w, 3 + j, chip, 1 - c, (x, y, c)).wait_recv()
        for cp in first + passed:
            cp.wait_send()

    return _comm_call(body, "gather_weights", shards,
                      [jax.ShapeDtypeStruct((4,) + w.shape, w.dtype) for w in shards], 6 * NW)


def _swap_halves(grads):
    def body(*refs):
        gs, gots = refs[:NW], refs[NW:2 * NW]
        send_sems, recv_sems = refs[2 * NW:]
        x, y, c = _place()
        copies = [pltpu.make_async_remote_copy(
            src_ref=gs[w].at[k, 1 - c], dst_ref=gots[w].at[k], send_sem=send_sems.at[4 * w + k],
            recv_sem=recv_sems.at[4 * w + k], device_id=(x, y, 1 - c), device_id_type=MESH)
            for w in range(NW) for k in range(4)]
        for cp in copies:
            cp.start()
        for cp in copies:
            cp.wait()

    return _comm_call(body, "swap_halves", grads,
                      [jax.ShapeDtypeStruct((4,) + g.shape[2:], g.dtype) for g in grads], 4 * NW)


def _share_with_sibling(arrs):
    na = len(arrs)

    def body(*refs):
        outs = refs[na:2 * na]
        send_sems, recv_sems = refs[2 * na:]
        x, y, c = _place()

        def half(ref):
            hr = ref.shape[0] // 2
            return ref.at[pl.ds(c * hr, hr), :]

        copies = [pltpu.make_async_remote_copy(
            src_ref=half(outs[k]), dst_ref=half(outs[k]), send_sem=send_sems.at[k], recv_sem=recv_sems.at[k],
            device_id=(x, y, 1 - c), device_id_type=MESH) for k in range(na)]
        for cp in copies:
            cp.start()
        for cp in copies:
            cp.wait()

    return _comm_call(body, "share_with_sibling", arrs, [jax.ShapeDtypeStruct(a.shape, a.dtype) for a in arrs],
                      na, aliases={k: k for k in range(na)})


def _norm_allreduce_adamw(part, w, m, v):
    r, lanes = part.shape

    def body(p_ref, w_ref, m_ref, v_ref, g_ref, d_ref, nm_ref, nv_ref, all_ref, send_sems, recv_sems):
        x, y, c = _place()
        me = 4 * x + 2 * y + c
        all_ref[me] = p_ref[...]
        copies = []
        for k in range(1, 8):
            peer = (x ^ (k >> 2), y ^ ((k >> 1) & 1), c ^ (k & 1))
            copies.append(pltpu.make_async_remote_copy(
                src_ref=p_ref, dst_ref=all_ref.at[me], send_sem=send_sems.at[k - 1], recv_sem=recv_sems.at[k - 1],
                device_id=peer, device_id_type=MESH))
        for cp in copies:
            cp.start()
        for cp in copies:
            cp.wait()
        g = all_ref[0]
        for k in range(1, 8):
            g = g + all_ref[k]
        g_ref[...] = g
        d_ref[...], nm_ref[...], nv_ref[...] = _adamw(w_ref[...], g, m_ref[...], v_ref[...])

    vm = pl.BlockSpec(memory_space=pltpu.VMEM)
    sd = jax.ShapeDtypeStruct((r, lanes), F32)
    return pl.pallas_call(
        body, name="norm_allreduce_adamw", in_specs=[vm] * 4, out_specs=[vm] * 4, out_shape=[sd] * 4,
        scratch_shapes=[pltpu.VMEM((8, r, lanes), F32), pltpu.SemaphoreType.DMA((7,)),
                        pltpu.SemaphoreType.DMA((7,))],
        compiler_params=pltpu.CompilerParams(has_side_effects=True),
    )(part, w, m, v)


LANES = 128


def _perm_in(shards):
    s0, s1, s2, s3 = shards
    w = D_IN // 4
    return jnp.concatenate([s0, s1, s2[:, :4096 - 2 * w], s3[:, 4928 - 3 * w:], s2[:, 4096 - 2 * w:],
                            s3[:, :4928 - 3 * w], jnp.zeros((s0.shape[0], D_INP - D_IN), s0.dtype)], axis=1)


def _by_chip_and_half(g, axis):
    rows, cols = g.shape
    if axis == 0:
        return g.reshape(4, 2, rows // 8, cols)
    return g.reshape(rows, 4, cols // 4).transpose(1, 0, 2).reshape(4, 2, rows // 2, cols // 4)


NORM_NAMES = ("pre_norm_w", "q_norm_w", "kv_norm_w", "post_norm_w")
NORM_SIZES = (D_MODEL, Q_RANK, KV_RANK, D_MODEL)
NORM_ROWS = 40


def _pack_norm(vs):
    flat = jnp.concatenate([v.reshape(-1) for v in vs])
    return jnp.pad(flat, (0, NORM_ROWS * LANES - flat.shape[0])).reshape(NORM_ROWS, LANES)


def _unpack_norm(p):
    flat, out, at = p.reshape(-1), [], 0
    for n in NORM_SIZES:
        out.append(flat[at:at + n].reshape(1, n))
        at += n
    return out


def _rope_tables(positions):
    inv_freq = ROPE_THETA ** (-jnp.arange(0, MLA_ROPE, 2, dtype=F32) / MLA_ROPE)
    ang = positions.astype(F32)[:, None] * inv_freq
    cos, sin, z = jnp.cos(ang), jnp.sin(ang), jnp.zeros_like(ang)
    return (jnp.concatenate([cos, cos, z, z], axis=1), jnp.concatenate([z, sin, z, z], axis=1),
            jnp.concatenate([-sin, z, z, z], axis=1))


def _local_step(x, positions, pre_w, win, q_w, wq, kv_w, wk, wv, wout, post_w, target):
    s = x.shape[0]
    rc, rs1, rs2 = _rope_tables(positions)
    h = _prenorm(x, pre_w)
    projb = _matmul(h, win, mode="nn", out_dtype=BF16, tm=512, tn=1024, tk=D_MODEL, name="proj_b", n=PB_W)
    projf = _matmul(h, win, mode="nn", out_dtype=F32, tm=512, tn=1024, tk=D_MODEL, name="proj_f", n=PF_W,
                    b_off=PB_W // 1024)
    oa, a_st, b_st = _sb_fwd(projb)
    nq, nkv, qm, km, vm = _mla_prep(projf, q_w, kv_w, wq, wk, wv, rc, rs1, rs2)
    ob, lse = _mla_fwd(qm, km, vm)
    mixed, dy, dout, err2, dpost = _out_post(oa, ob, projf, wout, x, target, post_w)

    dwout = _matmul(mixed, dy, mode="tn", out_dtype=F32, tm=1024, tn=1024, tk=min(4096, s), name="dw_out")
    dmixed = _matmul(dy, wout, mode="nt", out_dtype=F32, tm=512, tn=1024, tk=D_MODEL, name="d_mixed")
    dqa, dka, dva, dga = _sb_bwd(projb, projf, dmixed, oa, a_st, b_st)
    dqm, dkm, dvm, dgb = _mla_bwd(qm, km, vm, projf, dmixed, ob, lse)
    dqp, dkn, dlat, dqw, dkvw = _mla_bwd_post(dqm, dkm, dvm, projf, q_w, kv_w, wq, wk, wv, rc, rs1, rs2)
    tks = min(4096, s)
    dwq = _matmul(nq, dqp, mode="tn", out_dtype=F32, tm=Q_RANK, tn=1024, tk=tks, name="dw_q")
    dwk = _matmul(nkv, dkn, mode="tn", out_dtype=F32, tm=KV_RANK, tn=1024, tk=tks, name="dw_k")
    dwv = _matmul(nkv, dvm, mode="tn", out_dtype=F32, tm=KV_RANK, tn=1024, tk=tks, name="dw_v")
    lat = Q_RANK + KV_RANK + MLA_ROPE
    orig = jnp.concatenate([dqa, dka, dva, dga, dlat[:, :lat], dgb], axis=1)
    dwin = _dw_by_shard(h, jnp.stack(jnp.split(orig, 4, axis=1)))
    return err2, (dqa, dka, dva, dga, dgb, dlat), dout, dwin, dqw, dwq, dkvw, dwk, dwv, dwout, dpost


def _kernel_layouts(in_shards, w_q_up, w_kv_up, w_out):
    win = _perm_in(in_shards)
    wq = jnp.pad(w_q_up.reshape(Q_RANK, HEADS, 192), ((0, 0), (0, 0), (0, 64))).reshape(Q_RANK, HEADS * MLA_QK_PAD)
    kv = w_kv_up.reshape(KV_RANK, HEADS, 256)
    wk = kv[:, :, :128].reshape(KV_RANK, SB_WIDTH)
    wv = kv[:, :, 128:].reshape(KV_RANK, SB_WIDTH)
    return win, wq, wk, wv, w_out


def _original_layouts(dwq, dwk, dwv):
    dq = dwq.reshape(Q_RANK, HEADS, MLA_QK_PAD)[:, :, :192].reshape(Q_RANK, HEADS * 192)
    dkv = jnp.concatenate([dwk.reshape(KV_RANK, HEADS, 128), dwv.reshape(KV_RANK, HEADS, 128)], axis=2)
    return dq, dkv.reshape(KV_RANK, 2 * SB_WIDTH)


def kernel(x, positions, pre_norm_w, w_in, q_norm_w, w_q_up, kv_norm_w, w_kv_up, w_out, post_norm_w, loss_target, m_pre_norm_w, m_w_in, m_q_norm_w, m_w_q_up, m_kv_norm_w, m_w_kv_up, m_w_out, m_post_norm_w, v_pre_norm_w, v_w_in, v_q_norm_w, v_w_q_up, v_kv_norm_w, v_w_kv_up, v_w_out, v_post_norm_w):
    c = lax.axis_index("c")
    chip = 2 * lax.axis_index("x") + lax.axis_index("y")
    shards = (w_in[0], w_q_up[0], w_kv_up[0], w_out[0])
    mine16 = [w.astype(BF16) for w in shards]
    others = _gather_weights(mine16)
    slot = lambda w, k: jnp.where(chip == k, mine16[w], others[w][k])
    cat = lambda w, ax: jnp.concatenate([slot(w, k) for k in range(4)], axis=ax)
    win, wq, wk, wv, wout = _kernel_layouts([slot(0, k) for k in range(4)], cat(1, 1), cat(2, 1), cat(3, 0))

    err2, pieces, dout, dwin, dqw, dwq, dkvw, dwk, dwv, dwout, dpost = _local_step(
        x[0], positions[0], pre_norm_w, win, q_norm_w, wq, kv_norm_w, wk, wv, wout, post_norm_w, loss_target[0])
    loss = lax.psum(0.5 * jnp.sum(err2) / D_MODEL, ("x", "y", "c"))

    dq, dkv = _original_layouts(dwq, dwk, dwv)
    grads = [dwin.reshape(4, 2, D_MODEL // 2, D_IN // 4)] + [
        _by_chip_and_half(g, ax) for g, ax in ((dq, 1), (dkv, 1), (dwout, 0))]
    place = jnp.stack([c, chip])
    halves = _swap_halves(grads)
    pairs = [_pair_sum(place, grads[w], halves[w], "pair_sum_" + W_NAMES[w]) for w in range(NW)]
    gx, dpre, gots = _pre_bwd(pieces, win, x[0], pre_norm_w, dout, [p[0] for p in pairs])
    sums = _share_with_sibling([_sum_parts(c.reshape(1), pairs[w][1], gots[w], "sum_parts_" + W_NAMES[w])
                                for w in range(NW)])
    ms = (m_w_in[0], m_w_q_up[0], m_w_kv_up[0], m_w_out[0])
    vs = (v_w_in[0], v_w_q_up[0], v_w_kv_up[0], v_w_out[0])
    done = [_adamw_shard(sums[w], shards[w], ms[w], vs[w], "adamw_" + W_NAMES[w]) for w in range(NW)]
    big = [[done[w][k] for w in range(NW)] for k in range(4)]

    small = _norm_allreduce_adamw(
        _pack_norm([dpre, dqw, dkvw, dpost]), _pack_norm([pre_norm_w, q_norm_w, kv_norm_w, post_norm_w]),
        _pack_norm([m_pre_norm_w, m_q_norm_w, m_kv_norm_w, m_post_norm_w]),
        _pack_norm([v_pre_norm_w, v_q_norm_w, v_kv_norm_w, v_post_norm_w]))
    small = [_unpack_norm(p) for p in small]

    def group(k):
        n, b = small[k], big[k]
        return (n[0], b[0][None], n[1], b[1][None], n[2], b[2][None], b[3][None], n[3])

    return (loss, gx[None], *group(0), *group(1), *group(2), *group(3))
```
